```python
import math
import jax, jax.numpy as jnp
from jax import lax
import numpy as np

D_MODEL = 2048
BATCH = 32
SEQ = 256
DEPTH = 4
DEC_BATCH = 4
DEC_SEQ = 2048
PAST_LEN = 512

GRID_W = 64
N_MIXERS = 3
N_DN = (DEPTH + 2) // 3
N_GLA = (DEPTH + 1) // 3
N_HGRN = DEPTH // 3
EPS = 1e-6
D_FF = 4 * D_MODEL

DN_HK = 16
DN_HV = 32
DN_DK = 128
DN_DV = 128
DN_KD = DN_HK * DN_DK
DN_VD = DN_HV * DN_DV
CONV_K = 5
DELTA_CHUNK = 64

GLA_H = 4
GLA_KD = D_MODEL // 2
GLA_VD = D_MODEL
GLA_DK = GLA_KD // GLA_H
GLA_DV = GLA_VD // GLA_H
GLA_RANK = 16
GLA_TAU = 16.0
GATED_CHUNK = 32

HGRN_EXPAND = 128
HGRN_H = D_MODEL // HGRN_EXPAND
HGRN_DK = HGRN_EXPAND
HGRN_DV = D_MODEL // HGRN_H

kernel_name = 'bidir_deltanet_gla_hgrn2_flow_step'


def rms_norm(x, w):
    xf = x.astype(jnp.float32)
    y = xf * lax.rsqrt(jnp.mean(xf * xf, axis=-1, keepdims=True) + EPS)
    return (y * w.astype(jnp.float32)).astype(x.dtype)


def l2norm(x):
    xf = x.astype(jnp.float32)
    return (xf * lax.rsqrt(jnp.sum(xf * xf, axis=-1, keepdims=True) + EPS)).astype(x.dtype)


def split_heads(x, n):
    b, t, _ = x.shape
    return x.reshape(b, t, n, -1).transpose(0, 2, 1, 3)


def short_conv(x, w, grid):
    b, t, ch = x.shape
    if grid:
        x = x.reshape(b, t // GRID_W, GRID_W, ch)
    length = x.shape[-2]
    pad = CONV_K // 2
    xp = jnp.pad(x, [(0, 0)] * (x.ndim - 2) + [(pad, pad), (0, 0)])
    y = sum(xp[..., s:s + length, :] * w[s] for s in range(CONV_K))
    return y.reshape(b, t, ch)


def zero_states(s0, b, h, dk, dv, dtype):
    if s0 is not None:
        return s0
    z = jnp.zeros((b, h, dk, dv), dtype)
    return (z, z)


def chunk_delta(q, k, v, beta, g, s0):
    out_dtype = v.dtype
    q, k, v, beta, g, s = [a.astype(jnp.float32) for a in (q, k, v, beta, g, s0)]
    b, h, t, dk = q.shape
    dv = v.shape[-1]
    c = DELTA_CHUNK
    n = t // c
    q, k, v = [a.reshape(b, h, n, c, -1) for a in (q, k, v)]
    beta = beta.reshape(b, h, n, c)
    g = jnp.cumsum(g.reshape(b, h, n, c), axis=-1)
    incl = jnp.tril(jnp.ones((c, c), bool))
    strict = jnp.tril(jnp.ones((c, c), bool), -1)
    decay = jnp.where(incl, jnp.exp(jnp.where(incl, g[..., :, None] - g[..., None, :], 0.0)), 0.0)
    lmat = jnp.where(strict, beta[..., :, None] * jnp.einsum('bhnid,bhnjd->bhnij', k, k) * decay, 0.0)
    rhs = jnp.concatenate([beta[..., None] * v, (beta * jnp.exp(g))[..., None] * k], axis=-1)
    sol = lax.linalg.triangular_solve(jnp.eye(c, dtype=jnp.float32) + lmat, rhs, left_side=True, lower=True)
    u0, w = sol[..., :dv], sol[..., dv:]
    attn = jnp.einsum('bhnid,bhnjd->bhnij', q, k) * decay
    qg = q * jnp.exp(g)[..., None]
    kd = k * jnp.exp(g[..., -1:] - g)[..., None]
    g_end = jnp.exp(g[..., -1])

    def step(state, xs):
        u0c, wc, ac, qc, kc, gc = xs
        u = u0c - jnp.einsum('bhik,bhkv->bhiv', wc, state)
        o = jnp.einsum('bhij,bhjv->bhiv', ac, u) + jnp.einsum('bhik,bhkv->bhiv', qc, state)
        state = gc[..., None, None] * state + jnp.einsum('bhjk,bhjv->bhkv', kc, u)
        return state, o

    xs = [jnp.moveaxis(a, 2, 0) for a in (u0, w, attn, qg, kd, g_end)]
    s, o = lax.scan(step, s, xs)
    o = jnp.moveaxis(o, 0, 2).reshape(b, h, t, dv)
    return o.astype(out_dtype), s.astype(out_dtype)


def chunk_gla(q, k, v, log_a, s0):
    out_dtype = v.dtype
    q, k, v, log_a, s = [a.astype(jnp.float32) for a in (q, k, v, log_a, s0)]
    b, h, t, dk = q.shape
    c = GATED_CHUNK
    n = t // c
    chunks = lambda a: jnp.moveaxis(a.reshape(b, h, n, c, a.shape[-1]), 2, 0)
    incl = jnp.tril(jnp.ones((c, c), bool))[..., None]

    def step(state, xs):
        qc, kc, vc, lc = xs
        cb = jnp.cumsum(lc, axis=-2)
        diff = cb[..., :, None, :] - cb[..., None, :, :]
        decay = jnp.where(incl, jnp.exp(jnp.where(incl, diff, 0.0)), 0.0)
        attn = jnp.einsum('bhik,bhjk,bhijk->bhij', qc, kc, decay)
        o = jnp.einsum('bhij,bhjv->bhiv', attn, vc) + jnp.einsum('bhik,bhkv->bhiv', qc * jnp.exp(cb), state)
        cb_end = cb[..., -1:, :]
        state = jnp.exp(cb_end[..., 0, :])[..., None] * state + jnp.einsum(
            'bhjk,bhjv->bhkv', kc * jnp.exp(cb_end - cb), vc)
        return state, o

    s, o = lax.scan(step, s, [chunks(a) for a in (q, k, v, log_a)])
    o = jnp.moveaxis(o, 0, 2).reshape(b, h, t, -1)
    return o.astype(out_dtype), s.astype(out_dtype)


def bidirectional(core, fwd, bwd, s0):
    o_f, s_f = core(*fwd, s0[0])
    o_b, s_b = core(*(jnp.flip(a, axis=2) for a in bwd), s0[1])
    return o_f + jnp.flip(o_b, axis=2), (s_f, s_b)


def gated_out(o, norm_w, gate, w_out):
    b, h, t, d = o.shape
    y = rms_norm(o.transpose(0, 2, 1, 3), norm_w) * jax.nn.silu(gate).reshape(b, t, h, d)
    return y.reshape(b, t, h * d) @ w_out


def deltanet_mixer(h, p, j, i, s0, grid):
    b, t, _ = h.shape
    proj = h @ p['dn_w_in'][j]
    n_conv = 2 * DN_KD + DN_VD
    qkv = jax.nn.silu(short_conv(proj[..., :n_conv], p['dn_conv'][j], grid))
    q, k, v = jnp.split(qkv, [DN_KD, 2 * DN_KD], axis=-1)
    z = proj[..., n_conv:n_conv + DN_VD]
    beta_f, beta_b, a_f, a_b = jnp.split(proj[..., n_conv + DN_VD:], 4, axis=-1)
    rep = DN_HV // DN_HK
    q = jnp.repeat(l2norm(split_heads(q, DN_HK)), rep, axis=1) * DN_DK ** -0.5
    k = jnp.repeat(l2norm(split_heads(k, DN_HK)), rep, axis=1)
    v = split_heads(v, DN_HV)

    def gates(bl, al, d):
        beta = jax.nn.sigmoid(bl).transpose(0, 2, 1)
        g = -jnp.exp(p['dn_A_log'][j, d])[:, None] * jax.nn.softplus(
            al + p['dn_dt_bias'][j, d]).transpose(0, 2, 1)
        return beta, g

    bf, gf = gates(beta_f, a_f, 0)
    bb, gb = gates(beta_b, a_b, 1)
    s0 = zero_states(s0, b, DN_HV, DN_DK, DN_DV, h.dtype)
    o, states = bidirectional(chunk_delta, (q, k, v, bf, gf), (q, k, v, bb, gb), s0)
    return gated_out(o, p['dn_norm'][j], z, p['dn_w_out'][j]), states


def gla_mixer(h, p, j, i, s0, grid):
    b, t, _ = h.shape
    q, k, v, r, a1 = jnp.split(h @ p['gla_w_in'][j],
                               [GLA_KD, 2 * GLA_KD, 2 * GLA_KD + GLA_VD, 2 * GLA_KD + 2 * GLA_VD], axis=-1)
    q = split_heads(q, GLA_H) * GLA_DK ** -0.5
    k = split_heads(k, GLA_H)
    v = split_heads(v, GLA_H)

    def log_alpha(d):
        a = a1[..., d * GLA_RANK:(d + 1) * GLA_RANK] @ p['gla_w_a2'][j, d] + p['gla_b_a'][j, d]
        return split_heads(jax.nn.log_sigmoid(a.astype(jnp.float32)) / GLA_TAU, GLA_H)

    s0 = zero_states(s0, b, GLA_H, GLA_DK, GLA_DV, h.dtype)
    o, states = bidirectional(chunk_gla, (q, k, v, log_alpha(0)), (q, k, v, log_alpha(1)), s0)
    return gated_out(o, p['gla_norm'][j], r, p['gla_w_out'][j]), states


def hgrn2_mixer(h, p, j, i, s0, grid):
    b, t, _ = h.shape
    q, inp, gate, f_f, f_b = jnp.split(h @ p['hgrn_w_in'][j], 5, axis=-1)
    probs = jax.nn.softmax(p['hgrn_lb_logits'].astype(jnp.float32), axis=0)
    lb = (jnp.cumsum(probs, axis=0) - probs[0])[i]

    def forget(fl):
        fl = fl.astype(jnp.float32)
        log_f = jnp.logaddexp(jnp.log(lb), jnp.log1p(-lb) + jax.nn.log_sigmoid(fl))
        k = (1.0 - lb) * jax.nn.sigmoid(-fl)
        return split_heads(k, HGRN_H), split_heads(log_f, HGRN_H)

    q = split_heads(jax.nn.silu(q), HGRN_H) * HGRN_DK ** -0.5
    v = split_heads(inp, HGRN_H)
    k_f, lf_f = forget(f_f)
    k_b, lf_b = forget(f_b)
    s0 = zero_states(s0, b, HGRN_H, HGRN_DK, HGRN_DV, h.dtype)
    o, states = bidirectional(chunk_gla, (q, k_f, v, lf_f), (q, k_b, v, lf_b), s0)
    return gated_out(o, p['hgrn_norm'][j], gate, p['hgrn_w_out'][j]), states


def sq_relu_mlp(h, w_up, w_down):
    return jnp.square(jax.nn.relu(h @ w_up)) @ w_down


def trunk(x, cond, cache, p):
    grid = cache is not None
    silu_cond = jax.nn.silu(cond)
    collected = ([], [], [])
    for i in range(DEPTH):
        kind, j = i % N_MIXERS, i // N_MIXERS
        mod = silu_cond @ p['ada_w'][i] + p['ada_b'][i]
        sh1, sc1, g1, sh2, sc2, g2 = jnp.split(mod[:, None, :], 6, axis=-1)
        h = rms_norm(x, p['norm1'][i]) * (1 + sc1) + sh1
        s0 = None if cache is None else (cache[kind][:, j, 0], cache[kind][:, j, 1])
        mixer = (deltanet_mixer, gla_mixer, hgrn2_mixer)[kind]
        out, states = mixer(h, p, j, i, s0, grid)
        x = x + g1 * out
        h = rms_norm(x, p['norm2'][i]) * (1 + sc2) + sh2
        x = x + g2 * sq_relu_mlp(h, p['w_up'][i], p['w_down'][i])
        if cache is None:
            collected[kind].append(jnp.stack(states, axis=1))
    y = rms_norm(x, p['final_norm'])
    if cache is not None:
        return y, None
    return y, [jnp.stack(s, axis=1) for s in collected]


def setup_inputs(seed: int = 0) -> dict:
    key = jax.random.key(seed)
    ks = iter(jax.random.split(key, 32))
    nrm = lambda shape, scale: jax.random.normal(next(ks), shape, jnp.float32) * scale
    d = D_MODEL
    dn_cols = 2 * DN_KD + 2 * DN_VD + 4 * DN_HV
    gla_cols = 2 * GLA_KD + 2 * GLA_VD + 2 * GLA_RANK
    inputs = {}
    inputs['x_prompt'] = nrm((BATCH, SEQ, d), 1.0)
    inputs['x_sample'] = nrm((DEC_BATCH, DEC_SEQ, d), 1.0)
    inputs['c'] = nrm((DEC_BATCH, d), 1.0)
    inputs['state_deltanet'] = nrm((DEC_BATCH, N_DN, 2, DN_HV, DN_DK, DN_DV), DN_DK ** -0.5)
    inputs['state_gla'] = nrm((DEC_BATCH, N_GLA, 2, GLA_H, GLA_DK, GLA_DV), GLA_DK ** -0.5)
    inputs['state_hgrn'] = nrm((DEC_BATCH, N_HGRN, 2, HGRN_H, HGRN_DK, HGRN_DV), HGRN_DK ** -0.5)
    inputs['c_ctx'] = nrm((d,), 1.0)
    inputs['ada_w'] = nrm((DEPTH, d, 6 * d), 0.5 * d ** -0.5)
    inputs['ada_b'] = nrm((DEPTH, 6 * d), 0.02)
    inputs['norm1'] = 1.0 + nrm((DEPTH, d), 0.02)
    inputs['norm2'] = 1.0 + nrm((DEPTH, d), 0.02)
    inputs['final_norm'] = 1.0 + nrm((d,), 0.02)
    inputs['w_up'] = nrm((DEPTH, d, D_FF), d ** -0.5)
    inputs['w_down'] = nrm((DEPTH, D_FF, d), D_FF ** -0.5)
    inputs['dn_w_in'] = nrm((N_DN, d, dn_cols), d ** -0.5)
    inputs['dn_conv'] = nrm((N_DN, CONV_K, 2 * DN_KD + DN_VD), CONV_K ** -0.5)
    inputs['dn_A_log'] = jnp.log(jax.random.uniform(next(ks), (N_DN, 2, DN_HV), jnp.float32, 1.0, 16.0))
    dt = jnp.exp(jax.random.uniform(next(ks), (N_DN, 2, DN_HV), jnp.float32, math.log(1e-3), math.log(1e-1)))
    inputs['dn_dt_bias'] = dt + jnp.log(-jnp.expm1(-dt))
    inputs['dn_norm'] = 1.0 + nrm((N_DN, DN_DV), 0.02)
    inputs['dn_w_out'] = nrm((N_DN, DN_VD, d), DN_VD ** -0.5)
    inputs['gla_w_in'] = nrm((N_GLA, d, gla_cols), d ** -0.5)
    inputs['gla_w_a2'] = nrm((N_GLA, 2, GLA_RANK, GLA_KD), GLA_RANK ** -0.5)
    inputs['gla_b_a'] = nrm((N_GLA, 2, GLA_KD), 0.1)
    inputs['gla_norm'] = 1.0 + nrm((N_GLA, GLA_DV), 0.02)
    inputs['gla_w_out'] = nrm((N_GLA, GLA_VD, d), GLA_VD ** -0.5)
    inputs['hgrn_w_in'] = nrm((N_HGRN, d, 5 * d), d ** -0.5)
    inputs['hgrn_lb_logits'] = nrm((DEPTH, d), 0.1)
    inputs['hgrn_norm'] = 1.0 + nrm((N_HGRN, HGRN_DV), 0.02)
    inputs['hgrn_w_out'] = nrm((N_HGRN, d, d), d ** -0.5)
    return inputs


def reference(x_prompt, x_sample, c, state_deltanet, state_gla, state_hgrn, c_ctx, ada_w, ada_b, norm1,
              norm2, final_norm, w_up, w_down, dn_w_in, dn_conv, dn_A_log, dn_dt_bias, dn_norm, dn_w_out,
              gla_w_in, gla_w_a2, gla_b_a, gla_norm, gla_w_out, hgrn_w_in, hgrn_lb_logits, hgrn_norm,
              hgrn_w_out):
    p = dict(ada_w=ada_w, ada_b=ada_b, norm1=norm1, norm2=norm2, final_norm=final_norm, w_up=w_up,
             w_down=w_down, dn_w_in=dn_w_in, dn_conv=dn_conv, dn_A_log=dn_A_log, dn_dt_bias=dn_dt_bias,
             dn_norm=dn_norm, dn_w_out=dn_w_out, gla_w_in=gla_w_in, gla_w_a2=gla_w_a2, gla_b_a=gla_b_a,
             gla_norm=gla_norm, gla_w_out=gla_w_out, hgrn_w_in=hgrn_w_in, hgrn_lb_logits=hgrn_lb_logits,
             hgrn_norm=hgrn_norm, hgrn_w_out=hgrn_w_out)
    y_prompt, ctx_states = trunk(x_prompt, c_ctx[None, :], None, p)
    new_state_deltanet, new_state_gla, new_state_hgrn = ctx_states
    y_sample, _ = trunk(x_sample, c, (state_deltanet, state_gla, state_hgrn), p)
    return (y_prompt, y_sample, new_state_deltanet, new_state_gla, new_state_hgrn)
```

```python
import functools
from typing import NamedTuple

import jax
import jax.numpy as jnp
from jax import lax
from jax.experimental import pallas as pl
from jax.experimental.pallas import tpu as pltpu

F32 = jnp.float32
BF16 = jnp.bfloat16

EPS = 1e-6
GRID_W = 64
CONV_K = 5
GLA_RANK = 16
GLA_TAU = 16.0
N_MIXERS = 3
CHUNK = 64
SUB = 16
EXP_CLAMP = 80.0
V7X_VMEM_LIMIT_BYTES = 56 * 1024 * 1024
HIGHEST = lax.Precision.HIGHEST


class Tokens(NamedTuple):
    m_ctx: int
    seq: int
    dec_batch: int
    dec_seq: int

    @property
    def m(self):
        return self.m_ctx + self.dec_batch * self.dec_seq

    @property
    def n_ctx_blocks(self):
        return self.m_ctx // self.seq

    @property
    def blocks_per_dec(self):
        return self.dec_seq // self.seq

    @property
    def n_blocks(self):
        return self.m // self.seq


def _tile(n, pref):
    t = min(n, pref)
    while n % t:
        t //= 2
    return t


def _token_tile(tok, pref):
    t = pref
    while tok.m_ctx % t or tok.dec_seq % t:
        t //= 2
    return t


def _params(*sem):
    return pltpu.CompilerParams(dimension_semantics=sem, vmem_limit_bytes=V7X_VMEM_LIMIT_BYTES)


def _mod_row(i, tm, tok):
    start = i * tm
    return jnp.where(start < tok.m_ctx, 0, 1 + (start - tok.m_ctx) // tok.dec_seq)


def _sigmoid(x):
    return 1.0 / (1.0 + jnp.exp(-x))


def _softplus(x):
    return jnp.maximum(x, 0.0) + jnp.log(1.0 + jnp.exp(-jnp.abs(x)))


def _norm_mod(x, nw, shift, scale):
    ms = jnp.mean(x * x, axis=-1, keepdims=True)
    return x * lax.rsqrt(ms + EPS) * nw * (1.0 + scale) + shift


def _dot(a, b):
    return jnp.dot(a, b, preferred_element_type=F32)


def _dot_nt(a, b):
    return lax.dot_general(a, b, (((1,), (1,)), ((), ())), preferred_element_type=F32)


def _dot_tn(a, b):
    return lax.dot_general(a, b, (((0,), (0,)), ((), ())), preferred_element_type=F32)


def _split_dot(tri, x):
    hi = x.astype(BF16)
    lo = (x - hi.astype(F32)).astype(BF16)
    return _dot(tri, hi) + _dot(tri, lo)


def _split_matmul(a, b):
    m = a.shape[0]
    ah = a.astype(BF16)
    al = (a - ah.astype(F32)).astype(BF16)
    bh = b.astype(BF16)
    bl = (b - bh.astype(F32)).astype(BF16)
    t = _dot(jnp.concatenate([ah, al], axis=0), bh)
    return t[:m] + t[m:] + _dot(ah, bl)


def _tri(n, upper):
    r = lax.broadcasted_iota(jnp.int32, (n, n), 0)
    c = lax.broadcasted_iota(jnp.int32, (n, n), 1)
    return jnp.where((r <= c) if upper else (r >= c), 1.0, 0.0).astype(BF16)


def _mod_kernel(c_ref, w_ref, b_ref, o_ref):
    c = c_ref[...]
    s = c * _sigmoid(c)
    o_ref[0] = jnp.dot(s, w_ref[0], precision=HIGHEST, preferred_element_type=F32) + b_ref[0]


def _mod_table(cond, ada_w, ada_b):
    depth, d, n = ada_w.shape
    rows = cond.shape[0]
    tn = _tile(n, 1024)
    return pl.pallas_call(
        _mod_kernel,
        grid=(depth, n // tn),
        in_specs=[
            pl.BlockSpec((rows, d), lambda l, j: (0, 0)),
            pl.BlockSpec((1, d, tn), lambda l, j: (l, 0, j)),
            pl.BlockSpec((1, 1, tn), lambda l, j: (l, 0, j)),
        ],
        out_specs=pl.BlockSpec((1, rows, tn), lambda l, j: (l, 0, j)),
        out_shape=jax.ShapeDtypeStruct((depth, rows, n), F32),
        compiler_params=_params("parallel", "parallel"),
        name="mod_table",
    )(cond, ada_w, ada_b.reshape(depth, 1, n))


def _proj_kernel(x_ref, nw_ref, mod_ref, w_ref, o_ref, h_ref, *, precise):
    @pl.when(pl.program_id(1) == 0)
    def _():
        h = _norm_mod(x_ref[...], nw_ref[...], mod_ref[0, 0:1, :], mod_ref[0, 1:2, :])
        h_ref[...] = h.astype(h_ref.dtype)

    if precise:
        o_ref[...] = jnp.dot(h_ref[...], w_ref[...], precision=HIGHEST, preferred_element_type=F32)
    else:
        o_ref[...] = _dot(h_ref[...], w_ref[...]).astype(o_ref.dtype)


def _proj(x, nw, mod, w, tok, *, precise=False, tm=512, tn=1024):
    m, d = x.shape
    n = w.shape[1]
    tm = _token_tile(tok, tm)
    tn = _tile(n, tn)
    return pl.pallas_call(
        functools.partial(_proj_kernel, precise=precise),
        grid=(m // tm, n // tn),
        in_specs=[
            pl.BlockSpec((tm, d), lambda i, j: (i, 0)),
            pl.BlockSpec((1, d), lambda i, j: (0, 0)),
            pl.BlockSpec((1, 6, d), lambda i, j: (_mod_row(i, tm, tok), 0, 0)),
            pl.BlockSpec((d, tn), lambda i, j: (0, j)),
        ],
        out_specs=pl.BlockSpec((tm, tn), lambda i, j: (i, j)),
        out_shape=jax.ShapeDtypeStruct((m, n), F32),
        scratch_shapes=[pltpu.VMEM((tm, d), F32 if precise else BF16)],
        compiler_params=_params("parallel", "arbitrary"),
        name="proj_precise" if precise else "proj",
    )(x, nw.reshape(1, d), mod, w)


def _mlp_kernel(x_ref, nw_ref, mod_ref, wu_ref, wd_ref, fw_ref, o_ref, h_ref, acc_ref, *, final):
    f = pl.program_id(1)

    @pl.when(f == 0)
    def _():
        h = _norm_mod(x_ref[...], nw_ref[...], mod_ref[0, 3:4, :], mod_ref[0, 4:5, :])
        h_ref[...] = h.astype(BF16)
        acc_ref[...] = jnp.zeros_like(acc_ref)

    a = jnp.maximum(_dot(h_ref[...], wu_ref[...]), 0.0)
    acc_ref[...] += _dot((a * a).astype(BF16), wd_ref[...])

    @pl.when(f == pl.num_programs(1) - 1)
    def _():
        y = x_ref[...] + mod_ref[0, 5:6, :] * acc_ref[...]
        if final:
            ms = jnp.mean(y * y, axis=-1, keepdims=True)
            y = y * lax.rsqrt(ms + EPS) * fw_ref[...]
        o_ref[...] = y


def _mlp(x, nw, mod, w_up, w_down, final_w, tok, *, final, tm=512, tf=1024):
    m, d = x.shape
    ff = w_up.shape[1]
    tm = _token_tile(tok, tm)
    tf = _tile(ff, tf)
    return pl.pallas_call(
        functools.partial(_mlp_kernel, final=final),
        grid=(m // tm, ff // tf),
        in_specs=[
            pl.BlockSpec((tm, d), lambda i, f: (i, 0)),
            pl.BlockSpec((1, d), lambda i, f: (0, 0)),
            pl.BlockSpec((1, 6, d), lambda i, f: (_mod_row(i, tm, tok), 0, 0)),
            pl.BlockSpec((d, tf), lambda i, f: (0, f)),
            pl.BlockSpec((tf, d), lambda i, f: (f, 0)),
            pl.BlockSpec((1, d), lambda i, f: (0, 0)),
        ],
        out_specs=pl.BlockSpec((tm, d), lambda i, f: (i, 0)),
        out_shape=jax.ShapeDtypeStruct((m, d), F32),
        scratch_shapes=[pltpu.VMEM((tm, d), BF16), pltpu.VMEM((tm, d), F32)],
        compiler_params=_params("parallel", "arbitrary"),
        name="mlp_final" if final else "mlp",
    )(x, nw.reshape(1, d), mod, w_up, w_down, final_w.reshape(1, d))


def _out_kernel(of_ref, ob_ref, g_ref, nw_ref, x_ref, mod_ref, w_ref, o_ref, y_ref, *, n_heads, dh):
    @pl.when(pl.program_id(1) == 0)
    def _():
        for h in range(n_heads):
            sl = slice(h * dh, (h + 1) * dh)
            o = of_ref[:, sl] + ob_ref[:, sl]
            ms = jnp.mean(o * o, axis=-1, keepdims=True)
            g = g_ref[:, sl]
            y_ref[:, sl] = (o * lax.rsqrt(ms + EPS) * nw_ref[...] * (g * _sigmoid(g))).astype(BF16)

    o_ref[...] = x_ref[...] + mod_ref[0, 2:3, :] * _dot(y_ref[...], w_ref[...])


def _out_proj(o_f, o_b, gate_src, gate_blk, norm_w, x, mod, w_out, tok, *, tm=256, tn=1024):
    m, vd = o_f.shape
    d = x.shape[1]
    dh = norm_w.shape[0]
    tm = _token_tile(tok, tm)
    tn = _tile(d, tn)
    return pl.pallas_call(
        functools.partial(_out_kernel, n_heads=vd // dh, dh=dh),
        grid=(m // tm, d // tn),
        in_specs=[
            pl.BlockSpec((tm, vd), lambda i, j: (i, 0)),
            pl.BlockSpec((tm, vd), lambda i, j: (i, 0)),
            pl.BlockSpec((tm, vd), lambda i, j: (i, gate_blk)),
            pl.BlockSpec((1, dh), lambda i, j: (0, 0)),
            pl.BlockSpec((tm, tn), lambda i, j: (i, j)),
            pl.BlockSpec((1, 6, tn), lambda i, j: (_mod_row(i, tm, tok), 0, j)),
            pl.BlockSpec((vd, tn), lambda i, j: (0, j)),
        ],
        out_specs=pl.BlockSpec((tm, tn), lambda i, j: (i, j)),
        out_shape=jax.ShapeDtypeStruct((m, d), F32),
        scratch_shapes=[pltpu.VMEM((tm, vd), BF16)],
        compiler_params=_params("parallel", "arbitrary"),
        name="out_proj",
    )(o_f, o_b, gate_src, norm_w.reshape(1, dh), x, mod, w_out)


def _bwd_block(n, tok):
    nc, per = tok.n_ctx_blocks, tok.blocks_per_dec
    r = jnp.maximum(n - nc, 0)
    return jnp.where(n < nc, n, nc + (r // per) * per + (per - 1 - r % per))


def _dec_batch(n, tok):
    return jnp.maximum(n - tok.n_ctx_blocks, 0) // tok.blocks_per_dec


def _state_block(n, tok):
    return jnp.minimum(n, tok.n_ctx_blocks - 1)


def _seq_flags(n, tok):
    nc, per = tok.n_ctx_blocks, tok.blocks_per_dec
    is_ctx = n < nc
    r = jnp.maximum(n - nc, 0) % per
    return is_ctx, jnp.logical_and(jnp.logical_not(is_ctx), r == 0)


def _dn_prep_kernel(x_ref, w_ref, o_ref, *, n_ctx_blocks, n_q_blocks, dk, tt):
    i = pl.program_id(0)
    j = pl.program_id(1)
    x = x_ref[...]
    tc = x.shape[1]
    row = lax.broadcasted_iota(jnp.int32, (tt, 1), 0)
    is_ctx = i < n_ctx_blocks
    pos = jnp.where(is_ctx, row, row % GRID_W)
    seg = jnp.where(is_ctx, tt, GRID_W)
    acc = x * w_ref[CONV_K // 2:CONV_K // 2 + 1, :]
    for s in range(CONV_K):
        d = s - CONV_K // 2
        if d == 0:
            continue
        xs = pltpu.roll(x, (-d) % tt, 0)
        valid = (pos >= -d) if d < 0 else (pos < seg - d)
        acc = acc + jnp.where(valid, xs, 0.0) * w_ref[s:s + 1, :]
    y = acc * _sigmoid(acc)

    def normed(scale):
        for g in range(tc // dk):
            blk = y[:, g * dk:(g + 1) * dk]
            ss = jnp.sum(blk * blk, axis=-1, keepdims=True)
            o_ref[:, g * dk:(g + 1) * dk] = (blk * (lax.rsqrt(ss + EPS) * scale)).astype(o_ref.dtype)

    @pl.when(j < n_q_blocks)
    def _():
        normed(dk ** -0.5)

    @pl.when(jnp.logical_and(j >= n_q_blocks, j < 2 * n_q_blocks))
    def _():
        normed(1.0)

    @pl.when(j >= 2 * n_q_blocks)
    def _():
        o_ref[...] = y.astype(o_ref.dtype)


def _dn_prep(proj, conv_w, n_conv, kd, dk, tok):
    m = proj.shape[0]
    tt = tok.seq
    tc = _tile(kd, 512)
    return pl.pallas_call(
        functools.partial(_dn_prep_kernel, n_ctx_blocks=tok.n_ctx_blocks, n_q_blocks=kd // tc, dk=dk, tt=tt),
        grid=(m // tt, n_conv // tc),
        in_specs=[
            pl.BlockSpec((tt, tc), lambda i, j: (i, j)),
            pl.BlockSpec((CONV_K, tc), lambda i, j: (0, j)),
        ],
        out_specs=pl.BlockSpec((tt, tc), lambda i, j: (i, j)),
        out_shape=jax.ShapeDtypeStruct((m, n_conv), BF16),
        compiler_params=_params("parallel", "parallel"),
        name="dn_prep",
    )(proj, conv_w)


def _dn_gate_kernel(x_ref, a_ref, bias_ref, o_ref, gt_ref, *, hv, tt):
    x = x_ref[...]
    lanes = x.shape[1]
    lane = lax.broadcasted_iota(jnp.int32, (1, lanes), 1)
    is_fwd = lane < 3 * hv
    g = -a_ref[...] * _softplus(x + bias_ref[...])
    beta = pltpu.roll(_sigmoid(x), 2 * hv, 1)
    lower = _tri(CHUNK, upper=False)
    upper = _tri(CHUNK, upper=True)
    for c in range(tt // CHUNK):
        rows = slice(c * CHUNK, (c + 1) * CHUNK)
        gch = g[rows]
        gc = jnp.where(is_fwd, _split_dot(lower, gch), _split_dot(upper, gch))
        g_end = jnp.where(is_fwd, gc[CHUNK - 1:CHUNK, :], gc[0:1, :])
        eg = jnp.exp(gc)
        o_ref[0, rows, :] = beta[rows]
        o_ref[1, rows, :] = gc
        o_ref[2, rows, :] = eg
        o_ref[3, rows, :] = jnp.exp(g_end - gc)
        o_ref[4, rows, :] = beta[rows] * eg
    gt_ref[...] = o_ref[1].T


def _dn_gates(raw, a_log, dt_bias, tok):
    m, lanes = raw.shape
    hv = lanes // 4
    tt = tok.seq
    zeros = jnp.zeros((2 * hv,), F32)
    a_full = jnp.concatenate([zeros, jnp.exp(a_log).reshape(-1)]).reshape(1, lanes)
    bias_full = jnp.concatenate([zeros, dt_bias.reshape(-1)]).reshape(1, lanes)
    return pl.pallas_call(
        functools.partial(_dn_gate_kernel, hv=hv, tt=tt),
        grid=(m // tt,),
        in_specs=[
            pl.BlockSpec((tt, lanes), lambda i: (i, 0)),
            pl.BlockSpec((1, lanes), lambda i: (0, 0)),
            pl.BlockSpec((1, lanes), lambda i: (0, 0)),
        ],
        out_specs=[
            pl.BlockSpec((5, tt, lanes), lambda i: (0, i, 0)),
            pl.BlockSpec((lanes, tt), lambda i: (0, i)),
        ],
        out_shape=[jax.ShapeDtypeStruct((5, m, lanes), F32), jax.ShapeDtypeStruct((lanes, m), F32)],
        compiler_params=_params("parallel"),
        name="dn_gates",
    )(raw, a_full, bias_full)


def _dn_scan_kernel(qf_ref, kf_ref, vf_ref, gf_ref, gtf_ref, qb_ref, kb_ref, vb_ref, gb_ref, gtb_ref, s0_ref,
                    of_ref, ob_ref, ns_ref, s_ref, *, tok, hv, dv):
    hk = pl.program_id(0)
    n = pl.program_id(1)
    tt = tok.seq
    nch = tt // CHUNK
    is_ctx, dec_start = _seq_flags(n, tok)

    @pl.when(is_ctx)
    def _():
        s_ref[...] = jnp.zeros_like(s_ref)

    @pl.when(dec_start)
    def _():
        s_ref[...] = s0_ref[0, 0]

    lanes = gf_ref.shape[2]
    lane_iota = lax.broadcasted_iota(jnp.int32, (1, lanes), 1)
    ri = lax.broadcasted_iota(jnp.int32, (CHUNK, CHUNK), 0)
    ci = lax.broadcasted_iota(jnp.int32, (CHUNK, CHUNK), 1)

    for d, (q_ref, k_ref, v_ref, g_ref, gt_ref, o_ref) in enumerate(
            ((qf_ref, kf_ref, vf_ref, gf_ref, gtf_ref, of_ref), (qb_ref, kb_ref, vb_ref, gb_ref, gtb_ref, ob_ref))):
        incl = (ri >= ci) if d == 0 else (ri <= ci)
        strict = (ri > ci) if d == 0 else (ri < ci)
        end_row = CHUNK - 1 if d == 0 else 0
        q = q_ref[...]
        k = k_ref[...]
        kf32 = k.astype(F32)
        qf32 = q.astype(F32)
        kk = [_dot_nt(k[c * CHUNK:(c + 1) * CHUNK], k[c * CHUNK:(c + 1) * CHUNK]) for c in range(nch)]
        qk = [_dot_nt(q[c * CHUNK:(c + 1) * CHUNK], k[c * CHUNK:(c + 1) * CHUNK]) for c in range(nch)]
        for e in range(2):
            a_lane = 2 * hv + d * hv + 2 * hk + e
            sel = lane_iota == a_lane
            col = [jnp.sum(jnp.where(sel, g_ref[i], 0.0), axis=1, keepdims=True) for i in range(5)]
            beta_c, gc_c, eg_c, ekd_c, beg_c = col
            g_row = gt_ref[pl.ds(a_lane, 1), :]
            v = vf32 = v_ref[:, e * dv:(e + 1) * dv].astype(F32)
            pre = []
            for c in range(nch):
                rows = slice(c * CHUNK, (c + 1) * CHUNK)
                diff = gc_c[rows] - g_row[:, rows]
                decay = jnp.where(incl, jnp.exp(jnp.where(incl, diff, 0.0)), 0.0)
                p = jnp.where(strict, -(beta_c[rows] * kk[c] * decay), 0.0)
                r = jnp.concatenate([beta_c[rows] * v[rows], beg_c[rows] * kf32[rows]], axis=1)
                steps = CHUNK.bit_length() - 1
                for it in range(steps):
                    last = it + 1 == steps
                    both = _split_matmul(p, r if last else jnp.concatenate([r, p], axis=1))
                    r = r + both[:, :2 * dv]
                    if not last:
                        p = both[:, 2 * dv:]
                u0, w = r[:, :dv], r[:, dv:]
                attn = (qk[c] * decay).astype(BF16)
                qg = qf32[rows] * eg_c[rows]
                kd = (kf32[rows] * ekd_c[rows]).astype(BF16)
                wq = jnp.concatenate([w, qg], axis=0).astype(BF16)
                pre.append((u0, wq, attn, kd, eg_c[c * CHUNK + end_row:c * CHUNK + end_row + 1, :]))
            s = s_ref[d, e]
            for c in (range(nch) if d == 0 else range(nch - 1, -1, -1)):
                u0, wq, attn, kd, g_end = pre[c]
                ws = _dot(wq, s.astype(BF16))
                u = u0 - ws[:CHUNK]
                ub = u.astype(BF16)
                o_ref[c * CHUNK:(c + 1) * CHUNK, e * dv:(e + 1) * dv] = _dot(attn, ub) + ws[CHUNK:]
                s = g_end * s + _dot_tn(kd, ub)
            s_ref[d, e] = s

    @pl.when(is_ctx)
    def _():
        ns_ref[0] = s_ref[...]


def _dn_scan(qkv, gates, gates_t, s0, j, tok, *, hk_n, hv, dk, dv):
    m = qkv.shape[0]
    tt = tok.seq
    rep = hv // hk_n
    assert rep == 2
    lanes = gates.shape[2]
    kd = hk_n * dk
    fwd = lambda n: n
    bwd = lambda n: _bwd_block(n, tok)

    def stream(blk):
        return [
            pl.BlockSpec((tt, dk), lambda h, n: (blk(n), h)),
            pl.BlockSpec((tt, dk), lambda h, n: (blk(n), kd // dk + h)),
            pl.BlockSpec((tt, rep * dv), lambda h, n: (blk(n), (2 * kd) // (rep * dv) + h)),
            pl.BlockSpec((5, tt, lanes), lambda h, n: (0, blk(n), 0)),
            pl.BlockSpec((lanes, tt), lambda h, n: (0, blk(n))),
        ]

    n_batch = tok.n_ctx_blocks
    return pl.pallas_call(
        functools.partial(_dn_scan_kernel, tok=tok, hv=hv, dv=dv),
        grid=(hk_n, tok.n_blocks),
        in_specs=stream(fwd) + stream(bwd) + [
            pl.BlockSpec((1, 1, 2, rep, dk, dv), lambda h, n: (_dec_batch(n, tok), j, 0, h, 0, 0)),
        ],
        out_specs=[
            pl.BlockSpec((tt, rep * dv), lambda h, n: (n, h)),
            pl.BlockSpec((tt, rep * dv), lambda h, n: (bwd(n), h)),
            pl.BlockSpec((1, 2, rep, dk, dv), lambda h, n: (_state_block(n, tok), 0, h, 0, 0)),
        ],
        out_shape=[
            jax.ShapeDtypeStruct((m, hv * dv), F32),
            jax.ShapeDtypeStruct((m, hv * dv), F32),
            jax.ShapeDtypeStruct((n_batch, 2, hv, dk, dv), F32),
        ],
        scratch_shapes=[pltpu.VMEM((2, rep, dk, dv), F32)],
        compiler_params=_params("parallel", "arbitrary"),
        name="dn_scan",
    )(qkv, qkv, qkv, gates, gates_t, qkv, qkv, qkv, gates, gates_t, s0)


def _chunk_cums(la_f, la_b, cbf_ref, cbb_ref, tt):
    lower = _tri(CHUNK, upper=False)
    upper = _tri(CHUNK, upper=True)
    for c in range(tt // CHUNK):
        rows = slice(c * CHUNK, (c + 1) * CHUNK)
        cbf_ref[rows, :] = _split_dot(lower, la_f[rows])
        cbb_ref[rows, :] = _split_dot(upper, la_b[rows])


def _log_sigmoid(x):
    return jnp.minimum(x, 0.0) - jnp.log(1.0 + jnp.exp(-jnp.abs(x)))


def _gla_prep_kernel(a1_ref, w2_ref, b_ref, cbf_ref, cbb_ref, *, tt):
    a1 = a1_ref[...]
    la = [_log_sigmoid(_dot(a1, w2_ref[d]) + b_ref[d]) * (1.0 / GLA_TAU) for d in range(2)]
    _chunk_cums(la[0], la[1], cbf_ref, cbb_ref, tt)


def _gla_prep(a1, w_a2, b_a, tok):
    m, r2 = a1.shape
    kd = w_a2.shape[2]
    tt = tok.seq
    tc = _tile(kd, 512)
    zeros = jnp.zeros((GLA_RANK, kd), F32)
    w2 = jnp.stack([jnp.concatenate([w_a2[0], zeros]), jnp.concatenate([zeros, w_a2[1]])])
    out = jax.ShapeDtypeStruct((m, kd), F32)
    return pl.pallas_call(
        functools.partial(_gla_prep_kernel, tt=tt),
        grid=(m // tt, kd // tc),
        in_specs=[
            pl.BlockSpec((tt, r2), lambda i, j: (i, 0)),
            pl.BlockSpec((2, r2, tc), lambda i, j: (0, 0, j)),
            pl.BlockSpec((2, 1, tc), lambda i, j: (0, 0, j)),
        ],
        out_specs=[pl.BlockSpec((tt, tc), lambda i, j: (i, j))] * 2,
        out_shape=[out, out],
        compiler_params=_params("parallel", "parallel"),
        name="gla_prep",
    )(a1, w2, b_a.reshape(2, 1, kd))


def _hgrn_prep_kernel(ff_ref, fb_ref, lb_ref, kf_ref, kb_ref, cbf_ref, cbb_ref, *, layer, tt):
    logits = lb_ref[...]
    ex = jnp.exp(logits - jnp.max(logits, axis=0, keepdims=True))
    probs = ex / jnp.sum(ex, axis=0, keepdims=True)
    lb = jnp.sum(probs[1:layer + 1], axis=0, keepdims=True)
    la = []
    for f_ref, k_ref in ((ff_ref, kf_ref), (fb_ref, kb_ref)):
        fl = f_ref[...]
        sg = _sigmoid(fl)
        la.append(jnp.log(lb + (1.0 - lb) * sg))
        k_ref[...] = (1.0 - lb) * (1.0 - sg)
    _chunk_cums(la[0], la[1], cbf_ref, cbb_ref, tt)


def _hgrn_prep(proj, lb_logits, layer, d, tok):
    m = proj.shape[0]
    depth = lb_logits.shape[0]
    tt = tok.seq
    tc = _tile(d, 512)
    nb = d // tc
    out = jax.ShapeDtypeStruct((m, d), F32)
    return pl.pallas_call(
        functools.partial(_hgrn_prep_kernel, layer=layer, tt=tt),
        grid=(m // tt, nb),
        in_specs=[
            pl.BlockSpec((tt, tc), lambda i, j: (i, 3 * nb + j)),
            pl.BlockSpec((tt, tc), lambda i, j: (i, 4 * nb + j)),
            pl.BlockSpec((depth, tc), lambda i, j: (0, j)),
        ],
        out_specs=[pl.BlockSpec((tt, tc), lambda i, j: (i, j))] * 4,
        out_shape=[out] * 4,
        compiler_params=_params("parallel", "parallel"),
        name="hgrn_prep",
    )(proj, proj, lb_logits)


def _gla_chunk(q, k, v, cb, s, rev):
    c = q.shape[0]
    nb = c // SUB
    o_inter = _dot_nt((q * jnp.exp(cb)).astype(BF16), s.astype(BF16))
    outs = []
    for blk in range(nb):
        lo, hi = blk * SUB, (blk + 1) * SUB
        if not rev:
            ref = cb[lo - 1:lo, :] if blk > 0 else jnp.zeros_like(cb[0:1, :])
            cols = slice(0, hi)
        else:
            ref = cb[hi:hi + 1, :] if blk < nb - 1 else jnp.zeros_like(cb[0:1, :])
            cols = slice(lo, c)
        qt = (q[lo:hi] * jnp.exp(cb[lo:hi] - ref)).astype(BF16)
        kt = (k[cols] * jnp.exp(jnp.minimum(ref - cb[cols], EXP_CLAMP))).astype(BF16)
        a = _dot_nt(qt, kt)
        ri = lax.broadcasted_iota(jnp.int32, a.shape, 0) + lo
        ci = lax.broadcasted_iota(jnp.int32, a.shape, 1) + cols.start
        a = jnp.where((ri >= ci) if not rev else (ri <= ci), a, 0.0)
        outs.append(_dot(a.astype(BF16), v[cols]))
    o = o_inter + jnp.concatenate(outs, axis=0)
    end = cb[c - 1:c, :] if not rev else cb[0:1, :]
    kd = (k * jnp.exp(end - cb)).astype(BF16)
    s_new = jnp.exp(end) * s + _dot_tn(v, kd)
    return o, s_new


def _gla_scan_kernel(qf_ref, kf_ref, vf_ref, cf_ref, qb_ref, kb_ref, vb_ref, cb_ref, s0_ref,
                     of_ref, ob_ref, ns_ref, s_ref, *, tok, q_silu, scale):
    n = pl.program_id(1)
    tt = tok.seq
    nch = tt // CHUNK
    is_ctx, dec_start = _seq_flags(n, tok)

    @pl.when(is_ctx)
    def _():
        s_ref[...] = jnp.zeros_like(s_ref)

    @pl.when(dec_start)
    def _():
        for d in range(2):
            s_ref[d] = s0_ref[0, 0, d, 0].T

    for d, (q_ref, k_ref, v_ref, c_ref, o_ref) in enumerate(
            ((qf_ref, kf_ref, vf_ref, cf_ref, of_ref), (qb_ref, kb_ref, vb_ref, cb_ref, ob_ref))):
        q = q_ref[...]
        if q_silu:
            q = q * _sigmoid(q)
        q = q * scale
        k = k_ref[...]
        v = v_ref[...].astype(BF16)
        cb = c_ref[...]
        s = s_ref[d]
        for c in (range(nch) if d == 0 else range(nch - 1, -1, -1)):
            rows = slice(c * CHUNK, (c + 1) * CHUNK)
            o, s = _gla_chunk(q[rows], k[rows], v[rows], cb[rows], s, rev=(d == 1))
            o_ref[rows, :] = o
        s_ref[d] = s

    @pl.when(is_ctx)
    def _():
        for d in range(2):
            ns_ref[0, d, 0] = s_ref[d].T


def _gla_scan(q_src, q_blk, kf_src, kf_blk, kb_src, kb_blk, v_src, v_blk, cb_f, cb_b, s0, j, tok,
              *, heads, dk, dv, q_silu):
    m = q_src.shape[0]
    tt = tok.seq
    bwd = lambda n: _bwd_block(n, tok)
    fwd = lambda n: n

    def stream(blk, k_blk, c_blk_src):
        return [
            pl.BlockSpec((tt, dk), lambda h, n: (blk(n), q_blk + h)),
            pl.BlockSpec((tt, dk), lambda h, n: (blk(n), k_blk + h)),
            pl.BlockSpec((tt, dv), lambda h, n: (blk(n), v_blk + h)),
            pl.BlockSpec((tt, dk), lambda h, n: (blk(n), h)),
        ]

    n_batch = tok.n_ctx_blocks
    return pl.pallas_call(
        functools.partial(_gla_scan_kernel, tok=tok, q_silu=q_silu, scale=dk ** -0.5),
        grid=(heads, tok.n_blocks),
        in_specs=stream(fwd, kf_blk, cb_f) + stream(bwd, kb_blk, cb_b) + [
            pl.BlockSpec((1, 1, 2, 1, dk, dv), lambda h, n: (_dec_batch(n, tok), j, 0, h, 0, 0)),
        ],
        out_specs=[
            pl.BlockSpec((tt, dv), lambda h, n: (n, h)),
            pl.BlockSpec((tt, dv), lambda h, n: (bwd(n), h)),
            pl.BlockSpec((1, 2, 1, dk, dv), lambda h, n: (_state_block(n, tok), 0, h, 0, 0)),
        ],
        out_shape=[
            jax.ShapeDtypeStruct((m, heads * dv), F32),
            jax.ShapeDtypeStruct((m, heads * dv), F32),
            jax.ShapeDtypeStruct((n_batch, 2, heads, dk, dv), F32),
        ],
        scratch_shapes=[pltpu.VMEM((2, dv, dk), F32)],
        compiler_params=_params("parallel", "arbitrary"),
        name="gla_scan",
    )(q_src, kf_src, v_src, cb_f, q_src, kb_src, v_src, cb_b, s0)


def kernel(x_prompt, x_sample, c, state_deltanet, state_gla, state_hgrn, c_ctx, ada_w, ada_b, norm1, norm2, final_norm, w_up, w_down, dn_w_in, dn_conv, dn_A_log, dn_dt_bias, dn_norm, dn_w_out, gla_w_in, gla_w_a2, gla_b_a, gla_norm, gla_w_out, hgrn_w_in, hgrn_lb_logits, hgrn_norm, hgrn_w_out):
    batch, seq, d = x_prompt.shape
    dec_batch, dec_seq, _ = x_sample.shape
    depth = ada_w.shape[0]
    tok = Tokens(m_ctx=batch * seq, seq=seq, dec_batch=dec_batch, dec_seq=dec_seq)
    assert seq % CHUNK == 0 and dec_seq % seq == 0 and seq % GRID_W == 0

    dn_hv, dn_dk, dn_dv = state_deltanet.shape[3:]
    dn_vd = dn_hv * dn_dv
    n_conv = dn_conv.shape[2]
    dn_kd = (n_conv - dn_vd) // 2
    dn_hk = dn_kd // dn_dk
    gla_h, gla_dk, gla_dv = state_gla.shape[3:]
    gla_kd, gla_vd = gla_h * gla_dk, gla_h * gla_dv
    hg_h, hg_dk, hg_dv = state_hgrn.shape[3:]

    x = jnp.concatenate([x_prompt.reshape(tok.m_ctx, d), x_sample.reshape(dec_batch * dec_seq, d)], axis=0)
    n_rows = 1 + dec_batch
    pad = (-n_rows) % 8
    cond = jnp.concatenate([c_ctx[None, :], c, jnp.zeros((pad, d), F32)], axis=0)
    mod_all = _mod_table(cond, ada_w, ada_b).reshape(depth, n_rows + pad, 6, d)

    new_states = ([], [], [])
    for i in range(depth):
        kind, j = i % N_MIXERS, i // N_MIXERS
        mod = mod_all[i]
        if kind == 0:
            w_in = dn_w_in[j]
            proj = _proj(x, norm1[i], mod, w_in[:, :n_conv + dn_vd].astype(BF16), tok)
            raw = _proj(x, norm1[i], mod, w_in[:, n_conv + dn_vd:], tok, precise=True)
            qkv = _dn_prep(proj, dn_conv[j], n_conv, dn_kd, dn_dk, tok)
            gates, gates_t = _dn_gates(raw, dn_A_log[j], dn_dt_bias[j], tok)
            o_f, o_b, ns = _dn_scan(qkv, gates, gates_t, state_deltanet, j, tok,
                                    hk_n=dn_hk, hv=dn_hv, dk=dn_dk, dv=dn_dv)
            x = _out_proj(o_f, o_b, proj, n_conv // dn_vd, dn_norm[j], x, mod, dn_w_out[j].astype(BF16), tok)
        elif kind == 1:
            w_in = gla_w_in[j]
            n_main = 2 * gla_kd + 2 * gla_vd
            proj = _proj(x, norm1[i], mod, w_in[:, :n_main].astype(BF16), tok)
            a1 = _proj(x, norm1[i], mod, w_in[:, n_main:], tok, precise=True)
            cb_f, cb_b = _gla_prep(a1, gla_w_a2[j], gla_b_a[j], tok)
            o_f, o_b, ns = _gla_scan(proj, 0, proj, gla_h, proj, gla_h, proj, (2 * gla_kd) // gla_dv,
                                     cb_f, cb_b, state_gla, j, tok,
                                     heads=gla_h, dk=gla_dk, dv=gla_dv, q_silu=False)
            x = _out_proj(o_f, o_b, proj, (2 * gla_kd + gla_vd) // gla_vd, gla_norm[j], x, mod,
                          gla_w_out[j].astype(BF16), tok)
        else:
            proj = _proj(x, norm1[i], mod, hgrn_w_in[j].astype(BF16), tok)
            k_f, k_b, cb_f, cb_b = _hgrn_prep(proj, hgrn_lb_logits, i, d, tok)
            o_f, o_b, ns = _gla_scan(proj, 0, k_f, 0, k_b, 0, proj, d // hg_dv, cb_f, cb_b, state_hgrn, j, tok,
                                     heads=hg_h, dk=hg_dk, dv=hg_dv, q_silu=True)
            x = _out_proj(o_f, o_b, proj, 2, hgrn_norm[j], x, mod, hgrn_w_out[j].astype(BF16), tok)
        new_states[kind].append(ns)
        x = _mlp(x, norm2[i], mod, w_up[i].astype(BF16), w_down[i].astype(BF16), final_norm, tok,
                 final=(i == depth - 1))

    y_prompt = x[:tok.m_ctx].reshape(batch, seq, d)
    y_sample = x[tok.m_ctx:].reshape(dec_batch, dec_seq, d)
    return (y_prompt, y_sample) + tuple(jnp.stack(s, axis=1) for s in new_states)
```

```python
import functools
from typing import NamedTuple

import jax
import jax.numpy as jnp
from jax import lax
from jax.experimental import pallas as pl
from jax.experimental.pallas import tpu as pltpu

F32 = jnp.float32
BF16 = jnp.bfloat16

EPS = 1e-6
GRID_W = 64
CONV_K = 5
GLA_RANK = 16
GLA_TAU = 16.0
N_MIXERS = 3
CHUNK = 64
SUB = 16
EXP_CLAMP = 80.0
V7X_VMEM_LIMIT_BYTES = 56 * 1024 * 1024
HIGHEST = lax.Precision.HIGHEST


class Tokens(NamedTuple):
    m_ctx: int
    seq: int
    dec_batch: int
    dec_seq: int

    @property
    def m(self):
        return self.m_ctx + self.dec_batch * self.dec_seq

    @property
    def n_ctx_blocks(self):
        return self.m_ctx // self.seq

    @property
    def blocks_per_dec(self):
        return self.dec_seq // self.seq

    @property
    def n_blocks(self):
        return self.m // self.seq


def _tile(n, pref):
    t = min(n, pref)
    while n % t:
        t //= 2
    return t


def _token_tile(tok, pref):
    t = pref
    while tok.m_ctx % t or tok.dec_seq % t:
        t //= 2
    return t


def _params(*sem):
    return pltpu.CompilerParams(dimension_semantics=sem, vmem_limit_bytes=V7X_VMEM_LIMIT_BYTES)


def _mod_row(i, tm, tok):
    start = i * tm
    return jnp.where(start < tok.m_ctx, 0, 1 + (start - tok.m_ctx) // tok.dec_seq)


def _sigmoid(x):
    return 1.0 / (1.0 + jnp.exp(-x))


def _softplus(x):
    return jnp.maximum(x, 0.0) + jnp.log(1.0 + jnp.exp(-jnp.abs(x)))


def _norm_mod(x, nw, shift, scale):
    ms = jnp.mean(x * x, axis=-1, keepdims=True)
    return x * lax.rsqrt(ms + EPS) * nw * (1.0 + scale) + shift


def _dot(a, b):
    return jnp.dot(a, b, preferred_element_type=F32)


def _dot_nt(a, b):
    return lax.dot_general(a, b, (((1,), (1,)), ((), ())), preferred_element_type=F32)


def _dot_tn(a, b):
    return lax.dot_general(a, b, (((0,), (0,)), ((), ())), preferred_element_type=F32)


def _split_dot(tri, x):
    hi = x.astype(BF16)
    lo = (x - hi.astype(F32)).astype(BF16)
    return _dot(tri, hi) + _dot(tri, lo)


def _split_matmul(a, b):
    m = a.shape[0]
    ah = a.astype(BF16)
    al = (a - ah.astype(F32)).astype(BF16)
    bh = b.astype(BF16)
    bl = (b - bh.astype(F32)).astype(BF16)
    t = _dot(jnp.concatenate([ah, al], axis=0), bh)
    return t[:m] + t[m:] + _dot(ah, bl)


def _tri(n, upper):
    r = lax.broadcasted_iota(jnp.int32, (n, n), 0)
    c = lax.broadcasted_iota(jnp.int32, (n, n), 1)
    return jnp.where((r <= c) if upper else (r >= c), 1.0, 0.0).astype(BF16)


def _mod_kernel(c_ref, w_ref, b_ref, o_ref):
    c = c_ref[...]
    s = c * _sigmoid(c)
    o_ref[0] = jnp.dot(s, w_ref[0], precision=HIGHEST, preferred_element_type=F32) + b_ref[0]


def _mod_table(cond, ada_w, ada_b):
    depth, d, n = ada_w.shape
    rows = cond.shape[0]
    tn = _tile(n, 1024)
    return pl.pallas_call(
        _mod_kernel,
        grid=(depth, n // tn),
        in_specs=[
            pl.BlockSpec((rows, d), lambda l, j: (0, 0)),
            pl.BlockSpec((1, d, tn), lambda l, j: (l, 0, j)),
            pl.BlockSpec((1, 1, tn), lambda l, j: (l, 0, j)),
        ],
        out_specs=pl.BlockSpec((1, rows, tn), lambda l, j: (l, 0, j)),
        out_shape=jax.ShapeDtypeStruct((depth, rows, n), F32),
        compiler_params=_params("parallel", "parallel"),
        name="mod_table",
    )(cond, ada_w, ada_b.reshape(depth, 1, n))


def _proj_kernel(x_ref, nw_ref, mod_ref, w_ref, o_ref, h_ref, *, precise):
    @pl.when(pl.program_id(1) == 0)
    def _():
        h = _norm_mod(x_ref[...], nw_ref[...], mod_ref[0, 0:1, :], mod_ref[0, 1:2, :])
        h_ref[...] = h.astype(h_ref.dtype)

    if precise:
        o_ref[...] = jnp.dot(h_ref[...], w_ref[...], precision=HIGHEST, preferred_element_type=F32)
    else:
        o_ref[...] = _dot(h_ref[...], w_ref[...]).astype(o_ref.dtype)


def _proj(x, nw, mod, w, tok, *, precise=False, tm=512, tn=1024):
    m, d = x.shape
    n = w.shape[1]
    tm = _token_tile(tok, tm)
    tn = _tile(n, tn)
    return pl.pallas_call(
        functools.partial(_proj_kernel, precise=precise),
        grid=(m // tm, n // tn),
        in_specs=[
            pl.BlockSpec((tm, d), lambda i, j: (i, 0)),
            pl.BlockSpec((1, d), lambda i, j: (0, 0)),
            pl.BlockSpec((1, 6, d), lambda i, j: (_mod_row(i, tm, tok), 0, 0)),
            pl.BlockSpec((d, tn), lambda i, j: (0, j)),
        ],
        out_specs=pl.BlockSpec((tm, tn), lambda i, j: (i, j)),
        out_shape=jax.ShapeDtypeStruct((m, n), F32),
        scratch_shapes=[pltpu.VMEM((tm, d), F32 if precise else BF16)],
        compiler_params=_params("parallel", "arbitrary"),
        name="proj_precise" if precise else "proj",
    )(x, nw.reshape(1, d), mod, w)


def _mlp_kernel(x_ref, nw_ref, mod_ref, wu_ref, wd_ref, fw_ref, o_ref, h_ref, acc_ref, *, final):
    f = pl.program_id(1)

    @pl.when(f == 0)
    def _():
        h = _norm_mod(x_ref[...], nw_ref[...], mod_ref[0, 3:4, :], mod_ref[0, 4:5, :])
        h_ref[...] = h.astype(BF16)
        acc_ref[...] = jnp.zeros_like(acc_ref)

    a = jnp.maximum(_dot(h_ref[...], wu_ref[...]), 0.0)
    acc_ref[...] += _dot((a * a).astype(BF16), wd_ref[...])

    @pl.when(f == pl.num_programs(1) - 1)
    def _():
        y = x_ref[...] + mod_ref[0, 5:6, :] * acc_ref[...]
        if final:
            ms = jnp.mean(y * y, axis=-1, keepdims=True)
            y = y * lax.rsqrt(ms + EPS) * fw_ref[...]
        o_ref[...] = y


def _mlp(x, nw, mod, w_up, w_down, final_w, tok, *, final, tm=512, tf=1024):
    m, d = x.shape
    ff = w_up.shape[1]
    tm = _token_tile(tok, tm)
    tf = _tile(ff, tf)
    return pl.pallas_call(
        functools.partial(_mlp_kernel, final=final),
        grid=(m // tm, ff // tf),
        in_specs=[
            pl.BlockSpec((tm, d), lambda i, f: (i, 0)),
            pl.BlockSpec((1, d), lambda i, f: (0, 0)),
            pl.BlockSpec((1, 6, d), lambda i, f: (_mod_row(i, tm, tok), 0, 0)),
            pl.BlockSpec((d, tf), lambda i, f: (0, f)),
            pl.BlockSpec((tf, d), lambda i, f: (f, 0)),
            pl.BlockSpec((1, d), lambda i, f: (0, 0)),
        ],
        out_specs=pl.BlockSpec((tm, d), lambda i, f: (i, 0)),
        out_shape=jax.ShapeDtypeStruct((m, d), F32),
        scratch_shapes=[pltpu.VMEM((tm, d), BF16), pltpu.VMEM((tm, d), F32)],
        compiler_params=_params("parallel", "arbitrary"),
        name="mlp_final" if final else "mlp",
    )(x, nw.reshape(1, d), mod, w_up, w_down, final_w.reshape(1, d))


def _out_kernel(of_ref, ob_ref, g_ref, nw_ref, x_ref, mod_ref, w_ref, o_ref, y_ref, *, n_heads, dh):
    @pl.when(pl.program_id(1) == 0)
    def _():
        for h in range(n_heads):
            sl = slice(h * dh, (h + 1) * dh)
            o = of_ref[:, sl] + ob_ref[:, sl]
            ms = jnp.mean(o * o, axis=-1, keepdims=True)
            g = g_ref[:, sl]
            y_ref[:, sl] = (o * lax.rsqrt(ms + EPS) * nw_ref[...] * (g * _sigmoid(g))).astype(BF16)

    o_ref[...] = x_ref[...] + mod_ref[0, 2:3, :] * _dot(y_ref[...], w_ref[...])


def _out_proj(o_f, o_b, gate_src, gate_blk, norm_w, x, mod, w_out, tok, *, tm=256, tn=1024):
    m, vd = o_f.shape
    d = x.shape[1]
    dh = norm_w.shape[0]
    tm = _token_tile(tok, tm)
    tn = _tile(d, tn)
    return pl.pallas_call(
        functools.partial(_out_kernel, n_heads=vd // dh, dh=dh),
        grid=(m // tm, d // tn),
        in_specs=[
            pl.BlockSpec((tm, vd), lambda i, j: (i, 0)),
            pl.BlockSpec((tm, vd), lambda i, j: (i, 0)),
            pl.BlockSpec((tm, vd), lambda i, j: (i, gate_blk)),
            pl.BlockSpec((1, dh), lambda i, j: (0, 0)),
            pl.BlockSpec((tm, tn), lambda i, j: (i, j)),
            pl.BlockSpec((1, 6, tn), lambda i, j: (_mod_row(i, tm, tok), 0, j)),
            pl.BlockSpec((vd, tn), lambda i, j: (0, j)),
        ],
        out_specs=pl.BlockSpec((tm, tn), lambda i, j: (i, j)),
        out_shape=jax.ShapeDtypeStruct((m, d), F32),
        scratch_shapes=[pltpu.VMEM((tm, vd), BF16)],
        compiler_params=_params("parallel", "arbitrary"),
        name="out_proj",
    )(o_f, o_b, gate_src, norm_w.reshape(1, dh), x, mod, w_out)


def _bwd_block(n, tok):
    nc, per = tok.n_ctx_blocks, tok.blocks_per_dec
    r = jnp.maximum(n - nc, 0)
    return jnp.where(n < nc, n, nc + (r // per) * per + (per - 1 - r % per))


def _dec_batch(n, tok):
    return jnp.maximum(n - tok.n_ctx_blocks, 0) // tok.blocks_per_dec


def _state_block(n, tok):
    return jnp.minimum(n, tok.n_ctx_blocks - 1)


def _seq_flags(n, tok):
    nc, per = tok.n_ctx_blocks, tok.blocks_per_dec
    is_ctx = n < nc
    r = jnp.maximum(n - nc, 0) % per
    return is_ctx, jnp.logical_and(jnp.logical_not(is_ctx), r == 0)


def _dn_prep_kernel(x_ref, w_ref, o_ref, *, n_ctx_blocks, n_q_blocks, dk, tt):
    i = pl.program_id(0)
    j = pl.program_id(1)
    x = x_ref[...]
    tc = x.shape[1]
    row = lax.broadcasted_iota(jnp.int32, (tt, 1), 0)
    is_ctx = i < n_ctx_blocks
    pos = jnp.where(is_ctx, row, row % GRID_W)
    seg = jnp.where(is_ctx, tt, GRID_W)
    acc = x * w_ref[CONV_K // 2:CONV_K // 2 + 1, :]
    for s in range(CONV_K):
        d = s - CONV_K // 2
        if d == 0:
            continue
        xs = pltpu.roll(x, (-d) % tt, 0)
        valid = (pos >= -d) if d < 0 else (pos < seg - d)
        acc = acc + jnp.where(valid, xs, 0.0) * w_ref[s:s + 1, :]
    y = acc * _sigmoid(acc)

    def normed(scale):
        for g in range(tc // dk):
            blk = y[:, g * dk:(g + 1) * dk]
            ss = jnp.sum(blk * blk, axis=-1, keepdims=True)
            o_ref[:, g * dk:(g + 1) * dk] = (blk * (lax.rsqrt(ss + EPS) * scale)).astype(o_ref.dtype)

    @pl.when(j < n_q_blocks)
    def _():
        normed(dk ** -0.5)

    @pl.when(jnp.logical_and(j >= n_q_blocks, j < 2 * n_q_blocks))
    def _():
        normed(1.0)

    @pl.when(j >= 2 * n_q_blocks)
    def _():
        o_ref[...] = y.astype(o_ref.dtype)


def _dn_prep(proj, conv_w, n_conv, kd, dk, tok):
    m = proj.shape[0]
    tt = tok.seq
    tc = _tile(kd, 512)
    return pl.pallas_call(
        functools.partial(_dn_prep_kernel, n_ctx_blocks=tok.n_ctx_blocks, n_q_blocks=kd // tc, dk=dk, tt=tt),
        grid=(m // tt, n_conv // tc),
        in_specs=[
            pl.BlockSpec((tt, tc), lambda i, j: (i, j)),
            pl.BlockSpec((CONV_K, tc), lambda i, j: (0, j)),
        ],
        out_specs=pl.BlockSpec((tt, tc), lambda i, j: (i, j)),
        out_shape=jax.ShapeDtypeStruct((m, n_conv), BF16),
        compiler_params=_params("parallel", "parallel"),
        name="dn_prep",
    )(proj, conv_w)


def _dn_gate_kernel(x_ref, a_ref, bias_ref, o_ref, gt_ref, *, hv, tt):
    x = x_ref[...]
    lanes = x.shape[1]
    lane = lax.broadcasted_iota(jnp.int32, (1, lanes), 1)
    is_fwd = lane < 3 * hv
    g = -a_ref[...] * _softplus(x + bias_ref[...])
    beta = pltpu.roll(_sigmoid(x), 2 * hv, 1)
    lower = _tri(CHUNK, upper=False)
    upper = _tri(CHUNK, upper=True)
    for c in range(tt // CHUNK):
        rows = slice(c * CHUNK, (c + 1) * CHUNK)
        gch = g[rows]
        gc = jnp.where(is_fwd, _split_dot(lower, gch), _split_dot(upper, gch))
        g_end = jnp.where(is_fwd, gc[CHUNK - 1:CHUNK, :], gc[0:1, :])
        eg = jnp.exp(gc)
        o_ref[0, rows, :] = beta[rows]
        o_ref[1, rows, :] = gc
        o_ref[2, rows, :] = eg
        o_ref[3, rows, :] = jnp.exp(g_end - gc)
        o_ref[4, rows, :] = beta[rows] * eg
    gt_ref[...] = o_ref[1].T


def _dn_gates(raw, a_log, dt_bias, tok):
    m, lanes = raw.shape
    hv = lanes // 4
    tt = tok.seq
    zeros = jnp.zeros((2 * hv,), F32)
    a_full = jnp.concatenate([zeros, jnp.exp(a_log).reshape(-1)]).reshape(1, lanes)
    bias_full = jnp.concatenate([zeros, dt_bias.reshape(-1)]).reshape(1, lanes)
    return pl.pallas_call(
        functools.partial(_dn_gate_kernel, hv=hv, tt=tt),
        grid=(m // tt,),
        in_specs=[
            pl.BlockSpec((tt, lanes), lambda i: (i, 0)),
            pl.BlockSpec((1, lanes), lambda i: (0, 0)),
            pl.BlockSpec((1, lanes), lambda i: (0, 0)),
        ],
        out_specs=[
            pl.BlockSpec((5, tt, lanes), lambda i: (0, i, 0)),
            pl.BlockSpec((lanes, tt), lambda i: (0, i)),
        ],
        out_shape=[jax.ShapeDtypeStruct((5, m, lanes), F32), jax.ShapeDtypeStruct((lanes, m), F32)],
        compiler_params=_params("parallel"),
        name="dn_gates",
    )(raw, a_full, bias_full)


def _dn_scan_kernel(qf_ref, kf_ref, vf_ref, gf_ref, gtf_ref, qb_ref, kb_ref, vb_ref, gb_ref, gtb_ref, s0_ref,
                    of_ref, ob_ref, ns_ref, s_ref, *, tok, hv, dv):
    hk = pl.program_id(0)
    n = pl.program_id(1)
    tt = tok.seq
    nch = tt // CHUNK
    is_ctx, dec_start = _seq_flags(n, tok)

    @pl.when(is_ctx)
    def _():
        s_ref[...] = jnp.zeros_like(s_ref)

    @pl.when(dec_start)
    def _():
        s_ref[...] = s0_ref[0, 0]

    lanes = gf_ref.shape[2]
    lane_iota = lax.broadcasted_iota(jnp.int32, (1, lanes), 1)
    ri = lax.broadcasted_iota(jnp.int32, (CHUNK, CHUNK), 0)
    ci = lax.broadcasted_iota(jnp.int32, (CHUNK, CHUNK), 1)

    dirs = ((qf_ref, kf_ref, vf_ref, gf_ref, gtf_ref, of_ref), (qb_ref, kb_ref, vb_ref, gb_ref, gtb_ref, ob_ref))
    chains = {}
    for d, (q_ref, k_ref, v_ref, g_ref, gt_ref, o_ref) in enumerate(dirs):
        incl = (ri >= ci) if d == 0 else (ri <= ci)
        strict = (ri > ci) if d == 0 else (ri < ci)
        end_row = CHUNK - 1 if d == 0 else 0
        q = q_ref[...]
        k = k_ref[...]
        kf32 = k.astype(F32)
        qf32 = q.astype(F32)
        kk = [_dot_nt(k[c * CHUNK:(c + 1) * CHUNK], k[c * CHUNK:(c + 1) * CHUNK]) for c in range(nch)]
        qk = [_dot_nt(q[c * CHUNK:(c + 1) * CHUNK], k[c * CHUNK:(c + 1) * CHUNK]) for c in range(nch)]
        for e in range(2):
            a_lane = 2 * hv + d * hv + 2 * hk + e
            sel = lane_iota == a_lane
            col = [jnp.sum(jnp.where(sel, g_ref[i], 0.0), axis=1, keepdims=True) for i in range(5)]
            beta_c, gc_c, eg_c, ekd_c, beg_c = col
            g_row = gt_ref[pl.ds(a_lane, 1), :]
            v = v_ref[:, e * dv:(e + 1) * dv].astype(F32)
            for c in range(nch):
                rows = slice(c * CHUNK, (c + 1) * CHUNK)
                diff = gc_c[rows] - g_row[:, rows]
                decay = jnp.where(incl, jnp.exp(jnp.where(incl, diff, 0.0)), 0.0)
                chains[d, e, c] = dict(
                    p=jnp.where(strict, -(beta_c[rows] * kk[c] * decay), 0.0),
                    r=jnp.concatenate([beta_c[rows] * v[rows], beg_c[rows] * kf32[rows]], axis=1),
                    attn=(qk[c] * decay).astype(BF16),
                    qg=qf32[rows] * eg_c[rows],
                    kd=(kf32[rows] * ekd_c[rows]).astype(BF16),
                    g_end=eg_c[c * CHUNK + end_row:c * CHUNK + end_row + 1, :])

    steps = CHUNK.bit_length() - 1
    for it in range(steps):
        last = it + 1 == steps
        for ch in chains.values():
            p, r = ch["p"], ch["r"]
            both = _split_matmul(p, r if last else jnp.concatenate([r, p], axis=1))
            ch["r"] = r + both[:, :2 * dv]
            if not last:
                ch["p"] = both[:, 2 * dv:]
    for ch in chains.values():
        r = ch["r"]
        ch["u0"] = r[:, :dv]
        ch["wq"] = jnp.concatenate([r[:, dv:], ch["qg"]], axis=0).astype(BF16)

    state = {(d, e): s_ref[d, e] for d in range(2) for e in range(2)}
    for step in range(nch):
        cur = {(d, e): chains[d, e, step if d == 0 else nch - 1 - step] for (d, e) in state}
        ws = {key: _dot(cur[key]["wq"], s.astype(BF16)) for key, s in state.items()}
        ub = {key: (cur[key]["u0"] - ws[key][:CHUNK]).astype(BF16) for key in state}
        for (d, e), s in state.items():
            c = step if d == 0 else nch - 1 - step
            ch = cur[d, e]
            dirs[d][5][c * CHUNK:(c + 1) * CHUNK, e * dv:(e + 1) * dv] = (
                _dot(ch["attn"], ub[d, e]) + ws[d, e][CHUNK:])
            state[d, e] = ch["g_end"] * s + _dot_tn(ch["kd"], ub[d, e])
    for (d, e), s in state.items():
        s_ref[d, e] = s

    @pl.when(is_ctx)
    def _():
        ns_ref[0] = s_ref[...]


def _dn_scan(qkv, gates, gates_t, s0, j, tok, *, hk_n, hv, dk, dv):
    m = qkv.shape[0]
    tt = tok.seq
    rep = hv // hk_n
    assert rep == 2
    lanes = gates.shape[2]
    kd = hk_n * dk
    fwd = lambda n: n
    bwd = lambda n: _bwd_block(n, tok)

    def stream(blk):
        return [
            pl.BlockSpec((tt, dk), lambda h, n: (blk(n), h)),
            pl.BlockSpec((tt, dk), lambda h, n: (blk(n), kd // dk + h)),
            pl.BlockSpec((tt, rep * dv), lambda h, n: (blk(n), (2 * kd) // (rep * dv) + h)),
            pl.BlockSpec((5, tt, lanes), lambda h, n: (0, blk(n), 0)),
            pl.BlockSpec((lanes, tt), lambda h, n: (0, blk(n))),
        ]

    n_batch = tok.n_ctx_blocks
    return pl.pallas_call(
        functools.partial(_dn_scan_kernel, tok=tok, hv=hv, dv=dv),
        grid=(hk_n, tok.n_blocks),
        in_specs=stream(fwd) + stream(bwd) + [
            pl.BlockSpec((1, 1, 2, rep, dk, dv), lambda h, n: (_dec_batch(n, tok), j, 0, h, 0, 0)),
        ],
        out_specs=[
            pl.BlockSpec((tt, rep * dv), lambda h, n: (n, h)),
            pl.BlockSpec((tt, rep * dv), lambda h, n: (bwd(n), h)),
            pl.BlockSpec((1, 2, rep, dk, dv), lambda h, n: (_state_block(n, tok), 0, h, 0, 0)),
        ],
        out_shape=[
            jax.ShapeDtypeStruct((m, hv * dv), F32),
            jax.ShapeDtypeStruct((m, hv * dv), F32),
            jax.ShapeDtypeStruct((n_batch, 2, hv, dk, dv), F32),
        ],
        scratch_shapes=[pltpu.VMEM((2, rep, dk, dv), F32)],
        compiler_params=_params("parallel", "arbitrary"),
        name="dn_scan",
    )(qkv, qkv, qkv, gates, gates_t, qkv, qkv, qkv, gates, gates_t, s0)


def _chunk_cums(la_f, la_b, cbf_ref, cbb_ref, tt):
    lower = _tri(CHUNK, upper=False)
    upper = _tri(CHUNK, upper=True)
    for c in range(tt // CHUNK):
        rows = slice(c * CHUNK, (c + 1) * CHUNK)
        cbf_ref[rows, :] = _split_dot(lower, la_f[rows])
        cbb_ref[rows, :] = _split_dot(upper, la_b[rows])


def _log_sigmoid(x):
    return jnp.minimum(x, 0.0) - jnp.log(1.0 + jnp.exp(-jnp.abs(x)))


def _gla_prep_kernel(a1_ref, w2_ref, b_ref, cbf_ref, cbb_ref, *, tt):
    a1 = a1_ref[...]
    la = [_log_sigmoid(_dot(a1, w2_ref[d]) + b_ref[d]) * (1.0 / GLA_TAU) for d in range(2)]
    _chunk_cums(la[0], la[1], cbf_ref, cbb_ref, tt)


def _gla_prep(a1, w_a2, b_a, tok):
    m, r2 = a1.shape
    kd = w_a2.shape[2]
    tt = tok.seq
    tc = _tile(kd, 512)
    zeros = jnp.zeros((GLA_RANK, kd), F32)
    w2 = jnp.stack([jnp.concatenate([w_a2[0], zeros]), jnp.concatenate([zeros, w_a2[1]])])
    out = jax.ShapeDtypeStruct((m, kd), F32)
    return pl.pallas_call(
        functools.partial(_gla_prep_kernel, tt=tt),
        grid=(m // tt, kd // tc),
        in_specs=[
            pl.BlockSpec((tt, r2), lambda i, j: (i, 0)),
            pl.BlockSpec((2, r2, tc), lambda i, j: (0, 0, j)),
            pl.BlockSpec((2, 1, tc), lambda i, j: (0, 0, j)),
        ],
        out_specs=[pl.BlockSpec((tt, tc), lambda i, j: (i, j))] * 2,
        out_shape=[out, out],
        compiler_params=_params("parallel", "parallel"),
        name="gla_prep",
    )(a1, w2, b_a.reshape(2, 1, kd))


def _hgrn_prep_kernel(ff_ref, fb_ref, lb_ref, kf_ref, kb_ref, cbf_ref, cbb_ref, *, layer, tt):
    logits = lb_ref[...]
    ex = jnp.exp(logits - jnp.max(logits, axis=0, keepdims=True))
    probs = ex / jnp.sum(ex, axis=0, keepdims=True)
    lb = jnp.sum(probs[1:layer + 1], axis=0, keepdims=True)
    la = []
    for f_ref, k_ref in ((ff_ref, kf_ref), (fb_ref, kb_ref)):
        fl = f_ref[...]
        sg = _sigmoid(fl)
        la.append(jnp.log(lb + (1.0 - lb) * sg))
        k_ref[...] = (1.0 - lb) * (1.0 - sg)
    _chunk_cums(la[0], la[1], cbf_ref, cbb_ref, tt)


def _hgrn_prep(proj, lb_logits, layer, d, tok):
    m = proj.shape[0]
    depth = lb_logits.shape[0]
    tt = tok.seq
    tc = _tile(d, 512)
    nb = d // tc
    out = jax.ShapeDtypeStruct((m, d), F32)
    return pl.pallas_call(
        functools.partial(_hgrn_prep_kernel, layer=layer, tt=tt),
        grid=(m // tt, nb),
        in_specs=[
            pl.BlockSpec((tt, tc), lambda i, j: (i, 3 * nb + j)),
            pl.BlockSpec((tt, tc), lambda i, j: (i, 4 * nb + j)),
            pl.BlockSpec((depth, tc), lambda i, j: (0, j)),
        ],
        out_specs=[pl.BlockSpec((tt, tc), lambda i, j: (i, j))] * 4,
        out_shape=[out] * 4,
        compiler_params=_params("parallel", "parallel"),
        name="hgrn_prep",
    )(proj, proj, lb_logits)


def _gla_offdiag_pairs(c):
    pairs = []
    size = c
    while size > SUB:
        half = size // 2
        for start in range(0, c, size):
            pairs.append(((start + half, start + size), (start, start + half)))
        size = half
    return pairs


def _gla_operands(q, k, cb, rev):
    c, dk = q.shape
    zero_rows = lambda n: jnp.zeros((n, dk), BF16)

    def padded(x, rows):
        parts = ([zero_rows(rows[0])] if rows[0] else []) + [x] + ([zero_rows(c - rows[1])] if rows[1] < c else [])
        return jnp.concatenate(parts, axis=0) if len(parts) > 1 else x

    q_parts, k_parts = [], []
    for qr, kr in _gla_offdiag_pairs(c):
        if rev:
            qr, kr = kr, qr
        ref_row = qr[0] - 1 if not rev else qr[1]
        ref = cb[ref_row:ref_row + 1, :]
        qs, ks = slice(*qr), slice(*kr)
        q_parts.append(padded((q[qs] * jnp.exp(cb[qs] - ref)).astype(BF16), qr))
        k_parts.append(padded((k[ks] * jnp.exp(ref - cb[ks])).astype(BF16), kr))
    refs = []
    for blk in range(c // SUB):
        lo, hi = blk * SUB, (blk + 1) * SUB
        if not rev:
            ref = cb[lo - 1:lo, :] if blk > 0 else jnp.zeros_like(cb[0:1, :])
        else:
            ref = cb[hi:hi + 1, :] if hi < c else jnp.zeros_like(cb[0:1, :])
        refs.append(jnp.broadcast_to(ref, (SUB, dk)))
    ref_d = jnp.concatenate(refs, axis=0)
    q_diag = (q * jnp.exp(cb - ref_d)).astype(BF16)
    k_diag = (k * jnp.exp(jnp.minimum(ref_d - cb, EXP_CLAMP))).astype(BF16)
    return jnp.concatenate(q_parts, axis=1), jnp.concatenate(k_parts, axis=1), q_diag, k_diag


def _gla_diag_mask(c, rev):
    ri = lax.broadcasted_iota(jnp.int32, (c, c), 0)
    ci = lax.broadcasted_iota(jnp.int32, (c, c), 1)
    same_block = (ri // SUB) == (ci // SUB)
    return jnp.logical_and(same_block, (ri <= ci) if rev else (ri >= ci))


def _gla_scan_kernel(qf_ref, kf_ref, vf_ref, cf_ref, qb_ref, kb_ref, vb_ref, cb_ref, s0_ref,
                     of_ref, ob_ref, ns_ref, s_ref, *, tok, q_silu, scale, hb, dk, dv):
    n = pl.program_id(1)
    tt = tok.seq
    nch = tt // CHUNK
    is_ctx, dec_start = _seq_flags(n, tok)
    streams = [(d, h) for d in range(2) for h in range(hb)]

    @pl.when(is_ctx)
    def _():
        s_ref[...] = jnp.zeros_like(s_ref)

    @pl.when(dec_start)
    def _():
        for d, h in streams:
            s_ref[d, h] = s0_ref[0, 0, d, h].T

    dirs = ((qf_ref, kf_ref, vf_ref, cf_ref, of_ref), (qb_ref, kb_ref, vb_ref, cb_ref, ob_ref))
    ops = {}
    for d, (q_ref, k_ref, v_ref, c_ref, _) in enumerate(dirs):
        rev = d == 1
        for h in range(hb):
            q = q_ref[:, h * dk:(h + 1) * dk]
            if q_silu:
                q = q * _sigmoid(q)
            q = q * scale
            k = k_ref[:, h * dk:(h + 1) * dk]
            v = v_ref[:, h * dv:(h + 1) * dv].astype(BF16)
            cb = c_ref[:, h * dk:(h + 1) * dk]
            for c in range(nch):
                rows = slice(c * CHUNK, (c + 1) * CHUNK)
                qc, kc, cbc = q[rows], k[rows], cb[rows]
                end = cbc[0:1, :] if rev else cbc[CHUNK - 1:CHUNK, :]
                ops[d, h, c] = dict(
                    intra=_gla_operands(qc, kc, cbc, rev), v=v[rows],
                    q_state=(qc * jnp.exp(cbc)).astype(BF16),
                    k_state=(kc * jnp.exp(end - cbc)).astype(BF16),
                    s_decay=jnp.exp(end))
    a_off = {key: _dot_nt(op["intra"][0], op["intra"][1]) for key, op in ops.items()}
    a_diag = {key: _dot_nt(op["intra"][2], op["intra"][3]) for key, op in ops.items()}
    masks = [_gla_diag_mask(CHUNK, rev) for rev in (False, True)]
    attn = {key: (a_off[key] + jnp.where(masks[key[0]], a_diag[key], 0.0)).astype(BF16) for key in ops}
    o_intra = {key: _dot(attn[key], op["v"]) for key, op in ops.items()}
    s_inc = {key: _dot_tn(op["v"], op["k_state"]) for key, op in ops.items()}

    state = {key: s_ref[key] for key in streams}
    for step in range(nch):
        for (d, h), s in state.items():
            c = step if d == 0 else nch - 1 - step
            dirs[d][4][c * CHUNK:(c + 1) * CHUNK, h * dv:(h + 1) * dv] = (
                o_intra[d, h, c] + _dot_nt(ops[d, h, c]["q_state"], s.astype(BF16)))
            state[d, h] = ops[d, h, c]["s_decay"] * s + s_inc[d, h, c]
    for key, s in state.items():
        s_ref[key] = s

    @pl.when(is_ctx)
    def _():
        for d, h in streams:
            ns_ref[0, d, h] = s_ref[d, h].T


def _gla_scan(q_src, q_blk, kf_src, kf_blk, kb_src, kb_blk, v_src, v_blk, cb_f, cb_b, s0, j, tok,
              *, heads, dk, dv, q_silu, hb):
    m = q_src.shape[0]
    tt = tok.seq
    assert heads % hb == 0 and q_blk % hb == 0 and kf_blk % hb == 0 and kb_blk % hb == 0 and v_blk % hb == 0
    bwd = lambda n: _bwd_block(n, tok)
    fwd = lambda n: n

    def stream(blk, k_blk):
        return [
            pl.BlockSpec((tt, hb * dk), lambda h, n: (blk(n), q_blk // hb + h)),
            pl.BlockSpec((tt, hb * dk), lambda h, n: (blk(n), k_blk // hb + h)),
            pl.BlockSpec((tt, hb * dv), lambda h, n: (blk(n), v_blk // hb + h)),
            pl.BlockSpec((tt, hb * dk), lambda h, n: (blk(n), h)),
        ]

    n_batch = tok.n_ctx_blocks
    return pl.pallas_call(
        functools.partial(_gla_scan_kernel, tok=tok, q_silu=q_silu, scale=dk ** -0.5, hb=hb, dk=dk, dv=dv),
        grid=(heads // hb, tok.n_blocks),
        in_specs=stream(fwd, kf_blk) + stream(bwd, kb_blk) + [
            pl.BlockSpec((1, 1, 2, hb, dk, dv), lambda h, n: (_dec_batch(n, tok), j, 0, h, 0, 0)),
        ],
        out_specs=[
            pl.BlockSpec((tt, hb * dv), lambda h, n: (n, h)),
            pl.BlockSpec((tt, hb * dv), lambda h, n: (bwd(n), h)),
            pl.BlockSpec((1, 2, hb, dk, dv), lambda h, n: (_state_block(n, tok), 0, h, 0, 0)),
        ],
        out_shape=[
            jax.ShapeDtypeStruct((m, heads * dv), F32),
            jax.ShapeDtypeStruct((m, heads * dv), F32),
            jax.ShapeDtypeStruct((n_batch, 2, heads, dk, dv), F32),
        ],
        scratch_shapes=[pltpu.VMEM((2, hb, dv, dk), F32)],
        compiler_params=_params("parallel", "arbitrary"),
        name="gla_scan",
    )(q_src, kf_src, v_src, cb_f, q_src, kb_src, v_src, cb_b, s0)


def kernel(x_prompt, x_sample, c, state_deltanet, state_gla, state_hgrn, c_ctx, ada_w, ada_b, norm1, norm2, final_norm, w_up, w_down, dn_w_in, dn_conv, dn_A_log, dn_dt_bias, dn_norm, dn_w_out, gla_w_in, gla_w_a2, gla_b_a, gla_norm, gla_w_out, hgrn_w_in, hgrn_lb_logits, hgrn_norm, hgrn_w_out):
    batch, seq, d = x_prompt.shape
    dec_batch, dec_seq, _ = x_sample.shape
    depth = ada_w.shape[0]
    tok = Tokens(m_ctx=batch * seq, seq=seq, dec_batch=dec_batch, dec_seq=dec_seq)
    assert seq % CHUNK == 0 and dec_seq % seq == 0 and seq % GRID_W == 0

    dn_hv, dn_dk, dn_dv = state_deltanet.shape[3:]
    dn_vd = dn_hv * dn_dv
    n_conv = dn_conv.shape[2]
    dn_kd = (n_conv - dn_vd) // 2
    dn_hk = dn_kd // dn_dk
    gla_h, gla_dk, gla_dv = state_gla.shape[3:]
    gla_kd, gla_vd = gla_h * gla_dk, gla_h * gla_dv
    hg_h, hg_dk, hg_dv = state_hgrn.shape[3:]

    x = jnp.concatenate([x_prompt.reshape(tok.m_ctx, d), x_sample.reshape(dec_batch * dec_seq, d)], axis=0)
    n_rows = 1 + dec_batch
    pad = (-n_rows) % 8
    cond = jnp.concatenate([c_ctx[None, :], c, jnp.zeros((pad, d), F32)], axis=0)
    mod_all = _mod_table(cond, ada_w, ada_b).reshape(depth, n_rows + pad, 6, d)

    new_states = ([], [], [])
    for i in range(depth):
        kind, j = i % N_MIXERS, i // N_MIXERS
        mod = mod_all[i]
        if kind == 0:
            w_in = dn_w_in[j]
            proj = _proj(x, norm1[i], mod, w_in[:, :n_conv + dn_vd].astype(BF16), tok)
            raw = _proj(x, norm1[i], mod, w_in[:, n_conv + dn_vd:], tok, precise=True)
            qkv = _dn_prep(proj, dn_conv[j], n_conv, dn_kd, dn_dk, tok)
            gates, gates_t = _dn_gates(raw, dn_A_log[j], dn_dt_bias[j], tok)
            o_f, o_b, ns = _dn_scan(qkv, gates, gates_t, state_deltanet, j, tok,
                                    hk_n=dn_hk, hv=dn_hv, dk=dn_dk, dv=dn_dv)
            x = _out_proj(o_f, o_b, proj, n_conv // dn_vd, dn_norm[j], x, mod, dn_w_out[j].astype(BF16), tok)
        elif kind == 1:
            w_in = gla_w_in[j]
            n_main = 2 * gla_kd + 2 * gla_vd
            proj = _proj(x, norm1[i], mod, w_in[:, :n_main].astype(BF16), tok)
            a1 = _proj(x, norm1[i], mod, w_in[:, n_main:], tok, precise=True)
            cb_f, cb_b = _gla_prep(a1, gla_w_a2[j], gla_b_a[j], tok)
            o_f, o_b, ns = _gla_scan(proj, 0, proj, gla_h, proj, gla_h, proj, (2 * gla_kd) // gla_dv,
                                     cb_f, cb_b, state_gla, j, tok,
                                     heads=gla_h, dk=gla_dk, dv=gla_dv, q_silu=False, hb=1)
            x = _out_proj(o_f, o_b, proj, (2 * gla_kd + gla_vd) // gla_vd, gla_norm[j], x, mod,
                          gla_w_out[j].astype(BF16), tok)
        else:
            proj = _proj(x, norm1[i], mod, hgrn_w_in[j].astype(BF16), tok)
            k_f, k_b, cb_f, cb_b = _hgrn_prep(proj, hgrn_lb_logits, i, d, tok)
            o_f, o_b, ns = _gla_scan(proj, 0, k_f, 0, k_b, 0, proj, d // hg_dv, cb_f, cb_b, state_hgrn, j, tok,
                                     heads=hg_h, dk=hg_dk, dv=hg_dv, q_silu=True, hb=min(hg_h, 4))
            x = _out_proj(o_f, o_b, proj, 2, hgrn_norm[j], x, mod, hgrn_w_out[j].astype(BF16), tok)
        new_states[kind].append(ns)
        x = _mlp(x, norm2[i], mod, w_up[i].astype(BF16), w_down[i].astype(BF16), final_norm, tok,
                 final=(i == depth - 1))

    y_prompt = x[:tok.m_ctx].reshape(batch, seq, d)
    y_sample = x[tok.m_ctx:].reshape(dec_batch, dec_seq, d)
    return (y_prompt, y_sample) + tuple(jnp.stack(s, axis=1) for s in new_states)
```

```python
import functools
from typing import NamedTuple

import jax
import jax.numpy as jnp
from jax import lax
from jax.experimental import pallas as pl
from jax.experimental.pallas import tpu as pltpu

F32 = jnp.float32
BF16 = jnp.bfloat16

EPS = 1e-6
GRID_W = 64
CONV_K = 5
GLA_RANK = 16
GLA_TAU = 16.0
N_MIXERS = 3
CHUNK = 64
SUB = 16
EXP_CLAMP = 80.0
V7X_VMEM_LIMIT_BYTES = 56 * 1024 * 1024
HIGHEST = lax.Precision.HIGHEST


class Tokens(NamedTuple):
    m_ctx: int
    seq: int
    dec_batch: int
    dec_seq: int

    @property
    def m(self):
        return self.m_ctx + self.dec_batch * self.dec_seq

    @property
    def n_ctx_blocks(self):
        return self.m_ctx // self.seq

    @property
    def blocks_per_dec(self):
        return self.dec_seq // self.seq

    @property
    def n_blocks(self):
        return self.m // self.seq


def _tile(n, pref):
    t = min(n, pref)
    while n % t:
        t //= 2
    return t


def _token_tile(tok, pref):
    t = pref
    while tok.m_ctx % t or tok.dec_seq % t:
        t //= 2
    return t


def _params(*sem):
    return pltpu.CompilerParams(dimension_semantics=sem, vmem_limit_bytes=V7X_VMEM_LIMIT_BYTES)


def _mod_row(i, tm, tok):
    start = i * tm
    return jnp.where(start < tok.m_ctx, 0, 1 + (start - tok.m_ctx) // tok.dec_seq)


def _sigmoid(x):
    return 1.0 / (1.0 + jnp.exp(-x))


def _softplus(x):
    return jnp.maximum(x, 0.0) + jnp.log(1.0 + jnp.exp(-jnp.abs(x)))


def _norm_mod(x, nw, shift, scale):
    ms = jnp.mean(x * x, axis=-1, keepdims=True)
    return x * lax.rsqrt(ms + EPS) * nw * (1.0 + scale) + shift


def _dot(a, b):
    return jnp.dot(a, b, preferred_element_type=F32)


def _dot_nt(a, b):
    return lax.dot_general(a, b, (((1,), (1,)), ((), ())), preferred_element_type=F32)


def _dot_tn(a, b):
    return lax.dot_general(a, b, (((0,), (0,)), ((), ())), preferred_element_type=F32)


def _split_dot(tri, x):
    hi = x.astype(BF16)
    lo = (x - hi.astype(F32)).astype(BF16)
    return _dot(tri, hi) + _dot(tri, lo)


def _split_matmul(a, b):
    m = a.shape[0]
    ah = a.astype(BF16)
    al = (a - ah.astype(F32)).astype(BF16)
    bh = b.astype(BF16)
    bl = (b - bh.astype(F32)).astype(BF16)
    t = _dot(jnp.concatenate([ah, al], axis=0), bh)
    return t[:m] + t[m:] + _dot(ah, bl)


def _tri(n, upper):
    r = lax.broadcasted_iota(jnp.int32, (n, n), 0)
    c = lax.broadcasted_iota(jnp.int32, (n, n), 1)
    return jnp.where((r <= c) if upper else (r >= c), 1.0, 0.0).astype(BF16)


def _mod_kernel(c_ref, w_ref, b_ref, o_ref):
    c = c_ref[...]
    s = c * _sigmoid(c)
    o_ref[0] = jnp.dot(s, w_ref[0], precision=HIGHEST, preferred_element_type=F32) + b_ref[0]


def _mod_table(cond, ada_w, ada_b):
    depth, d, n = ada_w.shape
    rows = cond.shape[0]
    tn = _tile(n, 1024)
    return pl.pallas_call(
        _mod_kernel,
        grid=(depth, n // tn),
        in_specs=[
            pl.BlockSpec((rows, d), lambda l, j: (0, 0)),
            pl.BlockSpec((1, d, tn), lambda l, j: (l, 0, j)),
            pl.BlockSpec((1, 1, tn), lambda l, j: (l, 0, j)),
        ],
        out_specs=pl.BlockSpec((1, rows, tn), lambda l, j: (l, 0, j)),
        out_shape=jax.ShapeDtypeStruct((depth, rows, n), F32),
        compiler_params=_params("parallel", "parallel"),
        name="mod_table",
    )(cond, ada_w, ada_b.reshape(depth, 1, n))


def _proj_kernel(x_ref, nw_ref, mod_ref, w_ref, o_ref, h_ref, *, precise):
    @pl.when(pl.program_id(1) == 0)
    def _():
        h = _norm_mod(x_ref[...], nw_ref[...], mod_ref[0, 0:1, :], mod_ref[0, 1:2, :])
        h_ref[...] = h.astype(h_ref.dtype)

    if precise:
        o_ref[...] = jnp.dot(h_ref[...], w_ref[...], precision=HIGHEST, preferred_element_type=F32)
    else:
        o_ref[...] = _dot(h_ref[...], w_ref[...]).astype(o_ref.dtype)


def _proj(x, nw, mod, w, tok, *, precise=False, tm=1024, tn=1024):
    m, d = x.shape
    n = w.shape[1]
    tm = _token_tile(tok, tm)
    tn = _tile(n, tn)
    return pl.pallas_call(
        functools.partial(_proj_kernel, precise=precise),
        grid=(m // tm, n // tn),
        in_specs=[
            pl.BlockSpec((tm, d), lambda i, j: (i, 0)),
            pl.BlockSpec((1, d), lambda i, j: (0, 0)),
            pl.BlockSpec((1, 6, d), lambda i, j: (_mod_row(i, tm, tok), 0, 0)),
            pl.BlockSpec((d, tn), lambda i, j: (0, j)),
        ],
        out_specs=pl.BlockSpec((tm, tn), lambda i, j: (i, j)),
        out_shape=jax.ShapeDtypeStruct((m, n), F32 if precise else BF16),
        scratch_shapes=[pltpu.VMEM((tm, d), F32 if precise else BF16)],
        compiler_params=_params("parallel", "arbitrary"),
        name="proj_precise" if precise else "proj",
    )(x, nw.reshape(1, d), mod, w)


def _mlp_kernel(x_ref, nw_ref, mod_ref, wu_ref, wd_ref, fw_ref, o_ref, h_ref, acc_ref, *, final):
    f = pl.program_id(1)

    @pl.when(f == 0)
    def _():
        h = _norm_mod(x_ref[...], nw_ref[...], mod_ref[0, 3:4, :], mod_ref[0, 4:5, :])
        h_ref[...] = h.astype(BF16)
        acc_ref[...] = jnp.zeros_like(acc_ref)

    a = jnp.maximum(_dot(h_ref[...], wu_ref[...]), 0.0)
    acc_ref[...] += _dot((a * a).astype(BF16), wd_ref[...])

    @pl.when(f == pl.num_programs(1) - 1)
    def _():
        y = x_ref[...] + mod_ref[0, 5:6, :] * acc_ref[...]
        if final:
            ms = jnp.mean(y * y, axis=-1, keepdims=True)
            y = y * lax.rsqrt(ms + EPS) * fw_ref[...]
        o_ref[...] = y


def _mlp(x, nw, mod, w_up, w_down, final_w, tok, *, final, tm=512, tf=1024):
    m, d = x.shape
    ff = w_up.shape[1]
    tm = _token_tile(tok, tm)
    tf = _tile(ff, tf)
    return pl.pallas_call(
        functools.partial(_mlp_kernel, final=final),
        grid=(m // tm, ff // tf),
        in_specs=[
            pl.BlockSpec((tm, d), lambda i, f: (i, 0)),
            pl.BlockSpec((1, d), lambda i, f: (0, 0)),
            pl.BlockSpec((1, 6, d), lambda i, f: (_mod_row(i, tm, tok), 0, 0)),
            pl.BlockSpec((d, tf), lambda i, f: (0, f)),
            pl.BlockSpec((tf, d), lambda i, f: (f, 0)),
            pl.BlockSpec((1, d), lambda i, f: (0, 0)),
        ],
        out_specs=pl.BlockSpec((tm, d), lambda i, f: (i, 0)),
        out_shape=jax.ShapeDtypeStruct((m, d), F32),
        scratch_shapes=[pltpu.VMEM((tm, d), BF16), pltpu.VMEM((tm, d), F32)],
        compiler_params=_params("parallel", "arbitrary"),
        name="mlp_final" if final else "mlp",
    )(x, nw.reshape(1, d), mod, w_up, w_down, final_w.reshape(1, d))


def _out_kernel(of_ref, ob_ref, g_ref, nw_ref, x_ref, mod_ref, w_ref, o_ref, y_ref, *, n_heads, dh):
    @pl.when(pl.program_id(1) == 0)
    def _():
        for h in range(n_heads):
            sl = slice(h * dh, (h + 1) * dh)
            o = of_ref[:, sl].astype(F32) + ob_ref[:, sl].astype(F32)
            ms = jnp.mean(o * o, axis=-1, keepdims=True)
            g = g_ref[:, sl].astype(F32)
            y_ref[:, sl] = (o * lax.rsqrt(ms + EPS) * nw_ref[...] * (g * _sigmoid(g))).astype(BF16)

    o_ref[...] = x_ref[...] + mod_ref[0, 2:3, :] * _dot(y_ref[...], w_ref[...])


def _out_proj(o_f, o_b, gate_src, gate_blk, norm_w, x, mod, w_out, tok, *, tm=512, tn=512):
    m, vd = o_f.shape
    d = x.shape[1]
    dh = norm_w.shape[0]
    tm = _token_tile(tok, tm)
    tn = _tile(d, tn)
    return pl.pallas_call(
        functools.partial(_out_kernel, n_heads=vd // dh, dh=dh),
        grid=(m // tm, d // tn),
        in_specs=[
            pl.BlockSpec((tm, vd), lambda i, j: (i, 0)),
            pl.BlockSpec((tm, vd), lambda i, j: (i, 0)),
            pl.BlockSpec((tm, vd), lambda i, j: (i, gate_blk)),
            pl.BlockSpec((1, dh), lambda i, j: (0, 0)),
            pl.BlockSpec((tm, tn), lambda i, j: (i, j)),
            pl.BlockSpec((1, 6, tn), lambda i, j: (_mod_row(i, tm, tok), 0, j)),
            pl.BlockSpec((vd, tn), lambda i, j: (0, j)),
        ],
        out_specs=pl.BlockSpec((tm, tn), lambda i, j: (i, j)),
        out_shape=jax.ShapeDtypeStruct((m, d), F32),
        scratch_shapes=[pltpu.VMEM((tm, vd), BF16)],
        compiler_params=_params("parallel", "arbitrary"),
        name="out_proj",
    )(o_f, o_b, gate_src, norm_w.reshape(1, dh), x, mod, w_out)


def _bwd_block(n, tok):
    nc, per = tok.n_ctx_blocks, tok.blocks_per_dec
    r = jnp.maximum(n - nc, 0)
    return jnp.where(n < nc, n, nc + (r // per) * per + (per - 1 - r % per))


def _dec_batch(n, tok):
    return jnp.maximum(n - tok.n_ctx_blocks, 0) // tok.blocks_per_dec


def _state_block(n, tok):
    return jnp.minimum(n, tok.n_ctx_blocks - 1)


def _seq_flags(n, tok):
    nc, per = tok.n_ctx_blocks, tok.blocks_per_dec
    is_ctx = n < nc
    r = jnp.maximum(n - nc, 0) % per
    return is_ctx, jnp.logical_and(jnp.logical_not(is_ctx), r == 0)


def _dn_prep_kernel(x_ref, shift_ref, w_ref, o_ref, *, n_q_blocks, dk):
    j = pl.program_id(1)
    x = x_ref[...]
    tc = x.shape[1]
    acc = None
    for s in range(CONV_K):
        xs = x.astype(F32) if s == CONV_K // 2 else _dot(shift_ref[0, s], x)
        term = xs * w_ref[s:s + 1, :]
        acc = term if acc is None else acc + term
    y = acc * _sigmoid(acc)

    def normed(scale):
        for g in range(tc // dk):
            blk = y[:, g * dk:(g + 1) * dk]
            ss = jnp.sum(blk * blk, axis=-1, keepdims=True)
            o_ref[:, g * dk:(g + 1) * dk] = (blk * (lax.rsqrt(ss + EPS) * scale)).astype(o_ref.dtype)

    @pl.when(j < n_q_blocks)
    def _():
        normed(dk ** -0.5)

    @pl.when(jnp.logical_and(j >= n_q_blocks, j < 2 * n_q_blocks))
    def _():
        normed(1.0)

    @pl.when(j >= 2 * n_q_blocks)
    def _():
        o_ref[...] = y.astype(o_ref.dtype)


def _conv_shift_matrices(tt):
    t = jnp.arange(tt)[:, None]
    u = jnp.arange(tt)[None, :]
    mats = []
    for seg in (tt, GRID_W):
        same = (t // seg) == (u // seg)
        mats.append(jnp.stack([jnp.logical_and(u == t + s - CONV_K // 2, same) for s in range(CONV_K)]))
    return jnp.stack(mats).astype(BF16)


def _dn_prep(proj, conv_w, n_conv, kd, dk, tok):
    m = proj.shape[0]
    tt = tok.seq
    tc = _tile(kd, 512)
    n_ctx = tok.n_ctx_blocks
    return pl.pallas_call(
        functools.partial(_dn_prep_kernel, n_q_blocks=kd // tc, dk=dk),
        grid=(m // tt, n_conv // tc),
        in_specs=[
            pl.BlockSpec((tt, tc), lambda i, j: (i, j)),
            pl.BlockSpec((1, CONV_K, tt, tt), lambda i, j: (jnp.where(i < n_ctx, 0, 1), 0, 0, 0)),
            pl.BlockSpec((CONV_K, tc), lambda i, j: (0, j)),
        ],
        out_specs=pl.BlockSpec((tt, tc), lambda i, j: (i, j)),
        out_shape=jax.ShapeDtypeStruct((m, n_conv), BF16),
        compiler_params=_params("parallel", "parallel"),
        name="dn_prep",
    )(proj, _conv_shift_matrices(tt), conv_w)


def _dn_gate_kernel(x_ref, a_ref, bias_ref, o_ref, gt_ref, *, hv, tt):
    x = x_ref[...]
    lanes = x.shape[1]
    lane = lax.broadcasted_iota(jnp.int32, (1, lanes), 1)
    is_fwd = lane < 3 * hv
    g = -a_ref[...] * _softplus(x + bias_ref[...])
    beta = pltpu.roll(_sigmoid(x), 2 * hv, 1)
    lower = _tri(CHUNK, upper=False)
    upper = _tri(CHUNK, upper=True)
    for c in range(tt // CHUNK):
        rows = slice(c * CHUNK, (c + 1) * CHUNK)
        gch = g[rows]
        gc = jnp.where(is_fwd, _split_dot(lower, gch), _split_dot(upper, gch))
        g_end = jnp.where(is_fwd, gc[CHUNK - 1:CHUNK, :], gc[0:1, :])
        eg = jnp.exp(gc)
        o_ref[0, rows, :] = beta[rows]
        o_ref[1, rows, :] = gc
        o_ref[2, rows, :] = eg
        o_ref[3, rows, :] = jnp.exp(g_end - gc)
        o_ref[4, rows, :] = beta[rows] * eg
    gt_ref[...] = o_ref[1].T


def _dn_gates(raw, a_log, dt_bias, tok):
    m, lanes = raw.shape
    hv = lanes // 4
    tt = tok.seq
    zeros = jnp.zeros((2 * hv,), F32)
    a_full = jnp.concatenate([zeros, jnp.exp(a_log).reshape(-1)]).reshape(1, lanes)
    bias_full = jnp.concatenate([zeros, dt_bias.reshape(-1)]).reshape(1, lanes)
    return pl.pallas_call(
        functools.partial(_dn_gate_kernel, hv=hv, tt=tt),
        grid=(m // tt,),
        in_specs=[
            pl.BlockSpec((tt, lanes), lambda i: (i, 0)),
            pl.BlockSpec((1, lanes), lambda i: (0, 0)),
            pl.BlockSpec((1, lanes), lambda i: (0, 0)),
        ],
        out_specs=[
            pl.BlockSpec((5, tt, lanes), lambda i: (0, i, 0)),
            pl.BlockSpec((lanes, tt), lambda i: (0, i)),
        ],
        out_shape=[jax.ShapeDtypeStruct((5, m, lanes), F32), jax.ShapeDtypeStruct((lanes, m), F32)],
        compiler_params=_params("parallel"),
        name="dn_gates",
    )(raw, a_full, bias_full)


def _dn_scan_kernel(qf_ref, kf_ref, vf_ref, gf_ref, gtf_ref, qb_ref, kb_ref, vb_ref, gb_ref, gtb_ref, s0_ref,
                    of_ref, ob_ref, ns_ref, s_ref, *, tok, hv, dv):
    hk = pl.program_id(0)
    n = pl.program_id(1)
    tt = tok.seq
    nch = tt // CHUNK
    is_ctx, dec_start = _seq_flags(n, tok)

    @pl.when(is_ctx)
    def _():
        s_ref[...] = jnp.zeros_like(s_ref)

    @pl.when(dec_start)
    def _():
        s_ref[...] = s0_ref[0, 0]

    lanes = gf_ref.shape[2]
    lane_iota = lax.broadcasted_iota(jnp.int32, (1, lanes), 1)
    ri = lax.broadcasted_iota(jnp.int32, (CHUNK, CHUNK), 0)
    ci = lax.broadcasted_iota(jnp.int32, (CHUNK, CHUNK), 1)

    dirs = ((qf_ref, kf_ref, vf_ref, gf_ref, gtf_ref, of_ref), (qb_ref, kb_ref, vb_ref, gb_ref, gtb_ref, ob_ref))
    chains = {}
    for d, (q_ref, k_ref, v_ref, g_ref, gt_ref, o_ref) in enumerate(dirs):
        incl = (ri >= ci) if d == 0 else (ri <= ci)
        strict = (ri > ci) if d == 0 else (ri < ci)
        end_row = CHUNK - 1 if d == 0 else 0
        q = q_ref[...]
        k = k_ref[...]
        kf32 = k.astype(F32)
        qf32 = q.astype(F32)
        kk = [_dot_nt(k[c * CHUNK:(c + 1) * CHUNK], k[c * CHUNK:(c + 1) * CHUNK]) for c in range(nch)]
        qk = [_dot_nt(q[c * CHUNK:(c + 1) * CHUNK], k[c * CHUNK:(c + 1) * CHUNK]) for c in range(nch)]
        for e in range(2):
            a_lane = 2 * hv + d * hv + 2 * hk + e
            sel = lane_iota == a_lane
            col = [jnp.sum(jnp.where(sel, g_ref[i], 0.0), axis=1, keepdims=True) for i in range(5)]
            beta_c, gc_c, eg_c, ekd_c, beg_c = col
            g_row = gt_ref[pl.ds(a_lane, 1), :]
            v = v_ref[:, e * dv:(e + 1) * dv].astype(F32)
            for c in range(nch):
                rows = slice(c * CHUNK, (c + 1) * CHUNK)
                diff = gc_c[rows] - g_row[:, rows]
                decay = jnp.where(incl, jnp.exp(jnp.where(incl, diff, 0.0)), 0.0)
                chains[d, e, c] = dict(
                    p=jnp.where(strict, -(beta_c[rows] * kk[c] * decay), 0.0),
                    r=jnp.concatenate([beta_c[rows] * v[rows], beg_c[rows] * kf32[rows]], axis=1),
                    attn=(qk[c] * decay).astype(BF16),
                    qg=qf32[rows] * eg_c[rows],
                    kd=(kf32[rows] * ekd_c[rows]).astype(BF16),
                    g_end=eg_c[c * CHUNK + end_row:c * CHUNK + end_row + 1, :])

    same = [(ri // b) == (ci // b) for b in (SUB, 2 * SUB)]
    assert CHUNK == 4 * SUB
    bf = lambda x: x.astype(BF16)
    for (d, _, _), ch in chains.items():
        p = ch.pop("p")
        ch["pd"] = jnp.where(same[0], p, 0.0)
        ch["p32"] = jnp.where(jnp.logical_and(same[1], jnp.logical_not(same[0])), p, 0.0)
        ch["p64"] = jnp.where(same[1], 0.0, p)
    for ch in chains.values():
        ch["q"] = _dot(bf(ch["pd"]), bf(ch["pd"]))
        ch["xm"] = ch["pd"]
    for it in range(2):
        for ch in chains.values():
            q, xm = ch["q"], ch["xm"]
            both = _dot(bf(q), bf(jnp.concatenate([xm, q], axis=1)))
            ch["xm"] = xm + q + both[:, :CHUNK]
            ch["q"] = both[:, CHUNK:]
    for ch in chains.values():
        q, xm = ch.pop("q"), ch["xm"]
        ch["xm"] = xm + q + _dot(bf(q), bf(xm))
    for level in ("p32", "p64"):
        for ch in chains.values():
            ch["y"] = ch[level] + _dot(bf(ch[level]), bf(ch["xm"]))
        for ch in chains.values():
            xm, y = ch["xm"], ch.pop("y")
            ch["xm"] = xm + y + _dot(bf(xm), bf(y))
    for ch in chains.values():
        r = ch["r"]
        sol = r + _dot(bf(ch["xm"]), bf(r))
        ch["u0"] = sol[:, :dv]
        ch["wq"] = jnp.concatenate([sol[:, dv:], ch["qg"]], axis=0).astype(BF16)

    state = {(d, e): s_ref[d, e] for d in range(2) for e in range(2)}
    for step in range(nch):
        cur = {(d, e): chains[d, e, step if d == 0 else nch - 1 - step] for (d, e) in state}
        ws = {key: _dot(cur[key]["wq"], s.astype(BF16)) for key, s in state.items()}
        ub = {key: (cur[key]["u0"] - ws[key][:CHUNK]).astype(BF16) for key in state}
        for (d, e), s in state.items():
            c = step if d == 0 else nch - 1 - step
            ch = cur[d, e]
            dirs[d][5][c * CHUNK:(c + 1) * CHUNK, e * dv:(e + 1) * dv] = (
                _dot(ch["attn"], ub[d, e]) + ws[d, e][CHUNK:]).astype(BF16)
            state[d, e] = ch["g_end"] * s + _dot_tn(ch["kd"], ub[d, e])
    for (d, e), s in state.items():
        s_ref[d, e] = s

    @pl.when(is_ctx)
    def _():
        ns_ref[0] = s_ref[...]


def _dn_scan(qkv, gates, gates_t, s0, j, tok, *, hk_n, hv, dk, dv):
    m = qkv.shape[0]
    tt = tok.seq
    rep = hv // hk_n
    assert rep == 2
    lanes = gates.shape[2]
    kd = hk_n * dk
    fwd = lambda n: n
    bwd = lambda n: _bwd_block(n, tok)

    def stream(blk):
        return [
            pl.BlockSpec((tt, dk), lambda h, n: (blk(n), h)),
            pl.BlockSpec((tt, dk), lambda h, n: (blk(n), kd // dk + h)),
            pl.BlockSpec((tt, rep * dv), lambda h, n: (blk(n), (2 * kd) // (rep * dv) + h)),
            pl.BlockSpec((5, tt, lanes), lambda h, n: (0, blk(n), 0)),
            pl.BlockSpec((lanes, tt), lambda h, n: (0, blk(n))),
        ]

    n_batch = tok.n_ctx_blocks
    return pl.pallas_call(
        functools.partial(_dn_scan_kernel, tok=tok, hv=hv, dv=dv),
        grid=(hk_n, tok.n_blocks),
        in_specs=stream(fwd) + stream(bwd) + [
            pl.BlockSpec((1, 1, 2, rep, dk, dv), lambda h, n: (_dec_batch(n, tok), j, 0, h, 0, 0)),
        ],
        out_specs=[
            pl.BlockSpec((tt, rep * dv), lambda h, n: (n, h)),
            pl.BlockSpec((tt, rep * dv), lambda h, n: (bwd(n), h)),
            pl.BlockSpec((1, 2, rep, dk, dv), lambda h, n: (_state_block(n, tok), 0, h, 0, 0)),
        ],
        out_shape=[
            jax.ShapeDtypeStruct((m, hv * dv), BF16),
            jax.ShapeDtypeStruct((m, hv * dv), BF16),
            jax.ShapeDtypeStruct((n_batch, 2, hv, dk, dv), F32),
        ],
        scratch_shapes=[pltpu.VMEM((2, rep, dk, dv), F32)],
        compiler_params=_params("parallel", "arbitrary"),
        name="dn_scan",
    )(qkv, qkv, qkv, gates, gates_t, qkv, qkv, qkv, gates, gates_t, s0)


def _chunk_cums(la_f, la_b, cbf_ref, cbb_ref, tt):
    lower = _tri(CHUNK, upper=False)
    upper = _tri(CHUNK, upper=True)
    for c in range(tt // CHUNK):
        rows = slice(c * CHUNK, (c + 1) * CHUNK)
        cbf_ref[rows, :] = _split_dot(lower, la_f[rows])
        cbb_ref[rows, :] = _split_dot(upper, la_b[rows])


def _log_sigmoid(x):
    return jnp.minimum(x, 0.0) - jnp.log(1.0 + jnp.exp(-jnp.abs(x)))


def _gla_prep_kernel(a1_ref, w2_ref, b_ref, cbf_ref, cbb_ref, *, tt):
    a1 = a1_ref[...]
    la = [_log_sigmoid(_dot(a1, w2_ref[d]) + b_ref[d]) * (1.0 / GLA_TAU) for d in range(2)]
    _chunk_cums(la[0], la[1], cbf_ref, cbb_ref, tt)


def _gla_prep(a1, w_a2, b_a, tok):
    m, r2 = a1.shape
    kd = w_a2.shape[2]
    tt = tok.seq
    tc = _tile(kd, 512)
    zeros = jnp.zeros((GLA_RANK, kd), F32)
    w2 = jnp.stack([jnp.concatenate([w_a2[0], zeros]), jnp.concatenate([zeros, w_a2[1]])])
    out = jax.ShapeDtypeStruct((m, kd), F32)
    return pl.pallas_call(
        functools.partial(_gla_prep_kernel, tt=tt),
        grid=(m // tt, kd // tc),
        in_specs=[
            pl.BlockSpec((tt, r2), lambda i, j: (i, 0)),
            pl.BlockSpec((2, r2, tc), lambda i, j: (0, 0, j)),
            pl.BlockSpec((2, 1, tc), lambda i, j: (0, 0, j)),
        ],
        out_specs=[pl.BlockSpec((tt, tc), lambda i, j: (i, j))] * 2,
        out_shape=[out, out],
        compiler_params=_params("parallel", "parallel"),
        name="gla_prep",
    )(a1, w2, b_a.reshape(2, 1, kd))


def _hgrn_prep_kernel(ff_ref, fb_ref, lb_ref, kf_ref, kb_ref, cbf_ref, cbb_ref, *, layer, tt):
    logits = lb_ref[...]
    ex = jnp.exp(logits - jnp.max(logits, axis=0, keepdims=True))
    probs = ex / jnp.sum(ex, axis=0, keepdims=True)
    lb = jnp.sum(probs[1:layer + 1], axis=0, keepdims=True)
    la = []
    for f_ref, k_ref in ((ff_ref, kf_ref), (fb_ref, kb_ref)):
        fl = f_ref[...].astype(F32)
        sg = _sigmoid(fl)
        la.append(jnp.log(lb + (1.0 - lb) * sg))
        k_ref[...] = (1.0 - lb) * (1.0 - sg)
    _chunk_cums(la[0], la[1], cbf_ref, cbb_ref, tt)


def _hgrn_prep(proj, lb_logits, layer, d, tok):
    m = proj.shape[0]
    depth = lb_logits.shape[0]
    tt = tok.seq
    tc = _tile(d, 512)
    nb = d // tc
    out = jax.ShapeDtypeStruct((m, d), F32)
    return pl.pallas_call(
        functools.partial(_hgrn_prep_kernel, layer=layer, tt=tt),
        grid=(m // tt, nb),
        in_specs=[
            pl.BlockSpec((tt, tc), lambda i, j: (i, 3 * nb + j)),
            pl.BlockSpec((tt, tc), lambda i, j: (i, 4 * nb + j)),
            pl.BlockSpec((depth, tc), lambda i, j: (0, j)),
        ],
        out_specs=[pl.BlockSpec((tt, tc), lambda i, j: (i, j))] * 4,
        out_shape=[out] * 4,
        compiler_params=_params("parallel", "parallel"),
        name="hgrn_prep",
    )(proj, proj, lb_logits)


def _gla_offdiag_pairs(c):
    pairs = []
    size = c
    while size > SUB:
        half = size // 2
        for start in range(0, c, size):
            pairs.append(((start + half, start + size), (start, start + half)))
        size = half
    return pairs


def _gla_operands(q, k, cb, rev):
    c, dk = q.shape
    zero_rows = lambda n: jnp.zeros((n, dk), BF16)

    def padded(x, rows):
        parts = ([zero_rows(rows[0])] if rows[0] else []) + [x] + ([zero_rows(c - rows[1])] if rows[1] < c else [])
        return jnp.concatenate(parts, axis=0) if len(parts) > 1 else x

    q_parts, k_parts = [], []
    for qr, kr in _gla_offdiag_pairs(c):
        if rev:
            qr, kr = kr, qr
        ref_row = qr[0] - 1 if not rev else qr[1]
        ref = cb[ref_row:ref_row + 1, :]
        qs, ks = slice(*qr), slice(*kr)
        q_parts.append(padded((q[qs] * jnp.exp(cb[qs] - ref)).astype(BF16), qr))
        k_parts.append(padded((k[ks] * jnp.exp(ref - cb[ks])).astype(BF16), kr))
    refs = []
    for blk in range(c // SUB):
        lo, hi = blk * SUB, (blk + 1) * SUB
        if not rev:
            ref = cb[lo - 1:lo, :] if blk > 0 else jnp.zeros_like(cb[0:1, :])
        else:
            ref = cb[hi:hi + 1, :] if hi < c else jnp.zeros_like(cb[0:1, :])
        refs.append(jnp.broadcast_to(ref, (SUB, dk)))
    ref_d = jnp.concatenate(refs, axis=0)
    q_diag = (q * jnp.exp(cb - ref_d)).astype(BF16)
    k_diag = (k * jnp.exp(jnp.minimum(ref_d - cb, EXP_CLAMP))).astype(BF16)
    return jnp.concatenate(q_parts, axis=1), jnp.concatenate(k_parts, axis=1), q_diag, k_diag


def _gla_diag_mask(c, rev):
    ri = lax.broadcasted_iota(jnp.int32, (c, c), 0)
    ci = lax.broadcasted_iota(jnp.int32, (c, c), 1)
    same_block = (ri // SUB) == (ci // SUB)
    return jnp.logical_and(same_block, (ri <= ci) if rev else (ri >= ci))


def _gla_scan_kernel(qf_ref, kf_ref, vf_ref, cf_ref, qb_ref, kb_ref, vb_ref, cb_ref, s0_ref,
                     of_ref, ob_ref, ns_ref, s_ref, *, tok, q_silu, scale, hb, dk, dv):
    n = pl.program_id(1)
    tt = tok.seq
    nch = tt // CHUNK
    is_ctx, dec_start = _seq_flags(n, tok)
    streams = [(d, h) for d in range(2) for h in range(hb)]

    @pl.when(is_ctx)
    def _():
        s_ref[...] = jnp.zeros_like(s_ref)

    @pl.when(dec_start)
    def _():
        for d, h in streams:
            s_ref[d, h] = s0_ref[0, 0, d, h].T

    dirs = ((qf_ref, kf_ref, vf_ref, cf_ref, of_ref), (qb_ref, kb_ref, vb_ref, cb_ref, ob_ref))
    ops = {}
    for d, (q_ref, k_ref, v_ref, c_ref, _) in enumerate(dirs):
        rev = d == 1
        for h in range(hb):
            q = q_ref[:, h * dk:(h + 1) * dk].astype(F32)
            if q_silu:
                q = q * _sigmoid(q)
            q = q * scale
            k = k_ref[:, h * dk:(h + 1) * dk].astype(F32)
            v = v_ref[:, h * dv:(h + 1) * dv].astype(BF16)
            cb = c_ref[:, h * dk:(h + 1) * dk]
            for c in range(nch):
                rows = slice(c * CHUNK, (c + 1) * CHUNK)
                qc, kc, cbc = q[rows], k[rows], cb[rows]
                end = cbc[0:1, :] if rev else cbc[CHUNK - 1:CHUNK, :]
                ops[d, h, c] = dict(
                    intra=_gla_operands(qc, kc, cbc, rev), v=v[rows],
                    q_state=(qc * jnp.exp(cbc)).astype(BF16),
                    k_state=(kc * jnp.exp(end - cbc)).astype(BF16),
                    s_decay=jnp.exp(end))
    a_off = {key: _dot_nt(op["intra"][0], op["intra"][1]) for key, op in ops.items()}
    a_diag = {key: _dot_nt(op["intra"][2], op["intra"][3]) for key, op in ops.items()}
    masks = [_gla_diag_mask(CHUNK, rev) for rev in (False, True)]
    attn = {key: (a_off[key] + jnp.where(masks[key[0]], a_diag[key], 0.0)).astype(BF16) for key in ops}
    o_intra = {key: _dot(attn[key], op["v"]) for key, op in ops.items()}
    s_inc = {key: _dot_tn(op["v"], op["k_state"]) for key, op in ops.items()}

    state = {key: s_ref[key] for key in streams}
    for step in range(nch):
        for (d, h), s in state.items():
            c = step if d == 0 else nch - 1 - step
            dirs[d][4][c * CHUNK:(c + 1) * CHUNK, h * dv:(h + 1) * dv] = (
                o_intra[d, h, c] + _dot_nt(ops[d, h, c]["q_state"], s.astype(BF16))).astype(BF16)
            state[d, h] = ops[d, h, c]["s_decay"] * s + s_inc[d, h, c]
    for key, s in state.items():
        s_ref[key] = s

    @pl.when(is_ctx)
    def _():
        for d, h in streams:
            ns_ref[0, d, h] = s_ref[d, h].T


def _gla_scan(q_src, q_blk, kf_src, kf_blk, kb_src, kb_blk, v_src, v_blk, cb_f, cb_b, s0, j, tok,
              *, heads, dk, dv, q_silu, hb):
    m = q_src.shape[0]
    tt = tok.seq
    assert heads % hb == 0 and q_blk % hb == 0 and kf_blk % hb == 0 and kb_blk % hb == 0 and v_blk % hb == 0
    bwd = lambda n: _bwd_block(n, tok)
    fwd = lambda n: n

    def stream(blk, k_blk):
        return [
            pl.BlockSpec((tt, hb * dk), lambda h, n: (blk(n), q_blk // hb + h)),
            pl.BlockSpec((tt, hb * dk), lambda h, n: (blk(n), k_blk // hb + h)),
            pl.BlockSpec((tt, hb * dv), lambda h, n: (blk(n), v_blk // hb + h)),
            pl.BlockSpec((tt, hb * dk), lambda h, n: (blk(n), h)),
        ]

    n_batch = tok.n_ctx_blocks
    return pl.pallas_call(
        functools.partial(_gla_scan_kernel, tok=tok, q_silu=q_silu, scale=dk ** -0.5, hb=hb, dk=dk, dv=dv),
        grid=(heads // hb, tok.n_blocks),
        in_specs=stream(fwd, kf_blk) + stream(bwd, kb_blk) + [
            pl.BlockSpec((1, 1, 2, hb, dk, dv), lambda h, n: (_dec_batch(n, tok), j, 0, h, 0, 0)),
        ],
        out_specs=[
            pl.BlockSpec((tt, hb * dv), lambda h, n: (n, h)),
            pl.BlockSpec((tt, hb * dv), lambda h, n: (bwd(n), h)),
            pl.BlockSpec((1, 2, hb, dk, dv), lambda h, n: (_state_block(n, tok), 0, h, 0, 0)),
        ],
        out_shape=[
            jax.ShapeDtypeStruct((m, heads * dv), BF16),
            jax.ShapeDtypeStruct((m, heads * dv), BF16),
            jax.ShapeDtypeStruct((n_batch, 2, heads, dk, dv), F32),
        ],
        scratch_shapes=[pltpu.VMEM((2, hb, dv, dk), F32)],
        compiler_params=_params("parallel", "arbitrary"),
        name="gla_scan",
    )(q_src, kf_src, v_src, cb_f, q_src, kb_src, v_src, cb_b, s0)


def kernel(x_prompt, x_sample, c, state_deltanet, state_gla, state_hgrn, c_ctx, ada_w, ada_b, norm1, norm2, final_norm, w_up, w_down, dn_w_in, dn_conv, dn_A_log, dn_dt_bias, dn_norm, dn_w_out, gla_w_in, gla_w_a2, gla_b_a, gla_norm, gla_w_out, hgrn_w_in, hgrn_lb_logits, hgrn_norm, hgrn_w_out):
    batch, seq, d = x_prompt.shape
    dec_batch, dec_seq, _ = x_sample.shape
    depth = ada_w.shape[0]
    tok = Tokens(m_ctx=batch * seq, seq=seq, dec_batch=dec_batch, dec_seq=dec_seq)
    assert seq % CHUNK == 0 and dec_seq % seq == 0 and seq % GRID_W == 0

    dn_hv, dn_dk, dn_dv = state_deltanet.shape[3:]
    dn_vd = dn_hv * dn_dv
    n_conv = dn_conv.shape[2]
    dn_kd = (n_conv - dn_vd) // 2
    dn_hk = dn_kd // dn_dk
    gla_h, gla_dk, gla_dv = state_gla.shape[3:]
    gla_kd, gla_vd = gla_h * gla_dk, gla_h * gla_dv
    hg_h, hg_dk, hg_dv = state_hgrn.shape[3:]

    x = jnp.concatenate([x_prompt.reshape(tok.m_ctx, d), x_sample.reshape(dec_batch * dec_seq, d)], axis=0)
    n_rows = 1 + dec_batch
    pad = (-n_rows) % 8
    cond = jnp.concatenate([c_ctx[None, :], c, jnp.zeros((pad, d), F32)], axis=0)
    mod_all = _mod_table(cond, ada_w, ada_b).reshape(depth, n_rows + pad, 6, d)

    new_states = ([], [], [])
    for i in range(depth):
        kind, j = i % N_MIXERS, i // N_MIXERS
        mod = mod_all[i]
        if kind == 0:
            w_in = dn_w_in[j]
            proj = _proj(x, norm1[i], mod, w_in[:, :n_conv + dn_vd].astype(BF16), tok)
            raw = _proj(x, norm1[i], mod, w_in[:, n_conv + dn_vd:], tok, precise=True)
            qkv = _dn_prep(proj, dn_conv[j], n_conv, dn_kd, dn_dk, tok)
            gates, gates_t = _dn_gates(raw, dn_A_log[j], dn_dt_bias[j], tok)
            o_f, o_b, ns = _dn_scan(qkv, gates, gates_t, state_deltanet, j, tok,
                                    hk_n=dn_hk, hv=dn_hv, dk=dn_dk, dv=dn_dv)
            x = _out_proj(o_f, o_b, proj, n_conv // dn_vd, dn_norm[j], x, mod, dn_w_out[j].astype(BF16), tok)
        elif kind == 1:
            w_in = gla_w_in[j]
            n_main = 2 * gla_kd + 2 * gla_vd
            proj = _proj(x, norm1[i], mod, w_in[:, :n_main].astype(BF16), tok)
            a1 = _proj(x, norm1[i], mod, w_in[:, n_main:], tok, precise=True)
            cb_f, cb_b = _gla_prep(a1, gla_w_a2[j], gla_b_a[j], tok)
            o_f, o_b, ns = _gla_scan(proj, 0, proj, gla_h, proj, gla_h, proj, (2 * gla_kd) // gla_dv,
                                     cb_f, cb_b, state_gla, j, tok,
                                     heads=gla_h, dk=gla_dk, dv=gla_dv, q_silu=False, hb=1)
            x = _out_proj(o_f, o_b, proj, (2 * gla_kd + gla_vd) // gla_vd, gla_norm[j], x, mod,
                          gla_w_out[j].astype(BF16), tok)
        else:
            proj = _proj(x, norm1[i], mod, hgrn_w_in[j].astype(BF16), tok)
            k_f, k_b, cb_f, cb_b = _hgrn_prep(proj, hgrn_lb_logits, i, d, tok)
            o_f, o_b, ns = _gla_scan(proj, 0, k_f, 0, k_b, 0, proj, d // hg_dv, cb_f, cb_b, state_hgrn, j, tok,
                                     heads=hg_h, dk=hg_dk, dv=hg_dv, q_silu=True, hb=min(hg_h, 4))
            x = _out_proj(o_f, o_b, proj, 2, hgrn_norm[j], x, mod, hgrn_w_out[j].astype(BF16), tok)
        new_states[kind].append(ns)
        x = _mlp(x, norm2[i], mod, w_up[i].astype(BF16), w_down[i].astype(BF16), final_norm, tok,
                 final=(i == depth - 1))

    y_prompt = x[:tok.m_ctx].reshape(batch, seq, d)
    y_sample = x[tok.m_ctx:].reshape(dec_batch, dec_seq, d)
    return (y_prompt, y_sample) + tuple(jnp.stack(s, axis=1) for s in new_states)
```

```python
import functools
from typing import NamedTuple

import jax
import jax.numpy as jnp
from jax import lax
from jax.experimental import pallas as pl
from jax.experimental.pallas import tpu as pltpu

F32 = jnp.float32
BF16 = jnp.bfloat16

EPS = 1e-6
GRID_W = 64
CONV_K = 5
GLA_RANK = 16
GLA_TAU = 16.0
N_MIXERS = 3
CHUNK = 64
SUB = 16
EXP_CLAMP = 80.0
LANES = 128
V7X_VMEM_LIMIT_BYTES = 56 * 1024 * 1024
HIGHEST = lax.Precision.HIGHEST


class Tokens(NamedTuple):
    m_ctx: int
    seq: int
    dec_batch: int
    dec_seq: int

    @property
    def m(self):
        return self.m_ctx + self.dec_batch * self.dec_seq

    @property
    def n_ctx_blocks(self):
        return self.m_ctx // self.seq

    @property
    def blocks_per_dec(self):
        return self.dec_seq // self.seq

    @property
    def n_blocks(self):
        return self.m // self.seq


def _tile(n, pref):
    t = min(n, pref)
    while n % t:
        t //= 2
    return t


def _token_tile(tok, pref):
    t = pref
    while tok.m_ctx % t or tok.dec_seq % t:
        t //= 2
    return t


def _params(*sem):
    return pltpu.CompilerParams(dimension_semantics=sem, vmem_limit_bytes=V7X_VMEM_LIMIT_BYTES)


def _mod_row(i, tm, tok):
    start = i * tm
    return jnp.where(start < tok.m_ctx, 0, 1 + (start - tok.m_ctx) // tok.dec_seq)


def _sigmoid(x):
    return 1.0 / (1.0 + jnp.exp(-x))


def _softplus(x):
    return jnp.maximum(x, 0.0) + jnp.log(1.0 + jnp.exp(-jnp.abs(x)))


def _norm_mod(x, nw, shift, scale):
    ms = jnp.mean(x * x, axis=-1, keepdims=True)
    return x * lax.rsqrt(ms + EPS) * nw * (1.0 + scale) + shift


def _dot(a, b):
    return jnp.dot(a, b, preferred_element_type=F32)


def _dot_nt(a, b):
    return lax.dot_general(a, b, (((1,), (1,)), ((), ())), preferred_element_type=F32)


def _dot_tn(a, b):
    return lax.dot_general(a, b, (((0,), (0,)), ((), ())), preferred_element_type=F32)


def _split_dot(tri, x):
    hi = x.astype(BF16)
    lo = (x - hi.astype(F32)).astype(BF16)
    return _dot(tri, hi) + _dot(tri, lo)


def _split_matmul(a, b):
    m = a.shape[0]
    ah = a.astype(BF16)
    al = (a - ah.astype(F32)).astype(BF16)
    bh = b.astype(BF16)
    bl = (b - bh.astype(F32)).astype(BF16)
    t = _dot(jnp.concatenate([ah, al], axis=0), bh)
    return t[:m] + t[m:] + _dot(ah, bl)


def _tri(n, upper):
    r = lax.broadcasted_iota(jnp.int32, (n, n), 0)
    c = lax.broadcasted_iota(jnp.int32, (n, n), 1)
    return jnp.where((r <= c) if upper else (r >= c), 1.0, 0.0).astype(BF16)


def _mod_kernel(c_ref, w_ref, b_ref, o_ref):
    c = c_ref[...]
    s = c * _sigmoid(c)
    o_ref[0] = jnp.dot(s, w_ref[0], precision=HIGHEST, preferred_element_type=F32) + b_ref[0]


def _mod_table(cond, ada_w, ada_b):
    depth, d, n = ada_w.shape
    rows = cond.shape[0]
    tn = _tile(n, 1024)
    return pl.pallas_call(
        _mod_kernel,
        grid=(depth, n // tn),
        in_specs=[
            pl.BlockSpec((rows, d), lambda l, j: (0, 0)),
            pl.BlockSpec((1, d, tn), lambda l, j: (l, 0, j)),
            pl.BlockSpec((1, 1, tn), lambda l, j: (l, 0, j)),
        ],
        out_specs=pl.BlockSpec((1, rows, tn), lambda l, j: (l, 0, j)),
        out_shape=jax.ShapeDtypeStruct((depth, rows, n), F32),
        compiler_params=_params("parallel", "parallel"),
        name="mod_table",
    )(cond, ada_w, ada_b.reshape(depth, 1, n))


def _proj_kernel(x_ref, nw_ref, mod_ref, w_ref, o_ref, h_ref, *, precise):
    @pl.when(pl.program_id(1) == 0)
    def _():
        h = _norm_mod(x_ref[...], nw_ref[...], mod_ref[0, 0:1, :], mod_ref[0, 1:2, :])
        h_ref[...] = h.astype(h_ref.dtype)

    if precise:
        o_ref[...] = jnp.dot(h_ref[...], w_ref[...], precision=HIGHEST, preferred_element_type=F32)
    else:
        o_ref[...] = _dot(h_ref[...], w_ref[...]).astype(o_ref.dtype)


def _proj(x, nw, mod, w, tok, *, precise=False, tm=1024, tn=1024):
    m, d = x.shape
    n = w.shape[1]
    tm = _token_tile(tok, tm)
    tn = _tile(n, tn)
    return pl.pallas_call(
        functools.partial(_proj_kernel, precise=precise),
        grid=(m // tm, n // tn),
        in_specs=[
            pl.BlockSpec((tm, d), lambda i, j: (i, 0)),
            pl.BlockSpec((1, d), lambda i, j: (0, 0)),
            pl.BlockSpec((1, 6, d), lambda i, j: (_mod_row(i, tm, tok), 0, 0)),
            pl.BlockSpec((d, tn), lambda i, j: (0, j)),
        ],
        out_specs=pl.BlockSpec((tm, tn), lambda i, j: (i, j)),
        out_shape=jax.ShapeDtypeStruct((m, n), F32 if precise else BF16),
        scratch_shapes=[pltpu.VMEM((tm, d), F32 if precise else BF16)],
        compiler_params=_params("parallel", "arbitrary"),
        name="proj_precise" if precise else "proj",
    )(x, nw.reshape(1, d), mod, w)


def _mlp_kernel(x_ref, nw_ref, mod_ref, wu_ref, wd_ref, fw_ref, o_ref, h_ref, acc_ref, *, final):
    f = pl.program_id(1)

    @pl.when(f == 0)
    def _():
        h = _norm_mod(x_ref[...], nw_ref[...], mod_ref[0, 3:4, :], mod_ref[0, 4:5, :])
        h_ref[...] = h.astype(BF16)
        acc_ref[...] = jnp.zeros_like(acc_ref)

    a = jnp.maximum(_dot(h_ref[...], wu_ref[...]), 0.0)
    acc_ref[...] += _dot((a * a).astype(BF16), wd_ref[...])

    @pl.when(f == pl.num_programs(1) - 1)
    def _():
        y = x_ref[...] + mod_ref[0, 5:6, :] * acc_ref[...]
        if final:
            ms = jnp.mean(y * y, axis=-1, keepdims=True)
            y = y * lax.rsqrt(ms + EPS) * fw_ref[...]
        o_ref[...] = y


def _mlp(x, nw, mod, w_up, w_down, final_w, tok, *, final, tm=512, tf=1024):
    m, d = x.shape
    ff = w_up.shape[1]
    tm = _token_tile(tok, tm)
    tf = _tile(ff, tf)
    return pl.pallas_call(
        functools.partial(_mlp_kernel, final=final),
        grid=(m // tm, ff // tf),
        in_specs=[
            pl.BlockSpec((tm, d), lambda i, f: (i, 0)),
            pl.BlockSpec((1, d), lambda i, f: (0, 0)),
            pl.BlockSpec((1, 6, d), lambda i, f: (_mod_row(i, tm, tok), 0, 0)),
            pl.BlockSpec((d, tf), lambda i, f: (0, f)),
            pl.BlockSpec((tf, d), lambda i, f: (f, 0)),
            pl.BlockSpec((1, d), lambda i, f: (0, 0)),
        ],
        out_specs=pl.BlockSpec((tm, d), lambda i, f: (i, 0)),
        out_shape=jax.ShapeDtypeStruct((m, d), F32),
        scratch_shapes=[pltpu.VMEM((tm, d), BF16), pltpu.VMEM((tm, d), F32)],
        compiler_params=_params("parallel", "arbitrary"),
        name="mlp_final" if final else "mlp",
    )(x, nw.reshape(1, d), mod, w_up, w_down, final_w.reshape(1, d))


def _out_kernel(of_ref, ob_ref, g_ref, nw_ref, x_ref, mod_ref, w_ref, o_ref, y_ref, *, n_heads, dh):
    @pl.when(pl.program_id(1) == 0)
    def _():
        ones = jnp.ones((dh, LANES), BF16)
        for h in range(n_heads):
            sl = slice(h * dh, (h + 1) * dh)
            o = of_ref[:, sl].astype(F32) + ob_ref[:, sl].astype(F32)
            inv = lax.rsqrt(_dot((o * o).astype(BF16), ones) * (1.0 / dh) + EPS)
            g = g_ref[:, sl].astype(F32)
            y = o * nw_ref[...] * (g * _sigmoid(g))
            for c in range(dh // LANES):
                lo = h * dh + c * LANES
                y_ref[:, lo:lo + LANES] = (y[:, c * LANES:(c + 1) * LANES] * inv).astype(BF16)

    o_ref[...] = x_ref[...] + mod_ref[0, 2:3, :] * _dot(y_ref[...], w_ref[...])


def _out_proj(o_f, o_b, gate_src, gate_blk, norm_w, x, mod, w_out, tok, *, tm=512, tn=512):
    m, vd = o_f.shape
    d = x.shape[1]
    dh = norm_w.shape[0]
    tm = _token_tile(tok, tm)
    tn = _tile(d, tn)
    return pl.pallas_call(
        functools.partial(_out_kernel, n_heads=vd // dh, dh=dh),
        grid=(m // tm, d // tn),
        in_specs=[
            pl.BlockSpec((tm, vd), lambda i, j: (i, 0)),
            pl.BlockSpec((tm, vd), lambda i, j: (i, 0)),
            pl.BlockSpec((tm, vd), lambda i, j: (i, gate_blk)),
            pl.BlockSpec((1, dh), lambda i, j: (0, 0)),
            pl.BlockSpec((tm, tn), lambda i, j: (i, j)),
            pl.BlockSpec((1, 6, tn), lambda i, j: (_mod_row(i, tm, tok), 0, j)),
            pl.BlockSpec((vd, tn), lambda i, j: (0, j)),
        ],
        out_specs=pl.BlockSpec((tm, tn), lambda i, j: (i, j)),
        out_shape=jax.ShapeDtypeStruct((m, d), F32),
        scratch_shapes=[pltpu.VMEM((tm, vd), BF16)],
        compiler_params=_params("parallel", "arbitrary"),
        name="out_proj",
    )(o_f, o_b, gate_src, norm_w.reshape(1, dh), x, mod, w_out)


def _bwd_block(n, tok):
    nc, per = tok.n_ctx_blocks, tok.blocks_per_dec
    r = jnp.maximum(n - nc, 0)
    return jnp.where(n < nc, n, nc + (r // per) * per + (per - 1 - r % per))


def _dec_batch(n, tok):
    return jnp.maximum(n - tok.n_ctx_blocks, 0) // tok.blocks_per_dec


def _state_block(n, tok):
    return jnp.minimum(n, tok.n_ctx_blocks - 1)


def _seq_flags(n, tok):
    nc, per = tok.n_ctx_blocks, tok.blocks_per_dec
    is_ctx = n < nc
    r = jnp.maximum(n - nc, 0) % per
    return is_ctx, jnp.logical_and(jnp.logical_not(is_ctx), r == 0)


def _dn_prep_kernel(x_ref, shift_ref, w_ref, o_ref, *, n_q_blocks, dk):
    j = pl.program_id(1)
    x = x_ref[...]
    tc = x.shape[1]
    acc = None
    for s in range(CONV_K):
        xs = x.astype(F32) if s == CONV_K // 2 else _dot(shift_ref[0, s], x)
        term = xs * w_ref[s:s + 1, :]
        acc = term if acc is None else acc + term
    y = acc * _sigmoid(acc)

    def normed(scale):
        ones = jnp.ones((dk, dk), BF16)
        for g in range(tc // dk):
            blk = y[:, g * dk:(g + 1) * dk]
            ss = _dot((blk * blk).astype(BF16), ones)
            o_ref[:, g * dk:(g + 1) * dk] = (blk * (lax.rsqrt(ss + EPS) * scale)).astype(o_ref.dtype)

    @pl.when(j < n_q_blocks)
    def _():
        normed(dk ** -0.5)

    @pl.when(jnp.logical_and(j >= n_q_blocks, j < 2 * n_q_blocks))
    def _():
        normed(1.0)

    @pl.when(j >= 2 * n_q_blocks)
    def _():
        o_ref[...] = y.astype(o_ref.dtype)


def _conv_shift_matrices(tt):
    t = jnp.arange(tt)[:, None]
    u = jnp.arange(tt)[None, :]
    mats = []
    for seg in (tt, GRID_W):
        same = (t // seg) == (u // seg)
        mats.append(jnp.stack([jnp.logical_and(u == t + s - CONV_K // 2, same) for s in range(CONV_K)]))
    return jnp.stack(mats).astype(BF16)


def _dn_prep(proj, conv_w, n_conv, kd, dk, tok):
    m = proj.shape[0]
    tt = tok.seq
    tc = _tile(kd, 512)
    n_ctx = tok.n_ctx_blocks
    return pl.pallas_call(
        functools.partial(_dn_prep_kernel, n_q_blocks=kd // tc, dk=dk),
        grid=(m // tt, n_conv // tc),
        in_specs=[
            pl.BlockSpec((tt, tc), lambda i, j: (i, j)),
            pl.BlockSpec((1, CONV_K, tt, tt), lambda i, j: (jnp.where(i < n_ctx, 0, 1), 0, 0, 0)),
            pl.BlockSpec((CONV_K, tc), lambda i, j: (0, j)),
        ],
        out_specs=pl.BlockSpec((tt, tc), lambda i, j: (i, j)),
        out_shape=jax.ShapeDtypeStruct((m, n_conv), BF16),
        compiler_params=_params("parallel", "parallel"),
        name="dn_prep",
    )(proj, _conv_shift_matrices(tt), conv_w)


def _dn_gate_kernel(x_ref, a_ref, bias_ref, o_ref, gt_ref, *, hv, tt):
    x = x_ref[...]
    lanes = x.shape[1]
    lane = lax.broadcasted_iota(jnp.int32, (1, lanes), 1)
    is_fwd = lane < 3 * hv
    g = -a_ref[...] * _softplus(x + bias_ref[...])
    beta = pltpu.roll(_sigmoid(x), 2 * hv, 1)
    lower = _tri(CHUNK, upper=False)
    upper = _tri(CHUNK, upper=True)
    for c in range(tt // CHUNK):
        rows = slice(c * CHUNK, (c + 1) * CHUNK)
        gch = g[rows]
        gc = jnp.where(is_fwd, _split_dot(lower, gch), _split_dot(upper, gch))
        g_end = jnp.where(is_fwd, gc[CHUNK - 1:CHUNK, :], gc[0:1, :])
        eg = jnp.exp(gc)
        o_ref[0, rows, :] = beta[rows]
        o_ref[1, rows, :] = gc
        o_ref[2, rows, :] = eg
        o_ref[3, rows, :] = jnp.exp(g_end - gc)
        o_ref[4, rows, :] = beta[rows] * eg
    gt_ref[...] = o_ref[1].T


def _dn_gates(raw, a_log, dt_bias, tok):
    m, lanes = raw.shape
    hv = lanes // 4
    tt = tok.seq
    zeros = jnp.zeros((2 * hv,), F32)
    a_full = jnp.concatenate([zeros, jnp.exp(a_log).reshape(-1)]).reshape(1, lanes)
    bias_full = jnp.concatenate([zeros, dt_bias.reshape(-1)]).reshape(1, lanes)
    return pl.pallas_call(
        functools.partial(_dn_gate_kernel, hv=hv, tt=tt),
        grid=(m // tt,),
        in_specs=[
            pl.BlockSpec((tt, lanes), lambda i: (i, 0)),
            pl.BlockSpec((1, lanes), lambda i: (0, 0)),
            pl.BlockSpec((1, lanes), lambda i: (0, 0)),
        ],
        out_specs=[
            pl.BlockSpec((5, tt, lanes), lambda i: (0, i, 0)),
            pl.BlockSpec((lanes, tt), lambda i: (0, i)),
        ],
        out_shape=[jax.ShapeDtypeStruct((5, m, lanes), F32), jax.ShapeDtypeStruct((lanes, m), F32)],
        compiler_params=_params("parallel"),
        name="dn_gates",
    )(raw, a_full, bias_full)


def _dn_scan_kernel(qf_ref, kf_ref, vf_ref, gf_ref, gtf_ref, qb_ref, kb_ref, vb_ref, gb_ref, gtb_ref, s0_ref,
                    of_ref, ob_ref, ns_ref, s_ref, u0_ref, wq_ref, attn_ref, kd_ref, ge_ref, *, tok, hv, dv):
    hk = pl.program_id(0)
    n = pl.program_id(1)
    tt = tok.seq
    nch = tt // CHUNK
    assert dv == 2 * CHUNK and CHUNK == 4 * SUB
    has_prev = n > 0
    is_ctx, dec_start = _seq_flags(jnp.maximum(n - 1, 0), tok)
    wslot = n % 2
    rslot = 1 - wslot
    scratch = (u0_ref, wq_ref, attn_ref, kd_ref, ge_ref)

    @pl.when(n == 0)
    def _():
        s_ref[...] = jnp.zeros_like(s_ref)
        for ref in scratch:
            ref[1] = jnp.zeros(ref.shape[1:], ref.dtype)

    @pl.when(jnp.logical_and(has_prev, is_ctx))
    def _():
        s_ref[...] = jnp.zeros_like(s_ref)

    @pl.when(jnp.logical_and(has_prev, dec_start))
    def _():
        s_ref[...] = s0_ref[0, 0]

    bf = lambda x: x.astype(BF16)
    lanes = gf_ref.shape[2]
    lane_iota = lax.broadcasted_iota(jnp.int32, (1, lanes), 1)
    ri = lax.broadcasted_iota(jnp.int32, (CHUNK, 2 * CHUNK), 0)
    li = lax.broadcasted_iota(jnp.int32, (CHUNK, 2 * CHUNK), 1)
    left = li < CHUNK
    ci = li % CHUNK
    same = [(ri // b) == (ci // b) for b in (SUB, 2 * SUB)]
    zero_b = jnp.zeros((CHUNK, 2 * CHUNK), BF16)

    def block_diag(xb):
        return jnp.concatenate([jnp.where(left, xb, zero_b), jnp.where(left, zero_b, xb)], axis=0)

    def pair_dot(a, b):
        return _dot(bf(a), block_diag(bf(b)))

    dirs = ((qf_ref, kf_ref, vf_ref, gf_ref, gtf_ref, of_ref), (qb_ref, kb_ref, vb_ref, gb_ref, gtb_ref, ob_ref))

    def prepare():
        ch = {}
        for d, (q_ref, k_ref, v_ref, g_ref, gt_ref, _) in enumerate(dirs):
            incl = (ri >= ci) if d == 0 else (ri <= ci)
            strict = (ri > ci) if d == 0 else (ri < ci)
            end_row = CHUNK - 1 if d == 0 else 0
            q = q_ref[...]
            k = k_ref[...]
            kf32 = k.astype(F32)
            qf32 = q.astype(F32)
            cols, g_rows, vs = [], [], []
            for e in range(2):
                a_lane = 2 * hv + d * hv + 2 * hk + e
                sel = lane_iota == a_lane
                cols.append([jnp.sum(jnp.where(sel, g_ref[i], 0.0), axis=1, keepdims=True) for i in range(5)])
                g_rows.append(gt_ref[pl.ds(a_lane, 1), :])
                vs.append(v_ref[:, e * dv:(e + 1) * dv].astype(F32))
            for c in range(nch):
                rows = slice(c * CHUNK, (c + 1) * CHUNK)
                pair = lambda i: jnp.where(left, cols[0][i][rows], cols[1][i][rows])
                k2 = jnp.concatenate([k[rows], k[rows]], axis=0)
                kk = _dot_nt(k[rows], k2)
                qk = _dot_nt(q[rows], k2)
                g_row = jnp.concatenate([g_rows[0][:, rows], g_rows[1][:, rows]], axis=1)
                diff = pair(1) - g_row
                decay = jnp.where(incl, jnp.exp(jnp.where(incl, diff, 0.0)), 0.0)
                p = jnp.where(strict, -(pair(0) * kk * decay), 0.0)
                attn_ref[wslot, d, c] = bf(qk * decay)
                for e in range(2):
                    beta_c, _, eg_c, ekd_c, beg_c = cols[e]
                    qg = qf32[rows] * eg_c[rows]
                    kd_ref[wslot, d, e, c] = bf(kf32[rows] * ekd_c[rows])
                    wq_ref[wslot, d, e, c, CHUNK:, :] = bf(qg)
                    g_end = eg_c[c * CHUNK + end_row:c * CHUNK + end_row + 1, :]
                    ge_ref[wslot, d, e, c] = jnp.broadcast_to(g_end, ge_ref.shape[4:])
                ch[d, c] = dict(
                    pd=jnp.where(same[0], p, 0.0),
                    p32=jnp.where(jnp.logical_and(same[1], jnp.logical_not(same[0])), p, 0.0),
                    p64=jnp.where(same[1], 0.0, p),
                    r=[jnp.concatenate([cols[e][0][rows] * vs[e][rows], cols[e][4][rows] * kf32[rows]], axis=1)
                       for e in range(2)])
            yield
        for c in ch.values():
            c["q"] = pair_dot(c["pd"], c["pd"])
            c["xm"] = c["pd"]
        yield
        for it in range(2):
            for c in ch.values():
                q, xm = c["q"], c["xm"]
                both = _dot(bf(q), jnp.concatenate([block_diag(bf(xm)), block_diag(bf(q))], axis=1))
                c["xm"] = xm + q + both[:, :2 * CHUNK]
                c["q"] = both[:, 2 * CHUNK:]
            yield
        for c in ch.values():
            q, xm = c.pop("q"), c["xm"]
            c["xm"] = xm + q + pair_dot(q, xm)
        yield
        for level in ("p32", "p64"):
            for c in ch.values():
                c["y"] = c[level] + pair_dot(c[level], c["xm"])
            yield
            for c in ch.values():
                xm, y = c["xm"], c.pop("y")
                c["xm"] = xm + y + pair_dot(xm, y)
            yield
        zero_r = jnp.zeros((CHUNK, 2 * dv), BF16)
        for (d, cidx), c in ch.items():
            r0, r1 = c["r"]
            rhs = jnp.concatenate([jnp.concatenate([bf(r0), zero_r], axis=1),
                                   jnp.concatenate([zero_r, bf(r1)], axis=1)], axis=0)
            big = _dot(bf(c["xm"]), rhs)
            sol = [r0 + big[:, :2 * dv], r1 + big[:, 2 * dv:]]
            u0_ref[wslot, d, cidx] = jnp.concatenate([sol[0][:, :dv], sol[1][:, :dv]], axis=1)
            for e in range(2):
                wq_ref[wslot, d, e, cidx, :CHUNK, :] = bf(sol[e][:, dv:])
        yield

    def recur():
        state = {(d, e): s_ref[d, e] for d in range(2) for e in range(2)}
        zero_u = jnp.zeros((CHUNK, dv), BF16)
        for step in range(nch):
            chunk = lambda d: step if d == 0 else nch - 1 - step
            ws = {(d, e): _dot(wq_ref[rslot, d, e, chunk(d)], bf(s)) for (d, e), s in state.items()}
            yield
            for d in range(2):
                c = chunk(d)
                u0 = u0_ref[rslot, d, c]
                ub = [bf(u0[:, e * dv:(e + 1) * dv] - ws[d, e][:CHUNK]) for e in range(2)]
                u2 = jnp.concatenate([jnp.concatenate([ub[0], zero_u], axis=1),
                                      jnp.concatenate([zero_u, ub[1]], axis=1)], axis=0)
                o = _dot(attn_ref[rslot, d, c], u2) + jnp.concatenate([ws[d, 0][CHUNK:], ws[d, 1][CHUNK:]], axis=1)
                dirs[d][5][c * CHUNK:(c + 1) * CHUNK, :] = bf(o)
                for e in range(2):
                    state[d, e] = (ge_ref[rslot, d, e, c][0:1, :] * state[d, e]
                                   + _dot_tn(kd_ref[rslot, d, e, c], ub[e]))
            yield
        for (d, e), s in state.items():
            s_ref[d, e] = s
        yield

    halves = [recur(), prepare()]
    while halves:
        for g in list(halves):
            if next(g, "done") == "done":
                halves.remove(g)

    @pl.when(jnp.logical_and(has_prev, is_ctx))
    def _():
        ns_ref[0] = s_ref[...]


def _dn_scan(qkv, gates, gates_t, s0, j, tok, *, hk_n, hv, dk, dv):
    m = qkv.shape[0]
    tt = tok.seq
    rep = hv // hk_n
    assert rep == 2
    lanes = gates.shape[2]
    kd = hk_n * dk
    nb = tok.n_blocks
    nch = tt // CHUNK
    fwd = lambda n: jnp.minimum(n, nb - 1)
    bwd = lambda n: _bwd_block(jnp.minimum(n, nb - 1), tok)
    prev = lambda n: jnp.maximum(n - 1, 0)

    def stream(blk):
        return [
            pl.BlockSpec((tt, dk), lambda h, n: (blk(n), h)),
            pl.BlockSpec((tt, dk), lambda h, n: (blk(n), kd // dk + h)),
            pl.BlockSpec((tt, rep * dv), lambda h, n: (blk(n), (2 * kd) // (rep * dv) + h)),
            pl.BlockSpec((5, tt, lanes), lambda h, n: (0, blk(n), 0)),
            pl.BlockSpec((lanes, tt), lambda h, n: (0, blk(n))),
        ]

    n_batch = tok.n_ctx_blocks
    return pl.pallas_call(
        functools.partial(_dn_scan_kernel, tok=tok, hv=hv, dv=dv),
        grid=(hk_n, nb + 1),
        in_specs=stream(fwd) + stream(bwd) + [
            pl.BlockSpec((1, 1, 2, rep, dk, dv), lambda h, n: (_dec_batch(prev(n), tok), j, 0, h, 0, 0)),
        ],
        out_specs=[
            pl.BlockSpec((tt, rep * dv), lambda h, n: (prev(n), h)),
            pl.BlockSpec((tt, rep * dv), lambda h, n: (_bwd_block(prev(n), tok), h)),
            pl.BlockSpec((1, 2, rep, dk, dv), lambda h, n: (_state_block(prev(n), tok), 0, h, 0, 0)),
        ],
        out_shape=[
            jax.ShapeDtypeStruct((m, hv * dv), BF16),
            jax.ShapeDtypeStruct((m, hv * dv), BF16),
            jax.ShapeDtypeStruct((n_batch, 2, hv, dk, dv), F32),
        ],
        scratch_shapes=[
            pltpu.VMEM((2, rep, dk, dv), F32),
            pltpu.VMEM((2, 2, nch, CHUNK, rep * dv), F32),
            pltpu.VMEM((2, 2, rep, nch, 2 * CHUNK, dk), BF16),
            pltpu.VMEM((2, 2, nch, CHUNK, rep * CHUNK), BF16),
            pltpu.VMEM((2, 2, rep, nch, CHUNK, dk), BF16),
            pltpu.VMEM((2, 2, rep, nch, 8, dv), F32),
        ],
        compiler_params=_params("parallel", "arbitrary"),
        name="dn_scan",
    )(qkv, qkv, qkv, gates, gates_t, qkv, qkv, qkv, gates, gates_t, s0)


def _chunk_cums(la_f, la_b, cbf_ref, cbb_ref, tt):
    lower = _tri(CHUNK, upper=False)
    upper = _tri(CHUNK, upper=True)
    for c in range(tt // CHUNK):
        rows = slice(c * CHUNK, (c + 1) * CHUNK)
        cbf_ref[rows, :] = _split_dot(lower, la_f[rows])
        cbb_ref[rows, :] = _split_dot(upper, la_b[rows])


def _log_sigmoid(x):
    return jnp.minimum(x, 0.0) - jnp.log(1.0 + jnp.exp(-jnp.abs(x)))


def _gla_prep_kernel(a1_ref, w2_ref, b_ref, cbf_ref, cbb_ref, *, tt):
    a1 = a1_ref[...]
    la = [_log_sigmoid(_dot(a1, w2_ref[d]) + b_ref[d]) * (1.0 / GLA_TAU) for d in range(2)]
    _chunk_cums(la[0], la[1], cbf_ref, cbb_ref, tt)


def _gla_prep(a1, w_a2, b_a, tok):
    m, r2 = a1.shape
    kd = w_a2.shape[2]
    tt = tok.seq
    tc = _tile(kd, 512)
    zeros = jnp.zeros((GLA_RANK, kd), F32)
    w2 = jnp.stack([jnp.concatenate([w_a2[0], zeros]), jnp.concatenate([zeros, w_a2[1]])])
    out = jax.ShapeDtypeStruct((m, kd), F32)
    return pl.pallas_call(
        functools.partial(_gla_prep_kernel, tt=tt),
        grid=(m // tt, kd // tc),
        in_specs=[
            pl.BlockSpec((tt, r2), lambda i, j: (i, 0)),
            pl.BlockSpec((2, r2, tc), lambda i, j: (0, 0, j)),
            pl.BlockSpec((2, 1, tc), lambda i, j: (0, 0, j)),
        ],
        out_specs=[pl.BlockSpec((tt, tc), lambda i, j: (i, j))] * 2,
        out_shape=[out, out],
        compiler_params=_params("parallel", "parallel"),
        name="gla_prep",
    )(a1, w2, b_a.reshape(2, 1, kd))


def _hgrn_prep_kernel(ff_ref, fb_ref, lb_ref, kf_ref, kb_ref, cbf_ref, cbb_ref, *, layer, tt):
    logits = lb_ref[...]
    ex = jnp.exp(logits - jnp.max(logits, axis=0, keepdims=True))
    probs = ex / jnp.sum(ex, axis=0, keepdims=True)
    lb = jnp.sum(probs[1:layer + 1], axis=0, keepdims=True)
    la = []
    for f_ref, k_ref in ((ff_ref, kf_ref), (fb_ref, kb_ref)):
        fl = f_ref[...].astype(F32)
        sg = _sigmoid(fl)
        la.append(jnp.log(lb + (1.0 - lb) * sg))
        k_ref[...] = (1.0 - lb) * (1.0 - sg)
    _chunk_cums(la[0], la[1], cbf_ref, cbb_ref, tt)


def _hgrn_prep(proj, lb_logits, layer, d, tok):
    m = proj.shape[0]
    depth = lb_logits.shape[0]
    tt = tok.seq
    tc = _tile(d, 512)
    nb = d // tc
    out = jax.ShapeDtypeStruct((m, d), F32)
    return pl.pallas_call(
        functools.partial(_hgrn_prep_kernel, layer=layer, tt=tt),
        grid=(m // tt, nb),
        in_specs=[
            pl.BlockSpec((tt, tc), lambda i, j: (i, 3 * nb + j)),
            pl.BlockSpec((tt, tc), lambda i, j: (i, 4 * nb + j)),
            pl.BlockSpec((depth, tc), lambda i, j: (0, j)),
        ],
        out_specs=[pl.BlockSpec((tt, tc), lambda i, j: (i, j))] * 4,
        out_shape=[out] * 4,
        compiler_params=_params("parallel", "parallel"),
        name="hgrn_prep",
    )(proj, proj, lb_logits)


def _gla_offdiag_pairs(c):
    pairs = []
    size = c
    while size > SUB:
        half = size // 2
        for start in range(0, c, size):
            pairs.append(((start + half, start + size), (start, start + half)))
        size = half
    return pairs


def _gla_operands(q, k, cb, rev):
    c, dk = q.shape
    zero_rows = lambda n: jnp.zeros((n, dk), BF16)

    def padded(x, rows):
        parts = ([zero_rows(rows[0])] if rows[0] else []) + [x] + ([zero_rows(c - rows[1])] if rows[1] < c else [])
        return jnp.concatenate(parts, axis=0) if len(parts) > 1 else x

    q_parts, k_parts = [], []
    for qr, kr in _gla_offdiag_pairs(c):
        if rev:
            qr, kr = kr, qr
        ref_row = qr[0] - 1 if not rev else qr[1]
        ref = cb[ref_row:ref_row + 1, :]
        qs, ks = slice(*qr), slice(*kr)
        q_parts.append(padded((q[qs] * jnp.exp(cb[qs] - ref)).astype(BF16), qr))
        k_parts.append(padded((k[ks] * jnp.exp(ref - cb[ks])).astype(BF16), kr))
    refs = []
    for blk in range(c // SUB):
        lo, hi = blk * SUB, (blk + 1) * SUB
        if not rev:
            ref = cb[lo - 1:lo, :] if blk > 0 else jnp.zeros_like(cb[0:1, :])
        else:
            ref = cb[hi:hi + 1, :] if hi < c else jnp.zeros_like(cb[0:1, :])
        refs.append(jnp.broadcast_to(ref, (SUB, dk)))
    ref_d = jnp.concatenate(refs, axis=0)
    q_diag = (q * jnp.exp(cb - ref_d)).astype(BF16)
    k_diag = (k * jnp.exp(jnp.minimum(ref_d - cb, EXP_CLAMP))).astype(BF16)
    return jnp.concatenate(q_parts, axis=1), jnp.concatenate(k_parts, axis=1), q_diag, k_diag


def _gla_diag_mask(c, rev):
    ri = lax.broadcasted_iota(jnp.int32, (c, c), 0)
    ci = lax.broadcasted_iota(jnp.int32, (c, c), 1)
    same_block = (ri // SUB) == (ci // SUB)
    return jnp.logical_and(same_block, (ri <= ci) if rev else (ri >= ci))


def _gla_scan_kernel(qf_ref, kf_ref, vf_ref, cf_ref, qb_ref, kb_ref, vb_ref, cb_ref, s0_ref,
                     of_ref, ob_ref, ns_ref, s_ref, *, tok, q_silu, scale, hb, dk, dv):
    n = pl.program_id(1)
    tt = tok.seq
    nch = tt // CHUNK
    is_ctx, dec_start = _seq_flags(n, tok)
    streams = [(d, h) for d in range(2) for h in range(hb)]

    @pl.when(is_ctx)
    def _():
        s_ref[...] = jnp.zeros_like(s_ref)

    @pl.when(dec_start)
    def _():
        for d, h in streams:
            s_ref[d, h] = s0_ref[0, 0, d, h].T

    dirs = ((qf_ref, kf_ref, vf_ref, cf_ref, of_ref), (qb_ref, kb_ref, vb_ref, cb_ref, ob_ref))
    ops = {}
    for d, (q_ref, k_ref, v_ref, c_ref, _) in enumerate(dirs):
        rev = d == 1
        for h in range(hb):
            q = q_ref[:, h * dk:(h + 1) * dk].astype(F32)
            if q_silu:
                q = q * _sigmoid(q)
            q = q * scale
            k = k_ref[:, h * dk:(h + 1) * dk].astype(F32)
            v = v_ref[:, h * dv:(h + 1) * dv].astype(BF16)
            cb = c_ref[:, h * dk:(h + 1) * dk]
            for c in range(nch):
                rows = slice(c * CHUNK, (c + 1) * CHUNK)
                qc, kc, cbc = q[rows], k[rows], cb[rows]
                end = cbc[0:1, :] if rev else cbc[CHUNK - 1:CHUNK, :]
                ops[d, h, c] = dict(
                    intra=_gla_operands(qc, kc, cbc, rev), v=v[rows],
                    q_state=(qc * jnp.exp(cbc)).astype(BF16),
                    k_state=(kc * jnp.exp(end - cbc)).astype(BF16),
                    s_decay=jnp.exp(end))
    a_off = {key: _dot_nt(op["intra"][0], op["intra"][1]) for key, op in ops.items()}
    a_diag = {key: _dot_nt(op["intra"][2], op["intra"][3]) for key, op in ops.items()}
    masks = [_gla_diag_mask(CHUNK, rev) for rev in (False, True)]
    attn = {key: (a_off[key] + jnp.where(masks[key[0]], a_diag[key], 0.0)).astype(BF16) for key in ops}
    o_intra = {key: _dot(attn[key], op["v"]) for key, op in ops.items()}
    s_inc = {key: _dot_tn(op["v"], op["k_state"]) for key, op in ops.items()}

    state = {key: s_ref[key] for key in streams}
    for step in range(nch):
        for (d, h), s in state.items():
            c = step if d == 0 else nch - 1 - step
            dirs[d][4][c * CHUNK:(c + 1) * CHUNK, h * dv:(h + 1) * dv] = (
                o_intra[d, h, c] + _dot_nt(ops[d, h, c]["q_state"], s.astype(BF16))).astype(BF16)
            state[d, h] = ops[d, h, c]["s_decay"] * s + s_inc[d, h, c]
    for key, s in state.items():
        s_ref[key] = s

    @pl.when(is_ctx)
    def _():
        for d, h in streams:
            ns_ref[0, d, h] = s_ref[d, h].T


def _gla_scan(q_src, q_blk, kf_src, kf_blk, kb_src, kb_blk, v_src, v_blk, cb_f, cb_b, s0, j, tok,
              *, heads, dk, dv, q_silu, hb):
    m = q_src.shape[0]
    tt = tok.seq
    assert heads % hb == 0 and q_blk % hb == 0 and kf_blk % hb == 0 and kb_blk % hb == 0 and v_blk % hb == 0
    bwd = lambda n: _bwd_block(n, tok)
    fwd = lambda n: n

    def stream(blk, k_blk):
        return [
            pl.BlockSpec((tt, hb * dk), lambda h, n: (blk(n), q_blk // hb + h)),
            pl.BlockSpec((tt, hb * dk), lambda h, n: (blk(n), k_blk // hb + h)),
            pl.BlockSpec((tt, hb * dv), lambda h, n: (blk(n), v_blk // hb + h)),
            pl.BlockSpec((tt, hb * dk), lambda h, n: (blk(n), h)),
        ]

    n_batch = tok.n_ctx_blocks
    return pl.pallas_call(
        functools.partial(_gla_scan_kernel, tok=tok, q_silu=q_silu, scale=dk ** -0.5, hb=hb, dk=dk, dv=dv),
        grid=(heads // hb, tok.n_blocks),
        in_specs=stream(fwd, kf_blk) + stream(bwd, kb_blk) + [
            pl.BlockSpec((1, 1, 2, hb, dk, dv), lambda h, n: (_dec_batch(n, tok), j, 0, h, 0, 0)),
        ],
        out_specs=[
            pl.BlockSpec((tt, hb * dv), lambda h, n: (n, h)),
            pl.BlockSpec((tt, hb * dv), lambda h, n: (bwd(n), h)),
            pl.BlockSpec((1, 2, hb, dk, dv), lambda h, n: (_state_block(n, tok), 0, h, 0, 0)),
        ],
        out_shape=[
            jax.ShapeDtypeStruct((m, heads * dv), BF16),
            jax.ShapeDtypeStruct((m, heads * dv), BF16),
            jax.ShapeDtypeStruct((n_batch, 2, heads, dk, dv), F32),
        ],
        scratch_shapes=[pltpu.VMEM((2, hb, dv, dk), F32)],
        compiler_params=_params("parallel", "arbitrary"),
        name="gla_scan",
    )(q_src, kf_src, v_src, cb_f, q_src, kb_src, v_src, cb_b, s0)


def kernel(x_prompt, x_sample, c, state_deltanet, state_gla, state_hgrn, c_ctx, ada_w, ada_b, norm1, norm2, final_norm, w_up, w_down, dn_w_in, dn_conv, dn_A_log, dn_dt_bias, dn_norm, dn_w_out, gla_w_in, gla_w_a2, gla_b_a, gla_norm, gla_w_out, hgrn_w_in, hgrn_lb_logits, hgrn_norm, hgrn_w_out):
    batch, seq, d = x_prompt.shape
    dec_batch, dec_seq, _ = x_sample.shape
    depth = ada_w.shape[0]
    tok = Tokens(m_ctx=batch * seq, seq=seq, dec_batch=dec_batch, dec_seq=dec_seq)
    assert seq % CHUNK == 0 and dec_seq % seq == 0 and seq % GRID_W == 0

    dn_hv, dn_dk, dn_dv = state_deltanet.shape[3:]
    dn_vd = dn_hv * dn_dv
    n_conv = dn_conv.shape[2]
    dn_kd = (n_conv - dn_vd) // 2
    dn_hk = dn_kd // dn_dk
    gla_h, gla_dk, gla_dv = state_gla.shape[3:]
    gla_kd, gla_vd = gla_h * gla_dk, gla_h * gla_dv
    hg_h, hg_dk, hg_dv = state_hgrn.shape[3:]

    x = jnp.concatenate([x_prompt.reshape(tok.m_ctx, d), x_sample.reshape(dec_batch * dec_seq, d)], axis=0)
    n_rows = 1 + dec_batch
    pad = (-n_rows) % 8
    cond = jnp.concatenate([c_ctx[None, :], c, jnp.zeros((pad, d), F32)], axis=0)
    mod_all = _mod_table(cond, ada_w, ada_b).reshape(depth, n_rows + pad, 6, d)

    new_states = ([], [], [])
    for i in range(depth):
        kind, j = i % N_MIXERS, i // N_MIXERS
        mod = mod_all[i]
        if kind == 0:
            w_in = dn_w_in[j]
            proj = _proj(x, norm1[i], mod, w_in[:, :n_conv + dn_vd].astype(BF16), tok)
            raw = _proj(x, norm1[i], mod, w_in[:, n_conv + dn_vd:], tok, precise=True)
            qkv = _dn_prep(proj, dn_conv[j], n_conv, dn_kd, dn_dk, tok)
            gates, gates_t = _dn_gates(raw, dn_A_log[j], dn_dt_bias[j], tok)
            o_f, o_b, ns = _dn_scan(qkv, gates, gates_t, state_deltanet, j, tok,
                                    hk_n=dn_hk, hv=dn_hv, dk=dn_dk, dv=dn_dv)
            x = _out_proj(o_f, o_b, proj, n_conv // dn_vd, dn_norm[j], x, mod, dn_w_out[j].astype(BF16), tok)
        elif kind == 1:
            w_in = gla_w_in[j]
            n_main = 2 * gla_kd + 2 * gla_vd
            proj = _proj(x, norm1[i], mod, w_in[:, :n_main].astype(BF16), tok)
            a1 = _proj(x, norm1[i], mod, w_in[:, n_main:], tok, precise=True)
            cb_f, cb_b = _gla_prep(a1, gla_w_a2[j], gla_b_a[j], tok)
            o_f, o_b, ns = _gla_scan(proj, 0, proj, gla_h, proj, gla_h, proj, (2 * gla_kd) // gla_dv,
                                     cb_f, cb_b, state_gla, j, tok,
                                     heads=gla_h, dk=gla_dk, dv=gla_dv, q_silu=False, hb=1)
            x = _out_proj(o_f, o_b, proj, (2 * gla_kd + gla_vd) // gla_vd, gla_norm[j], x, mod,
                          gla_w_out[j].astype(BF16), tok)
        else:
            proj = _proj(x, norm1[i], mod, hgrn_w_in[j].astype(BF16), tok)
            k_f, k_b, cb_f, cb_b = _hgrn_prep(proj, hgrn_lb_logits, i, d, tok)
            o_f, o_b, ns = _gla_scan(proj, 0, k_f, 0, k_b, 0, proj, d // hg_dv, cb_f, cb_b, state_hgrn, j, tok,
                                     heads=hg_h, dk=hg_dk, dv=hg_dv, q_silu=True, hb=min(hg_h, 4))
            x = _out_proj(o_f, o_b, proj, 2, hgrn_norm[j], x, mod, hgrn_w_out[j].astype(BF16), tok)
        new_states[kind].append(ns)
        x = _mlp(x, norm2[i], mod, w_up[i].astype(BF16), w_down[i].astype(BF16), final_norm, tok,
                 final=(i == depth - 1))

    y_prompt = x[:tok.m_ctx].reshape(batch, seq, d)
    y_sample = x[tok.m_ctx:].reshape(dec_batch, dec_seq, d)
    return (y_prompt, y_sample) + tuple(jnp.stack(s, axis=1) for s in new_states)
```

```python
import functools
from typing import NamedTuple

import jax
import jax.numpy as jnp
from jax import lax
from jax.experimental import pallas as pl
from jax.experimental.pallas import tpu as pltpu

F32 = jnp.float32
BF16 = jnp.bfloat16

EPS = 1e-6
GRID_W = 64
CONV_K = 5
GLA_RANK = 16
GLA_TAU = 16.0
N_MIXERS = 3
CHUNK = 64
SUB = 16
EXP_CLAMP = 80.0
LANES = 128
V7X_MXU_COLS = 256
V7X_VMEM_LIMIT_BYTES = 56 * 1024 * 1024
HIGHEST = lax.Precision.HIGHEST


class Tokens(NamedTuple):
    m_ctx: int
    seq: int
    dec_batch: int
    dec_seq: int

    @property
    def m(self):
        return self.m_ctx + self.dec_batch * self.dec_seq

    @property
    def n_ctx_blocks(self):
        return self.m_ctx // self.seq

    @property
    def blocks_per_dec(self):
        return self.dec_seq // self.seq

    @property
    def n_blocks(self):
        return self.m // self.seq


def _tile(n, pref):
    t = min(n, pref)
    while n % t:
        t //= 2
    return t


def _token_tile(tok, pref):
    t = pref
    while tok.m_ctx % t or tok.dec_seq % t:
        t //= 2
    return t


def _params(*sem):
    return pltpu.CompilerParams(dimension_semantics=sem, vmem_limit_bytes=V7X_VMEM_LIMIT_BYTES)


def _mod_row(i, tm, tok):
    start = i * tm
    return jnp.where(start < tok.m_ctx, 0, 1 + (start - tok.m_ctx) // tok.dec_seq)


def _sigmoid(x):
    return 1.0 / (1.0 + jnp.exp(-x))


def _softplus(x):
    return jnp.maximum(x, 0.0) + jnp.log(1.0 + jnp.exp(-jnp.abs(x)))


def _norm_mod(x, nw, shift, scale):
    ms = jnp.mean(x * x, axis=-1, keepdims=True)
    return x * lax.rsqrt(ms + EPS) * nw * (1.0 + scale) + shift


def _dot(a, b):
    return jnp.dot(a, b, preferred_element_type=F32)


def _dot_nt(a, b):
    return lax.dot_general(a, b, (((1,), (1,)), ((), ())), preferred_element_type=F32)


def _dot_tn(a, b):
    return lax.dot_general(a, b, (((0,), (0,)), ((), ())), preferred_element_type=F32)


def _split_dot(tri, x):
    hi = x.astype(BF16)
    lo = (x - hi.astype(F32)).astype(BF16)
    return _dot(tri, hi) + _dot(tri, lo)


def _split_matmul(a, b):
    m = a.shape[0]
    ah = a.astype(BF16)
    al = (a - ah.astype(F32)).astype(BF16)
    bh = b.astype(BF16)
    bl = (b - bh.astype(F32)).astype(BF16)
    t = _dot(jnp.concatenate([ah, al], axis=0), bh)
    return t[:m] + t[m:] + _dot(ah, bl)


def _tri(n, upper):
    r = lax.broadcasted_iota(jnp.int32, (n, n), 0)
    c = lax.broadcasted_iota(jnp.int32, (n, n), 1)
    return jnp.where((r <= c) if upper else (r >= c), 1.0, 0.0).astype(BF16)


def _mod_kernel(c_ref, w_ref, b_ref, o_ref):
    c = c_ref[...]
    s = c * _sigmoid(c)
    o_ref[0] = jnp.dot(s, w_ref[0], precision=HIGHEST, preferred_element_type=F32) + b_ref[0]


def _mod_table(cond, ada_w, ada_b):
    depth, d, n = ada_w.shape
    rows = cond.shape[0]
    tn = _tile(n, 1024)
    return pl.pallas_call(
        _mod_kernel,
        grid=(depth, n // tn),
        in_specs=[
            pl.BlockSpec((rows, d), lambda l, j: (0, 0)),
            pl.BlockSpec((1, d, tn), lambda l, j: (l, 0, j)),
            pl.BlockSpec((1, 1, tn), lambda l, j: (l, 0, j)),
        ],
        out_specs=pl.BlockSpec((1, rows, tn), lambda l, j: (l, 0, j)),
        out_shape=jax.ShapeDtypeStruct((depth, rows, n), F32),
        compiler_params=_params("parallel", "parallel"),
        name="mod_table",
    )(cond, ada_w, ada_b.reshape(depth, 1, n))


def _proj_kernel(x_ref, nw_ref, mod_ref, w_ref, o_ref, h_ref, *, precise):
    @pl.when(pl.program_id(1) == 0)
    def _():
        h = _norm_mod(x_ref[...], nw_ref[...], mod_ref[0, 0:1, :], mod_ref[0, 1:2, :])
        h_ref[...] = h.astype(h_ref.dtype)

    if precise:
        o_ref[...] = jnp.dot(h_ref[...], w_ref[...], precision=HIGHEST, preferred_element_type=F32)
    else:
        o_ref[...] = _dot(h_ref[...], w_ref[...]).astype(o_ref.dtype)


def _proj(x, nw, mod, w, tok, *, precise=False, tm=1024, tn=1024):
    m, d = x.shape
    n = w.shape[1]
    tm = _token_tile(tok, tm)
    tn = _tile(n, tn)
    return pl.pallas_call(
        functools.partial(_proj_kernel, precise=precise),
        grid=(m // tm, n // tn),
        in_specs=[
            pl.BlockSpec((tm, d), lambda i, j: (i, 0)),
            pl.BlockSpec((1, d), lambda i, j: (0, 0)),
            pl.BlockSpec((1, 6, d), lambda i, j: (_mod_row(i, tm, tok), 0, 0)),
            pl.BlockSpec((d, tn), lambda i, j: (0, j)),
        ],
        out_specs=pl.BlockSpec((tm, tn), lambda i, j: (i, j)),
        out_shape=jax.ShapeDtypeStruct((m, n), F32 if precise else BF16),
        scratch_shapes=[pltpu.VMEM((tm, d), F32 if precise else BF16)],
        compiler_params=_params("parallel", "arbitrary"),
        name="proj_precise" if precise else "proj",
    )(x, nw.reshape(1, d), mod, w)


def _mlp_kernel(x_ref, nw_ref, mod_ref, wu_ref, wd_ref, fw_ref, o_ref, h_ref, acc_ref, *, final):
    f = pl.program_id(1)

    @pl.when(f == 0)
    def _():
        h = _norm_mod(x_ref[...], nw_ref[...], mod_ref[0, 3:4, :], mod_ref[0, 4:5, :])
        h_ref[...] = h.astype(BF16)
        acc_ref[...] = jnp.zeros_like(acc_ref)

    a = jnp.maximum(_dot(h_ref[...], wu_ref[...]), 0.0)
    acc_ref[...] += _dot((a * a).astype(BF16), wd_ref[...])

    @pl.when(f == pl.num_programs(1) - 1)
    def _():
        y = x_ref[...] + mod_ref[0, 5:6, :] * acc_ref[...]
        if final:
            ms = jnp.mean(y * y, axis=-1, keepdims=True)
            y = y * lax.rsqrt(ms + EPS) * fw_ref[...]
        o_ref[...] = y


def _mlp(x, nw, mod, w_up, w_down, final_w, tok, *, final, tm=512, tf=1024):
    m, d = x.shape
    ff = w_up.shape[1]
    tm = _token_tile(tok, tm)
    tf = _tile(ff, tf)
    return pl.pallas_call(
        functools.partial(_mlp_kernel, final=final),
        grid=(m // tm, ff // tf),
        in_specs=[
            pl.BlockSpec((tm, d), lambda i, f: (i, 0)),
            pl.BlockSpec((1, d), lambda i, f: (0, 0)),
            pl.BlockSpec((1, 6, d), lambda i, f: (_mod_row(i, tm, tok), 0, 0)),
            pl.BlockSpec((d, tf), lambda i, f: (0, f)),
            pl.BlockSpec((tf, d), lambda i, f: (f, 0)),
            pl.BlockSpec((1, d), lambda i, f: (0, 0)),
        ],
        out_specs=pl.BlockSpec((tm, d), lambda i, f: (i, 0)),
        out_shape=jax.ShapeDtypeStruct((m, d), F32),
        scratch_shapes=[pltpu.VMEM((tm, d), BF16), pltpu.VMEM((tm, d), F32)],
        compiler_params=_params("parallel", "arbitrary"),
        name="mlp_final" if final else "mlp",
    )(x, nw.reshape(1, d), mod, w_up, w_down, final_w.reshape(1, d))


def _out_kernel(of_ref, ob_ref, g_ref, nw_ref, x_ref, mod_ref, w_ref, o_ref, y0_ref, y1_ref, *, n_heads, dh):
    i = pl.program_id(0)
    d = w_ref.shape[1]
    n_chunks = d // V7X_MXU_COLS
    heads_per_chunk = -(-n_heads // n_chunks)

    @pl.when(i == 0)
    def _():
        y1_ref[...] = jnp.zeros_like(y1_ref)

    def body(y_write, y_read):
        ones = jnp.full((dh, LANES), 1.0 / dh, BF16) if dh & (dh - 1) == 0 else None
        for c in range(n_chunks):
            cols = slice(c * V7X_MXU_COLS, (c + 1) * V7X_MXU_COLS)
            o_ref[:, cols] = x_ref[:, cols] + mod_ref[0, 2:3, cols] * _dot(y_read[...], w_ref[:, cols])
            for h in range(c * heads_per_chunk, min((c + 1) * heads_per_chunk, n_heads)):
                sl = slice(h * dh, (h + 1) * dh)
                o = of_ref[:, sl].astype(F32) + ob_ref[:, sl].astype(F32)
                if ones is None:
                    ms = _dot((o * o).astype(BF16), jnp.ones((dh, LANES), BF16)) * (1.0 / dh)
                else:
                    ms = _dot((o * o).astype(BF16), ones)
                inv = lax.rsqrt(ms + EPS)
                half_g = 0.5 * g_ref[:, sl].astype(F32)
                silu = half_g + half_g * jnp.tanh(half_g)
                y = o * nw_ref[...] * silu
                for k in range(dh // LANES):
                    lo = h * dh + k * LANES
                    y_write[:, lo:lo + LANES] = (y[:, k * LANES:(k + 1) * LANES] * inv).astype(BF16)

    @pl.when(i % 2 == 0)
    def _():
        body(y0_ref, y1_ref)

    @pl.when(i % 2 == 1)
    def _():
        body(y1_ref, y0_ref)


def _out_proj(o_f, o_b, gate_src, gate_blk, norm_w, x, mod, w_out, tok, *, tm=256):
    m, vd = o_f.shape
    d = x.shape[1]
    dh = norm_w.shape[0]
    tm = _token_tile(tok, tm)
    n_tiles = m // tm
    assert d % V7X_MXU_COLS == 0 and dh % LANES == 0
    cur = lambda i: jnp.minimum(i, n_tiles - 1)
    prev = lambda i: jnp.maximum(i - 1, 0)
    return pl.pallas_call(
        functools.partial(_out_kernel, n_heads=vd // dh, dh=dh),
        grid=(n_tiles + 1,),
        in_specs=[
            pl.BlockSpec((tm, vd), lambda i: (cur(i), 0)),
            pl.BlockSpec((tm, vd), lambda i: (cur(i), 0)),
            pl.BlockSpec((tm, vd), lambda i: (cur(i), gate_blk)),
            pl.BlockSpec((1, dh), lambda i: (0, 0)),
            pl.BlockSpec((tm, d), lambda i: (prev(i), 0)),
            pl.BlockSpec((1, 6, d), lambda i: (_mod_row(prev(i), tm, tok), 0, 0)),
            pl.BlockSpec((vd, d), lambda i: (0, 0), pipeline_mode=pl.Buffered(1)),
        ],
        out_specs=pl.BlockSpec((tm, d), lambda i: (prev(i), 0)),
        out_shape=jax.ShapeDtypeStruct((m, d), F32),
        scratch_shapes=[pltpu.VMEM((tm, vd), BF16), pltpu.VMEM((tm, vd), BF16)],
        compiler_params=_params("arbitrary"),
        name="out_proj",
    )(o_f, o_b, gate_src, norm_w.reshape(1, dh), x, mod, w_out)


def _bwd_block(n, tok):
    nc, per = tok.n_ctx_blocks, tok.blocks_per_dec
    r = jnp.maximum(n - nc, 0)
    return jnp.where(n < nc, n, nc + (r // per) * per + (per - 1 - r % per))


def _dec_batch(n, tok):
    return jnp.maximum(n - tok.n_ctx_blocks, 0) // tok.blocks_per_dec


def _state_block(n, tok):
    return jnp.minimum(n, tok.n_ctx_blocks - 1)


def _seq_flags(n, tok):
    nc, per = tok.n_ctx_blocks, tok.blocks_per_dec
    is_ctx = n < nc
    r = jnp.maximum(n - nc, 0) % per
    return is_ctx, jnp.logical_and(jnp.logical_not(is_ctx), r == 0)


def _dn_prep_kernel(x_ref, shift_ref, w_ref, o_ref, *, n_q_blocks, dk):
    j = pl.program_id(1)
    x = x_ref[...]
    tc = x.shape[1]
    acc = None
    for s in range(CONV_K):
        xs = x.astype(F32) if s == CONV_K // 2 else _dot(shift_ref[0, s], x)
        term = xs * w_ref[s:s + 1, :]
        acc = term if acc is None else acc + term
    y = acc * _sigmoid(acc)

    def normed(scale):
        ones = jnp.ones((dk, dk), BF16)
        for g in range(tc // dk):
            blk = y[:, g * dk:(g + 1) * dk]
            ss = _dot((blk * blk).astype(BF16), ones)
            o_ref[:, g * dk:(g + 1) * dk] = (blk * (lax.rsqrt(ss + EPS) * scale)).astype(o_ref.dtype)

    @pl.when(j < n_q_blocks)
    def _():
        normed(dk ** -0.5)

    @pl.when(jnp.logical_and(j >= n_q_blocks, j < 2 * n_q_blocks))
    def _():
        normed(1.0)

    @pl.when(j >= 2 * n_q_blocks)
    def _():
        o_ref[...] = y.astype(o_ref.dtype)


def _conv_shift_matrices(tt):
    t = jnp.arange(tt)[:, None]
    u = jnp.arange(tt)[None, :]
    mats = []
    for seg in (tt, GRID_W):
        same = (t // seg) == (u // seg)
        mats.append(jnp.stack([jnp.logical_and(u == t + s - CONV_K // 2, same) for s in range(CONV_K)]))
    return jnp.stack(mats).astype(BF16)


def _dn_prep(proj, conv_w, n_conv, kd, dk, tok):
    m = proj.shape[0]
    tt = tok.seq
    tc = _tile(kd, 1024)
    n_ctx = tok.n_ctx_blocks
    return pl.pallas_call(
        functools.partial(_dn_prep_kernel, n_q_blocks=kd // tc, dk=dk),
        grid=(m // tt, n_conv // tc),
        in_specs=[
            pl.BlockSpec((tt, tc), lambda i, j: (i, j)),
            pl.BlockSpec((1, CONV_K, tt, tt), lambda i, j: (jnp.where(i < n_ctx, 0, 1), 0, 0, 0)),
            pl.BlockSpec((CONV_K, tc), lambda i, j: (0, j)),
        ],
        out_specs=pl.BlockSpec((tt, tc), lambda i, j: (i, j)),
        out_shape=jax.ShapeDtypeStruct((m, n_conv), BF16),
        compiler_params=_params("parallel", "parallel"),
        name="dn_prep",
    )(proj, _conv_shift_matrices(tt), conv_w)


def _dn_gate_kernel(x_ref, a_ref, bias_ref, o_ref, gt_ref, *, hv, tt):
    x = x_ref[...]
    lanes = x.shape[1]
    lane = lax.broadcasted_iota(jnp.int32, (1, lanes), 1)
    is_fwd = lane < 3 * hv
    g = -a_ref[...] * _softplus(x + bias_ref[...])
    beta = pltpu.roll(_sigmoid(x), 2 * hv, 1)
    lower = _tri(CHUNK, upper=False)
    upper = _tri(CHUNK, upper=True)
    for c in range(tt // CHUNK):
        rows = slice(c * CHUNK, (c + 1) * CHUNK)
        gch = g[rows]
        gc = jnp.where(is_fwd, _split_dot(lower, gch), _split_dot(upper, gch))
        g_end = jnp.where(is_fwd, gc[CHUNK - 1:CHUNK, :], gc[0:1, :])
        eg = jnp.exp(gc)
        o_ref[0, rows, :] = beta[rows]
        o_ref[1, rows, :] = gc
        o_ref[2, rows, :] = eg
        o_ref[3, rows, :] = jnp.exp(g_end - gc)
        o_ref[4, rows, :] = beta[rows] * eg
    gt_ref[...] = o_ref[1].T


def _dn_gates(raw, a_log, dt_bias, tok):
    m, lanes = raw.shape
    hv = lanes // 4
    tt = tok.seq
    zeros = jnp.zeros((2 * hv,), F32)
    a_full = jnp.concatenate([zeros, jnp.exp(a_log).reshape(-1)]).reshape(1, lanes)
    bias_full = jnp.concatenate([zeros, dt_bias.reshape(-1)]).reshape(1, lanes)
    return pl.pallas_call(
        functools.partial(_dn_gate_kernel, hv=hv, tt=tt),
        grid=(m // tt,),
        in_specs=[
            pl.BlockSpec((tt, lanes), lambda i: (i, 0)),
            pl.BlockSpec((1, lanes), lambda i: (0, 0)),
            pl.BlockSpec((1, lanes), lambda i: (0, 0)),
        ],
        out_specs=[
            pl.BlockSpec((5, tt, lanes), lambda i: (0, i, 0)),
            pl.BlockSpec((lanes, tt), lambda i: (0, i)),
        ],
        out_shape=[jax.ShapeDtypeStruct((5, m, lanes), F32), jax.ShapeDtypeStruct((lanes, m), F32)],
        compiler_params=_params("parallel"),
        name="dn_gates",
    )(raw, a_full, bias_full)


def _dn_scan_kernel(qf_ref, kf_ref, vf_ref, gf_ref, gtf_ref, qb_ref, kb_ref, vb_ref, gb_ref, gtb_ref, s0_ref,
                    of_ref, ob_ref, ns_ref, s_ref, u0_ref, wq_ref, attn_ref, kd_ref, ge_ref, *, tok, hv, dv):
    hk = pl.program_id(0)
    n = pl.program_id(1)
    tt = tok.seq
    nch = tt // CHUNK
    assert dv == 2 * CHUNK and CHUNK == 4 * SUB
    has_prev = n > 0
    is_ctx, dec_start = _seq_flags(jnp.maximum(n - 1, 0), tok)
    wslot = n % 2
    rslot = 1 - wslot
    scratch = (u0_ref, wq_ref, attn_ref, kd_ref, ge_ref)

    @pl.when(n == 0)
    def _():
        s_ref[...] = jnp.zeros_like(s_ref)
        for ref in scratch:
            ref[1] = jnp.zeros(ref.shape[1:], ref.dtype)

    @pl.when(jnp.logical_and(has_prev, is_ctx))
    def _():
        s_ref[...] = jnp.zeros_like(s_ref)

    @pl.when(jnp.logical_and(has_prev, dec_start))
    def _():
        s_ref[...] = s0_ref[0, 0]

    bf = lambda x: x.astype(BF16)
    lanes = gf_ref.shape[2]
    lane_iota = lax.broadcasted_iota(jnp.int32, (1, lanes), 1)
    ri = lax.broadcasted_iota(jnp.int32, (CHUNK, 2 * CHUNK), 0)
    li = lax.broadcasted_iota(jnp.int32, (CHUNK, 2 * CHUNK), 1)
    left = li < CHUNK
    ci = li % CHUNK
    same = [(ri // b) == (ci // b) for b in (SUB, 2 * SUB)]
    zero_b = jnp.zeros((CHUNK, 2 * CHUNK), BF16)

    def block_diag(xb):
        return jnp.concatenate([jnp.where(left, xb, zero_b), jnp.where(left, zero_b, xb)], axis=0)

    def pair_dot(a, b):
        return _dot(bf(a), block_diag(bf(b)))

    dirs = ((qf_ref, kf_ref, vf_ref, gf_ref, gtf_ref, of_ref), (qb_ref, kb_ref, vb_ref, gb_ref, gtb_ref, ob_ref))

    def prepare():
        ch = {}
        for d, (q_ref, k_ref, v_ref, g_ref, gt_ref, _) in enumerate(dirs):
            incl = (ri >= ci) if d == 0 else (ri <= ci)
            strict = (ri > ci) if d == 0 else (ri < ci)
            end_row = CHUNK - 1 if d == 0 else 0
            q = q_ref[...]
            k = k_ref[...]
            kf32 = k.astype(F32)
            qf32 = q.astype(F32)
            cols, g_rows, vs = [], [], []
            for e in range(2):
                a_lane = 2 * hv + d * hv + 2 * hk + e
                sel = lane_iota == a_lane
                cols.append([jnp.sum(jnp.where(sel, g_ref[i], 0.0), axis=1, keepdims=True) for i in range(5)])
                g_rows.append(gt_ref[pl.ds(a_lane, 1), :])
                vs.append(v_ref[:, e * dv:(e + 1) * dv].astype(F32))
            for c in range(nch):
                rows = slice(c * CHUNK, (c + 1) * CHUNK)
                pair = lambda i: jnp.where(left, cols[0][i][rows], cols[1][i][rows])
                k2 = jnp.concatenate([k[rows], k[rows]], axis=0)
                kk = _dot_nt(k[rows], k2)
                qk = _dot_nt(q[rows], k2)
                g_row = jnp.concatenate([g_rows[0][:, rows], g_rows[1][:, rows]], axis=1)
                diff = pair(1) - g_row
                decay = jnp.where(incl, jnp.exp(jnp.where(incl, diff, 0.0)), 0.0)
                p = jnp.where(strict, -(pair(0) * kk * decay), 0.0)
                attn_ref[wslot, d, c] = bf(qk * decay)
                for e in range(2):
                    beta_c, _, eg_c, ekd_c, beg_c = cols[e]
                    qg = qf32[rows] * eg_c[rows]
                    kd_ref[wslot, d, e, c] = bf(kf32[rows] * ekd_c[rows])
                    wq_ref[wslot, d, e, c, CHUNK:, :] = bf(qg)
                    g_end = eg_c[c * CHUNK + end_row:c * CHUNK + end_row + 1, :]
                    ge_ref[wslot, d, e, c] = jnp.broadcast_to(g_end, ge_ref.shape[4:])
                ch[d, c] = dict(
                    pd=jnp.where(same[0], p, 0.0),
                    p32=jnp.where(jnp.logical_and(same[1], jnp.logical_not(same[0])), p, 0.0),
                    p64=jnp.where(same[1], 0.0, p),
                    r=[jnp.concatenate([cols[e][0][rows] * vs[e][rows], cols[e][4][rows] * kf32[rows]], axis=1)
                       for e in range(2)])
            yield
        for c in ch.values():
            c["q"] = pair_dot(c["pd"], c["pd"])
            c["xm"] = c["pd"]
        yield
        for it in range(2):
            for c in ch.values():
                q, xm = c["q"], c["xm"]
                both = _dot(bf(q), jnp.concatenate([block_diag(bf(xm)), block_diag(bf(q))], axis=1))
                c["xm"] = xm + q + both[:, :2 * CHUNK]
                c["q"] = both[:, 2 * CHUNK:]
            yield
        for c in ch.values():
            q, xm = c.pop("q"), c["xm"]
            c["xm"] = xm + q + pair_dot(q, xm)
        yield
        for level in ("p32", "p64"):
            for c in ch.values():
                c["y"] = c[level] + pair_dot(c[level], c["xm"])
            yield
            for c in ch.values():
                xm, y = c["xm"], c.pop("y")
                c["xm"] = xm + y + pair_dot(xm, y)
            yield
        zero_r = jnp.zeros((CHUNK, 2 * dv), BF16)
        for (d, cidx), c in ch.items():
            r0, r1 = c["r"]
            rhs = jnp.concatenate([jnp.concatenate([bf(r0), zero_r], axis=1),
                                   jnp.concatenate([zero_r, bf(r1)], axis=1)], axis=0)
            big = _dot(bf(c["xm"]), rhs)
            sol = [r0 + big[:, :2 * dv], r1 + big[:, 2 * dv:]]
            u0_ref[wslot, d, cidx] = jnp.concatenate([sol[0][:, :dv], sol[1][:, :dv]], axis=1)
            for e in range(2):
                wq_ref[wslot, d, e, cidx, :CHUNK, :] = bf(sol[e][:, dv:])
        yield

    def recur():
        state = {(d, e): s_ref[d, e] for d in range(2) for e in range(2)}
        zero_u = jnp.zeros((CHUNK, dv), BF16)
        for step in range(nch):
            chunk = lambda d: step if d == 0 else nch - 1 - step
            ws = {(d, e): _dot(wq_ref[rslot, d, e, chunk(d)], bf(s)) for (d, e), s in state.items()}
            yield
            for d in range(2):
                c = chunk(d)
                u0 = u0_ref[rslot, d, c]
                ub = [bf(u0[:, e * dv:(e + 1) * dv] - ws[d, e][:CHUNK]) for e in range(2)]
                u2 = jnp.concatenate([jnp.concatenate([ub[0], zero_u], axis=1),
                                      jnp.concatenate([zero_u, ub[1]], axis=1)], axis=0)
                o = _dot(attn_ref[rslot, d, c], u2) + jnp.concatenate([ws[d, 0][CHUNK:], ws[d, 1][CHUNK:]], axis=1)
                dirs[d][5][c * CHUNK:(c + 1) * CHUNK, :] = bf(o)
                for e in range(2):
                    state[d, e] = (ge_ref[rslot, d, e, c][0:1, :] * state[d, e]
                                   + _dot_tn(kd_ref[rslot, d, e, c], ub[e]))
            yield
        for (d, e), s in state.items():
            s_ref[d, e] = s
        yield

    halves = [recur(), prepare()]
    while halves:
        for g in list(halves):
            if next(g, "done") == "done":
                halves.remove(g)

    @pl.when(jnp.logical_and(has_prev, is_ctx))
    def _():
        ns_ref[0, 0] = s_ref[...]


def _dn_scan_kernel_aliased(*refs, **kw):
    n_in = 11
    return _dn_scan_kernel(*refs[:n_in], *refs[n_in + 1:], **kw)


def _dn_scan(qkv, gates, gates_t, s0, j, new_states, tok, *, hk_n, hv, dk, dv):
    m = qkv.shape[0]
    tt = tok.seq
    rep = hv // hk_n
    assert rep == 2
    lanes = gates.shape[2]
    kd = hk_n * dk
    nb = tok.n_blocks
    nch = tt // CHUNK
    fwd = lambda n: jnp.minimum(n, nb - 1)
    bwd = lambda n: _bwd_block(jnp.minimum(n, nb - 1), tok)
    prev = lambda n: jnp.maximum(n - 1, 0)

    def stream(blk):
        return [
            pl.BlockSpec((tt, dk), lambda h, n: (blk(n), h)),
            pl.BlockSpec((tt, dk), lambda h, n: (blk(n), kd // dk + h)),
            pl.BlockSpec((tt, rep * dv), lambda h, n: (blk(n), (2 * kd) // (rep * dv) + h)),
            pl.BlockSpec((5, tt, lanes), lambda h, n: (0, blk(n), 0)),
            pl.BlockSpec((lanes, tt), lambda h, n: (0, blk(n))),
        ]

    n_batch = tok.n_ctx_blocks
    n_layers = s0.shape[1]
    aliased = new_states is not None
    extra_in = [pl.BlockSpec(memory_space=pl.ANY)] if aliased else []
    extra_args = (new_states,) if aliased else ()
    return pl.pallas_call(
        functools.partial(_dn_scan_kernel_aliased if aliased else _dn_scan_kernel, tok=tok, hv=hv, dv=dv),
        grid=(hk_n, nb + 1),
        in_specs=stream(fwd) + stream(bwd) + [
            pl.BlockSpec((1, 1, 2, rep, dk, dv), lambda h, n: (_dec_batch(prev(n), tok), j, 0, h, 0, 0)),
        ] + extra_in,
        out_specs=[
            pl.BlockSpec((tt, rep * dv), lambda h, n: (prev(n), h)),
            pl.BlockSpec((tt, rep * dv), lambda h, n: (_bwd_block(prev(n), tok), h)),
            pl.BlockSpec((1, 1, 2, rep, dk, dv), lambda h, n: (_state_block(prev(n), tok), j, 0, h, 0, 0)),
        ],
        out_shape=[
            jax.ShapeDtypeStruct((m, hv * dv), BF16),
            jax.ShapeDtypeStruct((m, hv * dv), BF16),
            jax.ShapeDtypeStruct((n_batch, n_layers, 2, hv, dk, dv), F32),
        ],
        input_output_aliases={11: 2} if aliased else {},
        scratch_shapes=[
            pltpu.VMEM((2, rep, dk, dv), F32),
            pltpu.VMEM((2, 2, nch, CHUNK, rep * dv), F32),
            pltpu.VMEM((2, 2, rep, nch, 2 * CHUNK, dk), BF16),
            pltpu.VMEM((2, 2, nch, CHUNK, rep * CHUNK), BF16),
            pltpu.VMEM((2, 2, rep, nch, CHUNK, dk), BF16),
            pltpu.VMEM((2, 2, rep, nch, 8, dv), F32),
        ],
        compiler_params=_params("parallel", "arbitrary"),
        name="dn_scan",
    )(qkv, qkv, qkv, gates, gates_t, qkv, qkv, qkv, gates, gates_t, s0, *extra_args)


def _chunk_cums(la_f, la_b, cbf_ref, cbb_ref, tt):
    lower = _tri(CHUNK, upper=False)
    upper = _tri(CHUNK, upper=True)
    for c in range(tt // CHUNK):
        rows = slice(c * CHUNK, (c + 1) * CHUNK)
        cbf_ref[rows, :] = _split_dot(lower, la_f[rows])
        cbb_ref[rows, :] = _split_dot(upper, la_b[rows])


def _log_sigmoid(x):
    return jnp.minimum(x, 0.0) - jnp.log(1.0 + jnp.exp(-jnp.abs(x)))


def _gla_prep_kernel(a1_ref, w2_ref, b_ref, cbf_ref, cbb_ref, *, tt):
    a1 = a1_ref[...]
    la = [_log_sigmoid(_dot(a1, w2_ref[d]) + b_ref[d]) * (1.0 / GLA_TAU) for d in range(2)]
    _chunk_cums(la[0], la[1], cbf_ref, cbb_ref, tt)


def _gla_prep(a1, w_a2, b_a, tok):
    m, r2 = a1.shape
    kd = w_a2.shape[2]
    tt = tok.seq
    tc = _tile(kd, 512)
    zeros = jnp.zeros((GLA_RANK, kd), F32)
    w2 = jnp.stack([jnp.concatenate([w_a2[0], zeros]), jnp.concatenate([zeros, w_a2[1]])])
    out = jax.ShapeDtypeStruct((m, kd), F32)
    return pl.pallas_call(
        functools.partial(_gla_prep_kernel, tt=tt),
        grid=(m // tt, kd // tc),
        in_specs=[
            pl.BlockSpec((tt, r2), lambda i, j: (i, 0)),
            pl.BlockSpec((2, r2, tc), lambda i, j: (0, 0, j)),
            pl.BlockSpec((2, 1, tc), lambda i, j: (0, 0, j)),
        ],
        out_specs=[pl.BlockSpec((tt, tc), lambda i, j: (i, j))] * 2,
        out_shape=[out, out],
        compiler_params=_params("parallel", "parallel"),
        name="gla_prep",
    )(a1, w2, b_a.reshape(2, 1, kd))


def _hgrn_prep_kernel(ff_ref, fb_ref, lb_ref, kf_ref, kb_ref, cbf_ref, cbb_ref, *, layer, tt):
    logits = lb_ref[...]
    ex = jnp.exp(logits - jnp.max(logits, axis=0, keepdims=True))
    probs = ex / jnp.sum(ex, axis=0, keepdims=True)
    lb = jnp.sum(probs[1:layer + 1], axis=0, keepdims=True)
    la = []
    for f_ref, k_ref in ((ff_ref, kf_ref), (fb_ref, kb_ref)):
        fl = f_ref[...].astype(F32)
        sg = _sigmoid(fl)
        la.append(jnp.log(lb + (1.0 - lb) * sg))
        k_ref[...] = (1.0 - lb) * (1.0 - sg)
    _chunk_cums(la[0], la[1], cbf_ref, cbb_ref, tt)


def _hgrn_prep(proj, lb_logits, layer, d, tok):
    m = proj.shape[0]
    depth = lb_logits.shape[0]
    tt = tok.seq
    tc = _tile(d, 512)
    nb = d // tc
    out = jax.ShapeDtypeStruct((m, d), F32)
    return pl.pallas_call(
        functools.partial(_hgrn_prep_kernel, layer=layer, tt=tt),
        grid=(m // tt, nb),
        in_specs=[
            pl.BlockSpec((tt, tc), lambda i, j: (i, 3 * nb + j)),
            pl.BlockSpec((tt, tc), lambda i, j: (i, 4 * nb + j)),
            pl.BlockSpec((depth, tc), lambda i, j: (0, j)),
        ],
        out_specs=[pl.BlockSpec((tt, tc), lambda i, j: (i, j))] * 4,
        out_shape=[out] * 4,
        compiler_params=_params("parallel", "parallel"),
        name="hgrn_prep",
    )(proj, proj, lb_logits)


def _gla_offdiag_pairs(c):
    pairs = []
    size = c
    while size > SUB:
        half = size // 2
        for start in range(0, c, size):
            pairs.append(((start + half, start + size), (start, start + half)))
        size = half
    return pairs


def _gla_operands(q, k, cb, rev):
    c, dk = q.shape
    zero_rows = lambda n: jnp.zeros((n, dk), BF16)

    def padded(x, rows):
        parts = ([zero_rows(rows[0])] if rows[0] else []) + [x] + ([zero_rows(c - rows[1])] if rows[1] < c else [])
        return jnp.concatenate(parts, axis=0) if len(parts) > 1 else x

    q_parts, k_parts = [], []
    for qr, kr in _gla_offdiag_pairs(c):
        if rev:
            qr, kr = kr, qr
        ref_row = qr[0] - 1 if not rev else qr[1]
        ref = cb[ref_row:ref_row + 1, :]
        qs, ks = slice(*qr), slice(*kr)
        q_parts.append(padded((q[qs] * jnp.exp(cb[qs] - ref)).astype(BF16), qr))
        k_parts.append(padded((k[ks] * jnp.exp(ref - cb[ks])).astype(BF16), kr))
    refs = []
    for blk in range(c // SUB):
        lo, hi = blk * SUB, (blk + 1) * SUB
        if not rev:
            ref = cb[lo - 1:lo, :] if blk > 0 else jnp.zeros_like(cb[0:1, :])
        else:
            ref = cb[hi:hi + 1, :] if hi < c else jnp.zeros_like(cb[0:1, :])
        refs.append(jnp.broadcast_to(ref, (SUB, dk)))
    ref_d = jnp.concatenate(refs, axis=0)
    q_diag = (q * jnp.exp(cb - ref_d)).astype(BF16)
    k_diag = (k * jnp.exp(jnp.minimum(ref_d - cb, EXP_CLAMP))).astype(BF16)
    return jnp.concatenate(q_parts, axis=1), jnp.concatenate(k_parts, axis=1), q_diag, k_diag


def _gla_diag_mask(c, rev):
    ri = lax.broadcasted_iota(jnp.int32, (c, c), 0)
    ci = lax.broadcasted_iota(jnp.int32, (c, c), 1)
    same_block = (ri // SUB) == (ci // SUB)
    return jnp.logical_and(same_block, (ri <= ci) if rev else (ri >= ci))


def _gla_scan_kernel(qf_ref, kf_ref, vf_ref, cf_ref, qb_ref, kb_ref, vb_ref, cb_ref, s0_ref,
                     of_ref, ob_ref, ns_ref, s_ref, *, tok, q_silu, scale, hb, dk, dv):
    n = pl.program_id(1)
    tt = tok.seq
    nch = tt // CHUNK
    is_ctx, dec_start = _seq_flags(n, tok)
    streams = [(d, h) for d in range(2) for h in range(hb)]

    @pl.when(is_ctx)
    def _():
        s_ref[...] = jnp.zeros_like(s_ref)

    @pl.when(dec_start)
    def _():
        for d, h in streams:
            s_ref[d, h] = s0_ref[0, 0, d, h].T

    dirs = ((qf_ref, kf_ref, vf_ref, cf_ref, of_ref), (qb_ref, kb_ref, vb_ref, cb_ref, ob_ref))
    ops = {}
    for d, (q_ref, k_ref, v_ref, c_ref, _) in enumerate(dirs):
        rev = d == 1
        for h in range(hb):
            q = q_ref[:, h * dk:(h + 1) * dk].astype(F32)
            if q_silu:
                q = q * _sigmoid(q)
            q = q * scale
            k = k_ref[:, h * dk:(h + 1) * dk].astype(F32)
            v = v_ref[:, h * dv:(h + 1) * dv].astype(BF16)
            cb = c_ref[:, h * dk:(h + 1) * dk]
            for c in range(nch):
                rows = slice(c * CHUNK, (c + 1) * CHUNK)
                qc, kc, cbc = q[rows], k[rows], cb[rows]
                end = cbc[0:1, :] if rev else cbc[CHUNK - 1:CHUNK, :]
                ops[d, h, c] = dict(
                    intra=_gla_operands(qc, kc, cbc, rev), v=v[rows],
                    q_state=(qc * jnp.exp(cbc)).astype(BF16),
                    k_state=(kc * jnp.exp(end - cbc)).astype(BF16),
                    s_decay=jnp.exp(end))
    a_off = {key: _dot_nt(op["intra"][0], op["intra"][1]) for key, op in ops.items()}
    a_diag = {key: _dot_nt(op["intra"][2], op["intra"][3]) for key, op in ops.items()}
    masks = [_gla_diag_mask(CHUNK, rev) for rev in (False, True)]
    attn = {key: (a_off[key] + jnp.where(masks[key[0]], a_diag[key], 0.0)).astype(BF16) for key in ops}
    o_intra = {key: _dot(attn[key], op["v"]) for key, op in ops.items()}
    s_inc = {key: _dot_tn(op["v"], op["k_state"]) for key, op in ops.items()}

    state = {key: s_ref[key] for key in streams}
    for step in range(nch):
        for (d, h), s in state.items():
            c = step if d == 0 else nch - 1 - step
            dirs[d][4][c * CHUNK:(c + 1) * CHUNK, h * dv:(h + 1) * dv] = (
                o_intra[d, h, c] + _dot_nt(ops[d, h, c]["q_state"], s.astype(BF16))).astype(BF16)
            state[d, h] = ops[d, h, c]["s_decay"] * s + s_inc[d, h, c]
    for key, s in state.items():
        s_ref[key] = s

    @pl.when(is_ctx)
    def _():
        for d, h in streams:
            ns_ref[0, d, h] = s_ref[d, h].T


def _gla_scan(q_src, q_blk, kf_src, kf_blk, kb_src, kb_blk, v_src, v_blk, cb_f, cb_b, s0, j, tok,
              *, heads, dk, dv, q_silu, hb):
    m = q_src.shape[0]
    tt = tok.seq
    assert heads % hb == 0 and q_blk % hb == 0 and kf_blk % hb == 0 and kb_blk % hb == 0 and v_blk % hb == 0
    bwd = lambda n: _bwd_block(n, tok)
    fwd = lambda n: n

    def stream(blk, k_blk):
        return [
            pl.BlockSpec((tt, hb * dk), lambda h, n: (blk(n), q_blk // hb + h)),
            pl.BlockSpec((tt, hb * dk), lambda h, n: (blk(n), k_blk // hb + h)),
            pl.BlockSpec((tt, hb * dv), lambda h, n: (blk(n), v_blk // hb + h)),
            pl.BlockSpec((tt, hb * dk), lambda h, n: (blk(n), h)),
        ]

    n_batch = tok.n_ctx_blocks
    return pl.pallas_call(
        functools.partial(_gla_scan_kernel, tok=tok, q_silu=q_silu, scale=dk ** -0.5, hb=hb, dk=dk, dv=dv),
        grid=(heads // hb, tok.n_blocks),
        in_specs=stream(fwd, kf_blk) + stream(bwd, kb_blk) + [
            pl.BlockSpec((1, 1, 2, hb, dk, dv), lambda h, n: (_dec_batch(n, tok), j, 0, h, 0, 0)),
        ],
        out_specs=[
            pl.BlockSpec((tt, hb * dv), lambda h, n: (n, h)),
            pl.BlockSpec((tt, hb * dv), lambda h, n: (bwd(n), h)),
            pl.BlockSpec((1, 2, hb, dk, dv), lambda h, n: (_state_block(n, tok), 0, h, 0, 0)),
        ],
        out_shape=[
            jax.ShapeDtypeStruct((m, heads * dv), BF16),
            jax.ShapeDtypeStruct((m, heads * dv), BF16),
            jax.ShapeDtypeStruct((n_batch, 2, heads, dk, dv), F32),
        ],
        scratch_shapes=[pltpu.VMEM((2, hb, dv, dk), F32)],
        compiler_params=_params("parallel", "arbitrary"),
        name="gla_scan",
    )(q_src, kf_src, v_src, cb_f, q_src, kb_src, v_src, cb_b, s0)


def kernel(x_prompt, x_sample, c, state_deltanet, state_gla, state_hgrn, c_ctx, ada_w, ada_b, norm1, norm2, final_norm, w_up, w_down, dn_w_in, dn_conv, dn_A_log, dn_dt_bias, dn_norm, dn_w_out, gla_w_in, gla_w_a2, gla_b_a, gla_norm, gla_w_out, hgrn_w_in, hgrn_lb_logits, hgrn_norm, hgrn_w_out):
    batch, seq, d = x_prompt.shape
    dec_batch, dec_seq, _ = x_sample.shape
    depth = ada_w.shape[0]
    tok = Tokens(m_ctx=batch * seq, seq=seq, dec_batch=dec_batch, dec_seq=dec_seq)
    assert seq % CHUNK == 0 and dec_seq % seq == 0 and seq % GRID_W == 0

    dn_hv, dn_dk, dn_dv = state_deltanet.shape[3:]
    dn_vd = dn_hv * dn_dv
    n_conv = dn_conv.shape[2]
    dn_kd = (n_conv - dn_vd) // 2
    dn_hk = dn_kd // dn_dk
    gla_h, gla_dk, gla_dv = state_gla.shape[3:]
    gla_kd, gla_vd = gla_h * gla_dk, gla_h * gla_dv
    hg_h, hg_dk, hg_dv = state_hgrn.shape[3:]

    x = jnp.concatenate([x_prompt.reshape(tok.m_ctx, d), x_sample.reshape(dec_batch * dec_seq, d)], axis=0)
    n_rows = 1 + dec_batch
    pad = (-n_rows) % 8
    cond = jnp.concatenate([c_ctx[None, :], c, jnp.zeros((pad, d), F32)], axis=0)
    mod_all = _mod_table(cond, ada_w, ada_b).reshape(depth, n_rows + pad, 6, d)

    new_states = ([], [], [])
    dn_states = None
    for i in range(depth):
        kind, j = i % N_MIXERS, i // N_MIXERS
        mod = mod_all[i]
        if kind == 0:
            w_in = dn_w_in[j]
            proj = _proj(x, norm1[i], mod, w_in[:, :n_conv + dn_vd].astype(BF16), tok)
            raw = _proj(x, norm1[i], mod, w_in[:, n_conv + dn_vd:], tok, precise=True)
            qkv = _dn_prep(proj, dn_conv[j], n_conv, dn_kd, dn_dk, tok)
            gates, gates_t = _dn_gates(raw, dn_A_log[j], dn_dt_bias[j], tok)
            o_f, o_b, dn_states = _dn_scan(qkv, gates, gates_t, state_deltanet, j, dn_states, tok,
                                           hk_n=dn_hk, hv=dn_hv, dk=dn_dk, dv=dn_dv)
            x = _out_proj(o_f, o_b, proj, n_conv // dn_vd, dn_norm[j], x, mod, dn_w_out[j].astype(BF16), tok)
        elif kind == 1:
            w_in = gla_w_in[j]
            n_main = 2 * gla_kd + 2 * gla_vd
            proj = _proj(x, norm1[i], mod, w_in[:, :n_main].astype(BF16), tok)
            a1 = _proj(x, norm1[i], mod, w_in[:, n_main:], tok, precise=True)
            cb_f, cb_b = _gla_prep(a1, gla_w_a2[j], gla_b_a[j], tok)
            o_f, o_b, ns = _gla_scan(proj, 0, proj, gla_h, proj, gla_h, proj, (2 * gla_kd) // gla_dv,
                                     cb_f, cb_b, state_gla, j, tok,
                                     heads=gla_h, dk=gla_dk, dv=gla_dv, q_silu=False, hb=1)
            x = _out_proj(o_f, o_b, proj, (2 * gla_kd + gla_vd) // gla_vd, gla_norm[j], x, mod,
                          gla_w_out[j].astype(BF16), tok)
        else:
            proj = _proj(x, norm1[i], mod, hgrn_w_in[j].astype(BF16), tok)
            k_f, k_b, cb_f, cb_b = _hgrn_prep(proj, hgrn_lb_logits, i, d, tok)
            o_f, o_b, ns = _gla_scan(proj, 0, k_f, 0, k_b, 0, proj, d // hg_dv, cb_f, cb_b, state_hgrn, j, tok,
                                     heads=hg_h, dk=hg_dk, dv=hg_dv, q_silu=True, hb=min(hg_h, 4))
            x = _out_proj(o_f, o_b, proj, 2, hgrn_norm[j], x, mod, hgrn_w_out[j].astype(BF16), tok)
        if kind != 0:
            new_states[kind].append(ns)
        x = _mlp(x, norm2[i], mod, w_up[i].astype(BF16), w_down[i].astype(BF16), final_norm, tok,
                 final=(i == depth - 1))

    y_prompt = x[:tok.m_ctx].reshape(batch, seq, d)
    y_sample = x[tok.m_ctx:].reshape(dec_batch, dec_seq, d)
    stacked = [s[0][:, None] if len(s) == 1 else jnp.stack(s, axis=1) for s in new_states[1:]]
    return (y_prompt, y_sample, dn_states) + tuple(stacked)
```

```python
import functools
from typing import NamedTuple

import jax
import jax.numpy as jnp
from jax import lax
from jax.experimental import pallas as pl
from jax.experimental.pallas import tpu as pltpu

F32 = jnp.float32
BF16 = jnp.bfloat16

EPS = 1e-6
GRID_W = 64
CONV_K = 5
GLA_RANK = 16
GLA_TAU = 16.0
N_MIXERS = 3
CHUNK = 64
SUB = 16
EXP_CLAMP = 80.0
LANES = 128
V7X_MXU_COLS = 256
V7X_VMEM_LIMIT_BYTES = 56 * 1024 * 1024
HIGHEST = lax.Precision.HIGHEST


class Tokens(NamedTuple):
    m_ctx: int
    seq: int
    dec_batch: int
    dec_seq: int

    @property
    def m(self):
        return self.m_ctx + self.dec_batch * self.dec_seq

    @property
    def n_ctx_blocks(self):
        return self.m_ctx // self.seq

    @property
    def blocks_per_dec(self):
        return self.dec_seq // self.seq

    @property
    def n_blocks(self):
        return self.m // self.seq


def _tile(n, pref):
    t = min(n, pref)
    while n % t:
        t //= 2
    return t


def _token_tile(tok, pref):
    t = pref
    while tok.m_ctx % t or tok.dec_seq % t:
        t //= 2
    return t


def _params(*sem):
    return pltpu.CompilerParams(dimension_semantics=sem, vmem_limit_bytes=V7X_VMEM_LIMIT_BYTES)


def _mod_row(i, tm, tok):
    start = i * tm
    return jnp.where(start < tok.m_ctx, 0, 1 + (start - tok.m_ctx) // tok.dec_seq)


def _sigmoid(x):
    return 1.0 / (1.0 + jnp.exp(-x))


def _softplus(x):
    return jnp.maximum(x, 0.0) + jnp.log(1.0 + jnp.exp(-jnp.abs(x)))


def _norm_mod(x, nw, shift, scale):
    ms = jnp.mean(x * x, axis=-1, keepdims=True)
    return x * lax.rsqrt(ms + EPS) * nw * (1.0 + scale) + shift


def _dot(a, b):
    return jnp.dot(a, b, preferred_element_type=F32)


def _dot_nt(a, b):
    return lax.dot_general(a, b, (((1,), (1,)), ((), ())), preferred_element_type=F32)


def _dot_tn(a, b):
    return lax.dot_general(a, b, (((0,), (0,)), ((), ())), preferred_element_type=F32)


def _split_dot(tri, x):
    hi = x.astype(BF16)
    lo = (x - hi.astype(F32)).astype(BF16)
    return _dot(tri, hi) + _dot(tri, lo)


def _split_matmul(a, b):
    m = a.shape[0]
    ah = a.astype(BF16)
    al = (a - ah.astype(F32)).astype(BF16)
    bh = b.astype(BF16)
    bl = (b - bh.astype(F32)).astype(BF16)
    t = _dot(jnp.concatenate([ah, al], axis=0), bh)
    return t[:m] + t[m:] + _dot(ah, bl)


def _tri(n, upper):
    r = lax.broadcasted_iota(jnp.int32, (n, n), 0)
    c = lax.broadcasted_iota(jnp.int32, (n, n), 1)
    return jnp.where((r <= c) if upper else (r >= c), 1.0, 0.0).astype(BF16)


def _mod_kernel(c_ref, w_ref, b_ref, o_ref):
    c = c_ref[...]
    s = c * _sigmoid(c)
    o_ref[0] = jnp.dot(s, w_ref[0], precision=HIGHEST, preferred_element_type=F32) + b_ref[0]


def _mod_table(cond, ada_w, ada_b):
    depth, d, n = ada_w.shape
    rows = cond.shape[0]
    tn = _tile(n, 1024)
    return pl.pallas_call(
        _mod_kernel,
        grid=(depth, n // tn),
        in_specs=[
            pl.BlockSpec((rows, d), lambda l, j: (0, 0)),
            pl.BlockSpec((1, d, tn), lambda l, j: (l, 0, j)),
            pl.BlockSpec((1, 1, tn), lambda l, j: (l, 0, j)),
        ],
        out_specs=pl.BlockSpec((1, rows, tn), lambda l, j: (l, 0, j)),
        out_shape=jax.ShapeDtypeStruct((depth, rows, n), F32),
        compiler_params=_params("parallel", "parallel"),
        name="mod_table",
    )(cond, ada_w, ada_b.reshape(depth, 1, n))


def _proj_kernel(x_ref, nw_ref, mod_ref, w_ref, o_ref, h_ref, *, precise):
    @pl.when(pl.program_id(1) == 0)
    def _():
        h = _norm_mod(x_ref[...], nw_ref[...], mod_ref[0, 0:1, :], mod_ref[0, 1:2, :])
        h_ref[...] = h.astype(h_ref.dtype)

    if precise:
        o_ref[...] = jnp.dot(h_ref[...], w_ref[...], precision=HIGHEST, preferred_element_type=F32)
    else:
        o_ref[...] = _dot(h_ref[...], w_ref[...]).astype(o_ref.dtype)


def _proj(x, nw, mod, w, tok, *, precise=False, tm=1024, tn=1024):
    m, d = x.shape
    n = w.shape[1]
    tm = _token_tile(tok, tm)
    tn = _tile(n, tn)
    return pl.pallas_call(
        functools.partial(_proj_kernel, precise=precise),
        grid=(m // tm, n // tn),
        in_specs=[
            pl.BlockSpec((tm, d), lambda i, j: (i, 0)),
            pl.BlockSpec((1, d), lambda i, j: (0, 0)),
            pl.BlockSpec((1, 6, d), lambda i, j: (_mod_row(i, tm, tok), 0, 0)),
            pl.BlockSpec((d, tn), lambda i, j: (0, j)),
        ],
        out_specs=pl.BlockSpec((tm, tn), lambda i, j: (i, j)),
        out_shape=jax.ShapeDtypeStruct((m, n), F32 if precise else BF16),
        scratch_shapes=[pltpu.VMEM((tm, d), F32 if precise else BF16)],
        compiler_params=_params("parallel", "arbitrary"),
        name="proj_precise" if precise else "proj",
    )(x, nw.reshape(1, d), mod, w)


def _mlp_kernel(x_ref, nw_ref, mod_ref, wu_ref, wd_ref, fw_ref, o_ref, h_ref, acc_ref, *, final):
    f = pl.program_id(1)

    @pl.when(f == 0)
    def _():
        h = _norm_mod(x_ref[...], nw_ref[...], mod_ref[0, 3:4, :], mod_ref[0, 4:5, :])
        h_ref[...] = h.astype(BF16)
        acc_ref[...] = jnp.zeros_like(acc_ref)

    a = jnp.maximum(_dot(h_ref[...], wu_ref[...]), 0.0)
    acc_ref[...] += _dot((a * a).astype(BF16), wd_ref[...])

    @pl.when(f == pl.num_programs(1) - 1)
    def _():
        y = x_ref[...] + mod_ref[0, 5:6, :] * acc_ref[...]
        if final:
            ms = jnp.mean(y * y, axis=-1, keepdims=True)
            y = y * lax.rsqrt(ms + EPS) * fw_ref[...]
        o_ref[...] = y


def _mlp(x, nw, mod, w_up, w_down, final_w, tok, *, final, tm=512, tf=1024):
    m, d = x.shape
    ff = w_up.shape[1]
    tm = _token_tile(tok, tm)
    tf = _tile(ff, tf)
    return pl.pallas_call(
        functools.partial(_mlp_kernel, final=final),
        grid=(m // tm, ff // tf),
        in_specs=[
            pl.BlockSpec((tm, d), lambda i, f: (i, 0)),
            pl.BlockSpec((1, d), lambda i, f: (0, 0)),
            pl.BlockSpec((1, 6, d), lambda i, f: (_mod_row(i, tm, tok), 0, 0)),
            pl.BlockSpec((d, tf), lambda i, f: (0, f)),
            pl.BlockSpec((tf, d), lambda i, f: (f, 0)),
            pl.BlockSpec((1, d), lambda i, f: (0, 0)),
        ],
        out_specs=pl.BlockSpec((tm, d), lambda i, f: (i, 0)),
        out_shape=jax.ShapeDtypeStruct((m, d), F32),
        scratch_shapes=[pltpu.VMEM((tm, d), BF16), pltpu.VMEM((tm, d), F32)],
        compiler_params=_params("parallel", "arbitrary"),
        name="mlp_final" if final else "mlp",
    )(x, nw.reshape(1, d), mod, w_up, w_down, final_w.reshape(1, d))


def _out_kernel(of_ref, ob_ref, g_ref, nw_ref, x_ref, mod_ref, w_ref, o_ref, y0_ref, y1_ref, *, n_heads, dh):
    i = pl.program_id(0)
    d = w_ref.shape[1]
    n_chunks = d // V7X_MXU_COLS
    heads_per_chunk = -(-n_heads // n_chunks)

    @pl.when(i == 0)
    def _():
        y1_ref[...] = jnp.zeros_like(y1_ref)

    def body(y_write, y_read):
        ones = jnp.full((dh, LANES), 1.0 / dh, BF16) if dh & (dh - 1) == 0 else None
        for c in range(n_chunks):
            cols = slice(c * V7X_MXU_COLS, (c + 1) * V7X_MXU_COLS)
            o_ref[:, cols] = x_ref[:, cols] + mod_ref[0, 2:3, cols] * _dot(y_read[...], w_ref[:, cols])
            for h in range(c * heads_per_chunk, min((c + 1) * heads_per_chunk, n_heads)):
                sl = slice(h * dh, (h + 1) * dh)
                o = of_ref[:, sl].astype(F32) + ob_ref[:, sl].astype(F32)
                if ones is None:
                    ms = _dot((o * o).astype(BF16), jnp.ones((dh, LANES), BF16)) * (1.0 / dh)
                else:
                    ms = _dot((o * o).astype(BF16), ones)
                inv = lax.rsqrt(ms + EPS)
                half_g = 0.5 * g_ref[:, sl].astype(F32)
                silu = half_g + half_g * jnp.tanh(half_g)
                y = o * nw_ref[...] * silu
                for k in range(dh // LANES):
                    lo = h * dh + k * LANES
                    y_write[:, lo:lo + LANES] = (y[:, k * LANES:(k + 1) * LANES] * inv).astype(BF16)

    @pl.when(i % 2 == 0)
    def _():
        body(y0_ref, y1_ref)

    @pl.when(i % 2 == 1)
    def _():
        body(y1_ref, y0_ref)


def _out_proj(o_f, o_b, gate_src, gate_blk, norm_w, x, mod, w_out, tok, *, tm=256):
    m, vd = o_f.shape
    d = x.shape[1]
    dh = norm_w.shape[0]
    tm = _token_tile(tok, tm)
    n_tiles = m // tm
    assert d % V7X_MXU_COLS == 0 and dh % LANES == 0
    cur = lambda i: jnp.minimum(i, n_tiles - 1)
    prev = lambda i: jnp.maximum(i - 1, 0)
    return pl.pallas_call(
        functools.partial(_out_kernel, n_heads=vd // dh, dh=dh),
        grid=(n_tiles + 1,),
        in_specs=[
            pl.BlockSpec((tm, vd), lambda i: (cur(i), 0)),
            pl.BlockSpec((tm, vd), lambda i: (cur(i), 0)),
            pl.BlockSpec((tm, vd), lambda i: (cur(i), gate_blk)),
            pl.BlockSpec((1, dh), lambda i: (0, 0)),
            pl.BlockSpec((tm, d), lambda i: (prev(i), 0)),
            pl.BlockSpec((1, 6, d), lambda i: (_mod_row(prev(i), tm, tok), 0, 0)),
            pl.BlockSpec((vd, d), lambda i: (0, 0), pipeline_mode=pl.Buffered(1)),
        ],
        out_specs=pl.BlockSpec((tm, d), lambda i: (prev(i), 0)),
        out_shape=jax.ShapeDtypeStruct((m, d), F32),
        scratch_shapes=[pltpu.VMEM((tm, vd), BF16), pltpu.VMEM((tm, vd), BF16)],
        compiler_params=_params("arbitrary"),
        name="out_proj",
    )(o_f, o_b, gate_src, norm_w.reshape(1, dh), x, mod, w_out)


def _bwd_block(n, tok):
    nc, per = tok.n_ctx_blocks, tok.blocks_per_dec
    r = jnp.maximum(n - nc, 0)
    return jnp.where(n < nc, n, nc + (r // per) * per + (per - 1 - r % per))


def _dec_batch(n, tok):
    return jnp.maximum(n - tok.n_ctx_blocks, 0) // tok.blocks_per_dec


def _state_block(n, tok):
    return jnp.minimum(n, tok.n_ctx_blocks - 1)


def _seq_flags(n, tok):
    nc, per = tok.n_ctx_blocks, tok.blocks_per_dec
    is_ctx = n < nc
    r = jnp.maximum(n - nc, 0) % per
    return is_ctx, jnp.logical_and(jnp.logical_not(is_ctx), r == 0)


def _dn_prep_kernel(x_ref, shift_ref, w_ref, o_ref, *, n_q_blocks, dk):
    j = pl.program_id(1)
    x = x_ref[...]
    tc = x.shape[1]
    acc = None
    for s in range(CONV_K):
        xs = x.astype(F32) if s == CONV_K // 2 else _dot(shift_ref[0, s], x)
        term = xs * w_ref[s:s + 1, :]
        acc = term if acc is None else acc + term
    y = acc * _sigmoid(acc)

    def normed(scale):
        ones = jnp.ones((dk, dk), BF16)
        for g in range(tc // dk):
            blk = y[:, g * dk:(g + 1) * dk]
            ss = _dot((blk * blk).astype(BF16), ones)
            o_ref[:, g * dk:(g + 1) * dk] = (blk * (lax.rsqrt(ss + EPS) * scale)).astype(o_ref.dtype)

    @pl.when(j < n_q_blocks)
    def _():
        normed(dk ** -0.5)

    @pl.when(jnp.logical_and(j >= n_q_blocks, j < 2 * n_q_blocks))
    def _():
        normed(1.0)

    @pl.when(j >= 2 * n_q_blocks)
    def _():
        o_ref[...] = y.astype(o_ref.dtype)


def _conv_shift_matrices(tt):
    t = jnp.arange(tt)[:, None]
    u = jnp.arange(tt)[None, :]
    mats = []
    for seg in (tt, GRID_W):
        same = (t // seg) == (u // seg)
        mats.append(jnp.stack([jnp.logical_and(u == t + s - CONV_K // 2, same) for s in range(CONV_K)]))
    return jnp.stack(mats).astype(BF16)


def _dn_prep(proj, conv_w, n_conv, kd, dk, tok):
    m = proj.shape[0]
    tt = tok.seq
    tc = _tile(kd, 1024)
    n_ctx = tok.n_ctx_blocks
    return pl.pallas_call(
        functools.partial(_dn_prep_kernel, n_q_blocks=kd // tc, dk=dk),
        grid=(m // tt, n_conv // tc),
        in_specs=[
            pl.BlockSpec((tt, tc), lambda i, j: (i, j)),
            pl.BlockSpec((1, CONV_K, tt, tt), lambda i, j: (jnp.where(i < n_ctx, 0, 1), 0, 0, 0)),
            pl.BlockSpec((CONV_K, tc), lambda i, j: (0, j)),
        ],
        out_specs=pl.BlockSpec((tt, tc), lambda i, j: (i, j)),
        out_shape=jax.ShapeDtypeStruct((m, n_conv), BF16),
        compiler_params=_params("parallel", "parallel"),
        name="dn_prep",
    )(proj, _conv_shift_matrices(tt), conv_w)


def _dn_gate_kernel(x_ref, a_ref, bias_ref, o_ref, gt_ref, *, hv, tt):
    x = x_ref[...]
    lanes = x.shape[1]
    lane = lax.broadcasted_iota(jnp.int32, (1, lanes), 1)
    is_fwd = lane < 3 * hv
    g = -a_ref[...] * _softplus(x + bias_ref[...])
    beta = pltpu.roll(_sigmoid(x), 2 * hv, 1)
    lower = _tri(CHUNK, upper=False)
    upper = _tri(CHUNK, upper=True)
    for c in range(tt // CHUNK):
        rows = slice(c * CHUNK, (c + 1) * CHUNK)
        gch = g[rows]
        gc = jnp.where(is_fwd, _split_dot(lower, gch), _split_dot(upper, gch))
        g_end = jnp.where(is_fwd, gc[CHUNK - 1:CHUNK, :], gc[0:1, :])
        eg = jnp.exp(gc)
        o_ref[0, rows, :] = beta[rows]
        o_ref[1, rows, :] = gc
        o_ref[2, rows, :] = eg
        o_ref[3, rows, :] = jnp.exp(g_end - gc)
        o_ref[4, rows, :] = beta[rows] * eg
    gt_ref[...] = o_ref[1].T


def _dn_gates(raw, a_log, dt_bias, tok):
    m, lanes = raw.shape
    hv = lanes // 4
    tt = tok.seq
    zeros = jnp.zeros((2 * hv,), F32)
    a_full = jnp.concatenate([zeros, jnp.exp(a_log).reshape(-1)]).reshape(1, lanes)
    bias_full = jnp.concatenate([zeros, dt_bias.reshape(-1)]).reshape(1, lanes)
    return pl.pallas_call(
        functools.partial(_dn_gate_kernel, hv=hv, tt=tt),
        grid=(m // tt,),
        in_specs=[
            pl.BlockSpec((tt, lanes), lambda i: (i, 0)),
            pl.BlockSpec((1, lanes), lambda i: (0, 0)),
            pl.BlockSpec((1, lanes), lambda i: (0, 0)),
        ],
        out_specs=[
            pl.BlockSpec((5, tt, lanes), lambda i: (0, i, 0)),
            pl.BlockSpec((lanes, tt), lambda i: (0, i)),
        ],
        out_shape=[jax.ShapeDtypeStruct((5, m, lanes), F32), jax.ShapeDtypeStruct((lanes, m), F32)],
        compiler_params=_params("parallel"),
        name="dn_gates",
    )(raw, a_full, bias_full)


def _dn_scan_kernel(qf_ref, kf_ref, vf_ref, gf_ref, gtf_ref, qb_ref, kb_ref, vb_ref, gb_ref, gtb_ref, s0_ref,
                    of_ref, ob_ref, ns_ref, s_ref, u0_ref, wq_ref, attn_ref, kd_ref, ge_ref, *, tok, hv, dv, hkb):
    hk0 = pl.program_id(0) * hkb
    n = pl.program_id(1)
    tt = tok.seq
    nch = tt // CHUNK
    dk = qf_ref.shape[1] // hkb
    assert dv == 2 * CHUNK and CHUNK == 4 * SUB
    has_prev = n > 0
    is_ctx, dec_start = _seq_flags(jnp.maximum(n - 1, 0), tok)
    wslot = n % 2
    rslot = 1 - wslot
    scratch = (u0_ref, wq_ref, attn_ref, kd_ref, ge_ref)

    @pl.when(n == 0)
    def _():
        s_ref[...] = jnp.zeros_like(s_ref)
        for ref in scratch:
            ref[1] = jnp.zeros(ref.shape[1:], ref.dtype)

    @pl.when(jnp.logical_and(has_prev, is_ctx))
    def _():
        s_ref[...] = jnp.zeros_like(s_ref)

    @pl.when(jnp.logical_and(has_prev, dec_start))
    def _():
        s_ref[...] = s0_ref[0, 0]

    bf = lambda x: x.astype(BF16)
    lanes = gf_ref.shape[2]
    lane_iota = lax.broadcasted_iota(jnp.int32, (1, lanes), 1)
    ri = lax.broadcasted_iota(jnp.int32, (CHUNK, 2 * CHUNK), 0)
    li = lax.broadcasted_iota(jnp.int32, (CHUNK, 2 * CHUNK), 1)
    left = li < CHUNK
    ci = li % CHUNK
    same = [(ri // b) == (ci // b) for b in (SUB, 2 * SUB)]
    zero_b = jnp.zeros((CHUNK, 2 * CHUNK), BF16)

    def block_diag(xb):
        return jnp.concatenate([jnp.where(left, xb, zero_b), jnp.where(left, zero_b, xb)], axis=0)

    def pair_dot(a, b):
        return _dot(bf(a), block_diag(bf(b)))

    dirs = ((qf_ref, kf_ref, vf_ref, gf_ref, gtf_ref, of_ref), (qb_ref, kb_ref, vb_ref, gb_ref, gtb_ref, ob_ref))

    def prepare():
        ch = {}
        for d, kh in [(d, kh) for d in range(2) for kh in range(hkb)]:
            q_ref, k_ref, v_ref, g_ref, gt_ref, _ = dirs[d]
            incl = (ri >= ci) if d == 0 else (ri <= ci)
            strict = (ri > ci) if d == 0 else (ri < ci)
            end_row = CHUNK - 1 if d == 0 else 0
            q = q_ref[:, kh * dk:(kh + 1) * dk]
            k = k_ref[:, kh * dk:(kh + 1) * dk]
            kf32 = k.astype(F32)
            qf32 = q.astype(F32)
            cols, g_rows, vs = [], [], []
            for e in range(2):
                a_lane = 2 * hv + d * hv + 2 * (hk0 + kh) + e
                sel = lane_iota == a_lane
                cols.append([jnp.sum(jnp.where(sel, g_ref[i], 0.0), axis=1, keepdims=True) for i in range(5)])
                g_rows.append(gt_ref[pl.ds(a_lane, 1), :])
                vs.append(v_ref[:, (2 * kh + e) * dv:(2 * kh + e + 1) * dv].astype(F32))
            for c in range(nch):
                rows = slice(c * CHUNK, (c + 1) * CHUNK)
                pair = lambda i: jnp.where(left, cols[0][i][rows], cols[1][i][rows])
                k2 = jnp.concatenate([k[rows], k[rows]], axis=0)
                kk = _dot_nt(k[rows], k2)
                qk = _dot_nt(q[rows], k2)
                g_row = jnp.concatenate([g_rows[0][:, rows], g_rows[1][:, rows]], axis=1)
                diff = pair(1) - g_row
                decay = jnp.where(incl, jnp.exp(jnp.where(incl, diff, 0.0)), 0.0)
                p = jnp.where(strict, -(pair(0) * kk * decay), 0.0)
                attn_ref[wslot, d, kh, c] = bf(qk * decay)
                for e in range(2):
                    beta_c, _, eg_c, ekd_c, beg_c = cols[e]
                    qg = qf32[rows] * eg_c[rows]
                    kd_ref[wslot, d, 2 * kh + e, c] = bf(kf32[rows] * ekd_c[rows])
                    wq_ref[wslot, d, 2 * kh + e, c, CHUNK:, :] = bf(qg)
                    g_end = eg_c[c * CHUNK + end_row:c * CHUNK + end_row + 1, :]
                    ge_ref[wslot, d, 2 * kh + e, c] = jnp.broadcast_to(g_end, ge_ref.shape[4:])
                ch[kh, d, c] = dict(
                    pd=jnp.where(same[0], p, 0.0),
                    p32=jnp.where(jnp.logical_and(same[1], jnp.logical_not(same[0])), p, 0.0),
                    p64=jnp.where(same[1], 0.0, p),
                    r=[jnp.concatenate([cols[e][0][rows] * vs[e][rows], cols[e][4][rows] * kf32[rows]], axis=1)
                       for e in range(2)])
            yield
        for c in ch.values():
            c["q"] = pair_dot(c["pd"], c["pd"])
            c["xm"] = c["pd"]
        yield
        for it in range(2):
            for c in ch.values():
                q, xm = c["q"], c["xm"]
                both = _dot(bf(q), jnp.concatenate([block_diag(bf(xm)), block_diag(bf(q))], axis=1))
                c["xm"] = xm + q + both[:, :2 * CHUNK]
                c["q"] = both[:, 2 * CHUNK:]
            yield
        for c in ch.values():
            q, xm = c.pop("q"), c["xm"]
            c["xm"] = xm + q + pair_dot(q, xm)
        yield
        for level in ("p32", "p64"):
            for c in ch.values():
                c["y"] = c[level] + pair_dot(c[level], c["xm"])
            yield
            for c in ch.values():
                xm, y = c["xm"], c.pop("y")
                c["xm"] = xm + y + pair_dot(xm, y)
            yield
        zero_r = jnp.zeros((CHUNK, 2 * dv), BF16)
        for (kh, d, cidx), c in ch.items():
            r0, r1 = c["r"]
            rhs = jnp.concatenate([jnp.concatenate([bf(r0), zero_r], axis=1),
                                   jnp.concatenate([zero_r, bf(r1)], axis=1)], axis=0)
            big = _dot(bf(c["xm"]), rhs)
            sol = [r0 + big[:, :2 * dv], r1 + big[:, 2 * dv:]]
            u0_ref[wslot, d, kh, cidx] = jnp.concatenate([sol[0][:, :dv], sol[1][:, :dv]], axis=1)
            for e in range(2):
                wq_ref[wslot, d, 2 * kh + e, cidx, :CHUNK, :] = bf(sol[e][:, dv:])
        yield

    def recur():
        state = {(d, e): s_ref[d, e] for d in range(2) for e in range(2 * hkb)}
        zero_u = jnp.zeros((CHUNK, dv), BF16)
        for step in range(nch):
            chunk = lambda d: step if d == 0 else nch - 1 - step
            ws = {(d, e): _dot(wq_ref[rslot, d, e, chunk(d)], bf(s)) for (d, e), s in state.items()}
            yield
            for d, kh in [(d, kh) for d in range(2) for kh in range(hkb)]:
                c = chunk(d)
                u0 = u0_ref[rslot, d, kh, c]
                ev = [2 * kh, 2 * kh + 1]
                ub = [bf(u0[:, e * dv:(e + 1) * dv] - ws[d, ev[e]][:CHUNK]) for e in range(2)]
                u2 = jnp.concatenate([jnp.concatenate([ub[0], zero_u], axis=1),
                                      jnp.concatenate([zero_u, ub[1]], axis=1)], axis=0)
                o = (_dot(attn_ref[rslot, d, kh, c], u2)
                     + jnp.concatenate([ws[d, ev[0]][CHUNK:], ws[d, ev[1]][CHUNK:]], axis=1))
                dirs[d][5][c * CHUNK:(c + 1) * CHUNK, 2 * kh * dv:2 * (kh + 1) * dv] = bf(o)
                for e in range(2):
                    state[d, ev[e]] = (ge_ref[rslot, d, ev[e], c][0:1, :] * state[d, ev[e]]
                                       + _dot_tn(kd_ref[rslot, d, ev[e], c], ub[e]))
            yield
        for (d, e), s in state.items():
            s_ref[d, e] = s
        yield

    halves = [recur(), prepare()]
    while halves:
        for g in list(halves):
            if next(g, "done") == "done":
                halves.remove(g)

    @pl.when(jnp.logical_and(has_prev, is_ctx))
    def _():
        ns_ref[0, 0] = s_ref[...]


def _dn_scan_kernel_aliased(*refs, **kw):
    n_in = 11
    return _dn_scan_kernel(*refs[:n_in], *refs[n_in + 1:], **kw)


def _dn_scan(qkv, gates, gates_t, s0, j, new_states, tok, *, hk_n, hv, dk, dv):
    m = qkv.shape[0]
    tt = tok.seq
    rep = hv // hk_n
    assert rep == 2
    lanes = gates.shape[2]
    kd = hk_n * dk
    nb = tok.n_blocks
    nch = tt // CHUNK
    fwd = lambda n: jnp.minimum(n, nb - 1)
    bwd = lambda n: _bwd_block(jnp.minimum(n, nb - 1), tok)
    prev = lambda n: jnp.maximum(n - 1, 0)

    hkb = 2 if hk_n % 2 == 0 else 1
    hvb = hkb * rep

    def stream(blk):
        return [
            pl.BlockSpec((tt, hkb * dk), lambda h, n: (blk(n), h)),
            pl.BlockSpec((tt, hkb * dk), lambda h, n: (blk(n), hk_n // hkb + h)),
            pl.BlockSpec((tt, hvb * dv), lambda h, n: (blk(n), (2 * kd) // (hvb * dv) + h)),
            pl.BlockSpec((5, tt, lanes), lambda h, n: (0, blk(n), 0)),
            pl.BlockSpec((lanes, tt), lambda h, n: (0, blk(n))),
        ]

    n_batch = tok.n_ctx_blocks
    n_layers = s0.shape[1]
    aliased = new_states is not None
    extra_in = [pl.BlockSpec(memory_space=pl.ANY)] if aliased else []
    extra_args = (new_states,) if aliased else ()
    return pl.pallas_call(
        functools.partial(_dn_scan_kernel_aliased if aliased else _dn_scan_kernel, tok=tok, hv=hv, dv=dv, hkb=hkb),
        grid=(hk_n // hkb, nb + 1),
        in_specs=stream(fwd) + stream(bwd) + [
            pl.BlockSpec((1, 1, 2, hvb, dk, dv), lambda h, n: (_dec_batch(prev(n), tok), j, 0, h, 0, 0)),
        ] + extra_in,
        out_specs=[
            pl.BlockSpec((tt, hvb * dv), lambda h, n: (prev(n), h)),
            pl.BlockSpec((tt, hvb * dv), lambda h, n: (_bwd_block(prev(n), tok), h)),
            pl.BlockSpec((1, 1, 2, hvb, dk, dv), lambda h, n: (_state_block(prev(n), tok), j, 0, h, 0, 0)),
        ],
        out_shape=[
            jax.ShapeDtypeStruct((m, hv * dv), BF16),
            jax.ShapeDtypeStruct((m, hv * dv), BF16),
            jax.ShapeDtypeStruct((n_batch, n_layers, 2, hv, dk, dv), F32),
        ],
        input_output_aliases={11: 2} if aliased else {},
        scratch_shapes=[
            pltpu.VMEM((2, hvb, dk, dv), F32),
            pltpu.VMEM((2, 2, hkb, nch, CHUNK, rep * dv), F32),
            pltpu.VMEM((2, 2, hvb, nch, 2 * CHUNK, dk), BF16),
            pltpu.VMEM((2, 2, hkb, nch, CHUNK, rep * CHUNK), BF16),
            pltpu.VMEM((2, 2, hvb, nch, CHUNK, dk), BF16),
            pltpu.VMEM((2, 2, hvb, nch, 8, dv), F32),
        ],
        compiler_params=_params("parallel", "arbitrary"),
        name="dn_scan",
    )(qkv, qkv, qkv, gates, gates_t, qkv, qkv, qkv, gates, gates_t, s0, *extra_args)


def _chunk_cums(la_f, la_b, cbf_ref, cbb_ref, tt):
    lower = _tri(CHUNK, upper=False)
    upper = _tri(CHUNK, upper=True)
    for c in range(tt // CHUNK):
        rows = slice(c * CHUNK, (c + 1) * CHUNK)
        cbf_ref[rows, :] = _split_dot(lower, la_f[rows])
        cbb_ref[rows, :] = _split_dot(upper, la_b[rows])


def _log_sigmoid(x):
    return jnp.minimum(x, 0.0) - jnp.log(1.0 + jnp.exp(-jnp.abs(x)))


def _gla_prep_kernel(a1_ref, w2_ref, b_ref, cbf_ref, cbb_ref, *, tt):
    a1 = a1_ref[...]
    la = [_log_sigmoid(_dot(a1, w2_ref[d]) + b_ref[d]) * (1.0 / GLA_TAU) for d in range(2)]
    _chunk_cums(la[0], la[1], cbf_ref, cbb_ref, tt)


def _gla_prep(a1, w_a2, b_a, tok):
    m, r2 = a1.shape
    kd = w_a2.shape[2]
    tt = tok.seq
    tc = _tile(kd, 512)
    zeros = jnp.zeros((GLA_RANK, kd), F32)
    w2 = jnp.stack([jnp.concatenate([w_a2[0], zeros]), jnp.concatenate([zeros, w_a2[1]])])
    out = jax.ShapeDtypeStruct((m, kd), F32)
    return pl.pallas_call(
        functools.partial(_gla_prep_kernel, tt=tt),
        grid=(m // tt, kd // tc),
        in_specs=[
            pl.BlockSpec((tt, r2), lambda i, j: (i, 0)),
            pl.BlockSpec((2, r2, tc), lambda i, j: (0, 0, j)),
            pl.BlockSpec((2, 1, tc), lambda i, j: (0, 0, j)),
        ],
        out_specs=[pl.BlockSpec((tt, tc), lambda i, j: (i, j))] * 2,
        out_shape=[out, out],
        compiler_params=_params("parallel", "parallel"),
        name="gla_prep",
    )(a1, w2, b_a.reshape(2, 1, kd))


def _hgrn_prep_kernel(ff_ref, fb_ref, lb_ref, kf_ref, kb_ref, cbf_ref, cbb_ref, *, layer, tt):
    logits = lb_ref[...]
    ex = jnp.exp(logits - jnp.max(logits, axis=0, keepdims=True))
    probs = ex / jnp.sum(ex, axis=0, keepdims=True)
    lb = jnp.sum(probs[1:layer + 1], axis=0, keepdims=True)
    la = []
    for f_ref, k_ref in ((ff_ref, kf_ref), (fb_ref, kb_ref)):
        fl = f_ref[...].astype(F32)
        sg = _sigmoid(fl)
        la.append(jnp.log(lb + (1.0 - lb) * sg))
        k_ref[...] = (1.0 - lb) * (1.0 - sg)
    _chunk_cums(la[0], la[1], cbf_ref, cbb_ref, tt)


def _hgrn_prep(proj, lb_logits, layer, d, tok):
    m = proj.shape[0]
    depth = lb_logits.shape[0]
    tt = tok.seq
    tc = _tile(d, 512)
    nb = d // tc
    out = jax.ShapeDtypeStruct((m, d), F32)
    return pl.pallas_call(
        functools.partial(_hgrn_prep_kernel, layer=layer, tt=tt),
        grid=(m // tt, nb),
        in_specs=[
            pl.BlockSpec((tt, tc), lambda i, j: (i, 3 * nb + j)),
            pl.BlockSpec((tt, tc), lambda i, j: (i, 4 * nb + j)),
            pl.BlockSpec((depth, tc), lambda i, j: (0, j)),
        ],
        out_specs=[pl.BlockSpec((tt, tc), lambda i, j: (i, j))] * 4,
        out_shape=[out] * 4,
        compiler_params=_params("parallel", "parallel"),
        name="hgrn_prep",
    )(proj, proj, lb_logits)


def _gla_offdiag_pairs(c):
    pairs = []
    size = c
    while size > SUB:
        half = size // 2
        for start in range(0, c, size):
            pairs.append(((start + half, start + size), (start, start + half)))
        size = half
    return pairs


def _gla_operands(q, k, cb, rev):
    c, dk = q.shape
    zero_rows = lambda n: jnp.zeros((n, dk), BF16)

    def padded(x, rows):
        parts = ([zero_rows(rows[0])] if rows[0] else []) + [x] + ([zero_rows(c - rows[1])] if rows[1] < c else [])
        return jnp.concatenate(parts, axis=0) if len(parts) > 1 else x

    q_parts, k_parts = [], []
    for qr, kr in _gla_offdiag_pairs(c):
        if rev:
            qr, kr = kr, qr
        ref_row = qr[0] - 1 if not rev else qr[1]
        ref = cb[ref_row:ref_row + 1, :]
        qs, ks = slice(*qr), slice(*kr)
        q_parts.append(padded((q[qs] * jnp.exp(cb[qs] - ref)).astype(BF16), qr))
        k_parts.append(padded((k[ks] * jnp.exp(ref - cb[ks])).astype(BF16), kr))
    refs = []
    for blk in range(c // SUB):
        lo, hi = blk * SUB, (blk + 1) * SUB
        if not rev:
            ref = cb[lo - 1:lo, :] if blk > 0 else jnp.zeros_like(cb[0:1, :])
        else:
            ref = cb[hi:hi + 1, :] if hi < c else jnp.zeros_like(cb[0:1, :])
        refs.append(jnp.broadcast_to(ref, (SUB, dk)))
    ref_d = jnp.concatenate(refs, axis=0)
    q_diag = (q * jnp.exp(cb - ref_d)).astype(BF16)
    k_diag = (k * jnp.exp(jnp.minimum(ref_d - cb, EXP_CLAMP))).astype(BF16)
    return jnp.concatenate(q_parts, axis=1), jnp.concatenate(k_parts, axis=1), q_diag, k_diag


def _gla_diag_mask(c, rev):
    ri = lax.broadcasted_iota(jnp.int32, (c, c), 0)
    ci = lax.broadcasted_iota(jnp.int32, (c, c), 1)
    same_block = (ri // SUB) == (ci // SUB)
    return jnp.logical_and(same_block, (ri <= ci) if rev else (ri >= ci))


def _gla_scan_kernel(qf_ref, kf_ref, vf_ref, cf_ref, qb_ref, kb_ref, vb_ref, cb_ref, s0_ref,
                     of_ref, ob_ref, ns_ref, s_ref, *, tok, q_silu, scale, hb, dk, dv):
    n = pl.program_id(1)
    tt = tok.seq
    nch = tt // CHUNK
    is_ctx, dec_start = _seq_flags(n, tok)
    streams = [(d, h) for d in range(2) for h in range(hb)]

    @pl.when(is_ctx)
    def _():
        s_ref[...] = jnp.zeros_like(s_ref)

    @pl.when(dec_start)
    def _():
        for d, h in streams:
            s_ref[d, h] = s0_ref[0, 0, d, h].T

    dirs = ((qf_ref, kf_ref, vf_ref, cf_ref, of_ref), (qb_ref, kb_ref, vb_ref, cb_ref, ob_ref))
    ops = {}
    for d, (q_ref, k_ref, v_ref, c_ref, _) in enumerate(dirs):
        rev = d == 1
        for h in range(hb):
            q = q_ref[:, h * dk:(h + 1) * dk].astype(F32)
            if q_silu:
                q = q * _sigmoid(q)
            q = q * scale
            k = k_ref[:, h * dk:(h + 1) * dk].astype(F32)
            v = v_ref[:, h * dv:(h + 1) * dv].astype(BF16)
            cb = c_ref[:, h * dk:(h + 1) * dk]
            for c in range(nch):
                rows = slice(c * CHUNK, (c + 1) * CHUNK)
                qc, kc, cbc = q[rows], k[rows], cb[rows]
                end = cbc[0:1, :] if rev else cbc[CHUNK - 1:CHUNK, :]
                ops[d, h, c] = dict(
                    intra=_gla_operands(qc, kc, cbc, rev), v=v[rows],
                    q_state=(qc * jnp.exp(cbc)).astype(BF16),
                    k_state=(kc * jnp.exp(end - cbc)).astype(BF16),
                    s_decay=jnp.exp(end))
    a_off = {key: _dot_nt(op["intra"][0], op["intra"][1]) for key, op in ops.items()}
    a_diag = {key: _dot_nt(op["intra"][2], op["intra"][3]) for key, op in ops.items()}
    masks = [_gla_diag_mask(CHUNK, rev) for rev in (False, True)]
    attn = {key: (a_off[key] + jnp.where(masks[key[0]], a_diag[key], 0.0)).astype(BF16) for key in ops}
    o_intra = {key: _dot(attn[key], op["v"]) for key, op in ops.items()}
    s_inc = {key: _dot_tn(op["v"], op["k_state"]) for key, op in ops.items()}

    state = {key: s_ref[key] for key in streams}
    for step in range(nch):
        for (d, h), s in state.items():
            c = step if d == 0 else nch - 1 - step
            dirs[d][4][c * CHUNK:(c + 1) * CHUNK, h * dv:(h + 1) * dv] = (
                o_intra[d, h, c] + _dot_nt(ops[d, h, c]["q_state"], s.astype(BF16))).astype(BF16)
            state[d, h] = ops[d, h, c]["s_decay"] * s + s_inc[d, h, c]
    for key, s in state.items():
        s_ref[key] = s

    @pl.when(is_ctx)
    def _():
        for d, h in streams:
            ns_ref[0, d, h] = s_ref[d, h].T


def _gla_scan(q_src, q_blk, kf_src, kf_blk, kb_src, kb_blk, v_src, v_blk, cb_f, cb_b, s0, j, tok,
              *, heads, dk, dv, q_silu, hb):
    m = q_src.shape[0]
    tt = tok.seq
    assert heads % hb == 0 and q_blk % hb == 0 and kf_blk % hb == 0 and kb_blk % hb == 0 and v_blk % hb == 0
    bwd = lambda n: _bwd_block(n, tok)
    fwd = lambda n: n

    def stream(blk, k_blk):
        return [
            pl.BlockSpec((tt, hb * dk), lambda h, n: (blk(n), q_blk // hb + h)),
            pl.BlockSpec((tt, hb * dk), lambda h, n: (blk(n), k_blk // hb + h)),
            pl.BlockSpec((tt, hb * dv), lambda h, n: (blk(n), v_blk // hb + h)),
            pl.BlockSpec((tt, hb * dk), lambda h, n: (blk(n), h)),
        ]

    n_batch = tok.n_ctx_blocks
    return pl.pallas_call(
        functools.partial(_gla_scan_kernel, tok=tok, q_silu=q_silu, scale=dk ** -0.5, hb=hb, dk=dk, dv=dv),
        grid=(heads // hb, tok.n_blocks),
        in_specs=stream(fwd, kf_blk) + stream(bwd, kb_blk) + [
            pl.BlockSpec((1, 1, 2, hb, dk, dv), lambda h, n: (_dec_batch(n, tok), j, 0, h, 0, 0)),
        ],
        out_specs=[
            pl.BlockSpec((tt, hb * dv), lambda h, n: (n, h)),
            pl.BlockSpec((tt, hb * dv), lambda h, n: (bwd(n), h)),
            pl.BlockSpec((1, 2, hb, dk, dv), lambda h, n: (_state_block(n, tok), 0, h, 0, 0)),
        ],
        out_shape=[
            jax.ShapeDtypeStruct((m, heads * dv), BF16),
            jax.ShapeDtypeStruct((m, heads * dv), BF16),
            jax.ShapeDtypeStruct((n_batch, 2, heads, dk, dv), F32),
        ],
        scratch_shapes=[pltpu.VMEM((2, hb, dv, dk), F32)],
        compiler_params=_params("parallel", "arbitrary"),
        name="gla_scan",
    )(q_src, kf_src, v_src, cb_f, q_src, kb_src, v_src, cb_b, s0)


def kernel(x_prompt, x_sample, c, state_deltanet, state_gla, state_hgrn, c_ctx, ada_w, ada_b, norm1, norm2, final_norm, w_up, w_down, dn_w_in, dn_conv, dn_A_log, dn_dt_bias, dn_norm, dn_w_out, gla_w_in, gla_w_a2, gla_b_a, gla_norm, gla_w_out, hgrn_w_in, hgrn_lb_logits, hgrn_norm, hgrn_w_out):
    batch, seq, d = x_prompt.shape
    dec_batch, dec_seq, _ = x_sample.shape
    depth = ada_w.shape[0]
    tok = Tokens(m_ctx=batch * seq, seq=seq, dec_batch=dec_batch, dec_seq=dec_seq)
    assert seq % CHUNK == 0 and dec_seq % seq == 0 and seq % GRID_W == 0

    dn_hv, dn_dk, dn_dv = state_deltanet.shape[3:]
    dn_vd = dn_hv * dn_dv
    n_conv = dn_conv.shape[2]
    dn_kd = (n_conv - dn_vd) // 2
    dn_hk = dn_kd // dn_dk
    gla_h, gla_dk, gla_dv = state_gla.shape[3:]
    gla_kd, gla_vd = gla_h * gla_dk, gla_h * gla_dv
    hg_h, hg_dk, hg_dv = state_hgrn.shape[3:]

    x = jnp.concatenate([x_prompt.reshape(tok.m_ctx, d), x_sample.reshape(dec_batch * dec_seq, d)], axis=0)
    n_rows = 1 + dec_batch
    pad = (-n_rows) % 8
    cond = jnp.concatenate([c_ctx[None, :], c, jnp.zeros((pad, d), F32)], axis=0)
    mod_all = _mod_table(cond, ada_w, ada_b).reshape(depth, n_rows + pad, 6, d)

    new_states = ([], [], [])
    dn_states = None
    for i in range(depth):
        kind, j = i % N_MIXERS, i // N_MIXERS
        mod = mod_all[i]
        if kind == 0:
            w_in = dn_w_in[j]
            proj = _proj(x, norm1[i], mod, w_in[:, :n_conv + dn_vd].astype(BF16), tok)
            raw = _proj(x, norm1[i], mod, w_in[:, n_conv + dn_vd:], tok, precise=True)
            qkv = _dn_prep(proj, dn_conv[j], n_conv, dn_kd, dn_dk, tok)
            gates, gates_t = _dn_gates(raw, dn_A_log[j], dn_dt_bias[j], tok)
            o_f, o_b, dn_states = _dn_scan(qkv, gates, gates_t, state_deltanet, j, dn_states, tok,
                                           hk_n=dn_hk, hv=dn_hv, dk=dn_dk, dv=dn_dv)
            x = _out_proj(o_f, o_b, proj, n_conv // dn_vd, dn_norm[j], x, mod, dn_w_out[j].astype(BF16), tok)
        elif kind == 1:
            w_in = gla_w_in[j]
            n_main = 2 * gla_kd + 2 * gla_vd
            proj = _proj(x, norm1[i], mod, w_in[:, :n_main].astype(BF16), tok)
            a1 = _proj(x, norm1[i], mod, w_in[:, n_main:], tok, precise=True)
            cb_f, cb_b = _gla_prep(a1, gla_w_a2[j], gla_b_a[j], tok)
            o_f, o_b, ns = _gla_scan(proj, 0, proj, gla_h, proj, gla_h, proj, (2 * gla_kd) // gla_dv,
                                     cb_f, cb_b, state_gla, j, tok,
                                     heads=gla_h, dk=gla_dk, dv=gla_dv, q_silu=False, hb=1)
            x = _out_proj(o_f, o_b, proj, (2 * gla_kd + gla_vd) // gla_vd, gla_norm[j], x, mod,
                          gla_w_out[j].astype(BF16), tok)
        else:
            proj = _proj(x, norm1[i], mod, hgrn_w_in[j].astype(BF16), tok)
            k_f, k_b, cb_f, cb_b = _hgrn_prep(proj, hgrn_lb_logits, i, d, tok)
            o_f, o_b, ns = _gla_scan(proj, 0, k_f, 0, k_b, 0, proj, d // hg_dv, cb_f, cb_b, state_hgrn, j, tok,
                                     heads=hg_h, dk=hg_dk, dv=hg_dv, q_silu=True, hb=min(hg_h, 4))
            x = _out_proj(o_f, o_b, proj, 2, hgrn_norm[j], x, mod, hgrn_w_out[j].astype(BF16), tok)
        if kind != 0:
            new_states[kind].append(ns)
        x = _mlp(x, norm2[i], mod, w_up[i].astype(BF16), w_down[i].astype(BF16), final_norm, tok,
                 final=(i == depth - 1))

    y_prompt = x[:tok.m_ctx].reshape(batch, seq, d)
    y_sample = x[tok.m_ctx:].reshape(dec_batch, dec_seq, d)
    stacked = [s[0][:, None] if len(s) == 1 else jnp.stack(s, axis=1) for s in new_states[1:]]
    return (y_prompt, y_sample, dn_states) + tuple(stacked)
```

```python
import functools
from typing import NamedTuple

import jax
import jax.numpy as jnp
from jax import lax
from jax.experimental import pallas as pl
from jax.experimental.pallas import tpu as pltpu

F32 = jnp.float32
BF16 = jnp.bfloat16

EPS = 1e-6
GRID_W = 64
CONV_K = 5
GLA_RANK = 16
GLA_TAU = 16.0
N_MIXERS = 3
CHUNK = 64
SUB = 16
EXP_CLAMP = 80.0
LANES = 128
V7X_MXU_COLS = 256
V7X_VMEM_LIMIT_BYTES = 56 * 1024 * 1024
HIGHEST = lax.Precision.HIGHEST


class Tokens(NamedTuple):
    m_ctx: int
    seq: int
    dec_batch: int
    dec_seq: int

    @property
    def m(self):
        return self.m_ctx + self.dec_batch * self.dec_seq

    @property
    def n_ctx_blocks(self):
        return self.m_ctx // self.seq

    @property
    def blocks_per_dec(self):
        return self.dec_seq // self.seq

    @property
    def n_blocks(self):
        return self.m // self.seq


def _tile(n, pref):
    t = min(n, pref)
    while n % t:
        t //= 2
    return t


def _token_tile(tok, pref):
    t = pref
    while tok.m_ctx % t or tok.dec_seq % t:
        t //= 2
    return t


def _params(*sem):
    return pltpu.CompilerParams(dimension_semantics=sem, vmem_limit_bytes=V7X_VMEM_LIMIT_BYTES)


def _mod_row(i, tm, tok):
    start = i * tm
    return jnp.where(start < tok.m_ctx, 0, 1 + (start - tok.m_ctx) // tok.dec_seq)


def _sigmoid(x):
    return 1.0 / (1.0 + jnp.exp(-x))


def _softplus(x):
    return jnp.maximum(x, 0.0) + jnp.log(1.0 + jnp.exp(-jnp.abs(x)))


def _norm_mod(x, nw, shift, scale):
    ms = jnp.mean(x * x, axis=-1, keepdims=True)
    return x * lax.rsqrt(ms + EPS) * nw * (1.0 + scale) + shift


def _dot(a, b):
    return jnp.dot(a, b, preferred_element_type=F32)


def _dot_nt(a, b):
    return lax.dot_general(a, b, (((1,), (1,)), ((), ())), preferred_element_type=F32)


def _dot_tn(a, b):
    return lax.dot_general(a, b, (((0,), (0,)), ((), ())), preferred_element_type=F32)


def _split_dot(tri, x):
    hi = x.astype(BF16)
    lo = (x - hi.astype(F32)).astype(BF16)
    return _dot(tri, hi) + _dot(tri, lo)


def _split_matmul(a, b):
    ah = a.astype(BF16)
    al = (a - ah.astype(F32)).astype(BF16)
    bh = b.astype(BF16)
    bl = (b - bh.astype(F32)).astype(BF16)
    return _dot(ah, bh) + _dot(al, bh) + _dot(ah, bl)


def _tri(n, upper):
    r = lax.broadcasted_iota(jnp.int32, (n, n), 0)
    c = lax.broadcasted_iota(jnp.int32, (n, n), 1)
    return jnp.where((r <= c) if upper else (r >= c), 1.0, 0.0).astype(BF16)


def _mod_kernel(c_ref, w_ref, b_ref, o_ref):
    c = c_ref[...]
    s = c * _sigmoid(c)
    o_ref[0] = jnp.dot(s, w_ref[0], precision=HIGHEST, preferred_element_type=F32) + b_ref[0]


def _mod_table(cond, ada_w, ada_b):
    depth, d, n = ada_w.shape
    rows = cond.shape[0]
    tn = _tile(n, 1024)
    return pl.pallas_call(
        _mod_kernel,
        grid=(depth, n // tn),
        in_specs=[
            pl.BlockSpec((rows, d), lambda l, j: (0, 0)),
            pl.BlockSpec((1, d, tn), lambda l, j: (l, 0, j)),
            pl.BlockSpec((1, 1, tn), lambda l, j: (l, 0, j)),
        ],
        out_specs=pl.BlockSpec((1, rows, tn), lambda l, j: (l, 0, j)),
        out_shape=jax.ShapeDtypeStruct((depth, rows, n), F32),
        compiler_params=_params("parallel", "parallel"),
        name="mod_table",
    )(cond, ada_w, ada_b.reshape(depth, 1, n))


def _proj_kernel(x_ref, nw_ref, mod_ref, w_ref, o_ref, h_ref, *, precise):
    @pl.when(pl.program_id(1) == 0)
    def _():
        h = _norm_mod(x_ref[...], nw_ref[...], mod_ref[0, 0:1, :], mod_ref[0, 1:2, :])
        h_ref[...] = h.astype(h_ref.dtype)

    if precise:
        o_ref[...] = _split_matmul(h_ref[...], w_ref[...])
    else:
        o_ref[...] = _dot(h_ref[...], w_ref[...]).astype(o_ref.dtype)


def _proj(x, nw, mod, w, tok, *, precise=False, tm=1024, tn=1024):
    m, d = x.shape
    n = w.shape[1]
    tm = _token_tile(tok, tm)
    tn = _tile(n, tn)
    return pl.pallas_call(
        functools.partial(_proj_kernel, precise=precise),
        grid=(m // tm, n // tn),
        in_specs=[
            pl.BlockSpec((tm, d), lambda i, j: (i, 0)),
            pl.BlockSpec((1, d), lambda i, j: (0, 0)),
            pl.BlockSpec((1, 6, d), lambda i, j: (_mod_row(i, tm, tok), 0, 0)),
            pl.BlockSpec((d, tn), lambda i, j: (0, j)),
        ],
        out_specs=pl.BlockSpec((tm, tn), lambda i, j: (i, j)),
        out_shape=jax.ShapeDtypeStruct((m, n), F32 if precise else BF16),
        scratch_shapes=[pltpu.VMEM((tm, d), F32 if precise else BF16)],
        compiler_params=_params("parallel", "arbitrary"),
        name="proj_precise" if precise else "proj",
    )(x, nw.reshape(1, d), mod, w)


def _mlp_kernel(x_ref, nw_ref, mod_ref, wu_ref, wd_ref, fw_ref, *rest, n_ctx_tiles):
    out_refs, (h_ref, acc_ref) = rest[:-2], rest[-2:]
    final = n_ctx_tiles is not None
    i = pl.program_id(0)
    f = pl.program_id(1)

    @pl.when(f == 0)
    def _():
        h = _norm_mod(x_ref[...], nw_ref[...], mod_ref[0, 3:4, :], mod_ref[0, 4:5, :])
        h_ref[...] = h.astype(BF16)
        acc_ref[...] = jnp.zeros_like(acc_ref)

    a = jnp.maximum(_dot(h_ref[...], wu_ref[...]), 0.0)
    acc_ref[...] += _dot((a * a).astype(BF16), wd_ref[...])

    last = f == pl.num_programs(1) - 1

    def result():
        y = x_ref[...] + mod_ref[0, 5:6, :] * acc_ref[...]
        if final:
            ms = jnp.mean(y * y, axis=-1, keepdims=True)
            y = y * lax.rsqrt(ms + EPS) * fw_ref[...]
        return y

    if not final:
        @pl.when(last)
        def _():
            out_refs[0][...] = result()
    else:
        @pl.when(jnp.logical_and(last, i < n_ctx_tiles))
        def _():
            out_refs[0][...] = result()

        @pl.when(jnp.logical_and(last, i >= n_ctx_tiles))
        def _():
            out_refs[1][...] = result()


def _mlp(x, nw, mod, w_up, w_down, final_w, tok, *, final, tm=512, tf=1024):
    m, d = x.shape
    ff = w_up.shape[1]
    tm = _token_tile(tok, tm)
    tf = _tile(ff, tf)
    n_ctx_tiles = tok.m_ctx // tm
    if final:
        out_specs = [pl.BlockSpec((tm, d), lambda i, f: (jnp.minimum(i, n_ctx_tiles - 1), 0)),
                     pl.BlockSpec((tm, d), lambda i, f: (jnp.maximum(i - n_ctx_tiles, 0), 0))]
        out_shape = [jax.ShapeDtypeStruct((tok.m_ctx, d), F32), jax.ShapeDtypeStruct((m - tok.m_ctx, d), F32)]
    else:
        out_specs = pl.BlockSpec((tm, d), lambda i, f: (i, 0))
        out_shape = jax.ShapeDtypeStruct((m, d), F32)
    return pl.pallas_call(
        functools.partial(_mlp_kernel, n_ctx_tiles=n_ctx_tiles if final else None),
        grid=(m // tm, ff // tf),
        in_specs=[
            pl.BlockSpec((tm, d), lambda i, f: (i, 0)),
            pl.BlockSpec((1, d), lambda i, f: (0, 0)),
            pl.BlockSpec((1, 6, d), lambda i, f: (_mod_row(i, tm, tok), 0, 0)),
            pl.BlockSpec((d, tf), lambda i, f: (0, f)),
            pl.BlockSpec((tf, d), lambda i, f: (f, 0)),
            pl.BlockSpec((1, d), lambda i, f: (0, 0)),
        ],
        out_specs=out_specs,
        out_shape=out_shape,
        scratch_shapes=[pltpu.VMEM((tm, d), BF16), pltpu.VMEM((tm, d), F32)],
        compiler_params=_params("arbitrary" if final else "parallel", "arbitrary"),
        name="mlp_final" if final else "mlp",
    )(x, nw.reshape(1, d), mod, w_up, w_down, final_w.reshape(1, d))


def _out_kernel(of_ref, ob_ref, g_ref, nw_ref, x_ref, mod_ref, w_ref, o_ref, y0_ref, y1_ref, *, n_heads, dh):
    i = pl.program_id(0)
    d = w_ref.shape[1]
    n_chunks = d // V7X_MXU_COLS
    heads_per_chunk = -(-n_heads // n_chunks)

    @pl.when(i == 0)
    def _():
        y1_ref[...] = jnp.zeros_like(y1_ref)

    def body(y_write, y_read):
        ones = jnp.full((dh, LANES), 1.0 / dh, BF16) if dh & (dh - 1) == 0 else None
        for c in range(n_chunks):
            cols = slice(c * V7X_MXU_COLS, (c + 1) * V7X_MXU_COLS)
            o_ref[:, cols] = x_ref[:, cols] + mod_ref[0, 2:3, cols] * _dot(y_read[...], w_ref[:, cols])
            for h in range(c * heads_per_chunk, min((c + 1) * heads_per_chunk, n_heads)):
                sl = slice(h * dh, (h + 1) * dh)
                o = of_ref[:, sl].astype(F32) + ob_ref[:, sl].astype(F32)
                if ones is None:
                    ms = _dot((o * o).astype(BF16), jnp.ones((dh, LANES), BF16)) * (1.0 / dh)
                else:
                    ms = _dot((o * o).astype(BF16), ones)
                inv = lax.rsqrt(ms + EPS)
                half_g = 0.5 * g_ref[:, sl].astype(F32)
                silu = half_g + half_g * jnp.tanh(half_g)
                y = o * nw_ref[...] * silu
                for k in range(dh // LANES):
                    lo = h * dh + k * LANES
                    y_write[:, lo:lo + LANES] = (y[:, k * LANES:(k + 1) * LANES] * inv).astype(BF16)

    @pl.when(i % 2 == 0)
    def _():
        body(y0_ref, y1_ref)

    @pl.when(i % 2 == 1)
    def _():
        body(y1_ref, y0_ref)


def _out_proj(o_f, o_b, gate_src, gate_blk, norm_w, x, mod, w_out, tok, *, tm=256):
    m, vd = o_f.shape
    d = x.shape[1]
    dh = norm_w.shape[0]
    tm = _token_tile(tok, tm)
    n_tiles = m // tm
    assert d % V7X_MXU_COLS == 0 and dh % LANES == 0
    cur = lambda i: jnp.minimum(i, n_tiles - 1)
    prev = lambda i: jnp.maximum(i - 1, 0)
    return pl.pallas_call(
        functools.partial(_out_kernel, n_heads=vd // dh, dh=dh),
        grid=(n_tiles + 1,),
        in_specs=[
            pl.BlockSpec((tm, vd), lambda i: (cur(i), 0)),
            pl.BlockSpec((tm, vd), lambda i: (cur(i), 0)),
            pl.BlockSpec((tm, vd), lambda i: (cur(i), gate_blk)),
            pl.BlockSpec((1, dh), lambda i: (0, 0)),
            pl.BlockSpec((tm, d), lambda i: (prev(i), 0)),
            pl.BlockSpec((1, 6, d), lambda i: (_mod_row(prev(i), tm, tok), 0, 0)),
            pl.BlockSpec((vd, d), lambda i: (0, 0), pipeline_mode=pl.Buffered(1)),
        ],
        out_specs=pl.BlockSpec((tm, d), lambda i: (prev(i), 0)),
        out_shape=jax.ShapeDtypeStruct((m, d), F32),
        scratch_shapes=[pltpu.VMEM((tm, vd), BF16), pltpu.VMEM((tm, vd), BF16)],
        compiler_params=_params("arbitrary"),
        name="out_proj",
    )(o_f, o_b, gate_src, norm_w.reshape(1, dh), x, mod, w_out)


def _bwd_block(n, tok):
    nc, per = tok.n_ctx_blocks, tok.blocks_per_dec
    r = jnp.maximum(n - nc, 0)
    return jnp.where(n < nc, n, nc + (r // per) * per + (per - 1 - r % per))


def _dec_batch(n, tok):
    return jnp.maximum(n - tok.n_ctx_blocks, 0) // tok.blocks_per_dec


def _state_block(n, tok):
    return jnp.minimum(n, tok.n_ctx_blocks - 1)


def _seq_flags(n, tok):
    nc, per = tok.n_ctx_blocks, tok.blocks_per_dec
    is_ctx = n < nc
    r = jnp.maximum(n - nc, 0) % per
    return is_ctx, jnp.logical_and(jnp.logical_not(is_ctx), r == 0)


def _dn_prep_kernel(x_ref, shift_ref, w_ref, o_ref, *, n_q_blocks, dk):
    j = pl.program_id(1)
    x = x_ref[...]
    tc = x.shape[1]
    acc = None
    for s in range(CONV_K):
        xs = x.astype(F32) if s == CONV_K // 2 else _dot(shift_ref[0, s], x)
        term = xs * w_ref[s:s + 1, :]
        acc = term if acc is None else acc + term
    y = acc * _sigmoid(acc)

    def normed(scale):
        ones = jnp.ones((dk, dk), BF16)
        for g in range(tc // dk):
            blk = y[:, g * dk:(g + 1) * dk]
            ss = _dot((blk * blk).astype(BF16), ones)
            o_ref[:, g * dk:(g + 1) * dk] = (blk * (lax.rsqrt(ss + EPS) * scale)).astype(o_ref.dtype)

    @pl.when(j < n_q_blocks)
    def _():
        normed(dk ** -0.5)

    @pl.when(jnp.logical_and(j >= n_q_blocks, j < 2 * n_q_blocks))
    def _():
        normed(1.0)

    @pl.when(j >= 2 * n_q_blocks)
    def _():
        o_ref[...] = y.astype(o_ref.dtype)


def _conv_shift_matrices(tt):
    t = jnp.arange(tt)[:, None]
    u = jnp.arange(tt)[None, :]
    mats = []
    for seg in (tt, GRID_W):
        same = (t // seg) == (u // seg)
        mats.append(jnp.stack([jnp.logical_and(u == t + s - CONV_K // 2, same) for s in range(CONV_K)]))
    return jnp.stack(mats).astype(BF16)


def _dn_prep(proj, conv_w, n_conv, kd, dk, tok):
    m = proj.shape[0]
    tt = tok.seq
    tc = _tile(kd, 1024)
    n_ctx = tok.n_ctx_blocks
    return pl.pallas_call(
        functools.partial(_dn_prep_kernel, n_q_blocks=kd // tc, dk=dk),
        grid=(m // tt, n_conv // tc),
        in_specs=[
            pl.BlockSpec((tt, tc), lambda i, j: (i, j)),
            pl.BlockSpec((1, CONV_K, tt, tt), lambda i, j: (jnp.where(i < n_ctx, 0, 1), 0, 0, 0)),
            pl.BlockSpec((CONV_K, tc), lambda i, j: (0, j)),
        ],
        out_specs=pl.BlockSpec((tt, tc), lambda i, j: (i, j)),
        out_shape=jax.ShapeDtypeStruct((m, n_conv), BF16),
        compiler_params=_params("parallel", "parallel"),
        name="dn_prep",
    )(proj, _conv_shift_matrices(tt), conv_w)


def _dn_gate_kernel(x_ref, a_ref, bias_ref, o_ref, gt_ref, *, hv, tt):
    x = x_ref[...]
    lanes = x.shape[1]
    lane = lax.broadcasted_iota(jnp.int32, (1, lanes), 1)
    is_fwd = lane < 3 * hv
    g = -a_ref[...] * _softplus(x + bias_ref[...])
    beta = pltpu.roll(_sigmoid(x), 2 * hv, 1)
    lower = _tri(CHUNK, upper=False)
    upper = _tri(CHUNK, upper=True)
    for c in range(tt // CHUNK):
        rows = slice(c * CHUNK, (c + 1) * CHUNK)
        gch = g[rows]
        gc = jnp.where(is_fwd, _split_dot(lower, gch), _split_dot(upper, gch))
        g_end = jnp.where(is_fwd, gc[CHUNK - 1:CHUNK, :], gc[0:1, :])
        eg = jnp.exp(gc)
        o_ref[0, rows, :] = beta[rows]
        o_ref[1, rows, :] = gc
        o_ref[2, rows, :] = eg
        o_ref[3, rows, :] = jnp.exp(g_end - gc)
        o_ref[4, rows, :] = beta[rows] * eg
    gt_ref[...] = o_ref[1].T


def _dn_gates(raw, a_log, dt_bias, tok):
    m, lanes = raw.shape
    hv = lanes // 4
    tt = tok.seq
    zeros = jnp.zeros((2 * hv,), F32)
    a_full = jnp.concatenate([zeros, jnp.exp(a_log).reshape(-1)]).reshape(1, lanes)
    bias_full = jnp.concatenate([zeros, dt_bias.reshape(-1)]).reshape(1, lanes)
    return pl.pallas_call(
        functools.partial(_dn_gate_kernel, hv=hv, tt=tt),
        grid=(m // tt,),
        in_specs=[
            pl.BlockSpec((tt, lanes), lambda i: (i, 0)),
            pl.BlockSpec((1, lanes), lambda i: (0, 0)),
            pl.BlockSpec((1, lanes), lambda i: (0, 0)),
        ],
        out_specs=[
            pl.BlockSpec((5, tt, lanes), lambda i: (0, i, 0)),
            pl.BlockSpec((lanes, tt), lambda i: (0, i)),
        ],
        out_shape=[jax.ShapeDtypeStruct((5, m, lanes), F32), jax.ShapeDtypeStruct((lanes, m), F32)],
        compiler_params=_params("parallel"),
        name="dn_gates",
    )(raw, a_full, bias_full)


def _dn_scan_kernel(qf_ref, kf_ref, vf_ref, gf_ref, gtf_ref, qb_ref, kb_ref, vb_ref, gb_ref, gtb_ref, s0_ref,
                    of_ref, ob_ref, ns_ref, s_ref, u0_ref, wq_ref, attn_ref, kd_ref, ge_ref, *, tok, hv, dv, hkb):
    hk0 = pl.program_id(0) * hkb
    n = pl.program_id(1)
    tt = tok.seq
    nch = tt // CHUNK
    dk = qf_ref.shape[1] // hkb
    assert dv == 2 * CHUNK and CHUNK == 4 * SUB
    has_prev = n > 0
    is_ctx, dec_start = _seq_flags(jnp.maximum(n - 1, 0), tok)
    wslot = n % 2
    rslot = 1 - wslot
    scratch = (u0_ref, wq_ref, attn_ref, kd_ref, ge_ref)

    @pl.when(n == 0)
    def _():
        s_ref[...] = jnp.zeros_like(s_ref)
        for ref in scratch:
            ref[1] = jnp.zeros(ref.shape[1:], ref.dtype)

    @pl.when(jnp.logical_and(has_prev, is_ctx))
    def _():
        s_ref[...] = jnp.zeros_like(s_ref)

    @pl.when(jnp.logical_and(has_prev, dec_start))
    def _():
        s_ref[...] = s0_ref[0, 0]

    bf = lambda x: x.astype(BF16)
    lanes = gf_ref.shape[2]
    lane_iota = lax.broadcasted_iota(jnp.int32, (1, lanes), 1)
    ri = lax.broadcasted_iota(jnp.int32, (CHUNK, 2 * CHUNK), 0)
    li = lax.broadcasted_iota(jnp.int32, (CHUNK, 2 * CHUNK), 1)
    left = li < CHUNK
    ci = li % CHUNK
    same = [(ri // b) == (ci // b) for b in (SUB, 2 * SUB)]
    zero_b = jnp.zeros((CHUNK, 2 * CHUNK), BF16)

    def block_diag(xb):
        return jnp.concatenate([jnp.where(left, xb, zero_b), jnp.where(left, zero_b, xb)], axis=0)

    def pair_dot(a, b):
        return _dot(bf(a), block_diag(bf(b)))

    dirs = ((qf_ref, kf_ref, vf_ref, gf_ref, gtf_ref, of_ref), (qb_ref, kb_ref, vb_ref, gb_ref, gtb_ref, ob_ref))

    def prepare():
        ch = {}
        for d, kh in [(d, kh) for d in range(2) for kh in range(hkb)]:
            q_ref, k_ref, v_ref, g_ref, gt_ref, _ = dirs[d]
            incl = (ri >= ci) if d == 0 else (ri <= ci)
            strict = (ri > ci) if d == 0 else (ri < ci)
            end_row = CHUNK - 1 if d == 0 else 0
            q = q_ref[:, kh * dk:(kh + 1) * dk]
            k = k_ref[:, kh * dk:(kh + 1) * dk]
            kf32 = k.astype(F32)
            qf32 = q.astype(F32)
            cols, g_rows, vs = [], [], []
            for e in range(2):
                a_lane = 2 * hv + d * hv + 2 * (hk0 + kh) + e
                sel = lane_iota == a_lane
                cols.append([jnp.sum(jnp.where(sel, g_ref[i], 0.0), axis=1, keepdims=True) for i in range(5)])
                g_rows.append(gt_ref[pl.ds(a_lane, 1), :])
                vs.append(v_ref[:, (2 * kh + e) * dv:(2 * kh + e + 1) * dv].astype(F32))
            for c in range(nch):
                rows = slice(c * CHUNK, (c + 1) * CHUNK)
                pair = lambda i: jnp.where(left, cols[0][i][rows], cols[1][i][rows])
                k2 = jnp.concatenate([k[rows], k[rows]], axis=0)
                kk = _dot_nt(k[rows], k2)
                qk = _dot_nt(q[rows], k2)
                g_row = jnp.concatenate([g_rows[0][:, rows], g_rows[1][:, rows]], axis=1)
                diff = pair(1) - g_row
                decay = jnp.where(incl, jnp.exp(jnp.where(incl, diff, 0.0)), 0.0)
                p = jnp.where(strict, -(pair(0) * kk * decay), 0.0)
                attn_ref[wslot, d, kh, c] = bf(qk * decay)
                for e in range(2):
                    beta_c, _, eg_c, ekd_c, beg_c = cols[e]
                    qg = qf32[rows] * eg_c[rows]
                    kd_ref[wslot, d, 2 * kh + e, c] = bf(kf32[rows] * ekd_c[rows])
                    wq_ref[wslot, d, 2 * kh + e, c, CHUNK:, :] = bf(qg)
                    g_end = eg_c[c * CHUNK + end_row:c * CHUNK + end_row + 1, :]
                    ge_ref[wslot, d, 2 * kh + e, c] = jnp.broadcast_to(g_end, ge_ref.shape[4:])
                ch[kh, d, c] = dict(
                    pd=jnp.where(same[0], p, 0.0),
                    p32=jnp.where(jnp.logical_and(same[1], jnp.logical_not(same[0])), p, 0.0),
                    p64=jnp.where(same[1], 0.0, p),
                    r=[jnp.concatenate([cols[e][0][rows] * vs[e][rows], cols[e][4][rows] * kf32[rows]], axis=1)
                       for e in range(2)])
            yield
        for c in ch.values():
            c["q"] = pair_dot(c["pd"], c["pd"])
            c["xm"] = c["pd"]
        yield
        for it in range(2):
            for c in ch.values():
                q, xm = c["q"], c["xm"]
                both = _dot(bf(q), jnp.concatenate([block_diag(bf(xm)), block_diag(bf(q))], axis=1))
                c["xm"] = xm + q + both[:, :2 * CHUNK]
                c["q"] = both[:, 2 * CHUNK:]
            yield
        for c in ch.values():
            q, xm = c.pop("q"), c["xm"]
            c["xm"] = xm + q + pair_dot(q, xm)
        yield
        for level in ("p32", "p64"):
            for c in ch.values():
                c["y"] = c[level] + pair_dot(c[level], c["xm"])
            yield
            for c in ch.values():
                xm, y = c["xm"], c.pop("y")
                c["xm"] = xm + y + pair_dot(xm, y)
            yield
        zero_r = jnp.zeros((CHUNK, 2 * dv), BF16)
        for (kh, d, cidx), c in ch.items():
            r0, r1 = c["r"]
            rhs = jnp.concatenate([jnp.concatenate([bf(r0), zero_r], axis=1),
                                   jnp.concatenate([zero_r, bf(r1)], axis=1)], axis=0)
            big = _dot(bf(c["xm"]), rhs)
            sol = [r0 + big[:, :2 * dv], r1 + big[:, 2 * dv:]]
            u0_ref[wslot, d, kh, cidx] = jnp.concatenate([sol[0][:, :dv], sol[1][:, :dv]], axis=1)
            for e in range(2):
                wq_ref[wslot, d, 2 * kh + e, cidx, :CHUNK, :] = bf(sol[e][:, dv:])
        yield

    def recur():
        state = {(d, e): s_ref[d, e] for d in range(2) for e in range(2 * hkb)}
        zero_u = jnp.zeros((CHUNK, dv), BF16)
        for step in range(nch):
            chunk = lambda d: step if d == 0 else nch - 1 - step
            ws = {(d, e): _dot(wq_ref[rslot, d, e, chunk(d)], bf(s)) for (d, e), s in state.items()}
            yield
            for d, kh in [(d, kh) for d in range(2) for kh in range(hkb)]:
                c = chunk(d)
                u0 = u0_ref[rslot, d, kh, c]
                ev = [2 * kh, 2 * kh + 1]
                ub = [bf(u0[:, e * dv:(e + 1) * dv] - ws[d, ev[e]][:CHUNK]) for e in range(2)]
                u2 = jnp.concatenate([jnp.concatenate([ub[0], zero_u], axis=1),
                                      jnp.concatenate([zero_u, ub[1]], axis=1)], axis=0)
                o = (_dot(attn_ref[rslot, d, kh, c], u2)
                     + jnp.concatenate([ws[d, ev[0]][CHUNK:], ws[d, ev[1]][CHUNK:]], axis=1))
                dirs[d][5][c * CHUNK:(c + 1) * CHUNK, 2 * kh * dv:2 * (kh + 1) * dv] = bf(o)
                for e in range(2):
                    state[d, ev[e]] = (ge_ref[rslot, d, ev[e], c][0:1, :] * state[d, ev[e]]
                                       + _dot_tn(kd_ref[rslot, d, ev[e], c], ub[e]))
            yield
        for (d, e), s in state.items():
            s_ref[d, e] = s
        yield

    halves = [recur(), prepare()]
    while halves:
        for g in list(halves):
            if next(g, "done") == "done":
                halves.remove(g)

    @pl.when(jnp.logical_and(has_prev, is_ctx))
    def _():
        ns_ref[0, 0] = s_ref[...]


def _dn_scan_kernel_aliased(*refs, **kw):
    n_in = 11
    return _dn_scan_kernel(*refs[:n_in], *refs[n_in + 1:], **kw)


def _dn_scan(qkv, gates, gates_t, s0, j, new_states, tok, *, hk_n, hv, dk, dv):
    m = qkv.shape[0]
    tt = tok.seq
    rep = hv // hk_n
    assert rep == 2
    lanes = gates.shape[2]
    kd = hk_n * dk
    nb = tok.n_blocks
    nch = tt // CHUNK
    fwd = lambda n: jnp.minimum(n, nb - 1)
    bwd = lambda n: _bwd_block(jnp.minimum(n, nb - 1), tok)
    prev = lambda n: jnp.maximum(n - 1, 0)

    hkb = max(b for b in (4, 2, 1) if hk_n % b == 0)
    hvb = hkb * rep

    def stream(blk):
        return [
            pl.BlockSpec((tt, hkb * dk), lambda h, n: (blk(n), h)),
            pl.BlockSpec((tt, hkb * dk), lambda h, n: (blk(n), hk_n // hkb + h)),
            pl.BlockSpec((tt, hvb * dv), lambda h, n: (blk(n), (2 * kd) // (hvb * dv) + h)),
            pl.BlockSpec((5, tt, lanes), lambda h, n: (0, blk(n), 0)),
            pl.BlockSpec((lanes, tt), lambda h, n: (0, blk(n))),
        ]

    n_batch = tok.n_ctx_blocks
    n_layers = s0.shape[1]
    aliased = new_states is not None
    extra_in = [pl.BlockSpec(memory_space=pl.ANY)] if aliased else []
    extra_args = (new_states,) if aliased else ()
    return pl.pallas_call(
        functools.partial(_dn_scan_kernel_aliased if aliased else _dn_scan_kernel, tok=tok, hv=hv, dv=dv, hkb=hkb),
        grid=(hk_n // hkb, nb + 1),
        in_specs=stream(fwd) + stream(bwd) + [
            pl.BlockSpec((1, 1, 2, hvb, dk, dv), lambda h, n: (_dec_batch(prev(n), tok), j, 0, h, 0, 0)),
        ] + extra_in,
        out_specs=[
            pl.BlockSpec((tt, hvb * dv), lambda h, n: (prev(n), h)),
            pl.BlockSpec((tt, hvb * dv), lambda h, n: (_bwd_block(prev(n), tok), h)),
            pl.BlockSpec((1, 1, 2, hvb, dk, dv), lambda h, n: (_state_block(prev(n), tok), j, 0, h, 0, 0)),
        ],
        out_shape=[
            jax.ShapeDtypeStruct((m, hv * dv), BF16),
            jax.ShapeDtypeStruct((m, hv * dv), BF16),
            jax.ShapeDtypeStruct((n_batch, n_layers, 2, hv, dk, dv), F32),
        ],
        input_output_aliases={11: 2} if aliased else {},
        scratch_shapes=[
            pltpu.VMEM((2, hvb, dk, dv), F32),
            pltpu.VMEM((2, 2, hkb, nch, CHUNK, rep * dv), F32),
            pltpu.VMEM((2, 2, hvb, nch, 2 * CHUNK, dk), BF16),
            pltpu.VMEM((2, 2, hkb, nch, CHUNK, rep * CHUNK), BF16),
            pltpu.VMEM((2, 2, hvb, nch, CHUNK, dk), BF16),
            pltpu.VMEM((2, 2, hvb, nch, 8, dv), F32),
        ],
        compiler_params=_params("parallel", "arbitrary"),
        name="dn_scan",
    )(qkv, qkv, qkv, gates, gates_t, qkv, qkv, qkv, gates, gates_t, s0, *extra_args)


def _chunk_cums(la_f, la_b, cbf_ref, cbb_ref, tt):
    lower = _tri(CHUNK, upper=False)
    upper = _tri(CHUNK, upper=True)
    for c in range(tt // CHUNK):
        rows = slice(c * CHUNK, (c + 1) * CHUNK)
        cbf_ref[rows, :] = _split_dot(lower, la_f[rows])
        cbb_ref[rows, :] = _split_dot(upper, la_b[rows])


def _log_sigmoid(x):
    return jnp.minimum(x, 0.0) - jnp.log(1.0 + jnp.exp(-jnp.abs(x)))


def _gla_prep_kernel(a1_ref, w2_ref, b_ref, cbf_ref, cbb_ref, *, tt):
    a1 = a1_ref[...]
    la = [_log_sigmoid(_dot(a1, w2_ref[d]) + b_ref[d]) * (1.0 / GLA_TAU) for d in range(2)]
    _chunk_cums(la[0], la[1], cbf_ref, cbb_ref, tt)


def _gla_prep(a1, w_a2, b_a, tok):
    m, r2 = a1.shape
    kd = w_a2.shape[2]
    tt = tok.seq
    tc = _tile(kd, 512)
    zeros = jnp.zeros((GLA_RANK, kd), F32)
    w2 = jnp.stack([jnp.concatenate([w_a2[0], zeros]), jnp.concatenate([zeros, w_a2[1]])])
    out = jax.ShapeDtypeStruct((m, kd), F32)
    return pl.pallas_call(
        functools.partial(_gla_prep_kernel, tt=tt),
        grid=(m // tt, kd // tc),
        in_specs=[
            pl.BlockSpec((tt, r2), lambda i, j: (i, 0)),
            pl.BlockSpec((2, r2, tc), lambda i, j: (0, 0, j)),
            pl.BlockSpec((2, 1, tc), lambda i, j: (0, 0, j)),
        ],
        out_specs=[pl.BlockSpec((tt, tc), lambda i, j: (i, j))] * 2,
        out_shape=[out, out],
        compiler_params=_params("parallel", "parallel"),
        name="gla_prep",
    )(a1, w2, b_a.reshape(2, 1, kd))


def _hgrn_prep_kernel(ff_ref, fb_ref, lb_ref, kf_ref, kb_ref, cbf_ref, cbb_ref, *, layer, tt):
    logits = lb_ref[...]
    ex = jnp.exp(logits - jnp.max(logits, axis=0, keepdims=True))
    probs = ex / jnp.sum(ex, axis=0, keepdims=True)
    lb = jnp.sum(probs[1:layer + 1], axis=0, keepdims=True)
    la = []
    for f_ref, k_ref in ((ff_ref, kf_ref), (fb_ref, kb_ref)):
        fl = f_ref[...].astype(F32)
        sg = _sigmoid(fl)
        la.append(jnp.log(lb + (1.0 - lb) * sg))
        k_ref[...] = (1.0 - lb) * (1.0 - sg)
    _chunk_cums(la[0], la[1], cbf_ref, cbb_ref, tt)


def _hgrn_prep(proj, lb_logits, layer, d, tok):
    m = proj.shape[0]
    depth = lb_logits.shape[0]
    tt = tok.seq
    tc = _tile(d, 512)
    nb = d // tc
    out = jax.ShapeDtypeStruct((m, d), F32)
    return pl.pallas_call(
        functools.partial(_hgrn_prep_kernel, layer=layer, tt=tt),
        grid=(m // tt, nb),
        in_specs=[
            pl.BlockSpec((tt, tc), lambda i, j: (i, 3 * nb + j)),
            pl.BlockSpec((tt, tc), lambda i, j: (i, 4 * nb + j)),
            pl.BlockSpec((depth, tc), lambda i, j: (0, j)),
        ],
        out_specs=[pl.BlockSpec((tt, tc), lambda i, j: (i, j))] * 4,
        out_shape=[out] * 4,
        compiler_params=_params("parallel", "parallel"),
        name="hgrn_prep",
    )(proj, proj, lb_logits)


def _gla_offdiag_pairs(c):
    pairs = []
    size = c
    while size > SUB:
        half = size // 2
        for start in range(0, c, size):
            pairs.append(((start + half, start + size), (start, start + half)))
        size = half
    return pairs


def _gla_operands(q, k, cb, rev):
    c, dk = q.shape
    zero_rows = lambda n: jnp.zeros((n, dk), BF16)

    def padded(x, rows):
        parts = ([zero_rows(rows[0])] if rows[0] else []) + [x] + ([zero_rows(c - rows[1])] if rows[1] < c else [])
        return jnp.concatenate(parts, axis=0) if len(parts) > 1 else x

    q_parts, k_parts = [], []
    for qr, kr in _gla_offdiag_pairs(c):
        if rev:
            qr, kr = kr, qr
        ref_row = qr[0] - 1 if not rev else qr[1]
        ref = cb[ref_row:ref_row + 1, :]
        qs, ks = slice(*qr), slice(*kr)
        q_parts.append(padded((q[qs] * jnp.exp(cb[qs] - ref)).astype(BF16), qr))
        k_parts.append(padded((k[ks] * jnp.exp(ref - cb[ks])).astype(BF16), kr))
    refs = []
    for blk in range(c // SUB):
        lo, hi = blk * SUB, (blk + 1) * SUB
        if not rev:
            ref = cb[lo - 1:lo, :] if blk > 0 else jnp.zeros_like(cb[0:1, :])
        else:
            ref = cb[hi:hi + 1, :] if hi < c else jnp.zeros_like(cb[0:1, :])
        refs.append(jnp.broadcast_to(ref, (SUB, dk)))
    ref_d = jnp.concatenate(refs, axis=0)
    q_diag = (q * jnp.exp(cb - ref_d)).astype(BF16)
    k_diag = (k * jnp.exp(jnp.minimum(ref_d - cb, EXP_CLAMP))).astype(BF16)
    return jnp.concatenate(q_parts, axis=1), jnp.concatenate(k_parts, axis=1), q_diag, k_diag


def _gla_diag_mask(c, rev):
    ri = lax.broadcasted_iota(jnp.int32, (c, c), 0)
    ci = lax.broadcasted_iota(jnp.int32, (c, c), 1)
    same_block = (ri // SUB) == (ci // SUB)
    return jnp.logical_and(same_block, (ri <= ci) if rev else (ri >= ci))


def _gla_scan_kernel(qf_ref, kf_ref, vf_ref, cf_ref, qb_ref, kb_ref, vb_ref, cb_ref, s0_ref,
                     of_ref, ob_ref, ns_ref, s_ref, *, tok, q_silu, scale, hb, dk, dv):
    n = pl.program_id(1)
    tt = tok.seq
    nch = tt // CHUNK
    is_ctx, dec_start = _seq_flags(n, tok)
    streams = [(d, h) for d in range(2) for h in range(hb)]

    @pl.when(is_ctx)
    def _():
        s_ref[...] = jnp.zeros_like(s_ref)

    @pl.when(dec_start)
    def _():
        for d, h in streams:
            s_ref[d, h] = s0_ref[0, 0, d, h].T

    dirs = ((qf_ref, kf_ref, vf_ref, cf_ref, of_ref), (qb_ref, kb_ref, vb_ref, cb_ref, ob_ref))
    ops = {}
    for d, (q_ref, k_ref, v_ref, c_ref, _) in enumerate(dirs):
        rev = d == 1
        for h in range(hb):
            q = q_ref[:, h * dk:(h + 1) * dk].astype(F32)
            if q_silu:
                q = q * _sigmoid(q)
            q = q * scale
            k = k_ref[:, h * dk:(h + 1) * dk].astype(F32)
            v = v_ref[:, h * dv:(h + 1) * dv].astype(BF16)
            cb = c_ref[:, h * dk:(h + 1) * dk]
            for c in range(nch):
                rows = slice(c * CHUNK, (c + 1) * CHUNK)
                qc, kc, cbc = q[rows], k[rows], cb[rows]
                end = cbc[0:1, :] if rev else cbc[CHUNK - 1:CHUNK, :]
                ops[d, h, c] = dict(
                    intra=_gla_operands(qc, kc, cbc, rev), v=v[rows],
                    q_state=(qc * jnp.exp(cbc)).astype(BF16),
                    k_state=(kc * jnp.exp(end - cbc)).astype(BF16),
                    s_decay=jnp.exp(end))
    a_off = {key: _dot_nt(op["intra"][0], op["intra"][1]) for key, op in ops.items()}
    a_diag = {key: _dot_nt(op["intra"][2], op["intra"][3]) for key, op in ops.items()}
    masks = [_gla_diag_mask(CHUNK, rev) for rev in (False, True)]
    attn = {key: (a_off[key] + jnp.where(masks[key[0]], a_diag[key], 0.0)).astype(BF16) for key in ops}
    o_intra = {key: _dot(attn[key], op["v"]) for key, op in ops.items()}
    s_inc = {key: _dot_tn(op["v"], op["k_state"]) for key, op in ops.items()}

    state = {key: s_ref[key] for key in streams}
    for step in range(nch):
        for (d, h), s in state.items():
            c = step if d == 0 else nch - 1 - step
            dirs[d][4][c * CHUNK:(c + 1) * CHUNK, h * dv:(h + 1) * dv] = (
                o_intra[d, h, c] + _dot_nt(ops[d, h, c]["q_state"], s.astype(BF16))).astype(BF16)
            state[d, h] = ops[d, h, c]["s_decay"] * s + s_inc[d, h, c]
    for key, s in state.items():
        s_ref[key] = s

    @pl.when(is_ctx)
    def _():
        for d, h in streams:
            ns_ref[0, d, h] = s_ref[d, h].T


def _gla_scan(q_src, q_blk, kf_src, kf_blk, kb_src, kb_blk, v_src, v_blk, cb_f, cb_b, s0, j, tok,
              *, heads, dk, dv, q_silu, hb):
    m = q_src.shape[0]
    tt = tok.seq
    assert heads % hb == 0 and q_blk % hb == 0 and kf_blk % hb == 0 and kb_blk % hb == 0 and v_blk % hb == 0
    bwd = lambda n: _bwd_block(n, tok)
    fwd = lambda n: n

    def stream(blk, k_blk):
        return [
            pl.BlockSpec((tt, hb * dk), lambda h, n: (blk(n), q_blk // hb + h)),
            pl.BlockSpec((tt, hb * dk), lambda h, n: (blk(n), k_blk // hb + h)),
            pl.BlockSpec((tt, hb * dv), lambda h, n: (blk(n), v_blk // hb + h)),
            pl.BlockSpec((tt, hb * dk), lambda h, n: (blk(n), h)),
        ]

    n_batch = tok.n_ctx_blocks
    return pl.pallas_call(
        functools.partial(_gla_scan_kernel, tok=tok, q_silu=q_silu, scale=dk ** -0.5, hb=hb, dk=dk, dv=dv),
        grid=(heads // hb, tok.n_blocks),
        in_specs=stream(fwd, kf_blk) + stream(bwd, kb_blk) + [
            pl.BlockSpec((1, 1, 2, hb, dk, dv), lambda h, n: (_dec_batch(n, tok), j, 0, h, 0, 0)),
        ],
        out_specs=[
            pl.BlockSpec((tt, hb * dv), lambda h, n: (n, h)),
            pl.BlockSpec((tt, hb * dv), lambda h, n: (bwd(n), h)),
            pl.BlockSpec((1, 2, hb, dk, dv), lambda h, n: (_state_block(n, tok), 0, h, 0, 0)),
        ],
        out_shape=[
            jax.ShapeDtypeStruct((m, heads * dv), BF16),
            jax.ShapeDtypeStruct((m, heads * dv), BF16),
            jax.ShapeDtypeStruct((n_batch, 2, heads, dk, dv), F32),
        ],
        scratch_shapes=[pltpu.VMEM((2, hb, dv, dk), F32)],
        compiler_params=_params("parallel", "arbitrary"),
        name="gla_scan",
    )(q_src, kf_src, v_src, cb_f, q_src, kb_src, v_src, cb_b, s0)


def kernel(x_prompt, x_sample, c, state_deltanet, state_gla, state_hgrn, c_ctx, ada_w, ada_b, norm1, norm2, final_norm, w_up, w_down, dn_w_in, dn_conv, dn_A_log, dn_dt_bias, dn_norm, dn_w_out, gla_w_in, gla_w_a2, gla_b_a, gla_norm, gla_w_out, hgrn_w_in, hgrn_lb_logits, hgrn_norm, hgrn_w_out):
    batch, seq, d = x_prompt.shape
    dec_batch, dec_seq, _ = x_sample.shape
    depth = ada_w.shape[0]
    tok = Tokens(m_ctx=batch * seq, seq=seq, dec_batch=dec_batch, dec_seq=dec_seq)
    assert seq % CHUNK == 0 and dec_seq % seq == 0 and seq % GRID_W == 0

    dn_hv, dn_dk, dn_dv = state_deltanet.shape[3:]
    dn_vd = dn_hv * dn_dv
    n_conv = dn_conv.shape[2]
    dn_kd = (n_conv - dn_vd) // 2
    dn_hk = dn_kd // dn_dk
    gla_h, gla_dk, gla_dv = state_gla.shape[3:]
    gla_kd, gla_vd = gla_h * gla_dk, gla_h * gla_dv
    hg_h, hg_dk, hg_dv = state_hgrn.shape[3:]

    x = jnp.concatenate([x_prompt.reshape(tok.m_ctx, d), x_sample.reshape(dec_batch * dec_seq, d)], axis=0)
    n_rows = 1 + dec_batch
    pad = (-n_rows) % 8
    cond = jnp.concatenate([c_ctx[None, :], c, jnp.zeros((pad, d), F32)], axis=0)
    mod_all = _mod_table(cond, ada_w, ada_b).reshape(depth, n_rows + pad, 6, d)

    new_states = ([], [], [])
    dn_states = None
    for i in range(depth):
        kind, j = i % N_MIXERS, i // N_MIXERS
        mod = mod_all[i]
        if kind == 0:
            w_in = dn_w_in[j]
            proj = _proj(x, norm1[i], mod, w_in[:, :n_conv + dn_vd].astype(BF16), tok)
            raw = _proj(x, norm1[i], mod, w_in[:, n_conv + dn_vd:], tok, precise=True)
            qkv = _dn_prep(proj, dn_conv[j], n_conv, dn_kd, dn_dk, tok)
            gates, gates_t = _dn_gates(raw, dn_A_log[j], dn_dt_bias[j], tok)
            o_f, o_b, dn_states = _dn_scan(qkv, gates, gates_t, state_deltanet, j, dn_states, tok,
                                           hk_n=dn_hk, hv=dn_hv, dk=dn_dk, dv=dn_dv)
            x = _out_proj(o_f, o_b, proj, n_conv // dn_vd, dn_norm[j], x, mod, dn_w_out[j].astype(BF16), tok)
        elif kind == 1:
            w_in = gla_w_in[j]
            n_main = 2 * gla_kd + 2 * gla_vd
            proj = _proj(x, norm1[i], mod, w_in[:, :n_main].astype(BF16), tok)
            a1 = _proj(x, norm1[i], mod, w_in[:, n_main:], tok, precise=True)
            cb_f, cb_b = _gla_prep(a1, gla_w_a2[j], gla_b_a[j], tok)
            o_f, o_b, ns = _gla_scan(proj, 0, proj, gla_h, proj, gla_h, proj, (2 * gla_kd) // gla_dv,
                                     cb_f, cb_b, state_gla, j, tok,
                                     heads=gla_h, dk=gla_dk, dv=gla_dv, q_silu=False, hb=min(gla_h, 2))
            x = _out_proj(o_f, o_b, proj, (2 * gla_kd + gla_vd) // gla_vd, gla_norm[j], x, mod,
                          gla_w_out[j].astype(BF16), tok)
        else:
            proj = _proj(x, norm1[i], mod, hgrn_w_in[j].astype(BF16), tok)
            k_f, k_b, cb_f, cb_b = _hgrn_prep(proj, hgrn_lb_logits, i, d, tok)
            o_f, o_b, ns = _gla_scan(proj, 0, k_f, 0, k_b, 0, proj, d // hg_dv, cb_f, cb_b, state_hgrn, j, tok,
                                     heads=hg_h, dk=hg_dk, dv=hg_dv, q_silu=True, hb=min(hg_h, 8))
            x = _out_proj(o_f, o_b, proj, 2, hgrn_norm[j], x, mod, hgrn_w_out[j].astype(BF16), tok)
        if kind != 0:
            new_states[kind].append(ns)
        x = _mlp(x, norm2[i], mod, w_up[i].astype(BF16), w_down[i].astype(BF16), final_norm, tok,
                 final=(i == depth - 1))

    y_ctx, y_dec = x
    y_prompt = y_ctx.reshape(batch, seq, d)
    y_sample = y_dec.reshape(dec_batch, dec_seq, d)
    stacked = [s[0][:, None] if len(s) == 1 else jnp.stack(s, axis=1) for s in new_states[1:]]
    return (y_prompt, y_sample, dn_states) + tuple(stacked)
```

```python
import functools
from typing import NamedTuple

import jax
import jax.numpy as jnp
from jax import lax
from jax.experimental import pallas as pl
from jax.experimental.pallas import tpu as pltpu

F32 = jnp.float32
BF16 = jnp.bfloat16

EPS = 1e-6
GRID_W = 64
CONV_K = 5
GLA_RANK = 16
GLA_TAU = 16.0
N_MIXERS = 3
CHUNK = 64
SUB = 16
EXP_CLAMP = 80.0
LANES = 128
V7X_MXU_COLS = 256
V7X_VMEM_LIMIT_BYTES = 56 * 1024 * 1024
HIGHEST = lax.Precision.HIGHEST


class Tokens(NamedTuple):
    m_ctx: int
    seq: int
    dec_batch: int
    dec_seq: int

    @property
    def m(self):
        return self.m_ctx + self.dec_batch * self.dec_seq

    @property
    def n_ctx_blocks(self):
        return self.m_ctx // self.seq

    @property
    def blocks_per_dec(self):
        return self.dec_seq // self.seq

    @property
    def n_blocks(self):
        return self.m // self.seq


def _tile(n, pref):
    t = min(n, pref)
    while n % t:
        t //= 2
    return t


def _token_tile(tok, pref):
    t = pref
    while tok.m_ctx % t or tok.dec_seq % t:
        t //= 2
    return t


def _params(*sem):
    return pltpu.CompilerParams(dimension_semantics=sem, vmem_limit_bytes=V7X_VMEM_LIMIT_BYTES)


def _mod_row(i, tm, tok):
    start = i * tm
    return jnp.where(start < tok.m_ctx, 0, 1 + (start - tok.m_ctx) // tok.dec_seq)


def _sigmoid(x):
    return 0.5 + 0.5 * jnp.tanh(0.5 * x)


def _silu(x):
    half = 0.5 * x
    return half + half * jnp.tanh(half)


def _softplus(x):
    return jnp.maximum(x, 0.0) + jnp.log(1.0 + jnp.exp(-jnp.abs(x)))


def _norm_mod(x, nw, shift, scale):
    ms = jnp.mean(x * x, axis=-1, keepdims=True)
    return x * lax.rsqrt(ms + EPS) * nw * (1.0 + scale) + shift


def _dot(a, b):
    return jnp.dot(a, b, preferred_element_type=F32)


def _dot_nt(a, b):
    return lax.dot_general(a, b, (((1,), (1,)), ((), ())), preferred_element_type=F32)


def _dot_tn(a, b):
    return lax.dot_general(a, b, (((0,), (0,)), ((), ())), preferred_element_type=F32)


def _split_dot(tri, x):
    hi = x.astype(BF16)
    lo = (x - hi.astype(F32)).astype(BF16)
    return _dot(tri, hi) + _dot(tri, lo)


def _split_matmul(a, b):
    ah = a.astype(BF16)
    al = (a - ah.astype(F32)).astype(BF16)
    bh = b.astype(BF16)
    bl = (b - bh.astype(F32)).astype(BF16)
    return _dot(ah, bh) + _dot(al, bh) + _dot(ah, bl)


def _tri(n, upper):
    r = lax.broadcasted_iota(jnp.int32, (n, n), 0)
    c = lax.broadcasted_iota(jnp.int32, (n, n), 1)
    return jnp.where((r <= c) if upper else (r >= c), 1.0, 0.0).astype(BF16)


def _mod_kernel(c_ref, w_ref, b_ref, o_ref):
    c = c_ref[...]
    s = _silu(c)
    o_ref[0] = jnp.dot(s, w_ref[0], precision=HIGHEST, preferred_element_type=F32) + b_ref[0]


def _mod_table(cond, ada_w, ada_b):
    depth, d, n = ada_w.shape
    rows = cond.shape[0]
    tn = _tile(n, 1024)
    return pl.pallas_call(
        _mod_kernel,
        grid=(depth, n // tn),
        in_specs=[
            pl.BlockSpec((rows, d), lambda l, j: (0, 0)),
            pl.BlockSpec((1, d, tn), lambda l, j: (l, 0, j)),
            pl.BlockSpec((1, 1, tn), lambda l, j: (l, 0, j)),
        ],
        out_specs=pl.BlockSpec((1, rows, tn), lambda l, j: (l, 0, j)),
        out_shape=jax.ShapeDtypeStruct((depth, rows, n), F32),
        compiler_params=_params("parallel", "parallel"),
        name="mod_table",
    )(cond, ada_w, ada_b.reshape(depth, 1, n))


def _proj_kernel(x_ref, nw_ref, mod_ref, w_ref, o_ref, h_ref, *, precise):
    @pl.when(pl.program_id(1) == 0)
    def _():
        h = _norm_mod(x_ref[...], nw_ref[...], mod_ref[0, 0:1, :], mod_ref[0, 1:2, :])
        h_ref[...] = h.astype(h_ref.dtype)

    if precise:
        o_ref[...] = _split_matmul(h_ref[...], w_ref[...])
    else:
        o_ref[...] = _dot(h_ref[...], w_ref[...]).astype(o_ref.dtype)


def _proj(x, nw, mod, w, tok, *, precise=False, tm=1024, tn=1024):
    m, d = x.shape
    n = w.shape[1]
    tm = _token_tile(tok, tm)
    tn = _tile(n, tn)
    return pl.pallas_call(
        functools.partial(_proj_kernel, precise=precise),
        grid=(m // tm, n // tn),
        in_specs=[
            pl.BlockSpec((tm, d), lambda i, j: (i, 0)),
            pl.BlockSpec((1, d), lambda i, j: (0, 0)),
            pl.BlockSpec((1, 6, d), lambda i, j: (_mod_row(i, tm, tok), 0, 0)),
            pl.BlockSpec((d, tn), lambda i, j: (0, j)),
        ],
        out_specs=pl.BlockSpec((tm, tn), lambda i, j: (i, j)),
        out_shape=jax.ShapeDtypeStruct((m, n), F32 if precise else BF16),
        scratch_shapes=[pltpu.VMEM((tm, d), F32 if precise else BF16)],
        compiler_params=_params("parallel", "arbitrary"),
        name="proj_precise" if precise else "proj",
    )(x, nw.reshape(1, d), mod, w)


def _mlp_kernel(x_ref, nw_ref, mod_ref, wu_ref, wd_ref, fw_ref, *rest, n_ctx_tiles):
    out_refs, (h_ref, acc_ref) = rest[:-2], rest[-2:]
    final = n_ctx_tiles is not None
    i = pl.program_id(0)
    f = pl.program_id(1)

    @pl.when(f == 0)
    def _():
        h = _norm_mod(x_ref[...], nw_ref[...], mod_ref[0, 3:4, :], mod_ref[0, 4:5, :])
        h_ref[...] = h.astype(BF16)
        acc_ref[...] = jnp.zeros_like(acc_ref)

    a = jnp.maximum(_dot(h_ref[...], wu_ref[...]), 0.0)
    acc_ref[...] += _dot((a * a).astype(BF16), wd_ref[...])

    last = f == pl.num_programs(1) - 1

    def result():
        y = x_ref[...] + mod_ref[0, 5:6, :] * acc_ref[...]
        if final:
            ms = jnp.mean(y * y, axis=-1, keepdims=True)
            y = y * lax.rsqrt(ms + EPS) * fw_ref[...]
        return y

    if not final:
        @pl.when(last)
        def _():
            out_refs[0][...] = result()
    else:
        @pl.when(jnp.logical_and(last, i < n_ctx_tiles))
        def _():
            out_refs[0][...] = result()

        @pl.when(jnp.logical_and(last, i >= n_ctx_tiles))
        def _():
            out_refs[1][...] = result()


def _mlp(x, nw, mod, w_up, w_down, final_w, tok, *, final, tm=512, tf=1024):
    m, d = x.shape
    ff = w_up.shape[1]
    tm = _token_tile(tok, tm)
    tf = _tile(ff, tf)
    n_ctx_tiles = tok.m_ctx // tm
    if final:
        out_specs = [pl.BlockSpec((tm, d), lambda i, f: (jnp.minimum(i, n_ctx_tiles - 1), 0)),
                     pl.BlockSpec((tm, d), lambda i, f: (jnp.maximum(i - n_ctx_tiles, 0), 0))]
        out_shape = [jax.ShapeDtypeStruct((tok.m_ctx, d), F32), jax.ShapeDtypeStruct((m - tok.m_ctx, d), F32)]
    else:
        out_specs = pl.BlockSpec((tm, d), lambda i, f: (i, 0))
        out_shape = jax.ShapeDtypeStruct((m, d), F32)
    return pl.pallas_call(
        functools.partial(_mlp_kernel, n_ctx_tiles=n_ctx_tiles if final else None),
        grid=(m // tm, ff // tf),
        in_specs=[
            pl.BlockSpec((tm, d), lambda i, f: (i, 0)),
            pl.BlockSpec((1, d), lambda i, f: (0, 0)),
            pl.BlockSpec((1, 6, d), lambda i, f: (_mod_row(i, tm, tok), 0, 0)),
            pl.BlockSpec((d, tf), lambda i, f: (0, f)),
            pl.BlockSpec((tf, d), lambda i, f: (f, 0)),
            pl.BlockSpec((1, d), lambda i, f: (0, 0)),
        ],
        out_specs=out_specs,
        out_shape=out_shape,
        scratch_shapes=[pltpu.VMEM((tm, d), BF16), pltpu.VMEM((tm, d), F32)],
        compiler_params=_params("arbitrary" if final else "parallel", "arbitrary"),
        name="mlp_final" if final else "mlp",
    )(x, nw.reshape(1, d), mod, w_up, w_down, final_w.reshape(1, d))


def _out_kernel(of_ref, ob_ref, g_ref, nw_ref, x_ref, mod_ref, w_ref, o_ref, y0_ref, y1_ref, *, n_heads, dh):
    i = pl.program_id(0)
    d = w_ref.shape[1]
    n_chunks = d // V7X_MXU_COLS
    heads_per_chunk = -(-n_heads // n_chunks)

    @pl.when(i == 0)
    def _():
        y1_ref[...] = jnp.zeros_like(y1_ref)

    def body(y_write, y_read):
        ones = jnp.full((dh, LANES), 1.0 / dh, BF16) if dh & (dh - 1) == 0 else None
        for c in range(n_chunks):
            cols = slice(c * V7X_MXU_COLS, (c + 1) * V7X_MXU_COLS)
            o_ref[:, cols] = x_ref[:, cols] + mod_ref[0, 2:3, cols] * _dot(y_read[...], w_ref[:, cols])
            for h in range(c * heads_per_chunk, min((c + 1) * heads_per_chunk, n_heads)):
                sl = slice(h * dh, (h + 1) * dh)
                o = of_ref[:, sl].astype(F32) + ob_ref[:, sl].astype(F32)
                if ones is None:
                    ms = _dot((o * o).astype(BF16), jnp.ones((dh, LANES), BF16)) * (1.0 / dh)
                else:
                    ms = _dot((o * o).astype(BF16), ones)
                inv = lax.rsqrt(ms + EPS)
                y = o * nw_ref[...] * _silu(g_ref[:, sl].astype(F32))
                for k in range(dh // LANES):
                    lo = h * dh + k * LANES
                    y_write[:, lo:lo + LANES] = (y[:, k * LANES:(k + 1) * LANES] * inv).astype(BF16)

    @pl.when(i % 2 == 0)
    def _():
        body(y0_ref, y1_ref)

    @pl.when(i % 2 == 1)
    def _():
        body(y1_ref, y0_ref)


def _out_proj(o_f, o_b, gate_src, gate_blk, norm_w, x, mod, w_out, tok, *, tm=256):
    m, vd = o_f.shape
    d = x.shape[1]
    dh = norm_w.shape[0]
    tm = _token_tile(tok, tm)
    n_tiles = m // tm
    assert d % V7X_MXU_COLS == 0 and dh % LANES == 0
    cur = lambda i: jnp.minimum(i, n_tiles - 1)
    prev = lambda i: jnp.maximum(i - 1, 0)
    return pl.pallas_call(
        functools.partial(_out_kernel, n_heads=vd // dh, dh=dh),
        grid=(n_tiles + 1,),
        in_specs=[
            pl.BlockSpec((tm, vd), lambda i: (cur(i), 0)),
            pl.BlockSpec((tm, vd), lambda i: (cur(i), 0)),
            pl.BlockSpec((tm, vd), lambda i: (cur(i), gate_blk)),
            pl.BlockSpec((1, dh), lambda i: (0, 0)),
            pl.BlockSpec((tm, d), lambda i: (prev(i), 0)),
            pl.BlockSpec((1, 6, d), lambda i: (_mod_row(prev(i), tm, tok), 0, 0)),
            pl.BlockSpec((vd, d), lambda i: (0, 0), pipeline_mode=pl.Buffered(1)),
        ],
        out_specs=pl.BlockSpec((tm, d), lambda i: (prev(i), 0)),
        out_shape=jax.ShapeDtypeStruct((m, d), F32),
        scratch_shapes=[pltpu.VMEM((tm, vd), BF16), pltpu.VMEM((tm, vd), BF16)],
        compiler_params=_params("arbitrary"),
        name="out_proj",
    )(o_f, o_b, gate_src, norm_w.reshape(1, dh), x, mod, w_out)


def _bwd_block(n, tok):
    nc, per = tok.n_ctx_blocks, tok.blocks_per_dec
    r = jnp.maximum(n - nc, 0)
    return jnp.where(n < nc, n, nc + (r // per) * per + (per - 1 - r % per))


def _dec_batch(n, tok):
    return jnp.maximum(n - tok.n_ctx_blocks, 0) // tok.blocks_per_dec


def _state_block(n, tok):
    return jnp.minimum(n, tok.n_ctx_blocks - 1)


def _seq_flags(n, tok):
    nc, per = tok.n_ctx_blocks, tok.blocks_per_dec
    is_ctx = n < nc
    r = jnp.maximum(n - nc, 0) % per
    return is_ctx, jnp.logical_and(jnp.logical_not(is_ctx), r == 0)


def _dn_prep_kernel(x_ref, shift_ref, w_ref, o_ref, *, n_q_blocks, dk):
    j = pl.program_id(1)
    x = x_ref[...]
    tc = x.shape[1]
    acc = None
    for s in range(CONV_K):
        xs = x.astype(F32) if s == CONV_K // 2 else _dot(shift_ref[0, s], x)
        term = xs * w_ref[s:s + 1, :]
        acc = term if acc is None else acc + term
    y = _silu(acc)

    def normed(scale):
        ones = jnp.ones((dk, dk), BF16)
        for g in range(tc // dk):
            blk = y[:, g * dk:(g + 1) * dk]
            ss = _dot((blk * blk).astype(BF16), ones)
            o_ref[:, g * dk:(g + 1) * dk] = (blk * (lax.rsqrt(ss + EPS) * scale)).astype(o_ref.dtype)

    @pl.when(j < n_q_blocks)
    def _():
        normed(dk ** -0.5)

    @pl.when(jnp.logical_and(j >= n_q_blocks, j < 2 * n_q_blocks))
    def _():
        normed(1.0)

    @pl.when(j >= 2 * n_q_blocks)
    def _():
        o_ref[...] = y.astype(o_ref.dtype)


def _conv_shift_matrices(tt):
    t = jnp.arange(tt)[:, None]
    u = jnp.arange(tt)[None, :]
    mats = []
    for seg in (tt, GRID_W):
        same = (t // seg) == (u // seg)
        mats.append(jnp.stack([jnp.logical_and(u == t + s - CONV_K // 2, same) for s in range(CONV_K)]))
    return jnp.stack(mats).astype(BF16)


def _dn_prep(proj, conv_w, n_conv, kd, dk, tok):
    m = proj.shape[0]
    tt = tok.seq
    tc = _tile(kd, 2048)
    n_ctx = tok.n_ctx_blocks
    return pl.pallas_call(
        functools.partial(_dn_prep_kernel, n_q_blocks=kd // tc, dk=dk),
        grid=(m // tt, n_conv // tc),
        in_specs=[
            pl.BlockSpec((tt, tc), lambda i, j: (i, j)),
            pl.BlockSpec((1, CONV_K, tt, tt), lambda i, j: (jnp.where(i < n_ctx, 0, 1), 0, 0, 0)),
            pl.BlockSpec((CONV_K, tc), lambda i, j: (0, j)),
        ],
        out_specs=pl.BlockSpec((tt, tc), lambda i, j: (i, j)),
        out_shape=jax.ShapeDtypeStruct((m, n_conv), BF16),
        compiler_params=_params("parallel", "parallel"),
        name="dn_prep",
    )(proj, _conv_shift_matrices(tt), conv_w)


def _dn_gate_kernel(x_ref, a_ref, bias_ref, o_ref, gt_ref, *, hv, tt):
    x = x_ref[...]
    lanes = x.shape[1]
    lane = lax.broadcasted_iota(jnp.int32, (1, lanes), 1)
    is_fwd = lane < 3 * hv
    g = -a_ref[...] * _softplus(x + bias_ref[...])
    beta = pltpu.roll(_sigmoid(x), 2 * hv, 1)
    lower = _tri(CHUNK, upper=False)
    upper = _tri(CHUNK, upper=True)
    for c in range(tt // CHUNK):
        rows = slice(c * CHUNK, (c + 1) * CHUNK)
        gch = g[rows]
        gc = jnp.where(is_fwd, _split_dot(lower, gch), _split_dot(upper, gch))
        g_end = jnp.where(is_fwd, gc[CHUNK - 1:CHUNK, :], gc[0:1, :])
        eg = jnp.exp(gc)
        o_ref[0, rows, :] = beta[rows]
        o_ref[1, rows, :] = gc
        o_ref[2, rows, :] = eg
        o_ref[3, rows, :] = jnp.exp(g_end - gc)
        o_ref[4, rows, :] = beta[rows] * eg
    gt_ref[...] = o_ref[1].T


def _dn_gates(raw, a_log, dt_bias, tok):
    m, lanes = raw.shape
    hv = lanes // 4
    tt = tok.seq
    zeros = jnp.zeros((2 * hv,), F32)
    a_full = jnp.concatenate([zeros, jnp.exp(a_log).reshape(-1)]).reshape(1, lanes)
    bias_full = jnp.concatenate([zeros, dt_bias.reshape(-1)]).reshape(1, lanes)
    return pl.pallas_call(
        functools.partial(_dn_gate_kernel, hv=hv, tt=tt),
        grid=(m // tt,),
        in_specs=[
            pl.BlockSpec((tt, lanes), lambda i: (i, 0)),
            pl.BlockSpec((1, lanes), lambda i: (0, 0)),
            pl.BlockSpec((1, lanes), lambda i: (0, 0)),
        ],
        out_specs=[
            pl.BlockSpec((5, tt, lanes), lambda i: (0, i, 0)),
            pl.BlockSpec((lanes, tt), lambda i: (0, i)),
        ],
        out_shape=[jax.ShapeDtypeStruct((5, m, lanes), F32), jax.ShapeDtypeStruct((lanes, m), F32)],
        compiler_params=_params("parallel"),
        name="dn_gates",
    )(raw, a_full, bias_full)


def _dn_scan_kernel(qf_ref, kf_ref, vf_ref, gf_ref, gtf_ref, qb_ref, kb_ref, vb_ref, gb_ref, gtb_ref, s0_ref,
                    of_ref, ob_ref, ns_ref, s_ref, u0_ref, wq_ref, attn_ref, kd_ref, ge_ref, *, tok, hv, dv, hkb):
    hk0 = pl.program_id(0) * hkb
    n = pl.program_id(1)
    tt = tok.seq
    nch = tt // CHUNK
    dk = qf_ref.shape[1] // hkb
    assert dv == 2 * CHUNK and CHUNK == 4 * SUB
    has_prev = n > 0
    is_ctx, dec_start = _seq_flags(jnp.maximum(n - 1, 0), tok)
    wslot = n % 2
    rslot = 1 - wslot
    scratch = (u0_ref, wq_ref, attn_ref, kd_ref, ge_ref)

    @pl.when(n == 0)
    def _():
        s_ref[...] = jnp.zeros_like(s_ref)
        for ref in scratch:
            ref[1] = jnp.zeros(ref.shape[1:], ref.dtype)

    @pl.when(jnp.logical_and(has_prev, is_ctx))
    def _():
        s_ref[...] = jnp.zeros_like(s_ref)

    @pl.when(jnp.logical_and(has_prev, dec_start))
    def _():
        s_ref[...] = s0_ref[0, 0]

    bf = lambda x: x.astype(BF16)
    lanes = gf_ref.shape[2]
    lane_iota = lax.broadcasted_iota(jnp.int32, (1, lanes), 1)
    ri = lax.broadcasted_iota(jnp.int32, (CHUNK, 2 * CHUNK), 0)
    li = lax.broadcasted_iota(jnp.int32, (CHUNK, 2 * CHUNK), 1)
    left = li < CHUNK
    ci = li % CHUNK
    same = [(ri // b) == (ci // b) for b in (SUB, 2 * SUB)]
    zero_b = jnp.zeros((CHUNK, 2 * CHUNK), BF16)

    def block_diag(xb):
        return jnp.concatenate([jnp.where(left, xb, zero_b), jnp.where(left, zero_b, xb)], axis=0)

    def pair_dot(a, b):
        return _dot(bf(a), block_diag(bf(b)))

    dirs = ((qf_ref, kf_ref, vf_ref, gf_ref, gtf_ref, of_ref), (qb_ref, kb_ref, vb_ref, gb_ref, gtb_ref, ob_ref))

    def prepare():
        ch = {}
        for d, kh in [(d, kh) for d in range(2) for kh in range(hkb)]:
            q_ref, k_ref, v_ref, g_ref, gt_ref, _ = dirs[d]
            incl = (ri >= ci) if d == 0 else (ri <= ci)
            strict = (ri > ci) if d == 0 else (ri < ci)
            end_row = CHUNK - 1 if d == 0 else 0
            q = q_ref[:, kh * dk:(kh + 1) * dk]
            k = k_ref[:, kh * dk:(kh + 1) * dk]
            kf32 = k.astype(F32)
            qf32 = q.astype(F32)
            cols, g_rows, vs = [], [], []
            for e in range(2):
                a_lane = 2 * hv + d * hv + 2 * (hk0 + kh) + e
                sel = lane_iota == a_lane
                cols.append([jnp.sum(jnp.where(sel, g_ref[i], 0.0), axis=1, keepdims=True) for i in range(5)])
                g_rows.append(gt_ref[pl.ds(a_lane, 1), :])
                vs.append(v_ref[:, (2 * kh + e) * dv:(2 * kh + e + 1) * dv].astype(F32))
            for c in range(nch):
                rows = slice(c * CHUNK, (c + 1) * CHUNK)
                pair = lambda i: jnp.where(left, cols[0][i][rows], cols[1][i][rows])
                k2 = jnp.concatenate([k[rows], k[rows]], axis=0)
                kk = _dot_nt(k[rows], k2)
                qk = _dot_nt(q[rows], k2)
                g_row = jnp.concatenate([g_rows[0][:, rows], g_rows[1][:, rows]], axis=1)
                diff = pair(1) - g_row
                decay = jnp.where(incl, jnp.exp(jnp.where(incl, diff, 0.0)), 0.0)
                p = jnp.where(strict, -(pair(0) * kk * decay), 0.0)
                attn_ref[wslot, d, kh, c] = bf(qk * decay)
                for e in range(2):
                    beta_c, _, eg_c, ekd_c, beg_c = cols[e]
                    qg = qf32[rows] * eg_c[rows]
                    kd_ref[wslot, d, 2 * kh + e, c] = bf(kf32[rows] * ekd_c[rows])
                    wq_ref[wslot, d, 2 * kh + e, c, CHUNK:, :] = bf(qg)
                    g_end = eg_c[c * CHUNK + end_row:c * CHUNK + end_row + 1, :]
                    ge_ref[wslot, d, 2 * kh + e, c] = jnp.broadcast_to(g_end, ge_ref.shape[4:])
                ch[kh, d, c] = dict(
                    pd=jnp.where(same[0], p, 0.0),
                    p32=jnp.where(jnp.logical_and(same[1], jnp.logical_not(same[0])), p, 0.0),
                    p64=jnp.where(same[1], 0.0, p),
                    r=[jnp.concatenate([cols[e][0][rows] * vs[e][rows], cols[e][4][rows] * kf32[rows]], axis=1)
                       for e in range(2)])
            yield
        for c in ch.values():
            c["q"] = pair_dot(c["pd"], c["pd"])
            c["xm"] = c["pd"]
        yield
        for it in range(2):
            for c in ch.values():
                q, xm = c["q"], c["xm"]
                both = _dot(bf(q), jnp.concatenate([block_diag(bf(xm)), block_diag(bf(q))], axis=1))
                c["xm"] = xm + q + both[:, :2 * CHUNK]
                c["q"] = both[:, 2 * CHUNK:]
            yield
        for c in ch.values():
            q, xm = c.pop("q"), c["xm"]
            c["xm"] = xm + q + pair_dot(q, xm)
        yield
        for level in ("p32", "p64"):
            for c in ch.values():
                c["y"] = c[level] + pair_dot(c[level], c["xm"])
            yield
            for c in ch.values():
                xm, y = c["xm"], c.pop("y")
                c["xm"] = xm + y + pair_dot(xm, y)
            yield
        zero_r = jnp.zeros((CHUNK, 2 * dv), BF16)
        for (kh, d, cidx), c in ch.items():
            r0, r1 = c["r"]
            rhs = jnp.concatenate([jnp.concatenate([bf(r0), zero_r], axis=1),
                                   jnp.concatenate([zero_r, bf(r1)], axis=1)], axis=0)
            big = _dot(bf(c["xm"]), rhs)
            sol = [r0 + big[:, :2 * dv], r1 + big[:, 2 * dv:]]
            u0_ref[wslot, d, kh, cidx] = jnp.concatenate([sol[0][:, :dv], sol[1][:, :dv]], axis=1)
            for e in range(2):
                wq_ref[wslot, d, 2 * kh + e, cidx, :CHUNK, :] = bf(sol[e][:, dv:])
        yield

    def recur():
        state = {(d, e): s_ref[d, e] for d in range(2) for e in range(2 * hkb)}
        zero_u = jnp.zeros((CHUNK, dv), BF16)
        for step in range(nch):
            chunk = lambda d: step if d == 0 else nch - 1 - step
            ws = {(d, e): _dot(wq_ref[rslot, d, e, chunk(d)], bf(s)) for (d, e), s in state.items()}
            yield
            for d, kh in [(d, kh) for d in range(2) for kh in range(hkb)]:
                c = chunk(d)
                u0 = u0_ref[rslot, d, kh, c]
                ev = [2 * kh, 2 * kh + 1]
                ub = [bf(u0[:, e * dv:(e + 1) * dv] - ws[d, ev[e]][:CHUNK]) for e in range(2)]
                u2 = jnp.concatenate([jnp.concatenate([ub[0], zero_u], axis=1),
                                      jnp.concatenate([zero_u, ub[1]], axis=1)], axis=0)
                o = (_dot(attn_ref[rslot, d, kh, c], u2)
                     + jnp.concatenate([ws[d, ev[0]][CHUNK:], ws[d, ev[1]][CHUNK:]], axis=1))
                dirs[d][5][c * CHUNK:(c + 1) * CHUNK, 2 * kh * dv:2 * (kh + 1) * dv] = bf(o)
                for e in range(2):
                    state[d, ev[e]] = (ge_ref[rslot, d, ev[e], c][0:1, :] * state[d, ev[e]]
                                       + _dot_tn(kd_ref[rslot, d, ev[e], c], ub[e]))
            yield
        for (d, e), s in state.items():
            s_ref[d, e] = s
        yield

    halves = [recur(), prepare()]
    while halves:
        for g in list(halves):
            if next(g, "done") == "done":
                halves.remove(g)

    @pl.when(jnp.logical_and(has_prev, is_ctx))
    def _():
        ns_ref[0, 0] = s_ref[...]


def _dn_scan_kernel_aliased(*refs, **kw):
    n_in = 11
    return _dn_scan_kernel(*refs[:n_in], *refs[n_in + 1:], **kw)


def _dn_scan(qkv, gates, gates_t, s0, j, new_states, tok, *, hk_n, hv, dk, dv):
    m = qkv.shape[0]
    tt = tok.seq
    rep = hv // hk_n
    assert rep == 2
    lanes = gates.shape[2]
    kd = hk_n * dk
    nb = tok.n_blocks
    nch = tt // CHUNK
    fwd = lambda n: jnp.minimum(n, nb - 1)
    bwd = lambda n: _bwd_block(jnp.minimum(n, nb - 1), tok)
    prev = lambda n: jnp.maximum(n - 1, 0)

    hkb = max(b for b in (4, 2, 1) if hk_n % b == 0)
    hvb = hkb * rep

    def stream(blk):
        return [
            pl.BlockSpec((tt, hkb * dk), lambda h, n: (blk(n), h)),
            pl.BlockSpec((tt, hkb * dk), lambda h, n: (blk(n), hk_n // hkb + h)),
            pl.BlockSpec((tt, hvb * dv), lambda h, n: (blk(n), (2 * kd) // (hvb * dv) + h)),
            pl.BlockSpec((5, tt, lanes), lambda h, n: (0, blk(n), 0)),
            pl.BlockSpec((lanes, tt), lambda h, n: (0, blk(n))),
        ]

    n_batch = tok.n_ctx_blocks
    n_layers = s0.shape[1]
    aliased = new_states is not None
    extra_in = [pl.BlockSpec(memory_space=pl.ANY)] if aliased else []
    extra_args = (new_states,) if aliased else ()
    return pl.pallas_call(
        functools.partial(_dn_scan_kernel_aliased if aliased else _dn_scan_kernel, tok=tok, hv=hv, dv=dv, hkb=hkb),
        grid=(hk_n // hkb, nb + 1),
        in_specs=stream(fwd) + stream(bwd) + [
            pl.BlockSpec((1, 1, 2, hvb, dk, dv), lambda h, n: (_dec_batch(prev(n), tok), j, 0, h, 0, 0)),
        ] + extra_in,
        out_specs=[
            pl.BlockSpec((tt, hvb * dv), lambda h, n: (prev(n), h)),
            pl.BlockSpec((tt, hvb * dv), lambda h, n: (_bwd_block(prev(n), tok), h)),
            pl.BlockSpec((1, 1, 2, hvb, dk, dv), lambda h, n: (_state_block(prev(n), tok), j, 0, h, 0, 0)),
        ],
        out_shape=[
            jax.ShapeDtypeStruct((m, hv * dv), BF16),
            jax.ShapeDtypeStruct((m, hv * dv), BF16),
            jax.ShapeDtypeStruct((n_batch, n_layers, 2, hv, dk, dv), F32),
        ],
        input_output_aliases={11: 2} if aliased else {},
        scratch_shapes=[
            pltpu.VMEM((2, hvb, dk, dv), F32),
            pltpu.VMEM((2, 2, hkb, nch, CHUNK, rep * dv), F32),
            pltpu.VMEM((2, 2, hvb, nch, 2 * CHUNK, dk), BF16),
            pltpu.VMEM((2, 2, hkb, nch, CHUNK, rep * CHUNK), BF16),
            pltpu.VMEM((2, 2, hvb, nch, CHUNK, dk), BF16),
            pltpu.VMEM((2, 2, hvb, nch, 8, dv), F32),
        ],
        compiler_params=_params("parallel", "arbitrary"),
        name="dn_scan",
    )(qkv, qkv, qkv, gates, gates_t, qkv, qkv, qkv, gates, gates_t, s0, *extra_args)


def _chunk_cums(la_f, la_b, cbf_ref, cbb_ref, tt):
    lower = _tri(CHUNK, upper=False)
    upper = _tri(CHUNK, upper=True)
    for c in range(tt // CHUNK):
        rows = slice(c * CHUNK, (c + 1) * CHUNK)
        cbf_ref[rows, :] = _split_dot(lower, la_f[rows])
        cbb_ref[rows, :] = _split_dot(upper, la_b[rows])


def _log_sigmoid(x):
    return jnp.minimum(x, 0.0) - jnp.log(1.0 + jnp.exp(-jnp.abs(x)))


def _gla_prep_kernel(a1_ref, w2_ref, b_ref, cbf_ref, cbb_ref, *, tt):
    a1 = a1_ref[...]
    la = [_log_sigmoid(_dot(a1, w2_ref[d]) + b_ref[d]) * (1.0 / GLA_TAU) for d in range(2)]
    _chunk_cums(la[0], la[1], cbf_ref, cbb_ref, tt)


def _gla_prep(a1, w_a2, b_a, tok):
    m, r2 = a1.shape
    kd = w_a2.shape[2]
    tt = tok.seq
    tc = _tile(kd, 1024)
    zeros = jnp.zeros((GLA_RANK, kd), F32)
    w2 = jnp.stack([jnp.concatenate([w_a2[0], zeros]), jnp.concatenate([zeros, w_a2[1]])])
    out = jax.ShapeDtypeStruct((m, kd), F32)
    return pl.pallas_call(
        functools.partial(_gla_prep_kernel, tt=tt),
        grid=(m // tt, kd // tc),
        in_specs=[
            pl.BlockSpec((tt, r2), lambda i, j: (i, 0)),
            pl.BlockSpec((2, r2, tc), lambda i, j: (0, 0, j)),
            pl.BlockSpec((2, 1, tc), lambda i, j: (0, 0, j)),
        ],
        out_specs=[pl.BlockSpec((tt, tc), lambda i, j: (i, j))] * 2,
        out_shape=[out, out],
        compiler_params=_params("parallel", "parallel"),
        name="gla_prep",
    )(a1, w2, b_a.reshape(2, 1, kd))


def _hgrn_prep_kernel(ff_ref, fb_ref, lb_ref, kf_ref, kb_ref, cbf_ref, cbb_ref, *, layer, tt):
    logits = lb_ref[...]
    ex = jnp.exp(logits - jnp.max(logits, axis=0, keepdims=True))
    probs = ex / jnp.sum(ex, axis=0, keepdims=True)
    lb = jnp.sum(probs[1:layer + 1], axis=0, keepdims=True)
    la = []
    for f_ref, k_ref in ((ff_ref, kf_ref), (fb_ref, kb_ref)):
        fl = f_ref[...].astype(F32)
        sg = _sigmoid(fl)
        la.append(jnp.log(lb + (1.0 - lb) * sg))
        k_ref[...] = (1.0 - lb) * (1.0 - sg)
    _chunk_cums(la[0], la[1], cbf_ref, cbb_ref, tt)


def _hgrn_prep(proj, lb_logits, layer, d, tok):
    m = proj.shape[0]
    depth = lb_logits.shape[0]
    tt = tok.seq
    tc = _tile(d, 2048)
    nb = d // tc
    out = jax.ShapeDtypeStruct((m, d), F32)
    return pl.pallas_call(
        functools.partial(_hgrn_prep_kernel, layer=layer, tt=tt),
        grid=(m // tt, nb),
        in_specs=[
            pl.BlockSpec((tt, tc), lambda i, j: (i, 3 * nb + j)),
            pl.BlockSpec((tt, tc), lambda i, j: (i, 4 * nb + j)),
            pl.BlockSpec((depth, tc), lambda i, j: (0, j)),
        ],
        out_specs=[pl.BlockSpec((tt, tc), lambda i, j: (i, j))] * 4,
        out_shape=[out] * 4,
        compiler_params=_params("parallel", "parallel"),
        name="hgrn_prep",
    )(proj, proj, lb_logits)


def _gla_offdiag_pairs(c):
    pairs = []
    size = c
    while size > SUB:
        half = size // 2
        for start in range(0, c, size):
            pairs.append(((start + half, start + size), (start, start + half)))
        size = half
    return pairs


def _gla_operands(q, k, cb, rev):
    c, dk = q.shape
    zero_rows = lambda n: jnp.zeros((n, dk), BF16)

    def padded(x, rows):
        parts = ([zero_rows(rows[0])] if rows[0] else []) + [x] + ([zero_rows(c - rows[1])] if rows[1] < c else [])
        return jnp.concatenate(parts, axis=0) if len(parts) > 1 else x

    q_parts, k_parts = [], []
    for qr, kr in _gla_offdiag_pairs(c):
        if rev:
            qr, kr = kr, qr
        ref_row = qr[0] - 1 if not rev else qr[1]
        ref = cb[ref_row:ref_row + 1, :]
        qs, ks = slice(*qr), slice(*kr)
        q_parts.append(padded((q[qs] * jnp.exp(cb[qs] - ref)).astype(BF16), qr))
        k_parts.append(padded((k[ks] * jnp.exp(ref - cb[ks])).astype(BF16), kr))
    refs = []
    for blk in range(c // SUB):
        lo, hi = blk * SUB, (blk + 1) * SUB
        if not rev:
            ref = cb[lo - 1:lo, :] if blk > 0 else jnp.zeros_like(cb[0:1, :])
        else:
            ref = cb[hi:hi + 1, :] if hi < c else jnp.zeros_like(cb[0:1, :])
        refs.append(jnp.broadcast_to(ref, (SUB, dk)))
    ref_d = jnp.concatenate(refs, axis=0)
    q_diag = (q * jnp.exp(cb - ref_d)).astype(BF16)
    k_diag = (k * jnp.exp(jnp.minimum(ref_d - cb, EXP_CLAMP))).astype(BF16)
    return jnp.concatenate(q_parts, axis=1), jnp.concatenate(k_parts, axis=1), q_diag, k_diag


def _gla_diag_mask(c, rev):
    ri = lax.broadcasted_iota(jnp.int32, (c, c), 0)
    ci = lax.broadcasted_iota(jnp.int32, (c, c), 1)
    same_block = (ri // SUB) == (ci // SUB)
    return jnp.logical_and(same_block, (ri <= ci) if rev else (ri >= ci))


def _gla_scan_kernel(qf_ref, kf_ref, vf_ref, cf_ref, qb_ref, kb_ref, vb_ref, cb_ref, s0_ref,
                     of_ref, ob_ref, ns_ref, s_ref, *, tok, q_silu, scale, hb, dk, dv):
    n = pl.program_id(1)
    tt = tok.seq
    nch = tt // CHUNK
    is_ctx, dec_start = _seq_flags(n, tok)
    streams = [(d, h) for d in range(2) for h in range(hb)]

    @pl.when(is_ctx)
    def _():
        s_ref[...] = jnp.zeros_like(s_ref)

    @pl.when(dec_start)
    def _():
        for d, h in streams:
            s_ref[d, h] = s0_ref[0, 0, d, h].T

    dirs = ((qf_ref, kf_ref, vf_ref, cf_ref, of_ref), (qb_ref, kb_ref, vb_ref, cb_ref, ob_ref))
    ops = {}
    for d, (q_ref, k_ref, v_ref, c_ref, _) in enumerate(dirs):
        rev = d == 1
        for h in range(hb):
            q = q_ref[:, h * dk:(h + 1) * dk].astype(F32)
            if q_silu:
                q = _silu(q)
            q = q * scale
            k = k_ref[:, h * dk:(h + 1) * dk].astype(F32)
            v = v_ref[:, h * dv:(h + 1) * dv].astype(BF16)
            cb = c_ref[:, h * dk:(h + 1) * dk]
            for c in range(nch):
                rows = slice(c * CHUNK, (c + 1) * CHUNK)
                qc, kc, cbc = q[rows], k[rows], cb[rows]
                end = cbc[0:1, :] if rev else cbc[CHUNK - 1:CHUNK, :]
                ops[d, h, c] = dict(
                    intra=_gla_operands(qc, kc, cbc, rev), v=v[rows],
                    q_state=(qc * jnp.exp(cbc)).astype(BF16),
                    k_state=(kc * jnp.exp(end - cbc)).astype(BF16),
                    s_decay=jnp.exp(end))
    a_off = {key: _dot_nt(op["intra"][0], op["intra"][1]) for key, op in ops.items()}
    a_diag = {key: _dot_nt(op["intra"][2], op["intra"][3]) for key, op in ops.items()}
    masks = [_gla_diag_mask(CHUNK, rev) for rev in (False, True)]
    attn = {key: (a_off[key] + jnp.where(masks[key[0]], a_diag[key], 0.0)).astype(BF16) for key in ops}
    o_intra = {key: _dot(attn[key], op["v"]) for key, op in ops.items()}
    s_inc = {key: _dot_tn(op["v"], op["k_state"]) for key, op in ops.items()}

    state = {key: s_ref[key] for key in streams}
    for step in range(nch):
        for (d, h), s in state.items():
            c = step if d == 0 else nch - 1 - step
            dirs[d][4][c * CHUNK:(c + 1) * CHUNK, h * dv:(h + 1) * dv] = (
                o_intra[d, h, c] + _dot_nt(ops[d, h, c]["q_state"], s.astype(BF16))).astype(BF16)
            state[d, h] = ops[d, h, c]["s_decay"] * s + s_inc[d, h, c]
    for key, s in state.items():
        s_ref[key] = s

    @pl.when(is_ctx)
    def _():
        for d, h in streams:
            ns_ref[0, d, h] = s_ref[d, h].T


def _gla_scan(q_src, q_blk, kf_src, kf_blk, kb_src, kb_blk, v_src, v_blk, cb_f, cb_b, s0, j, tok,
              *, heads, dk, dv, q_silu, hb):
    m = q_src.shape[0]
    tt = tok.seq
    assert heads % hb == 0 and q_blk % hb == 0 and kf_blk % hb == 0 and kb_blk % hb == 0 and v_blk % hb == 0
    bwd = lambda n: _bwd_block(n, tok)
    fwd = lambda n: n

    def stream(blk, k_blk):
        return [
            pl.BlockSpec((tt, hb * dk), lambda h, n: (blk(n), q_blk // hb + h)),
            pl.BlockSpec((tt, hb * dk), lambda h, n: (blk(n), k_blk // hb + h)),
            pl.BlockSpec((tt, hb * dv), lambda h, n: (blk(n), v_blk // hb + h)),
            pl.BlockSpec((tt, hb * dk), lambda h, n: (blk(n), h)),
        ]

    n_batch = tok.n_ctx_blocks
    return pl.pallas_call(
        functools.partial(_gla_scan_kernel, tok=tok, q_silu=q_silu, scale=dk ** -0.5, hb=hb, dk=dk, dv=dv),
        grid=(heads // hb, tok.n_blocks),
        in_specs=stream(fwd, kf_blk) + stream(bwd, kb_blk) + [
            pl.BlockSpec((1, 1, 2, hb, dk, dv), lambda h, n: (_dec_batch(n, tok), j, 0, h, 0, 0)),
        ],
        out_specs=[
            pl.BlockSpec((tt, hb * dv), lambda h, n: (n, h)),
            pl.BlockSpec((tt, hb * dv), lambda h, n: (bwd(n), h)),
            pl.BlockSpec((1, 2, hb, dk, dv), lambda h, n: (_state_block(n, tok), 0, h, 0, 0)),
        ],
        out_shape=[
            jax.ShapeDtypeStruct((m, heads * dv), BF16),
            jax.ShapeDtypeStruct((m, heads * dv), BF16),
            jax.ShapeDtypeStruct((n_batch, 2, heads, dk, dv), F32),
        ],
        scratch_shapes=[pltpu.VMEM((2, hb, dv, dk), F32)],
        compiler_params=_params("parallel", "arbitrary"),
        name="gla_scan",
    )(q_src, kf_src, v_src, cb_f, q_src, kb_src, v_src, cb_b, s0)


def kernel(x_prompt, x_sample, c, state_deltanet, state_gla, state_hgrn, c_ctx, ada_w, ada_b, norm1, norm2, final_norm, w_up, w_down, dn_w_in, dn_conv, dn_A_log, dn_dt_bias, dn_norm, dn_w_out, gla_w_in, gla_w_a2, gla_b_a, gla_norm, gla_w_out, hgrn_w_in, hgrn_lb_logits, hgrn_norm, hgrn_w_out):
    batch, seq, d = x_prompt.shape
    dec_batch, dec_seq, _ = x_sample.shape
    depth = ada_w.shape[0]
    tok = Tokens(m_ctx=batch * seq, seq=seq, dec_batch=dec_batch, dec_seq=dec_seq)
    assert seq % CHUNK == 0 and dec_seq % seq == 0 and seq % GRID_W == 0

    dn_hv, dn_dk, dn_dv = state_deltanet.shape[3:]
    dn_vd = dn_hv * dn_dv
    n_conv = dn_conv.shape[2]
    dn_kd = (n_conv - dn_vd) // 2
    dn_hk = dn_kd // dn_dk
    gla_h, gla_dk, gla_dv = state_gla.shape[3:]
    gla_kd, gla_vd = gla_h * gla_dk, gla_h * gla_dv
    hg_h, hg_dk, hg_dv = state_hgrn.shape[3:]

    x = jnp.concatenate([x_prompt.reshape(tok.m_ctx, d), x_sample.reshape(dec_batch * dec_seq, d)], axis=0)
    n_rows = 1 + dec_batch
    pad = (-n_rows) % 8
    cond = jnp.concatenate([c_ctx[None, :], c, jnp.zeros((pad, d), F32)], axis=0)
    mod_all = _mod_table(cond, ada_w, ada_b).reshape(depth, n_rows + pad, 6, d)

    new_states = ([], [], [])
    dn_states = None
    for i in range(depth):
        kind, j = i % N_MIXERS, i // N_MIXERS
        mod = mod_all[i]
        if kind == 0:
            w_in = dn_w_in[j]
            proj = _proj(x, norm1[i], mod, w_in[:, :n_conv + dn_vd].astype(BF16), tok)
            raw = _proj(x, norm1[i], mod, w_in[:, n_conv + dn_vd:], tok, precise=True)
            qkv = _dn_prep(proj, dn_conv[j], n_conv, dn_kd, dn_dk, tok)
            gates, gates_t = _dn_gates(raw, dn_A_log[j], dn_dt_bias[j], tok)
            o_f, o_b, dn_states = _dn_scan(qkv, gates, gates_t, state_deltanet, j, dn_states, tok,
                                           hk_n=dn_hk, hv=dn_hv, dk=dn_dk, dv=dn_dv)
            x = _out_proj(o_f, o_b, proj, n_conv // dn_vd, dn_norm[j], x, mod, dn_w_out[j].astype(BF16), tok)
        elif kind == 1:
            w_in = gla_w_in[j]
            n_main = 2 * gla_kd + 2 * gla_vd
            proj = _proj(x, norm1[i], mod, w_in[:, :n_main].astype(BF16), tok)
            a1 = _proj(x, norm1[i], mod, w_in[:, n_main:], tok, precise=True)
            cb_f, cb_b = _gla_prep(a1, gla_w_a2[j], gla_b_a[j], tok)
            o_f, o_b, ns = _gla_scan(proj, 0, proj, gla_h, proj, gla_h, proj, (2 * gla_kd) // gla_dv,
                                     cb_f, cb_b, state_gla, j, tok,
                                     heads=gla_h, dk=gla_dk, dv=gla_dv, q_silu=False, hb=min(gla_h, 2))
            x = _out_proj(o_f, o_b, proj, (2 * gla_kd + gla_vd) // gla_vd, gla_norm[j], x, mod,
                          gla_w_out[j].astype(BF16), tok)
        else:
            proj = _proj(x, norm1[i], mod, hgrn_w_in[j].astype(BF16), tok)
            k_f, k_b, cb_f, cb_b = _hgrn_prep(proj, hgrn_lb_logits, i, d, tok)
            o_f, o_b, ns = _gla_scan(proj, 0, k_f, 0, k_b, 0, proj, d // hg_dv, cb_f, cb_b, state_hgrn, j, tok,
                                     heads=hg_h, dk=hg_dk, dv=hg_dv, q_silu=True, hb=min(hg_h, 8))
            x = _out_proj(o_f, o_b, proj, 2, hgrn_norm[j], x, mod, hgrn_w_out[j].astype(BF16), tok)
        if kind != 0:
            new_states[kind].append(ns)
        x = _mlp(x, norm2[i], mod, w_up[i].astype(BF16), w_down[i].astype(BF16), final_norm, tok,
                 final=(i == depth - 1))

    y_ctx, y_dec = x
    y_prompt = y_ctx.reshape(batch, seq, d)
    y_sample = y_dec.reshape(dec_batch, dec_seq, d)
    stacked = [s[0][:, None] if len(s) == 1 else jnp.stack(s, axis=1) for s in new_states[1:]]
    return (y_prompt, y_sample, dn_states) + tuple(stacked)
```

```python
import functools
from typing import NamedTuple

import jax
import jax.numpy as jnp
from jax import lax
from jax.experimental import pallas as pl
from jax.experimental.pallas import tpu as pltpu

F32 = jnp.float32
BF16 = jnp.bfloat16

EPS = 1e-6
GRID_W = 64
CONV_K = 5
GLA_RANK = 16
GLA_TAU = 16.0
N_MIXERS = 3
CHUNK = 64
SUB = 16
EXP_CLAMP = 80.0
LANES = 128
V7X_MXU_COLS = 256
V7X_VMEM_LIMIT_BYTES = 56 * 1024 * 1024
HIGHEST = lax.Precision.HIGHEST


class Tokens(NamedTuple):
    m_ctx: int
    seq: int
    dec_batch: int
    dec_seq: int

    @property
    def m(self):
        return self.m_ctx + self.dec_batch * self.dec_seq

    @property
    def n_ctx_blocks(self):
        return self.m_ctx // self.seq

    @property
    def blocks_per_dec(self):
        return self.dec_seq // self.seq

    @property
    def n_blocks(self):
        return self.m // self.seq


def _tile(n, pref):
    t = min(n, pref)
    while n % t:
        t //= 2
    return t


def _token_tile(tok, pref):
    t = pref
    while tok.m_ctx % t or tok.dec_seq % t:
        t //= 2
    return t


def _params(*sem):
    return pltpu.CompilerParams(dimension_semantics=sem, vmem_limit_bytes=V7X_VMEM_LIMIT_BYTES)


def _mod_row(i, tm, tok):
    start = i * tm
    return jnp.where(start < tok.m_ctx, 0, 1 + (start - tok.m_ctx) // tok.dec_seq)


def _sigmoid(x):
    return 0.5 + 0.5 * jnp.tanh(0.5 * x)


def _silu(x):
    half = 0.5 * x
    return half + half * jnp.tanh(half)


def _softplus(x):
    return jnp.maximum(x, 0.0) + jnp.log(1.0 + jnp.exp(-jnp.abs(x)))


def _norm_mod(x, nw, shift, scale):
    ms = jnp.mean(x * x, axis=-1, keepdims=True)
    return x * lax.rsqrt(ms + EPS) * nw * (1.0 + scale) + shift


def _dot(a, b):
    return jnp.dot(a, b, preferred_element_type=F32)


def _dot_nt(a, b):
    return lax.dot_general(a, b, (((1,), (1,)), ((), ())), preferred_element_type=F32)


def _dot_tn(a, b):
    return lax.dot_general(a, b, (((0,), (0,)), ((), ())), preferred_element_type=F32)


def _split_dot(tri, x):
    hi = x.astype(BF16)
    lo = (x - hi.astype(F32)).astype(BF16)
    return _dot(tri, hi) + _dot(tri, lo)


def _split_matmul(a, b):
    ah = a.astype(BF16)
    al = (a - ah.astype(F32)).astype(BF16)
    bh = b.astype(BF16)
    bl = (b - bh.astype(F32)).astype(BF16)
    return _dot(ah, bh) + _dot(al, bh) + _dot(ah, bl)


def _tri(n, upper):
    r = lax.broadcasted_iota(jnp.int32, (n, n), 0)
    c = lax.broadcasted_iota(jnp.int32, (n, n), 1)
    return jnp.where((r <= c) if upper else (r >= c), 1.0, 0.0).astype(BF16)


def _mod_kernel(c_ref, w_ref, b_ref, o_ref):
    c = c_ref[...]
    s = _silu(c)
    o_ref[0] = jnp.dot(s, w_ref[0], precision=HIGHEST, preferred_element_type=F32) + b_ref[0]


def _mod_table(cond, ada_w, ada_b):
    depth, d, n = ada_w.shape
    rows = cond.shape[0]
    tn = _tile(n, 1024)
    return pl.pallas_call(
        _mod_kernel,
        grid=(depth, n // tn),
        in_specs=[
            pl.BlockSpec((rows, d), lambda l, j: (0, 0)),
            pl.BlockSpec((1, d, tn), lambda l, j: (l, 0, j)),
            pl.BlockSpec((1, 1, tn), lambda l, j: (l, 0, j)),
        ],
        out_specs=pl.BlockSpec((1, rows, tn), lambda l, j: (l, 0, j)),
        out_shape=jax.ShapeDtypeStruct((depth, rows, n), F32),
        compiler_params=_params("parallel", "parallel"),
        name="mod_table",
    )(cond, ada_w, ada_b.reshape(depth, 1, n))


def _proj_kernel(x_ref, nw_ref, mod_ref, w_ref, o_ref, h_ref, *, precise):
    @pl.when(pl.program_id(1) == 0)
    def _():
        h = _norm_mod(x_ref[...], nw_ref[...], mod_ref[0, 0:1, :], mod_ref[0, 1:2, :])
        h_ref[...] = h.astype(h_ref.dtype)

    if precise:
        o_ref[...] = _split_matmul(h_ref[...], w_ref[...])
    else:
        o_ref[...] = _dot(h_ref[...], w_ref[...]).astype(o_ref.dtype)


def _proj(x, nw, mod, w, tok, *, precise=False, tm=1024, tn=2048):
    m, d = x.shape
    n = w.shape[1]
    tm = _token_tile(tok, tm)
    tn = _tile(n, tn)
    return pl.pallas_call(
        functools.partial(_proj_kernel, precise=precise),
        grid=(m // tm, n // tn),
        in_specs=[
            pl.BlockSpec((tm, d), lambda i, j: (i, 0)),
            pl.BlockSpec((1, d), lambda i, j: (0, 0)),
            pl.BlockSpec((1, 6, d), lambda i, j: (_mod_row(i, tm, tok), 0, 0)),
            pl.BlockSpec((d, tn), lambda i, j: (0, j)),
        ],
        out_specs=pl.BlockSpec((tm, tn), lambda i, j: (i, j)),
        out_shape=jax.ShapeDtypeStruct((m, n), F32 if precise else BF16),
        scratch_shapes=[pltpu.VMEM((tm, d), F32 if precise else BF16)],
        compiler_params=_params("parallel", "arbitrary"),
        name="proj_precise" if precise else "proj",
    )(x, nw.reshape(1, d), mod, w)


def _mlp_kernel(x_ref, nw_ref, mod_ref, wu_ref, wd_ref, fw_ref, *rest, n_ctx_tiles):
    out_refs, (h_ref, acc_ref) = rest[:-2], rest[-2:]
    final = n_ctx_tiles is not None
    i = pl.program_id(0)
    f = pl.program_id(1)

    @pl.when(f == 0)
    def _():
        h = _norm_mod(x_ref[...], nw_ref[...], mod_ref[0, 3:4, :], mod_ref[0, 4:5, :])
        h_ref[...] = h.astype(BF16)
        acc_ref[...] = jnp.zeros_like(acc_ref)

    a = jnp.maximum(_dot(h_ref[...], wu_ref[...]), 0.0)
    acc_ref[...] += _dot((a * a).astype(BF16), wd_ref[...])

    last = f == pl.num_programs(1) - 1

    def result():
        y = x_ref[...] + mod_ref[0, 5:6, :] * acc_ref[...]
        if final:
            ms = jnp.mean(y * y, axis=-1, keepdims=True)
            y = y * lax.rsqrt(ms + EPS) * fw_ref[...]
        return y

    if not final:
        @pl.when(last)
        def _():
            out_refs[0][...] = result()
    else:
        @pl.when(jnp.logical_and(last, i < n_ctx_tiles))
        def _():
            out_refs[0][...] = result()

        @pl.when(jnp.logical_and(last, i >= n_ctx_tiles))
        def _():
            out_refs[1][...] = result()


def _mlp(x, nw, mod, w_up, w_down, final_w, tok, *, final, tm=512, tf=1024):
    m, d = x.shape
    ff = w_up.shape[1]
    tm = _token_tile(tok, tm)
    tf = _tile(ff, tf)
    n_ctx_tiles = tok.m_ctx // tm
    if final:
        out_specs = [pl.BlockSpec((tm, d), lambda i, f: (jnp.minimum(i, n_ctx_tiles - 1), 0)),
                     pl.BlockSpec((tm, d), lambda i, f: (jnp.maximum(i - n_ctx_tiles, 0), 0))]
        out_shape = [jax.ShapeDtypeStruct((tok.m_ctx, d), F32), jax.ShapeDtypeStruct((m - tok.m_ctx, d), F32)]
    else:
        out_specs = pl.BlockSpec((tm, d), lambda i, f: (i, 0))
        out_shape = jax.ShapeDtypeStruct((m, d), F32)
    return pl.pallas_call(
        functools.partial(_mlp_kernel, n_ctx_tiles=n_ctx_tiles if final else None),
        grid=(m // tm, ff // tf),
        in_specs=[
            pl.BlockSpec((tm, d), lambda i, f: (i, 0)),
            pl.BlockSpec((1, d), lambda i, f: (0, 0)),
            pl.BlockSpec((1, 6, d), lambda i, f: (_mod_row(i, tm, tok), 0, 0)),
            pl.BlockSpec((d, tf), lambda i, f: (0, f)),
            pl.BlockSpec((tf, d), lambda i, f: (f, 0)),
            pl.BlockSpec((1, d), lambda i, f: (0, 0)),
        ],
        out_specs=out_specs,
        out_shape=out_shape,
        scratch_shapes=[pltpu.VMEM((tm, d), BF16), pltpu.VMEM((tm, d), F32)],
        compiler_params=_params("arbitrary" if final else "parallel", "arbitrary"),
        name="mlp_final" if final else "mlp",
    )(x, nw.reshape(1, d), mod, w_up, w_down, final_w.reshape(1, d))


def _out_kernel(of_ref, ob_ref, g_ref, nw_ref, x_ref, mod_ref, w_ref, o_ref, y0_ref, y1_ref, *, n_heads, dh):
    i = pl.program_id(0)
    d = w_ref.shape[1]
    n_chunks = d // V7X_MXU_COLS
    heads_per_chunk = -(-n_heads // n_chunks)

    @pl.when(i == 0)
    def _():
        y1_ref[...] = jnp.zeros_like(y1_ref)

    def body(y_write, y_read):
        ones = jnp.full((dh, LANES), 1.0 / dh, BF16) if dh & (dh - 1) == 0 else None
        for c in range(n_chunks):
            cols = slice(c * V7X_MXU_COLS, (c + 1) * V7X_MXU_COLS)
            o_ref[:, cols] = x_ref[:, cols] + mod_ref[0, 2:3, cols] * _dot(y_read[...], w_ref[:, cols])
            for h in range(c * heads_per_chunk, min((c + 1) * heads_per_chunk, n_heads)):
                sl = slice(h * dh, (h + 1) * dh)
                o = of_ref[:, sl].astype(F32) + ob_ref[:, sl].astype(F32)
                if ones is None:
                    ms = _dot((o * o).astype(BF16), jnp.ones((dh, LANES), BF16)) * (1.0 / dh)
                else:
                    ms = _dot((o * o).astype(BF16), ones)
                inv = lax.rsqrt(ms + EPS)
                y = o * nw_ref[...] * _silu(g_ref[:, sl].astype(F32))
                for k in range(dh // LANES):
                    lo = h * dh + k * LANES
                    y_write[:, lo:lo + LANES] = (y[:, k * LANES:(k + 1) * LANES] * inv).astype(BF16)

    @pl.when(i % 2 == 0)
    def _():
        body(y0_ref, y1_ref)

    @pl.when(i % 2 == 1)
    def _():
        body(y1_ref, y0_ref)


def _out_proj(o_f, o_b, gate_src, gate_blk, norm_w, x, mod, w_out, tok, *, tm=256):
    m, vd = o_f.shape
    d = x.shape[1]
    dh = norm_w.shape[0]
    tm = _token_tile(tok, tm)
    n_tiles = m // tm
    assert d % V7X_MXU_COLS == 0 and dh % LANES == 0
    cur = lambda i: jnp.minimum(i, n_tiles - 1)
    prev = lambda i: jnp.maximum(i - 1, 0)
    return pl.pallas_call(
        functools.partial(_out_kernel, n_heads=vd // dh, dh=dh),
        grid=(n_tiles + 1,),
        in_specs=[
            pl.BlockSpec((tm, vd), lambda i: (cur(i), 0)),
            pl.BlockSpec((tm, vd), lambda i: (cur(i), 0)),
            pl.BlockSpec((tm, vd), lambda i: (cur(i), gate_blk)),
            pl.BlockSpec((1, dh), lambda i: (0, 0)),
            pl.BlockSpec((tm, d), lambda i: (prev(i), 0)),
            pl.BlockSpec((1, 6, d), lambda i: (_mod_row(prev(i), tm, tok), 0, 0)),
            pl.BlockSpec((vd, d), lambda i: (0, 0), pipeline_mode=pl.Buffered(1)),
        ],
        out_specs=pl.BlockSpec((tm, d), lambda i: (prev(i), 0)),
        out_shape=jax.ShapeDtypeStruct((m, d), F32),
        scratch_shapes=[pltpu.VMEM((tm, vd), BF16), pltpu.VMEM((tm, vd), BF16)],
        compiler_params=_params("arbitrary"),
        name="out_proj",
    )(o_f, o_b, gate_src, norm_w.reshape(1, dh), x, mod, w_out)


def _bwd_block(n, tok):
    nc, per = tok.n_ctx_blocks, tok.blocks_per_dec
    r = jnp.maximum(n - nc, 0)
    return jnp.where(n < nc, n, nc + (r // per) * per + (per - 1 - r % per))


def _dec_batch(n, tok):
    return jnp.maximum(n - tok.n_ctx_blocks, 0) // tok.blocks_per_dec


def _state_block(n, tok):
    return jnp.minimum(n, tok.n_ctx_blocks - 1)


def _seq_flags(n, tok):
    nc, per = tok.n_ctx_blocks, tok.blocks_per_dec
    is_ctx = n < nc
    r = jnp.maximum(n - nc, 0) % per
    return is_ctx, jnp.logical_and(jnp.logical_not(is_ctx), r == 0)


def _dn_prep_kernel(x_ref, shift_ref, w_ref, o_ref, *, n_q_blocks, dk):
    j = pl.program_id(1)
    x = x_ref[...]
    tc = x.shape[1]
    acc = None
    for s in range(CONV_K):
        xs = x.astype(F32) if s == CONV_K // 2 else _dot(shift_ref[0, s], x)
        term = xs * w_ref[s:s + 1, :]
        acc = term if acc is None else acc + term
    y = _silu(acc)

    def normed(scale):
        ones = jnp.ones((dk, dk), BF16)
        for g in range(tc // dk):
            blk = y[:, g * dk:(g + 1) * dk]
            ss = _dot((blk * blk).astype(BF16), ones)
            o_ref[:, g * dk:(g + 1) * dk] = (blk * (lax.rsqrt(ss + EPS) * scale)).astype(o_ref.dtype)

    @pl.when(j < n_q_blocks)
    def _():
        normed(dk ** -0.5)

    @pl.when(jnp.logical_and(j >= n_q_blocks, j < 2 * n_q_blocks))
    def _():
        normed(1.0)

    @pl.when(j >= 2 * n_q_blocks)
    def _():
        o_ref[...] = y.astype(o_ref.dtype)


def _conv_shift_matrices(tt):
    t = jnp.arange(tt)[:, None]
    u = jnp.arange(tt)[None, :]
    mats = []
    for seg in (tt, GRID_W):
        same = (t // seg) == (u // seg)
        mats.append(jnp.stack([jnp.logical_and(u == t + s - CONV_K // 2, same) for s in range(CONV_K)]))
    return jnp.stack(mats).astype(BF16)


def _dn_prep(proj, conv_w, n_conv, kd, dk, tok):
    m = proj.shape[0]
    tt = tok.seq
    tc = _tile(kd, 2048)
    n_ctx = tok.n_ctx_blocks
    return pl.pallas_call(
        functools.partial(_dn_prep_kernel, n_q_blocks=kd // tc, dk=dk),
        grid=(m // tt, n_conv // tc),
        in_specs=[
            pl.BlockSpec((tt, tc), lambda i, j: (i, j)),
            pl.BlockSpec((1, CONV_K, tt, tt), lambda i, j: (jnp.where(i < n_ctx, 0, 1), 0, 0, 0)),
            pl.BlockSpec((CONV_K, tc), lambda i, j: (0, j)),
        ],
        out_specs=pl.BlockSpec((tt, tc), lambda i, j: (i, j)),
        out_shape=jax.ShapeDtypeStruct((m, n_conv), BF16),
        compiler_params=_params("parallel", "parallel"),
        name="dn_prep",
    )(proj, _conv_shift_matrices(tt), conv_w)


def _dn_gate_kernel(x_ref, a_ref, bias_ref, o_ref, gt_ref, *, hv, tt):
    x = x_ref[...]
    lanes = x.shape[1]
    lane = lax.broadcasted_iota(jnp.int32, (1, lanes), 1)
    is_fwd = lane < 3 * hv
    g = -a_ref[...] * _softplus(x + bias_ref[...])
    beta = pltpu.roll(_sigmoid(x), 2 * hv, 1)
    lower = _tri(CHUNK, upper=False)
    upper = _tri(CHUNK, upper=True)
    for c in range(tt // CHUNK):
        rows = slice(c * CHUNK, (c + 1) * CHUNK)
        gch = g[rows]
        gc = jnp.where(is_fwd, _split_dot(lower, gch), _split_dot(upper, gch))
        g_end = jnp.where(is_fwd, gc[CHUNK - 1:CHUNK, :], gc[0:1, :])
        eg = jnp.exp(gc)
        o_ref[0, rows, :] = beta[rows]
        o_ref[1, rows, :] = gc
        o_ref[2, rows, :] = eg
        o_ref[3, rows, :] = jnp.exp(g_end - gc)
        o_ref[4, rows, :] = beta[rows] * eg
    gt_ref[...] = o_ref[1].T


def _dn_gates(raw, a_log, dt_bias, tok):
    m, lanes = raw.shape
    hv = lanes // 4
    tt = tok.seq
    zeros = jnp.zeros((2 * hv,), F32)
    a_full = jnp.concatenate([zeros, jnp.exp(a_log).reshape(-1)]).reshape(1, lanes)
    bias_full = jnp.concatenate([zeros, dt_bias.reshape(-1)]).reshape(1, lanes)
    return pl.pallas_call(
        functools.partial(_dn_gate_kernel, hv=hv, tt=tt),
        grid=(m // tt,),
        in_specs=[
            pl.BlockSpec((tt, lanes), lambda i: (i, 0)),
            pl.BlockSpec((1, lanes), lambda i: (0, 0)),
            pl.BlockSpec((1, lanes), lambda i: (0, 0)),
        ],
        out_specs=[
            pl.BlockSpec((5, tt, lanes), lambda i: (0, i, 0)),
            pl.BlockSpec((lanes, tt), lambda i: (0, i)),
        ],
        out_shape=[jax.ShapeDtypeStruct((5, m, lanes), F32), jax.ShapeDtypeStruct((lanes, m), F32)],
        compiler_params=_params("parallel"),
        name="dn_gates",
    )(raw, a_full, bias_full)


def _dn_scan_kernel(qf_ref, kf_ref, vf_ref, gf_ref, gtf_ref, qb_ref, kb_ref, vb_ref, gb_ref, gtb_ref, s0_ref,
                    of_ref, ob_ref, ns_ref, s_ref, u0_ref, wq_ref, attn_ref, kd_ref, ge_ref, *, tok, hv, dv, hkb):
    hk0 = pl.program_id(0) * hkb
    n = pl.program_id(1)
    tt = tok.seq
    nch = tt // CHUNK
    dk = qf_ref.shape[1] // hkb
    assert dv == 2 * CHUNK and CHUNK == 4 * SUB
    has_prev = n > 0
    is_ctx, dec_start = _seq_flags(jnp.maximum(n - 1, 0), tok)
    wslot = n % 2
    rslot = 1 - wslot
    scratch = (u0_ref, wq_ref, attn_ref, kd_ref, ge_ref)

    @pl.when(n == 0)
    def _():
        s_ref[...] = jnp.zeros_like(s_ref)
        for ref in scratch:
            ref[1] = jnp.zeros(ref.shape[1:], ref.dtype)

    @pl.when(jnp.logical_and(has_prev, is_ctx))
    def _():
        s_ref[...] = jnp.zeros_like(s_ref)

    @pl.when(jnp.logical_and(has_prev, dec_start))
    def _():
        s_ref[...] = s0_ref[0, 0]

    bf = lambda x: x.astype(BF16)
    lanes = gf_ref.shape[2]
    lane_iota = lax.broadcasted_iota(jnp.int32, (1, lanes), 1)
    ri = lax.broadcasted_iota(jnp.int32, (CHUNK, 2 * CHUNK), 0)
    li = lax.broadcasted_iota(jnp.int32, (CHUNK, 2 * CHUNK), 1)
    left = li < CHUNK
    ci = li % CHUNK
    same = [(ri // b) == (ci // b) for b in (SUB, 2 * SUB)]
    zero_b = jnp.zeros((CHUNK, 2 * CHUNK), BF16)

    def block_diag(xb):
        return jnp.concatenate([jnp.where(left, xb, zero_b), jnp.where(left, zero_b, xb)], axis=0)

    def pair_dot(a, b):
        return _dot(bf(a), block_diag(bf(b)))

    dirs = ((qf_ref, kf_ref, vf_ref, gf_ref, gtf_ref, of_ref), (qb_ref, kb_ref, vb_ref, gb_ref, gtb_ref, ob_ref))

    def prepare():
        ch = {}
        for d, kh in [(d, kh) for d in range(2) for kh in range(hkb)]:
            q_ref, k_ref, v_ref, g_ref, gt_ref, _ = dirs[d]
            incl = (ri >= ci) if d == 0 else (ri <= ci)
            strict = (ri > ci) if d == 0 else (ri < ci)
            end_row = CHUNK - 1 if d == 0 else 0
            q = q_ref[:, kh * dk:(kh + 1) * dk]
            k = k_ref[:, kh * dk:(kh + 1) * dk]
            kf32 = k.astype(F32)
            qf32 = q.astype(F32)
            cols, g_rows, vs = [], [], []
            for e in range(2):
                a_lane = 2 * hv + d * hv + 2 * (hk0 + kh) + e
                sel = lane_iota == a_lane
                cols.append([jnp.sum(jnp.where(sel, g_ref[i], 0.0), axis=1, keepdims=True) for i in range(5)])
                g_rows.append(gt_ref[pl.ds(a_lane, 1), :])
                vs.append(v_ref[:, (2 * kh + e) * dv:(2 * kh + e + 1) * dv].astype(F32))
            for c in range(nch):
                rows = slice(c * CHUNK, (c + 1) * CHUNK)
                pair = lambda i: jnp.where(left, cols[0][i][rows], cols[1][i][rows])
                k2 = jnp.concatenate([k[rows], k[rows]], axis=0)
                kk = _dot_nt(k[rows], k2)
                qk = _dot_nt(q[rows], k2)
                g_row = jnp.concatenate([g_rows[0][:, rows], g_rows[1][:, rows]], axis=1)
                diff = pair(1) - g_row
                decay = jnp.where(incl, jnp.exp(jnp.where(incl, diff, 0.0)), 0.0)
                p = jnp.where(strict, -(pair(0) * kk * decay), 0.0)
                attn_ref[wslot, d, kh, c] = bf(qk * decay)
                for e in range(2):
                    beta_c, _, eg_c, ekd_c, beg_c = cols[e]
                    qg = qf32[rows] * eg_c[rows]
                    kd_ref[wslot, d, 2 * kh + e, c] = bf(kf32[rows] * ekd_c[rows])
                    wq_ref[wslot, d, 2 * kh + e, c, CHUNK:, :] = bf(qg)
                    g_end = eg_c[c * CHUNK + end_row:c * CHUNK + end_row + 1, :]
                    ge_ref[wslot, d, 2 * kh + e, c] = jnp.broadcast_to(g_end, ge_ref.shape[4:])
                ch[kh, d, c] = dict(
                    pd=jnp.where(same[0], p, 0.0),
                    p32=jnp.where(jnp.logical_and(same[1], jnp.logical_not(same[0])), p, 0.0),
                    p64=jnp.where(same[1], 0.0, p),
                    r=[jnp.concatenate([cols[e][0][rows] * vs[e][rows], cols[e][4][rows] * kf32[rows]], axis=1)
                       for e in range(2)])
            yield
        for c in ch.values():
            c["q"] = pair_dot(c["pd"], c["pd"])
            c["xm"] = c["pd"]
        yield
        for it in range(2):
            for c in ch.values():
                q, xm = c["q"], c["xm"]
                both = _dot(bf(q), jnp.concatenate([block_diag(bf(xm)), block_diag(bf(q))], axis=1))
                c["xm"] = xm + q + both[:, :2 * CHUNK]
                c["q"] = both[:, 2 * CHUNK:]
            yield
        for c in ch.values():
            q, xm = c.pop("q"), c["xm"]
            c["xm"] = xm + q + pair_dot(q, xm)
        yield
        for level in ("p32", "p64"):
            for c in ch.values():
                c["y"] = c[level] + pair_dot(c[level], c["xm"])
            yield
            for c in ch.values():
                xm, y = c["xm"], c.pop("y")
                c["xm"] = xm + y + pair_dot(xm, y)
            yield
        zero_r = jnp.zeros((CHUNK, 2 * dv), BF16)
        for (kh, d, cidx), c in ch.items():
            r0, r1 = c["r"]
            rhs = jnp.concatenate([jnp.concatenate([bf(r0), zero_r], axis=1),
                                   jnp.concatenate([zero_r, bf(r1)], axis=1)], axis=0)
            big = _dot(bf(c["xm"]), rhs)
            sol = [r0 + big[:, :2 * dv], r1 + big[:, 2 * dv:]]
            u0_ref[wslot, d, kh, cidx] = jnp.concatenate([sol[0][:, :dv], sol[1][:, :dv]], axis=1)
            for e in range(2):
                wq_ref[wslot, d, 2 * kh + e, cidx, :CHUNK, :] = bf(sol[e][:, dv:])
        yield

    def recur():
        state = {(d, e): s_ref[d, e] for d in range(2) for e in range(2 * hkb)}
        zero_u = jnp.zeros((CHUNK, dv), BF16)
        for step in range(nch):
            chunk = lambda d: step if d == 0 else nch - 1 - step
            ws = {(d, e): _dot(wq_ref[rslot, d, e, chunk(d)], bf(s)) for (d, e), s in state.items()}
            yield
            for d, kh in [(d, kh) for d in range(2) for kh in range(hkb)]:
                c = chunk(d)
                u0 = u0_ref[rslot, d, kh, c]
                ev = [2 * kh, 2 * kh + 1]
                ub = [bf(u0[:, e * dv:(e + 1) * dv] - ws[d, ev[e]][:CHUNK]) for e in range(2)]
                u2 = jnp.concatenate([jnp.concatenate([ub[0], zero_u], axis=1),
                                      jnp.concatenate([zero_u, ub[1]], axis=1)], axis=0)
                o = (_dot(attn_ref[rslot, d, kh, c], u2)
                     + jnp.concatenate([ws[d, ev[0]][CHUNK:], ws[d, ev[1]][CHUNK:]], axis=1))
                dirs[d][5][c * CHUNK:(c + 1) * CHUNK, 2 * kh * dv:2 * (kh + 1) * dv] = bf(o)
                for e in range(2):
                    state[d, ev[e]] = (ge_ref[rslot, d, ev[e], c][0:1, :] * state[d, ev[e]]
                                       + _dot_tn(kd_ref[rslot, d, ev[e], c], ub[e]))
            yield
        for (d, e), s in state.items():
            s_ref[d, e] = s
        yield

    halves = [recur(), prepare()]
    while halves:
        for g in list(halves):
            if next(g, "done") == "done":
                halves.remove(g)

    @pl.when(jnp.logical_and(has_prev, is_ctx))
    def _():
        ns_ref[0, 0] = s_ref[...]


def _dn_scan_kernel_aliased(*refs, **kw):
    n_in = 11
    return _dn_scan_kernel(*refs[:n_in], *refs[n_in + 1:], **kw)


def _dn_scan(qkv, gates, gates_t, s0, j, new_states, tok, *, hk_n, hv, dk, dv):
    m = qkv.shape[0]
    tt = tok.seq
    rep = hv // hk_n
    assert rep == 2
    lanes = gates.shape[2]
    kd = hk_n * dk
    nb = tok.n_blocks
    nch = tt // CHUNK
    fwd = lambda n: jnp.minimum(n, nb - 1)
    bwd = lambda n: _bwd_block(jnp.minimum(n, nb - 1), tok)
    prev = lambda n: jnp.maximum(n - 1, 0)

    hkb = max(b for b in (4, 2, 1) if hk_n % b == 0)
    hvb = hkb * rep

    def stream(blk):
        return [
            pl.BlockSpec((tt, hkb * dk), lambda h, n: (blk(n), h)),
            pl.BlockSpec((tt, hkb * dk), lambda h, n: (blk(n), hk_n // hkb + h)),
            pl.BlockSpec((tt, hvb * dv), lambda h, n: (blk(n), (2 * kd) // (hvb * dv) + h)),
            pl.BlockSpec((5, tt, lanes), lambda h, n: (0, blk(n), 0)),
            pl.BlockSpec((lanes, tt), lambda h, n: (0, blk(n))),
        ]

    n_batch = tok.n_ctx_blocks
    n_layers = s0.shape[1]
    aliased = new_states is not None
    extra_in = [pl.BlockSpec(memory_space=pl.ANY)] if aliased else []
    extra_args = (new_states,) if aliased else ()
    return pl.pallas_call(
        functools.partial(_dn_scan_kernel_aliased if aliased else _dn_scan_kernel, tok=tok, hv=hv, dv=dv, hkb=hkb),
        grid=(hk_n // hkb, nb + 1),
        in_specs=stream(fwd) + stream(bwd) + [
            pl.BlockSpec((1, 1, 2, hvb, dk, dv), lambda h, n: (_dec_batch(prev(n), tok), j, 0, h, 0, 0)),
        ] + extra_in,
        out_specs=[
            pl.BlockSpec((tt, hvb * dv), lambda h, n: (prev(n), h)),
            pl.BlockSpec((tt, hvb * dv), lambda h, n: (_bwd_block(prev(n), tok), h)),
            pl.BlockSpec((1, 1, 2, hvb, dk, dv), lambda h, n: (_state_block(prev(n), tok), j, 0, h, 0, 0)),
        ],
        out_shape=[
            jax.ShapeDtypeStruct((m, hv * dv), BF16),
            jax.ShapeDtypeStruct((m, hv * dv), BF16),
            jax.ShapeDtypeStruct((n_batch, n_layers, 2, hv, dk, dv), F32),
        ],
        input_output_aliases={11: 2} if aliased else {},
        scratch_shapes=[
            pltpu.VMEM((2, hvb, dk, dv), F32),
            pltpu.VMEM((2, 2, hkb, nch, CHUNK, rep * dv), F32),
            pltpu.VMEM((2, 2, hvb, nch, 2 * CHUNK, dk), BF16),
            pltpu.VMEM((2, 2, hkb, nch, CHUNK, rep * CHUNK), BF16),
            pltpu.VMEM((2, 2, hvb, nch, CHUNK, dk), BF16),
            pltpu.VMEM((2, 2, hvb, nch, 8, dv), F32),
        ],
        compiler_params=_params("parallel", "arbitrary"),
        name="dn_scan",
    )(qkv, qkv, qkv, gates, gates_t, qkv, qkv, qkv, gates, gates_t, s0, *extra_args)


def _chunk_cums(la_f, la_b, cbf_ref, cbb_ref, tt):
    lower = _tri(CHUNK, upper=False)
    upper = _tri(CHUNK, upper=True)
    for c in range(tt // CHUNK):
        rows = slice(c * CHUNK, (c + 1) * CHUNK)
        cbf_ref[rows, :] = _split_dot(lower, la_f[rows])
        cbb_ref[rows, :] = _split_dot(upper, la_b[rows])


def _log_sigmoid(x):
    return jnp.minimum(x, 0.0) - jnp.log(1.0 + jnp.exp(-jnp.abs(x)))


def _gla_prep_kernel(a1_ref, w2_ref, b_ref, cbf_ref, cbb_ref, *, tt):
    a1 = a1_ref[...]
    la = [_log_sigmoid(_dot(a1, w2_ref[d]) + b_ref[d]) * (1.0 / GLA_TAU) for d in range(2)]
    _chunk_cums(la[0], la[1], cbf_ref, cbb_ref, tt)


def _gla_prep(a1, w_a2, b_a, tok):
    m, r2 = a1.shape
    kd = w_a2.shape[2]
    tt = tok.seq
    tc = _tile(kd, 1024)
    zeros = jnp.zeros((GLA_RANK, kd), F32)
    w2 = jnp.stack([jnp.concatenate([w_a2[0], zeros]), jnp.concatenate([zeros, w_a2[1]])])
    out = jax.ShapeDtypeStruct((m, kd), F32)
    return pl.pallas_call(
        functools.partial(_gla_prep_kernel, tt=tt),
        grid=(m // tt, kd // tc),
        in_specs=[
            pl.BlockSpec((tt, r2), lambda i, j: (i, 0)),
            pl.BlockSpec((2, r2, tc), lambda i, j: (0, 0, j)),
            pl.BlockSpec((2, 1, tc), lambda i, j: (0, 0, j)),
        ],
        out_specs=[pl.BlockSpec((tt, tc), lambda i, j: (i, j))] * 2,
        out_shape=[out, out],
        compiler_params=_params("parallel", "parallel"),
        name="gla_prep",
    )(a1, w2, b_a.reshape(2, 1, kd))


def _hgrn_prep_kernel(ff_ref, fb_ref, lb_ref, kf_ref, kb_ref, cbf_ref, cbb_ref, *, layer, tt):
    logits = lb_ref[...]
    ex = jnp.exp(logits - jnp.max(logits, axis=0, keepdims=True))
    probs = ex / jnp.sum(ex, axis=0, keepdims=True)
    lb = jnp.sum(probs[1:layer + 1], axis=0, keepdims=True)
    la = []
    for f_ref, k_ref in ((ff_ref, kf_ref), (fb_ref, kb_ref)):
        fl = f_ref[...].astype(F32)
        sg = _sigmoid(fl)
        la.append(jnp.log(lb + (1.0 - lb) * sg))
        k_ref[...] = (1.0 - lb) * (1.0 - sg)
    _chunk_cums(la[0], la[1], cbf_ref, cbb_ref, tt)


def _hgrn_prep(proj, lb_logits, layer, d, tok):
    m = proj.shape[0]
    depth = lb_logits.shape[0]
    tt = tok.seq
    tc = _tile(d, 2048)
    nb = d // tc
    out = jax.ShapeDtypeStruct((m, d), F32)
    return pl.pallas_call(
        functools.partial(_hgrn_prep_kernel, layer=layer, tt=tt),
        grid=(m // tt, nb),
        in_specs=[
            pl.BlockSpec((tt, tc), lambda i, j: (i, 3 * nb + j)),
            pl.BlockSpec((tt, tc), lambda i, j: (i, 4 * nb + j)),
            pl.BlockSpec((depth, tc), lambda i, j: (0, j)),
        ],
        out_specs=[pl.BlockSpec((tt, tc), lambda i, j: (i, j))] * 4,
        out_shape=[out] * 4,
        compiler_params=_params("parallel", "parallel"),
        name="hgrn_prep",
    )(proj, proj, lb_logits)


def _gla_offdiag_pairs(c):
    pairs = []
    size = c
    while size > SUB:
        half = size // 2
        for start in range(0, c, size):
            pairs.append(((start + half, start + size), (start, start + half)))
        size = half
    return pairs


def _gla_operands(q, k, cb, rev):
    c, dk = q.shape
    zero_rows = lambda n: jnp.zeros((n, dk), BF16)

    def padded(x, rows):
        parts = ([zero_rows(rows[0])] if rows[0] else []) + [x] + ([zero_rows(c - rows[1])] if rows[1] < c else [])
        return jnp.concatenate(parts, axis=0) if len(parts) > 1 else x

    q_parts, k_parts = [], []
    for qr, kr in _gla_offdiag_pairs(c):
        if rev:
            qr, kr = kr, qr
        ref_row = qr[0] - 1 if not rev else qr[1]
        ref = cb[ref_row:ref_row + 1, :]
        qs, ks = slice(*qr), slice(*kr)
        q_parts.append(padded((q[qs] * jnp.exp(cb[qs] - ref)).astype(BF16), qr))
        k_parts.append(padded((k[ks] * jnp.exp(ref - cb[ks])).astype(BF16), kr))
    refs = []
    for blk in range(c // SUB):
        lo, hi = blk * SUB, (blk + 1) * SUB
        if not rev:
            ref = cb[lo - 1:lo, :] if blk > 0 else jnp.zeros_like(cb[0:1, :])
        else:
            ref = cb[hi:hi + 1, :] if hi < c else jnp.zeros_like(cb[0:1, :])
        refs.append(jnp.broadcast_to(ref, (SUB, dk)))
    ref_d = jnp.concatenate(refs, axis=0)
    q_diag = (q * jnp.exp(cb - ref_d)).astype(BF16)
    k_diag = (k * jnp.exp(jnp.minimum(ref_d - cb, EXP_CLAMP))).astype(BF16)
    return jnp.concatenate(q_parts, axis=1), jnp.concatenate(k_parts, axis=1), q_diag, k_diag


def _gla_diag_mask(c, rev):
    ri = lax.broadcasted_iota(jnp.int32, (c, c), 0)
    ci = lax.broadcasted_iota(jnp.int32, (c, c), 1)
    same_block = (ri // SUB) == (ci // SUB)
    return jnp.logical_and(same_block, (ri <= ci) if rev else (ri >= ci))


def _gla_scan_kernel(qf_ref, kf_ref, vf_ref, cf_ref, qb_ref, kb_ref, vb_ref, cb_ref, s0_ref,
                     of_ref, ob_ref, ns_ref, s_ref, *, tok, q_silu, scale, hb, dk, dv):
    n = pl.program_id(1)
    tt = tok.seq
    nch = tt // CHUNK
    is_ctx, dec_start = _seq_flags(n, tok)
    streams = [(d, h) for d in range(2) for h in range(hb)]

    @pl.when(is_ctx)
    def _():
        s_ref[...] = jnp.zeros_like(s_ref)

    @pl.when(dec_start)
    def _():
        for d, h in streams:
            s_ref[d, h] = s0_ref[0, 0, d, h].T

    dirs = ((qf_ref, kf_ref, vf_ref, cf_ref, of_ref), (qb_ref, kb_ref, vb_ref, cb_ref, ob_ref))
    ops = {}
    for d, (q_ref, k_ref, v_ref, c_ref, _) in enumerate(dirs):
        rev = d == 1
        for h in range(hb):
            q = q_ref[:, h * dk:(h + 1) * dk].astype(F32)
            if q_silu:
                q = _silu(q)
            q = q * scale
            k = k_ref[:, h * dk:(h + 1) * dk].astype(F32)
            v = v_ref[:, h * dv:(h + 1) * dv].astype(BF16)
            cb = c_ref[:, h * dk:(h + 1) * dk]
            for c in range(nch):
                rows = slice(c * CHUNK, (c + 1) * CHUNK)
                qc, kc, cbc = q[rows], k[rows], cb[rows]
                end = cbc[0:1, :] if rev else cbc[CHUNK - 1:CHUNK, :]
                ops[d, h, c] = dict(
                    intra=_gla_operands(qc, kc, cbc, rev), v=v[rows],
                    q_state=(qc * jnp.exp(cbc)).astype(BF16),
                    k_state=(kc * jnp.exp(end - cbc)).astype(BF16),
                    s_decay=jnp.exp(end))
    a_off = {key: _dot_nt(op["intra"][0], op["intra"][1]) for key, op in ops.items()}
    a_diag = {key: _dot_nt(op["intra"][2], op["intra"][3]) for key, op in ops.items()}
    masks = [_gla_diag_mask(CHUNK, rev) for rev in (False, True)]
    attn = {key: (a_off[key] + jnp.where(masks[key[0]], a_diag[key], 0.0)).astype(BF16) for key in ops}
    o_intra = {key: _dot(attn[key], op["v"]) for key, op in ops.items()}
    s_inc = {key: _dot_tn(op["v"], op["k_state"]) for key, op in ops.items()}

    state = {key: s_ref[key] for key in streams}
    for step in range(nch):
        for (d, h), s in state.items():
            c = step if d == 0 else nch - 1 - step
            dirs[d][4][c * CHUNK:(c + 1) * CHUNK, h * dv:(h + 1) * dv] = (
                o_intra[d, h, c] + _dot_nt(ops[d, h, c]["q_state"], s.astype(BF16))).astype(BF16)
            state[d, h] = ops[d, h, c]["s_decay"] * s + s_inc[d, h, c]
    for key, s in state.items():
        s_ref[key] = s

    @pl.when(is_ctx)
    def _():
        for d, h in streams:
            ns_ref[0, d, h] = s_ref[d, h].T


def _gla_scan(q_src, q_blk, kf_src, kf_blk, kb_src, kb_blk, v_src, v_blk, cb_f, cb_b, s0, j, tok,
              *, heads, dk, dv, q_silu, hb):
    m = q_src.shape[0]
    tt = tok.seq
    assert heads % hb == 0 and q_blk % hb == 0 and kf_blk % hb == 0 and kb_blk % hb == 0 and v_blk % hb == 0
    bwd = lambda n: _bwd_block(n, tok)
    fwd = lambda n: n

    def stream(blk, k_blk):
        return [
            pl.BlockSpec((tt, hb * dk), lambda h, n: (blk(n), q_blk // hb + h)),
            pl.BlockSpec((tt, hb * dk), lambda h, n: (blk(n), k_blk // hb + h)),
            pl.BlockSpec((tt, hb * dv), lambda h, n: (blk(n), v_blk // hb + h)),
            pl.BlockSpec((tt, hb * dk), lambda h, n: (blk(n), h)),
        ]

    n_batch = tok.n_ctx_blocks
    return pl.pallas_call(
        functools.partial(_gla_scan_kernel, tok=tok, q_silu=q_silu, scale=dk ** -0.5, hb=hb, dk=dk, dv=dv),
        grid=(heads // hb, tok.n_blocks),
        in_specs=stream(fwd, kf_blk) + stream(bwd, kb_blk) + [
            pl.BlockSpec((1, 1, 2, hb, dk, dv), lambda h, n: (_dec_batch(n, tok), j, 0, h, 0, 0)),
        ],
        out_specs=[
            pl.BlockSpec((tt, hb * dv), lambda h, n: (n, h)),
            pl.BlockSpec((tt, hb * dv), lambda h, n: (bwd(n), h)),
            pl.BlockSpec((1, 2, hb, dk, dv), lambda h, n: (_state_block(n, tok), 0, h, 0, 0)),
        ],
        out_shape=[
            jax.ShapeDtypeStruct((m, heads * dv), BF16),
            jax.ShapeDtypeStruct((m, heads * dv), BF16),
            jax.ShapeDtypeStruct((n_batch, 2, heads, dk, dv), F32),
        ],
        scratch_shapes=[pltpu.VMEM((2, hb, dv, dk), F32)],
        compiler_params=_params("parallel", "arbitrary"),
        name="gla_scan",
    )(q_src, kf_src, v_src, cb_f, q_src, kb_src, v_src, cb_b, s0)


def kernel(x_prompt, x_sample, c, state_deltanet, state_gla, state_hgrn, c_ctx, ada_w, ada_b, norm1, norm2, final_norm, w_up, w_down, dn_w_in, dn_conv, dn_A_log, dn_dt_bias, dn_norm, dn_w_out, gla_w_in, gla_w_a2, gla_b_a, gla_norm, gla_w_out, hgrn_w_in, hgrn_lb_logits, hgrn_norm, hgrn_w_out):
    batch, seq, d = x_prompt.shape
    dec_batch, dec_seq, _ = x_sample.shape
    depth = ada_w.shape[0]
    tok = Tokens(m_ctx=batch * seq, seq=seq, dec_batch=dec_batch, dec_seq=dec_seq)
    assert seq % CHUNK == 0 and dec_seq % seq == 0 and seq % GRID_W == 0

    dn_hv, dn_dk, dn_dv = state_deltanet.shape[3:]
    dn_vd = dn_hv * dn_dv
    n_conv = dn_conv.shape[2]
    dn_kd = (n_conv - dn_vd) // 2
    dn_hk = dn_kd // dn_dk
    gla_h, gla_dk, gla_dv = state_gla.shape[3:]
    gla_kd, gla_vd = gla_h * gla_dk, gla_h * gla_dv
    hg_h, hg_dk, hg_dv = state_hgrn.shape[3:]

    x = jnp.concatenate([x_prompt.reshape(tok.m_ctx, d), x_sample.reshape(dec_batch * dec_seq, d)], axis=0)
    n_rows = 1 + dec_batch
    pad = (-n_rows) % 8
    cond = jnp.concatenate([c_ctx[None, :], c, jnp.zeros((pad, d), F32)], axis=0)
    mod_all = _mod_table(cond, ada_w, ada_b).reshape(depth, n_rows + pad, 6, d)

    new_states = ([], [], [])
    dn_states = None
    for i in range(depth):
        kind, j = i % N_MIXERS, i // N_MIXERS
        mod = mod_all[i]
        if kind == 0:
            w_in = dn_w_in[j]
            proj = _proj(x, norm1[i], mod, w_in[:, :n_conv + dn_vd].astype(BF16), tok)
            raw = _proj(x, norm1[i], mod, w_in[:, n_conv + dn_vd:], tok, precise=True)
            qkv = _dn_prep(proj, dn_conv[j], n_conv, dn_kd, dn_dk, tok)
            gates, gates_t = _dn_gates(raw, dn_A_log[j], dn_dt_bias[j], tok)
            o_f, o_b, dn_states = _dn_scan(qkv, gates, gates_t, state_deltanet, j, dn_states, tok,
                                           hk_n=dn_hk, hv=dn_hv, dk=dn_dk, dv=dn_dv)
            x = _out_proj(o_f, o_b, proj, n_conv // dn_vd, dn_norm[j], x, mod, dn_w_out[j].astype(BF16), tok)
        elif kind == 1:
            w_in = gla_w_in[j]
            n_main = 2 * gla_kd + 2 * gla_vd
            proj = _proj(x, norm1[i], mod, w_in[:, :n_main].astype(BF16), tok)
            a1 = _proj(x, norm1[i], mod, w_in[:, n_main:], tok, precise=True)
            cb_f, cb_b = _gla_prep(a1, gla_w_a2[j], gla_b_a[j], tok)
            o_f, o_b, ns = _gla_scan(proj, 0, proj, gla_h, proj, gla_h, proj, (2 * gla_kd) // gla_dv,
                                     cb_f, cb_b, state_gla, j, tok,
                                     heads=gla_h, dk=gla_dk, dv=gla_dv, q_silu=False, hb=min(gla_h, 2))
            x = _out_proj(o_f, o_b, proj, (2 * gla_kd + gla_vd) // gla_vd, gla_norm[j], x, mod,
                          gla_w_out[j].astype(BF16), tok)
        else:
            proj = _proj(x, norm1[i], mod, hgrn_w_in[j].astype(BF16), tok)
            k_f, k_b, cb_f, cb_b = _hgrn_prep(proj, hgrn_lb_logits, i, d, tok)
            o_f, o_b, ns = _gla_scan(proj, 0, k_f, 0, k_b, 0, proj, d // hg_dv, cb_f, cb_b, state_hgrn, j, tok,
                                     heads=hg_h, dk=hg_dk, dv=hg_dv, q_silu=True, hb=min(hg_h, 8))
            x = _out_proj(o_f, o_b, proj, 2, hgrn_norm[j], x, mod, hgrn_w_out[j].astype(BF16), tok)
        if kind != 0:
            new_states[kind].append(ns)
        x = _mlp(x, norm2[i], mod, w_up[i].astype(BF16), w_down[i].astype(BF16), final_norm, tok,
                 final=(i == depth - 1))

    y_ctx, y_dec = x
    y_prompt = y_ctx.reshape(batch, seq, d)
    y_sample = y_dec.reshape(dec_batch, dec_seq, d)
    stacked = [s[0][:, None] if len(s) == 1 else jnp.stack(s, axis=1) for s in new_states[1:]]
    return (y_prompt, y_sample, dn_states) + tuple(stacked)
```

```python
import functools
from typing import NamedTuple

import jax
import jax.numpy as jnp
from jax import lax
from jax.experimental import pallas as pl
from jax.experimental.pallas import tpu as pltpu

F32 = jnp.float32
BF16 = jnp.bfloat16

EPS = 1e-6
GRID_W = 64
CONV_K = 5
GLA_RANK = 16
GLA_TAU = 16.0
N_MIXERS = 3
CHUNK = 64
SUB = 16
EXP_CLAMP = 80.0
LANES = 128
V7X_MXU_COLS = 256
V7X_VMEM_LIMIT_BYTES = 56 * 1024 * 1024
HIGHEST = lax.Precision.HIGHEST


class Tokens(NamedTuple):
    m_ctx: int
    seq: int
    dec_batch: int
    dec_seq: int

    @property
    def m(self):
        return self.m_ctx + self.dec_batch * self.dec_seq

    @property
    def n_ctx_blocks(self):
        return self.m_ctx // self.seq

    @property
    def blocks_per_dec(self):
        return self.dec_seq // self.seq

    @property
    def n_blocks(self):
        return self.m // self.seq


def _tile(n, pref):
    t = min(n, pref)
    while n % t:
        t //= 2
    return t


def _token_tile(tok, pref):
    t = pref
    while tok.m_ctx % t or tok.dec_seq % t:
        t //= 2
    return t


def _params(*sem):
    return pltpu.CompilerParams(dimension_semantics=sem, vmem_limit_bytes=V7X_VMEM_LIMIT_BYTES)


def _mod_row(i, tm, tok):
    start = i * tm
    return jnp.where(start < tok.m_ctx, 0, 1 + (start - tok.m_ctx) // tok.dec_seq)


def _sigmoid(x):
    return 0.5 + 0.5 * jnp.tanh(0.5 * x)


def _silu(x):
    half = 0.5 * x
    return half + half * jnp.tanh(half)


def _softplus(x):
    return jnp.maximum(x, 0.0) + jnp.log(1.0 + jnp.exp(-jnp.abs(x)))


def _norm_mod(x, nw, shift, scale):
    ms = jnp.mean(x * x, axis=-1, keepdims=True)
    return x * lax.rsqrt(ms + EPS) * nw * (1.0 + scale) + shift


def _dot(a, b):
    return jnp.dot(a, b, preferred_element_type=F32)


def _dot_nt(a, b):
    return lax.dot_general(a, b, (((1,), (1,)), ((), ())), preferred_element_type=F32)


def _dot_tn(a, b):
    return lax.dot_general(a, b, (((0,), (0,)), ((), ())), preferred_element_type=F32)


def _split_dot(tri, x):
    hi = x.astype(BF16)
    lo = (x - hi.astype(F32)).astype(BF16)
    return _dot(tri, hi) + _dot(tri, lo)


def _split_matmul(a, b):
    ah = a.astype(BF16)
    al = (a - ah.astype(F32)).astype(BF16)
    bh = b.astype(BF16)
    bl = (b - bh.astype(F32)).astype(BF16)
    return _dot(ah, bh) + _dot(al, bh) + _dot(ah, bl)


def _tri(n, upper):
    r = lax.broadcasted_iota(jnp.int32, (n, n), 0)
    c = lax.broadcasted_iota(jnp.int32, (n, n), 1)
    return jnp.where((r <= c) if upper else (r >= c), 1.0, 0.0).astype(BF16)


def _mod_kernel(c_ref, w_ref, b_ref, o_ref):
    c = c_ref[...]
    s = _silu(c)
    o_ref[0] = jnp.dot(s, w_ref[0], precision=HIGHEST, preferred_element_type=F32) + b_ref[0]


def _mod_table(cond, ada_w, ada_b):
    depth, d, n = ada_w.shape
    rows = cond.shape[0]
    tn = _tile(n, 1024)
    return pl.pallas_call(
        _mod_kernel,
        grid=(depth, n // tn),
        in_specs=[
            pl.BlockSpec((rows, d), lambda l, j: (0, 0)),
            pl.BlockSpec((1, d, tn), lambda l, j: (l, 0, j)),
            pl.BlockSpec((1, 1, tn), lambda l, j: (l, 0, j)),
        ],
        out_specs=pl.BlockSpec((1, rows, tn), lambda l, j: (l, 0, j)),
        out_shape=jax.ShapeDtypeStruct((depth, rows, n), F32),
        compiler_params=_params("parallel", "parallel"),
        name="mod_table",
    )(cond, ada_w, ada_b.reshape(depth, 1, n))


def _proj_kernel(x_ref, nw_ref, mod_ref, w_ref, o_ref, h_ref, *, precise):
    @pl.when(pl.program_id(1) == 0)
    def _():
        h = _norm_mod(x_ref[...], nw_ref[...], mod_ref[0, 0:1, :], mod_ref[0, 1:2, :])
        h_ref[...] = h.astype(h_ref.dtype)

    if precise:
        o_ref[...] = _split_matmul(h_ref[...], w_ref[...])
    else:
        o_ref[...] = _dot(h_ref[...], w_ref[...]).astype(o_ref.dtype)


def _proj(x, nw, mod, w, tok, *, precise=False, tm=1024, tn=2048):
    m, d = x.shape
    n = w.shape[1]
    tm = _token_tile(tok, tm)
    tn = _tile(n, tn)
    return pl.pallas_call(
        functools.partial(_proj_kernel, precise=precise),
        grid=(m // tm, n // tn),
        in_specs=[
            pl.BlockSpec((tm, d), lambda i, j: (i, 0)),
            pl.BlockSpec((1, d), lambda i, j: (0, 0)),
            pl.BlockSpec((1, 6, d), lambda i, j: (_mod_row(i, tm, tok), 0, 0)),
            pl.BlockSpec((d, tn), lambda i, j: (0, j)),
        ],
        out_specs=pl.BlockSpec((tm, tn), lambda i, j: (i, j)),
        out_shape=jax.ShapeDtypeStruct((m, n), F32 if precise else BF16),
        scratch_shapes=[pltpu.VMEM((tm, d), F32 if precise else BF16)],
        compiler_params=_params("parallel", "arbitrary"),
        name="proj_precise" if precise else "proj",
    )(x, nw.reshape(1, d), mod, w)


def _mlp_kernel(x_ref, nw_ref, mod_ref, wu_ref, wd_ref, fw_ref, *rest, n_ctx_tiles):
    out_refs, (h_ref, acc_ref) = rest[:-2], rest[-2:]
    final = n_ctx_tiles is not None
    i = pl.program_id(0)
    f = pl.program_id(1)

    @pl.when(f == 0)
    def _():
        h = _norm_mod(x_ref[...], nw_ref[...], mod_ref[0, 3:4, :], mod_ref[0, 4:5, :])
        h_ref[...] = h.astype(BF16)
        acc_ref[...] = jnp.zeros_like(acc_ref)

    a = jnp.maximum(_dot(h_ref[...], wu_ref[...]), 0.0)
    acc_ref[...] += _dot((a * a).astype(BF16), wd_ref[...])

    last = f == pl.num_programs(1) - 1

    def result():
        y = x_ref[...] + mod_ref[0, 5:6, :] * acc_ref[...]
        if final:
            ms = jnp.mean(y * y, axis=-1, keepdims=True)
            y = y * lax.rsqrt(ms + EPS) * fw_ref[...]
        return y

    if not final:
        @pl.when(last)
        def _():
            out_refs[0][...] = result()
    else:
        @pl.when(jnp.logical_and(last, i < n_ctx_tiles))
        def _():
            out_refs[0][...] = result()

        @pl.when(jnp.logical_and(last, i >= n_ctx_tiles))
        def _():
            out_refs[1][...] = result()


def _mlp(x, nw, mod, w_up, w_down, final_w, tok, *, final):
    m, d = x.shape
    ff = w_up.shape[1]
    tm, tf = (512, 1024) if final else (1024, 512)
    tm = _token_tile(tok, tm)
    tf = _tile(ff, tf)
    n_ctx_tiles = tok.m_ctx // tm
    once = dict(pipeline_mode=pl.Buffered(1))
    if final:
        out_specs = [pl.BlockSpec((tm, d), lambda i, f: (jnp.minimum(i, n_ctx_tiles - 1), 0), **once),
                     pl.BlockSpec((tm, d), lambda i, f: (jnp.maximum(i - n_ctx_tiles, 0), 0), **once)]
        out_shape = [jax.ShapeDtypeStruct((tok.m_ctx, d), F32), jax.ShapeDtypeStruct((m - tok.m_ctx, d), F32)]
    else:
        out_specs = pl.BlockSpec((tm, d), lambda i, f: (i, 0), **once)
        out_shape = jax.ShapeDtypeStruct((m, d), F32)
    return pl.pallas_call(
        functools.partial(_mlp_kernel, n_ctx_tiles=n_ctx_tiles if final else None),
        grid=(m // tm, ff // tf),
        in_specs=[
            pl.BlockSpec((tm, d), lambda i, f: (i, 0)),
            pl.BlockSpec((1, d), lambda i, f: (0, 0)),
            pl.BlockSpec((1, 6, d), lambda i, f: (_mod_row(i, tm, tok), 0, 0)),
            pl.BlockSpec((d, tf), lambda i, f: (0, f)),
            pl.BlockSpec((tf, d), lambda i, f: (f, 0)),
            pl.BlockSpec((1, d), lambda i, f: (0, 0)),
        ],
        out_specs=out_specs,
        out_shape=out_shape,
        scratch_shapes=[pltpu.VMEM((tm, d), BF16), pltpu.VMEM((tm, d), F32)],
        compiler_params=_params("arbitrary" if final else "parallel", "arbitrary"),
        name="mlp_final" if final else "mlp",
    )(x, nw.reshape(1, d), mod, w_up, w_down, final_w.reshape(1, d))


def _out_kernel(of_ref, ob_ref, g_ref, nw_ref, x_ref, mod_ref, w_ref, o_ref, y0_ref, y1_ref, *, n_heads, dh):
    i = pl.program_id(0)
    d = w_ref.shape[1]
    n_chunks = d // V7X_MXU_COLS
    heads_per_chunk = -(-n_heads // n_chunks)

    @pl.when(i == 0)
    def _():
        y1_ref[...] = jnp.zeros_like(y1_ref)

    def body(y_write, y_read):
        ones = jnp.full((dh, LANES), 1.0 / dh, BF16) if dh & (dh - 1) == 0 else None
        for c in range(n_chunks):
            cols = slice(c * V7X_MXU_COLS, (c + 1) * V7X_MXU_COLS)
            o_ref[:, cols] = x_ref[:, cols] + mod_ref[0, 2:3, cols] * _dot(y_read[...], w_ref[:, cols])
            for h in range(c * heads_per_chunk, min((c + 1) * heads_per_chunk, n_heads)):
                sl = slice(h * dh, (h + 1) * dh)
                o = of_ref[:, sl].astype(F32) + ob_ref[:, sl].astype(F32)
                if ones is None:
                    ms = _dot((o * o).astype(BF16), jnp.ones((dh, LANES), BF16)) * (1.0 / dh)
                else:
                    ms = _dot((o * o).astype(BF16), ones)
                inv = lax.rsqrt(ms + EPS)
                y = o * nw_ref[...] * _silu(g_ref[:, sl].astype(F32))
                for k in range(dh // LANES):
                    lo = h * dh + k * LANES
                    y_write[:, lo:lo + LANES] = (y[:, k * LANES:(k + 1) * LANES] * inv).astype(BF16)

    @pl.when(i % 2 == 0)
    def _():
        body(y0_ref, y1_ref)

    @pl.when(i % 2 == 1)
    def _():
        body(y1_ref, y0_ref)


def _out_proj(o_f, o_b, gate_src, gate_blk, norm_w, x, mod, w_out, tok, *, tm=256):
    m, vd = o_f.shape
    d = x.shape[1]
    dh = norm_w.shape[0]
    tm = _token_tile(tok, tm)
    n_tiles = m // tm
    assert d % V7X_MXU_COLS == 0 and dh % LANES == 0
    cur = lambda i: jnp.minimum(i, n_tiles - 1)
    prev = lambda i: jnp.maximum(i - 1, 0)
    return pl.pallas_call(
        functools.partial(_out_kernel, n_heads=vd // dh, dh=dh),
        grid=(n_tiles + 1,),
        in_specs=[
            pl.BlockSpec((tm, vd), lambda i: (cur(i), 0)),
            pl.BlockSpec((tm, vd), lambda i: (cur(i), 0)),
            pl.BlockSpec((tm, vd), lambda i: (cur(i), gate_blk)),
            pl.BlockSpec((1, dh), lambda i: (0, 0)),
            pl.BlockSpec((tm, d), lambda i: (prev(i), 0)),
            pl.BlockSpec((1, 6, d), lambda i: (_mod_row(prev(i), tm, tok), 0, 0)),
            pl.BlockSpec((vd, d), lambda i: (0, 0), pipeline_mode=pl.Buffered(1)),
        ],
        out_specs=pl.BlockSpec((tm, d), lambda i: (prev(i), 0)),
        out_shape=jax.ShapeDtypeStruct((m, d), F32),
        scratch_shapes=[pltpu.VMEM((tm, vd), BF16), pltpu.VMEM((tm, vd), BF16)],
        compiler_params=_params("arbitrary"),
        name="out_proj",
    )(o_f, o_b, gate_src, norm_w.reshape(1, dh), x, mod, w_out)


def _bwd_block(n, tok):
    nc, per = tok.n_ctx_blocks, tok.blocks_per_dec
    r = jnp.maximum(n - nc, 0)
    return jnp.where(n < nc, n, nc + (r // per) * per + (per - 1 - r % per))


def _dec_batch(n, tok):
    return jnp.maximum(n - tok.n_ctx_blocks, 0) // tok.blocks_per_dec


def _state_block(n, tok):
    return jnp.minimum(n, tok.n_ctx_blocks - 1)


def _seq_flags(n, tok):
    nc, per = tok.n_ctx_blocks, tok.blocks_per_dec
    is_ctx = n < nc
    r = jnp.maximum(n - nc, 0) % per
    return is_ctx, jnp.logical_and(jnp.logical_not(is_ctx), r == 0)


def _dn_prep_kernel(x_ref, shift_ref, w_ref, o_ref, *, n_q_blocks, dk):
    j = pl.program_id(1)
    x = x_ref[...]
    tc = x.shape[1]
    acc = None
    for s in range(CONV_K):
        xs = x.astype(F32) if s == CONV_K // 2 else _dot(shift_ref[0, s], x)
        term = xs * w_ref[s:s + 1, :]
        acc = term if acc is None else acc + term
    y = _silu(acc)

    def normed(scale):
        ones = jnp.ones((dk, dk), BF16)
        for g in range(tc // dk):
            blk = y[:, g * dk:(g + 1) * dk]
            ss = _dot((blk * blk).astype(BF16), ones)
            o_ref[:, g * dk:(g + 1) * dk] = (blk * (lax.rsqrt(ss + EPS) * scale)).astype(o_ref.dtype)

    @pl.when(j < n_q_blocks)
    def _():
        normed(dk ** -0.5)

    @pl.when(jnp.logical_and(j >= n_q_blocks, j < 2 * n_q_blocks))
    def _():
        normed(1.0)

    @pl.when(j >= 2 * n_q_blocks)
    def _():
        o_ref[...] = y.astype(o_ref.dtype)


def _conv_shift_matrices(tt):
    t = jnp.arange(tt)[:, None]
    u = jnp.arange(tt)[None, :]
    mats = []
    for seg in (tt, GRID_W):
        same = (t // seg) == (u // seg)
        mats.append(jnp.stack([jnp.logical_and(u == t + s - CONV_K // 2, same) for s in range(CONV_K)]))
    return jnp.stack(mats).astype(BF16)


def _dn_prep(proj, conv_w, n_conv, kd, dk, tok):
    m = proj.shape[0]
    tt = tok.seq
    tc = _tile(kd, 2048)
    n_ctx = tok.n_ctx_blocks
    return pl.pallas_call(
        functools.partial(_dn_prep_kernel, n_q_blocks=kd // tc, dk=dk),
        grid=(m // tt, n_conv // tc),
        in_specs=[
            pl.BlockSpec((tt, tc), lambda i, j: (i, j)),
            pl.BlockSpec((1, CONV_K, tt, tt), lambda i, j: (jnp.where(i < n_ctx, 0, 1), 0, 0, 0)),
            pl.BlockSpec((CONV_K, tc), lambda i, j: (0, j)),
        ],
        out_specs=pl.BlockSpec((tt, tc), lambda i, j: (i, j)),
        out_shape=jax.ShapeDtypeStruct((m, n_conv), BF16),
        compiler_params=_params("parallel", "parallel"),
        name="dn_prep",
    )(proj, _conv_shift_matrices(tt), conv_w)


def _dn_gate_kernel(x_ref, a_ref, bias_ref, o_ref, gt_ref, *, hv, tt):
    x = x_ref[...]
    lanes = x.shape[1]
    lane = lax.broadcasted_iota(jnp.int32, (1, lanes), 1)
    is_fwd = lane < 3 * hv
    g = -a_ref[...] * _softplus(x + bias_ref[...])
    beta = pltpu.roll(_sigmoid(x), 2 * hv, 1)
    lower = _tri(CHUNK, upper=False)
    upper = _tri(CHUNK, upper=True)
    for c in range(tt // CHUNK):
        rows = slice(c * CHUNK, (c + 1) * CHUNK)
        gch = g[rows]
        gc = jnp.where(is_fwd, _split_dot(lower, gch), _split_dot(upper, gch))
        g_end = jnp.where(is_fwd, gc[CHUNK - 1:CHUNK, :], gc[0:1, :])
        eg = jnp.exp(gc)
        o_ref[0, rows, :] = beta[rows]
        o_ref[1, rows, :] = gc
        o_ref[2, rows, :] = eg
        o_ref[3, rows, :] = jnp.exp(g_end - gc)
        o_ref[4, rows, :] = beta[rows] * eg
    gt_ref[...] = o_ref[1].T


def _dn_gates(raw, a_log, dt_bias, tok):
    m, lanes = raw.shape
    hv = lanes // 4
    tt = tok.seq
    zeros = jnp.zeros((2 * hv,), F32)
    a_full = jnp.concatenate([zeros, jnp.exp(a_log).reshape(-1)]).reshape(1, lanes)
    bias_full = jnp.concatenate([zeros, dt_bias.reshape(-1)]).reshape(1, lanes)
    return pl.pallas_call(
        functools.partial(_dn_gate_kernel, hv=hv, tt=tt),
        grid=(m // tt,),
        in_specs=[
            pl.BlockSpec((tt, lanes), lambda i: (i, 0)),
            pl.BlockSpec((1, lanes), lambda i: (0, 0)),
            pl.BlockSpec((1, lanes), lambda i: (0, 0)),
        ],
        out_specs=[
            pl.BlockSpec((5, tt, lanes), lambda i: (0, i, 0)),
            pl.BlockSpec((lanes, tt), lambda i: (0, i)),
        ],
        out_shape=[jax.ShapeDtypeStruct((5, m, lanes), F32), jax.ShapeDtypeStruct((lanes, m), F32)],
        compiler_params=_params("parallel"),
        name="dn_gates",
    )(raw, a_full, bias_full)


def _dn_scan_kernel(qf_ref, kf_ref, vf_ref, gf_ref, gtf_ref, qb_ref, kb_ref, vb_ref, gb_ref, gtb_ref, s0_ref,
                    of_ref, ob_ref, ns_ref, s_ref, u0_ref, wq_ref, attn_ref, kd_ref, ge_ref, *, tok, hv, dv, hkb):
    hk0 = pl.program_id(0) * hkb
    n = pl.program_id(1)
    tt = tok.seq
    nch = tt // CHUNK
    dk = qf_ref.shape[1] // hkb
    assert dv == 2 * CHUNK and CHUNK == 4 * SUB
    has_prev = n > 0
    is_ctx, dec_start = _seq_flags(jnp.maximum(n - 1, 0), tok)
    wslot = n % 2
    rslot = 1 - wslot
    scratch = (u0_ref, wq_ref, attn_ref, kd_ref, ge_ref)

    @pl.when(n == 0)
    def _():
        s_ref[...] = jnp.zeros_like(s_ref)
        for ref in scratch:
            ref[1] = jnp.zeros(ref.shape[1:], ref.dtype)

    @pl.when(jnp.logical_and(has_prev, is_ctx))
    def _():
        s_ref[...] = jnp.zeros_like(s_ref)

    @pl.when(jnp.logical_and(has_prev, dec_start))
    def _():
        s_ref[...] = s0_ref[0, 0]

    bf = lambda x: x.astype(BF16)
    lanes = gf_ref.shape[2]
    lane_iota = lax.broadcasted_iota(jnp.int32, (1, lanes), 1)
    ri = lax.broadcasted_iota(jnp.int32, (CHUNK, 2 * CHUNK), 0)
    li = lax.broadcasted_iota(jnp.int32, (CHUNK, 2 * CHUNK), 1)
    left = li < CHUNK
    ci = li % CHUNK
    same = [(ri // b) == (ci // b) for b in (SUB, 2 * SUB)]
    zero_b = jnp.zeros((CHUNK, 2 * CHUNK), BF16)

    def block_diag(xb):
        return jnp.concatenate([jnp.where(left, xb, zero_b), jnp.where(left, zero_b, xb)], axis=0)

    def pair_dot(a, b):
        return _dot(bf(a), block_diag(bf(b)))

    dirs = ((qf_ref, kf_ref, vf_ref, gf_ref, gtf_ref, of_ref), (qb_ref, kb_ref, vb_ref, gb_ref, gtb_ref, ob_ref))

    def prepare():
        ch = {}
        for d, kh in [(d, kh) for d in range(2) for kh in range(hkb)]:
            q_ref, k_ref, v_ref, g_ref, gt_ref, _ = dirs[d]
            incl = (ri >= ci) if d == 0 else (ri <= ci)
            strict = (ri > ci) if d == 0 else (ri < ci)
            end_row = CHUNK - 1 if d == 0 else 0
            q = q_ref[:, kh * dk:(kh + 1) * dk]
            k = k_ref[:, kh * dk:(kh + 1) * dk]
            kf32 = k.astype(F32)
            qf32 = q.astype(F32)
            cols, g_rows, vs = [], [], []
            for e in range(2):
                a_lane = 2 * hv + d * hv + 2 * (hk0 + kh) + e
                sel = lane_iota == a_lane
                cols.append([jnp.sum(jnp.where(sel, g_ref[i], 0.0), axis=1, keepdims=True) for i in range(5)])
                g_rows.append(gt_ref[pl.ds(a_lane, 1), :])
                vs.append(v_ref[:, (2 * kh + e) * dv:(2 * kh + e + 1) * dv].astype(F32))
            for c in range(nch):
                rows = slice(c * CHUNK, (c + 1) * CHUNK)
                pair = lambda i: jnp.where(left, cols[0][i][rows], cols[1][i][rows])
                k2 = jnp.concatenate([k[rows], k[rows]], axis=0)
                kk = _dot_nt(k[rows], k2)
                qk = _dot_nt(q[rows], k2)
                g_row = jnp.concatenate([g_rows[0][:, rows], g_rows[1][:, rows]], axis=1)
                diff = pair(1) - g_row
                decay = jnp.where(incl, jnp.exp(jnp.where(incl, diff, 0.0)), 0.0)
                p = jnp.where(strict, -(pair(0) * kk * decay), 0.0)
                attn_ref[wslot, d, kh, c] = bf(qk * decay)
                for e in range(2):
                    beta_c, _, eg_c, ekd_c, beg_c = cols[e]
                    qg = qf32[rows] * eg_c[rows]
                    kd_ref[wslot, d, 2 * kh + e, c] = bf(kf32[rows] * ekd_c[rows])
                    wq_ref[wslot, d, 2 * kh + e, c, CHUNK:, :] = bf(qg)
                    g_end = eg_c[c * CHUNK + end_row:c * CHUNK + end_row + 1, :]
                    ge_ref[wslot, d, 2 * kh + e, c] = jnp.broadcast_to(g_end, ge_ref.shape[4:])
                ch[kh, d, c] = dict(
                    pd=jnp.where(same[0], p, 0.0),
                    p32=jnp.where(jnp.logical_and(same[1], jnp.logical_not(same[0])), p, 0.0),
                    p64=jnp.where(same[1], 0.0, p),
                    r=[jnp.concatenate([cols[e][0][rows] * vs[e][rows], cols[e][4][rows] * kf32[rows]], axis=1)
                       for e in range(2)])
            yield
        for c in ch.values():
            c["q"] = pair_dot(c["pd"], c["pd"])
            c["xm"] = c["pd"]
        yield
        for it in range(2):
            for c in ch.values():
                q, xm = c["q"], c["xm"]
                both = _dot(bf(q), jnp.concatenate([block_diag(bf(xm)), block_diag(bf(q))], axis=1))
                c["xm"] = xm + q + both[:, :2 * CHUNK]
                c["q"] = both[:, 2 * CHUNK:]
            yield
        for c in ch.values():
            q, xm = c.pop("q"), c["xm"]
            c["xm"] = xm + q + pair_dot(q, xm)
        yield
        for level in ("p32", "p64"):
            for c in ch.values():
                c["y"] = c[level] + pair_dot(c[level], c["xm"])
            yield
            for c in ch.values():
                xm, y = c["xm"], c.pop("y")
                c["xm"] = xm + y + pair_dot(xm, y)
            yield
        zero_r = jnp.zeros((CHUNK, 2 * dv), BF16)
        for (kh, d, cidx), c in ch.items():
            r0, r1 = c["r"]
            rhs = jnp.concatenate([jnp.concatenate([bf(r0), zero_r], axis=1),
                                   jnp.concatenate([zero_r, bf(r1)], axis=1)], axis=0)
            big = _dot(bf(c["xm"]), rhs)
            sol = [r0 + big[:, :2 * dv], r1 + big[:, 2 * dv:]]
            u0_ref[wslot, d, kh, cidx] = jnp.concatenate([sol[0][:, :dv], sol[1][:, :dv]], axis=1)
            for e in range(2):
                wq_ref[wslot, d, 2 * kh + e, cidx, :CHUNK, :] = bf(sol[e][:, dv:])
        yield

    def recur():
        state = {(d, e): s_ref[d, e] for d in range(2) for e in range(2 * hkb)}
        zero_u = jnp.zeros((CHUNK, dv), BF16)
        for step in range(nch):
            chunk = lambda d: step if d == 0 else nch - 1 - step
            ws = {(d, e): _dot(wq_ref[rslot, d, e, chunk(d)], bf(s)) for (d, e), s in state.items()}
            yield
            for d, kh in [(d, kh) for d in range(2) for kh in range(hkb)]:
                c = chunk(d)
                u0 = u0_ref[rslot, d, kh, c]
                ev = [2 * kh, 2 * kh + 1]
                ub = [bf(u0[:, e * dv:(e + 1) * dv] - ws[d, ev[e]][:CHUNK]) for e in range(2)]
                u2 = jnp.concatenate([jnp.concatenate([ub[0], zero_u], axis=1),
                                      jnp.concatenate([zero_u, ub[1]], axis=1)], axis=0)
                o = (_dot(attn_ref[rslot, d, kh, c], u2)
                     + jnp.concatenate([ws[d, ev[0]][CHUNK:], ws[d, ev[1]][CHUNK:]], axis=1))
                dirs[d][5][c * CHUNK:(c + 1) * CHUNK, 2 * kh * dv:2 * (kh + 1) * dv] = bf(o)
                for e in range(2):
                    state[d, ev[e]] = (ge_ref[rslot, d, ev[e], c][0:1, :] * state[d, ev[e]]
                                       + _dot_tn(kd_ref[rslot, d, ev[e], c], ub[e]))
            yield
        for (d, e), s in state.items():
            s_ref[d, e] = s
        yield

    halves = [recur(), prepare()]
    while halves:
        for g in list(halves):
            if next(g, "done") == "done":
                halves.remove(g)

    @pl.when(jnp.logical_and(has_prev, is_ctx))
    def _():
        ns_ref[0, 0] = s_ref[...]


def _dn_scan_kernel_aliased(*refs, **kw):
    n_in = 11
    return _dn_scan_kernel(*refs[:n_in], *refs[n_in + 1:], **kw)


def _dn_scan(qkv, gates, gates_t, s0, j, new_states, tok, *, hk_n, hv, dk, dv):
    m = qkv.shape[0]
    tt = tok.seq
    rep = hv // hk_n
    assert rep == 2
    lanes = gates.shape[2]
    kd = hk_n * dk
    nb = tok.n_blocks
    nch = tt // CHUNK
    fwd = lambda n: jnp.minimum(n, nb - 1)
    bwd = lambda n: _bwd_block(jnp.minimum(n, nb - 1), tok)
    prev = lambda n: jnp.maximum(n - 1, 0)

    hkb = max(b for b in (4, 2, 1) if hk_n % b == 0)
    hvb = hkb * rep

    def stream(blk):
        return [
            pl.BlockSpec((tt, hkb * dk), lambda h, n: (blk(n), h)),
            pl.BlockSpec((tt, hkb * dk), lambda h, n: (blk(n), hk_n // hkb + h)),
            pl.BlockSpec((tt, hvb * dv), lambda h, n: (blk(n), (2 * kd) // (hvb * dv) + h)),
            pl.BlockSpec((5, tt, lanes), lambda h, n: (0, blk(n), 0)),
            pl.BlockSpec((lanes, tt), lambda h, n: (0, blk(n))),
        ]

    n_batch = tok.n_ctx_blocks
    n_layers = s0.shape[1]
    aliased = new_states is not None
    extra_in = [pl.BlockSpec(memory_space=pl.ANY)] if aliased else []
    extra_args = (new_states,) if aliased else ()
    return pl.pallas_call(
        functools.partial(_dn_scan_kernel_aliased if aliased else _dn_scan_kernel, tok=tok, hv=hv, dv=dv, hkb=hkb),
        grid=(hk_n // hkb, nb + 1),
        in_specs=stream(fwd) + stream(bwd) + [
            pl.BlockSpec((1, 1, 2, hvb, dk, dv), lambda h, n: (_dec_batch(prev(n), tok), j, 0, h, 0, 0)),
        ] + extra_in,
        out_specs=[
            pl.BlockSpec((tt, hvb * dv), lambda h, n: (prev(n), h)),
            pl.BlockSpec((tt, hvb * dv), lambda h, n: (_bwd_block(prev(n), tok), h)),
            pl.BlockSpec((1, 1, 2, hvb, dk, dv), lambda h, n: (_state_block(prev(n), tok), j, 0, h, 0, 0)),
        ],
        out_shape=[
            jax.ShapeDtypeStruct((m, hv * dv), BF16),
            jax.ShapeDtypeStruct((m, hv * dv), BF16),
            jax.ShapeDtypeStruct((n_batch, n_layers, 2, hv, dk, dv), F32),
        ],
        input_output_aliases={11: 2} if aliased else {},
        scratch_shapes=[
            pltpu.VMEM((2, hvb, dk, dv), F32),
            pltpu.VMEM((2, 2, hkb, nch, CHUNK, rep * dv), F32),
            pltpu.VMEM((2, 2, hvb, nch, 2 * CHUNK, dk), BF16),
            pltpu.VMEM((2, 2, hkb, nch, CHUNK, rep * CHUNK), BF16),
            pltpu.VMEM((2, 2, hvb, nch, CHUNK, dk), BF16),
            pltpu.VMEM((2, 2, hvb, nch, 8, dv), F32),
        ],
        compiler_params=_params("parallel", "arbitrary"),
        name="dn_scan",
    )(qkv, qkv, qkv, gates, gates_t, qkv, qkv, qkv, gates, gates_t, s0, *extra_args)


def _chunk_cums(la_f, la_b, cbf_ref, cbb_ref, tt):
    lower = _tri(CHUNK, upper=False)
    upper = _tri(CHUNK, upper=True)
    for c in range(tt // CHUNK):
        rows = slice(c * CHUNK, (c + 1) * CHUNK)
        cbf_ref[rows, :] = _split_dot(lower, la_f[rows])
        cbb_ref[rows, :] = _split_dot(upper, la_b[rows])


def _log_sigmoid(x):
    return jnp.minimum(x, 0.0) - jnp.log(1.0 + jnp.exp(-jnp.abs(x)))


def _gla_prep_kernel(a1_ref, w2_ref, b_ref, cbf_ref, cbb_ref, *, tt):
    a1 = a1_ref[...]
    la = [_log_sigmoid(_dot(a1, w2_ref[d]) + b_ref[d]) * (1.0 / GLA_TAU) for d in range(2)]
    _chunk_cums(la[0], la[1], cbf_ref, cbb_ref, tt)


def _gla_prep(a1, w_a2, b_a, tok):
    m, r2 = a1.shape
    kd = w_a2.shape[2]
    tt = tok.seq
    tc = _tile(kd, 1024)
    zeros = jnp.zeros((GLA_RANK, kd), F32)
    w2 = jnp.stack([jnp.concatenate([w_a2[0], zeros]), jnp.concatenate([zeros, w_a2[1]])])
    out = jax.ShapeDtypeStruct((m, kd), F32)
    return pl.pallas_call(
        functools.partial(_gla_prep_kernel, tt=tt),
        grid=(m // tt, kd // tc),
        in_specs=[
            pl.BlockSpec((tt, r2), lambda i, j: (i, 0)),
            pl.BlockSpec((2, r2, tc), lambda i, j: (0, 0, j)),
            pl.BlockSpec((2, 1, tc), lambda i, j: (0, 0, j)),
        ],
        out_specs=[pl.BlockSpec((tt, tc), lambda i, j: (i, j))] * 2,
        out_shape=[out, out],
        compiler_params=_params("parallel", "parallel"),
        name="gla_prep",
    )(a1, w2, b_a.reshape(2, 1, kd))


def _hgrn_prep_kernel(ff_ref, fb_ref, lb_ref, kf_ref, kb_ref, cbf_ref, cbb_ref, *, layer, tt):
    logits = lb_ref[...]
    ex = jnp.exp(logits - jnp.max(logits, axis=0, keepdims=True))
    probs = ex / jnp.sum(ex, axis=0, keepdims=True)
    lb = jnp.sum(probs[1:layer + 1], axis=0, keepdims=True)
    la = []
    for f_ref, k_ref in ((ff_ref, kf_ref), (fb_ref, kb_ref)):
        fl = f_ref[...].astype(F32)
        sg = _sigmoid(fl)
        la.append(jnp.log(lb + (1.0 - lb) * sg))
        k_ref[...] = (1.0 - lb) * (1.0 - sg)
    _chunk_cums(la[0], la[1], cbf_ref, cbb_ref, tt)


def _hgrn_prep(proj, lb_logits, layer, d, tok):
    m = proj.shape[0]
    depth = lb_logits.shape[0]
    tt = tok.seq
    tc = _tile(d, 2048)
    nb = d // tc
    out = jax.ShapeDtypeStruct((m, d), F32)
    return pl.pallas_call(
        functools.partial(_hgrn_prep_kernel, layer=layer, tt=tt),
        grid=(m // tt, nb),
        in_specs=[
            pl.BlockSpec((tt, tc), lambda i, j: (i, 3 * nb + j)),
            pl.BlockSpec((tt, tc), lambda i, j: (i, 4 * nb + j)),
            pl.BlockSpec((depth, tc), lambda i, j: (0, j)),
        ],
        out_specs=[pl.BlockSpec((tt, tc), lambda i, j: (i, j))] * 4,
        out_shape=[out] * 4,
        compiler_params=_params("parallel", "parallel"),
        name="hgrn_prep",
    )(proj, proj, lb_logits)


def _gla_offdiag_pairs(c):
    pairs = []
    size = c
    while size > SUB:
        half = size // 2
        for start in range(0, c, size):
            pairs.append(((start + half, start + size), (start, start + half)))
        size = half
    return pairs


def _gla_operands(q, k, cb, rev):
    c, dk = q.shape
    zero_rows = lambda n: jnp.zeros((n, dk), BF16)

    def padded(x, rows):
        parts = ([zero_rows(rows[0])] if rows[0] else []) + [x] + ([zero_rows(c - rows[1])] if rows[1] < c else [])
        return jnp.concatenate(parts, axis=0) if len(parts) > 1 else x

    q_parts, k_parts = [], []
    for qr, kr in _gla_offdiag_pairs(c):
        if rev:
            qr, kr = kr, qr
        ref_row = qr[0] - 1 if not rev else qr[1]
        ref = cb[ref_row:ref_row + 1, :]
        qs, ks = slice(*qr), slice(*kr)
        q_parts.append(padded((q[qs] * jnp.exp(cb[qs] - ref)).astype(BF16), qr))
        k_parts.append(padded((k[ks] * jnp.exp(ref - cb[ks])).astype(BF16), kr))
    refs = []
    for blk in range(c // SUB):
        lo, hi = blk * SUB, (blk + 1) * SUB
        if not rev:
            ref = cb[lo - 1:lo, :] if blk > 0 else jnp.zeros_like(cb[0:1, :])
        else:
            ref = cb[hi:hi + 1, :] if hi < c else jnp.zeros_like(cb[0:1, :])
        refs.append(jnp.broadcast_to(ref, (SUB, dk)))
    ref_d = jnp.concatenate(refs, axis=0)
    q_diag = (q * jnp.exp(cb - ref_d)).astype(BF16)
    k_diag = (k * jnp.exp(jnp.minimum(ref_d - cb, EXP_CLAMP))).astype(BF16)
    return jnp.concatenate(q_parts, axis=1), jnp.concatenate(k_parts, axis=1), q_diag, k_diag


def _gla_diag_mask(c, rev):
    ri = lax.broadcasted_iota(jnp.int32, (c, c), 0)
    ci = lax.broadcasted_iota(jnp.int32, (c, c), 1)
    same_block = (ri // SUB) == (ci // SUB)
    return jnp.logical_and(same_block, (ri <= ci) if rev else (ri >= ci))


def _gla_scan_kernel(qf_ref, kf_ref, vf_ref, cf_ref, qb_ref, kb_ref, vb_ref, cb_ref, s0_ref,
                     of_ref, ob_ref, ns_ref, s_ref, *, tok, q_silu, scale, hb, dk, dv):
    n = pl.program_id(1)
    tt = tok.seq
    nch = tt // CHUNK
    is_ctx, dec_start = _seq_flags(n, tok)
    streams = [(d, h) for d in range(2) for h in range(hb)]

    @pl.when(is_ctx)
    def _():
        s_ref[...] = jnp.zeros_like(s_ref)

    @pl.when(dec_start)
    def _():
        for d, h in streams:
            s_ref[d, h] = s0_ref[0, 0, d, h].T

    dirs = ((qf_ref, kf_ref, vf_ref, cf_ref, of_ref), (qb_ref, kb_ref, vb_ref, cb_ref, ob_ref))
    ops = {}
    for d, (q_ref, k_ref, v_ref, c_ref, _) in enumerate(dirs):
        rev = d == 1
        for h in range(hb):
            q = q_ref[:, h * dk:(h + 1) * dk].astype(F32)
            if q_silu:
                q = _silu(q)
            q = q * scale
            k = k_ref[:, h * dk:(h + 1) * dk].astype(F32)
            v = v_ref[:, h * dv:(h + 1) * dv].astype(BF16)
            cb = c_ref[:, h * dk:(h + 1) * dk]
            for c in range(nch):
                rows = slice(c * CHUNK, (c + 1) * CHUNK)
                qc, kc, cbc = q[rows], k[rows], cb[rows]
                end = cbc[0:1, :] if rev else cbc[CHUNK - 1:CHUNK, :]
                ops[d, h, c] = dict(
                    intra=_gla_operands(qc, kc, cbc, rev), v=v[rows],
                    q_state=(qc * jnp.exp(cbc)).astype(BF16),
                    k_state=(kc * jnp.exp(end - cbc)).astype(BF16),
                    s_decay=jnp.exp(end))
    a_off = {key: _dot_nt(op["intra"][0], op["intra"][1]) for key, op in ops.items()}
    a_diag = {key: _dot_nt(op["intra"][2], op["intra"][3]) for key, op in ops.items()}
    masks = [_gla_diag_mask(CHUNK, rev) for rev in (False, True)]
    attn = {key: (a_off[key] + jnp.where(masks[key[0]], a_diag[key], 0.0)).astype(BF16) for key in ops}
    o_intra = {key: _dot(attn[key], op["v"]) for key, op in ops.items()}
    s_inc = {key: _dot_tn(op["v"], op["k_state"]) for key, op in ops.items()}

    state = {key: s_ref[key] for key in streams}
    for step in range(nch):
        for (d, h), s in state.items():
            c = step if d == 0 else nch - 1 - step
            dirs[d][4][c * CHUNK:(c + 1) * CHUNK, h * dv:(h + 1) * dv] = (
                o_intra[d, h, c] + _dot_nt(ops[d, h, c]["q_state"], s.astype(BF16))).astype(BF16)
            state[d, h] = ops[d, h, c]["s_decay"] * s + s_inc[d, h, c]
    for key, s in state.items():
        s_ref[key] = s

    @pl.when(is_ctx)
    def _():
        for d, h in streams:
            ns_ref[0, d, h] = s_ref[d, h].T


def _gla_scan(q_src, q_blk, kf_src, kf_blk, kb_src, kb_blk, v_src, v_blk, cb_f, cb_b, s0, j, tok,
              *, heads, dk, dv, q_silu, hb):
    m = q_src.shape[0]
    tt = tok.seq
    assert heads % hb == 0 and q_blk % hb == 0 and kf_blk % hb == 0 and kb_blk % hb == 0 and v_blk % hb == 0
    bwd = lambda n: _bwd_block(n, tok)
    fwd = lambda n: n

    def stream(blk, k_blk):
        return [
            pl.BlockSpec((tt, hb * dk), lambda h, n: (blk(n), q_blk // hb + h)),
            pl.BlockSpec((tt, hb * dk), lambda h, n: (blk(n), k_blk // hb + h)),
            pl.BlockSpec((tt, hb * dv), lambda h, n: (blk(n), v_blk // hb + h)),
            pl.BlockSpec((tt, hb * dk), lambda h, n: (blk(n), h)),
        ]

    n_batch = tok.n_ctx_blocks
    return pl.pallas_call(
        functools.partial(_gla_scan_kernel, tok=tok, q_silu=q_silu, scale=dk ** -0.5, hb=hb, dk=dk, dv=dv),
        grid=(heads // hb, tok.n_blocks),
        in_specs=stream(fwd, kf_blk) + stream(bwd, kb_blk) + [
            pl.BlockSpec((1, 1, 2, hb, dk, dv), lambda h, n: (_dec_batch(n, tok), j, 0, h, 0, 0)),
        ],
        out_specs=[
            pl.BlockSpec((tt, hb * dv), lambda h, n: (n, h)),
            pl.BlockSpec((tt, hb * dv), lambda h, n: (bwd(n), h)),
            pl.BlockSpec((1, 2, hb, dk, dv), lambda h, n: (_state_block(n, tok), 0, h, 0, 0)),
        ],
        out_shape=[
            jax.ShapeDtypeStruct((m, heads * dv), BF16),
            jax.ShapeDtypeStruct((m, heads * dv), BF16),
            jax.ShapeDtypeStruct((n_batch, 2, heads, dk, dv), F32),
        ],
        scratch_shapes=[pltpu.VMEM((2, hb, dv, dk), F32)],
        compiler_params=_params("parallel", "arbitrary"),
        name="gla_scan",
    )(q_src, kf_src, v_src, cb_f, q_src, kb_src, v_src, cb_b, s0)


def kernel(x_prompt, x_sample, c, state_deltanet, state_gla, state_hgrn, c_ctx, ada_w, ada_b, norm1, norm2, final_norm, w_up, w_down, dn_w_in, dn_conv, dn_A_log, dn_dt_bias, dn_norm, dn_w_out, gla_w_in, gla_w_a2, gla_b_a, gla_norm, gla_w_out, hgrn_w_in, hgrn_lb_logits, hgrn_norm, hgrn_w_out):
    batch, seq, d = x_prompt.shape
    dec_batch, dec_seq, _ = x_sample.shape
    depth = ada_w.shape[0]
    tok = Tokens(m_ctx=batch * seq, seq=seq, dec_batch=dec_batch, dec_seq=dec_seq)
    assert seq % CHUNK == 0 and dec_seq % seq == 0 and seq % GRID_W == 0

    dn_hv, dn_dk, dn_dv = state_deltanet.shape[3:]
    dn_vd = dn_hv * dn_dv
    n_conv = dn_conv.shape[2]
    dn_kd = (n_conv - dn_vd) // 2
    dn_hk = dn_kd // dn_dk
    gla_h, gla_dk, gla_dv = state_gla.shape[3:]
    gla_kd, gla_vd = gla_h * gla_dk, gla_h * gla_dv
    hg_h, hg_dk, hg_dv = state_hgrn.shape[3:]

    x = jnp.concatenate([x_prompt.reshape(tok.m_ctx, d), x_sample.reshape(dec_batch * dec_seq, d)], axis=0)
    n_rows = 1 + dec_batch
    pad = (-n_rows) % 8
    cond = jnp.concatenate([c_ctx[None, :], c, jnp.zeros((pad, d), F32)], axis=0)
    mod_all = _mod_table(cond, ada_w, ada_b).reshape(depth, n_rows + pad, 6, d)

    new_states = ([], [], [])
    dn_states = None
    for i in range(depth):
        kind, j = i % N_MIXERS, i // N_MIXERS
        mod = mod_all[i]
        if kind == 0:
            w_in = dn_w_in[j]
            proj = _proj(x, norm1[i], mod, w_in[:, :n_conv + dn_vd].astype(BF16), tok)
            raw = _proj(x, norm1[i], mod, w_in[:, n_conv + dn_vd:], tok, precise=True)
            qkv = _dn_prep(proj, dn_conv[j], n_conv, dn_kd, dn_dk, tok)
            gates, gates_t = _dn_gates(raw, dn_A_log[j], dn_dt_bias[j], tok)
            o_f, o_b, dn_states = _dn_scan(qkv, gates, gates_t, state_deltanet, j, dn_states, tok,
                                           hk_n=dn_hk, hv=dn_hv, dk=dn_dk, dv=dn_dv)
            x = _out_proj(o_f, o_b, proj, n_conv // dn_vd, dn_norm[j], x, mod, dn_w_out[j].astype(BF16), tok)
        elif kind == 1:
            w_in = gla_w_in[j]
            n_main = 2 * gla_kd + 2 * gla_vd
            proj = _proj(x, norm1[i], mod, w_in[:, :n_main].astype(BF16), tok)
            a1 = _proj(x, norm1[i], mod, w_in[:, n_main:], tok, precise=True)
            cb_f, cb_b = _gla_prep(a1, gla_w_a2[j], gla_b_a[j], tok)
            o_f, o_b, ns = _gla_scan(proj, 0, proj, gla_h, proj, gla_h, proj, (2 * gla_kd) // gla_dv,
                                     cb_f, cb_b, state_gla, j, tok,
                                     heads=gla_h, dk=gla_dk, dv=gla_dv, q_silu=False, hb=min(gla_h, 2))
            x = _out_proj(o_f, o_b, proj, (2 * gla_kd + gla_vd) // gla_vd, gla_norm[j], x, mod,
                          gla_w_out[j].astype(BF16), tok)
        else:
            proj = _proj(x, norm1[i], mod, hgrn_w_in[j].astype(BF16), tok)
            k_f, k_b, cb_f, cb_b = _hgrn_prep(proj, hgrn_lb_logits, i, d, tok)
            o_f, o_b, ns = _gla_scan(proj, 0, k_f, 0, k_b, 0, proj, d // hg_dv, cb_f, cb_b, state_hgrn, j, tok,
                                     heads=hg_h, dk=hg_dk, dv=hg_dv, q_silu=True, hb=min(hg_h, 8))
            x = _out_proj(o_f, o_b, proj, 2, hgrn_norm[j], x, mod, hgrn_w_out[j].astype(BF16), tok)
        if kind != 0:
            new_states[kind].append(ns)
        x = _mlp(x, norm2[i], mod, w_up[i].astype(BF16), w_down[i].astype(BF16), final_norm, tok,
                 final=(i == depth - 1))

    y_ctx, y_dec = x
    y_prompt = y_ctx.reshape(batch, seq, d)
    y_sample = y_dec.reshape(dec_batch, dec_seq, d)
    stacked = [s[0][:, None] if len(s) == 1 else jnp.stack(s, axis=1) for s in new_states[1:]]
    return (y_prompt, y_sample, dn_states) + tuple(stacked)
```

```python
import functools
from typing import NamedTuple

import jax
import jax.numpy as jnp
from jax import lax
from jax.experimental import pallas as pl
from jax.experimental.pallas import tpu as pltpu

F32 = jnp.float32
BF16 = jnp.bfloat16

EPS = 1e-6
GRID_W = 64
CONV_K = 5
GLA_RANK = 16
GLA_TAU = 16.0
N_MIXERS = 3
CHUNK = 64
SUB = 16
EXP_CLAMP = 80.0
LANES = 128
SUBLANES = 8
V7X_MXU_COLS = 256
V7X_VMEM_LIMIT_BYTES = 56 * 1024 * 1024
HIGHEST = lax.Precision.HIGHEST


class Tokens(NamedTuple):
    m_ctx: int
    seq: int
    dec_batch: int
    dec_seq: int

    @property
    def m(self):
        return self.m_ctx + self.dec_batch * self.dec_seq

    @property
    def n_ctx_blocks(self):
        return self.m_ctx // self.seq

    @property
    def blocks_per_dec(self):
        return self.dec_seq // self.seq

    @property
    def n_blocks(self):
        return self.m // self.seq


def _tile(n, pref):
    t = min(n, pref)
    while n % t:
        t //= 2
    return t


def _token_tile(tok, pref):
    t = pref
    while tok.m_ctx % t or tok.dec_seq % t:
        t //= 2
    return t


def _params(*sem):
    return pltpu.CompilerParams(dimension_semantics=sem, vmem_limit_bytes=V7X_VMEM_LIMIT_BYTES)


def _mod_row(i, tm, tok):
    start = i * tm
    return jnp.where(start < tok.m_ctx, 0, 1 + (start - tok.m_ctx) // tok.dec_seq)


def _sigmoid(x):
    return 0.5 + 0.5 * jnp.tanh(0.5 * x)


def _silu(x):
    half = 0.5 * x
    return half + half * jnp.tanh(half)


def _softplus(x):
    return jnp.maximum(x, 0.0) + jnp.log(1.0 + jnp.exp(-jnp.abs(x)))


def _norm_mod(x, nw, shift, scale):
    ms = jnp.mean(x * x, axis=-1, keepdims=True)
    return x * lax.rsqrt(ms + EPS) * nw * (1.0 + scale) + shift


def _dot(a, b):
    return jnp.dot(a, b, preferred_element_type=F32)


def _dot_nt(a, b):
    return lax.dot_general(a, b, (((1,), (1,)), ((), ())), preferred_element_type=F32)


def _dot_tn(a, b):
    return lax.dot_general(a, b, (((0,), (0,)), ((), ())), preferred_element_type=F32)


def _split_dot(tri, x):
    hi = x.astype(BF16)
    lo = (x - hi.astype(F32)).astype(BF16)
    return _dot(tri, hi) + _dot(tri, lo)


def _split_matmul(a, b):
    ah = a.astype(BF16)
    al = (a - ah.astype(F32)).astype(BF16)
    bh = b.astype(BF16)
    bl = (b - bh.astype(F32)).astype(BF16)
    return _dot(ah, bh) + _dot(al, bh) + _dot(ah, bl)


def _tri(n, upper):
    r = lax.broadcasted_iota(jnp.int32, (n, n), 0)
    c = lax.broadcasted_iota(jnp.int32, (n, n), 1)
    return jnp.where((r <= c) if upper else (r >= c), 1.0, 0.0).astype(BF16)


def _mod_kernel(c_ref, w_ref, b_ref, o_ref):
    c = c_ref[...]
    s = _silu(c)
    o_ref[0] = jnp.dot(s, w_ref[0], precision=HIGHEST, preferred_element_type=F32) + b_ref[0]


def _mod_table(cond, ada_w, ada_b):
    depth, d, n = ada_w.shape
    rows = cond.shape[0]
    tn = _tile(n, 1024)
    return pl.pallas_call(
        _mod_kernel,
        grid=(depth, n // tn),
        in_specs=[
            pl.BlockSpec((rows, d), lambda l, j: (0, 0)),
            pl.BlockSpec((1, d, tn), lambda l, j: (l, 0, j)),
            pl.BlockSpec((1, 1, tn), lambda l, j: (l, 0, j)),
        ],
        out_specs=pl.BlockSpec((1, rows, tn), lambda l, j: (l, 0, j)),
        out_shape=jax.ShapeDtypeStruct((depth, rows, n), F32),
        compiler_params=_params("parallel", "parallel"),
        name="mod_table",
    )(cond, ada_w, ada_b.reshape(depth, 1, n))


def _proj_kernel(x_ref, nw_ref, mod_ref, w_ref, o_ref, h_ref, *, precise):
    @pl.when(pl.program_id(1) == 0)
    def _():
        h = _norm_mod(x_ref[...], nw_ref[...], mod_ref[0, 0:1, :], mod_ref[0, 1:2, :])
        h_ref[...] = h.astype(h_ref.dtype)

    if precise:
        o_ref[...] = _split_matmul(h_ref[...], w_ref[...])
    else:
        o_ref[...] = _dot(h_ref[...], w_ref[...]).astype(o_ref.dtype)


def _proj(x, nw, mod, w, layer, col0, n, tok, *, precise=False, tm=1024, tn=2048):
    m, d = x.shape
    tm = _token_tile(tok, tm)
    tn = _tile(n, tn)
    assert col0 % tn == 0
    return pl.pallas_call(
        functools.partial(_proj_kernel, precise=precise),
        grid=(m // tm, n // tn),
        in_specs=[
            pl.BlockSpec((tm, d), lambda i, j: (i, 0)),
            pl.BlockSpec((1, d), lambda i, j: (0, 0)),
            pl.BlockSpec((1, 6, d), lambda i, j: (_mod_row(i, tm, tok), 0, 0)),
            pl.BlockSpec((None, d, tn), lambda i, j: (layer, 0, col0 // tn + j)),
        ],
        out_specs=pl.BlockSpec((tm, tn), lambda i, j: (i, j)),
        out_shape=jax.ShapeDtypeStruct((m, n), F32 if precise else BF16),
        scratch_shapes=[pltpu.VMEM((tm, d), F32 if precise else BF16)],
        compiler_params=_params("parallel", "arbitrary"),
        name="proj_precise" if precise else "proj",
    )(x, nw.reshape(1, d), mod, w)


def _mlp_kernel(x_ref, nw_ref, mod_ref, wu_ref, wd_ref, fw_ref, *rest, n_ctx_tiles):
    out_refs, (h_ref, acc_ref) = rest[:-2], rest[-2:]
    final = n_ctx_tiles is not None
    i = pl.program_id(0)
    f = pl.program_id(1)

    @pl.when(f == 0)
    def _():
        h = _norm_mod(x_ref[...], nw_ref[...], mod_ref[0, 3:4, :], mod_ref[0, 4:5, :])
        h_ref[...] = h.astype(BF16)
        acc_ref[...] = jnp.zeros_like(acc_ref)

    a = jnp.maximum(_dot(h_ref[...], wu_ref[...]), 0.0)
    acc_ref[...] += _dot((a * a).astype(BF16), wd_ref[...])

    last = f == pl.num_programs(1) - 1

    def result():
        y = x_ref[...] + mod_ref[0, 5:6, :] * acc_ref[...]
        if final:
            ms = jnp.mean(y * y, axis=-1, keepdims=True)
            y = y * lax.rsqrt(ms + EPS) * fw_ref[...]
        return y

    if not final:
        @pl.when(last)
        def _():
            out_refs[0][...] = result()
    else:
        @pl.when(jnp.logical_and(last, i < n_ctx_tiles))
        def _():
            out_refs[0][...] = result()

        @pl.when(jnp.logical_and(last, i >= n_ctx_tiles))
        def _():
            out_refs[1][...] = result()


def _mlp(x, nw, mod, w_up, w_down, layer, final_w, tok, *, final, tm=512, tf=1024):
    m, d = x.shape
    ff = w_up.shape[2]
    tm = _token_tile(tok, tm)
    tf = _tile(ff, tf)
    n_ctx_tiles = tok.m_ctx // tm
    if final:
        out_specs = [pl.BlockSpec((tm, d), lambda i, f: (jnp.minimum(i, n_ctx_tiles - 1), 0)),
                     pl.BlockSpec((tm, d), lambda i, f: (jnp.maximum(i - n_ctx_tiles, 0), 0))]
        out_shape = [jax.ShapeDtypeStruct((tok.m_ctx, d), F32), jax.ShapeDtypeStruct((m - tok.m_ctx, d), F32)]
    else:
        out_specs = pl.BlockSpec((tm, d), lambda i, f: (i, 0))
        out_shape = jax.ShapeDtypeStruct((m, d), F32)
    return pl.pallas_call(
        functools.partial(_mlp_kernel, n_ctx_tiles=n_ctx_tiles if final else None),
        grid=(m // tm, ff // tf),
        in_specs=[
            pl.BlockSpec((tm, d), lambda i, f: (i, 0)),
            pl.BlockSpec((1, d), lambda i, f: (0, 0)),
            pl.BlockSpec((1, 6, d), lambda i, f: (_mod_row(i, tm, tok), 0, 0)),
            pl.BlockSpec((None, d, tf), lambda i, f: (layer, 0, f)),
            pl.BlockSpec((None, tf, d), lambda i, f: (layer, f, 0)),
            pl.BlockSpec((1, d), lambda i, f: (0, 0)),
        ],
        out_specs=out_specs,
        out_shape=out_shape,
        scratch_shapes=[pltpu.VMEM((tm, d), BF16), pltpu.VMEM((tm, d), F32)],
        compiler_params=_params("arbitrary" if final else "parallel", "arbitrary"),
        name="mlp_final" if final else "mlp",
    )(x, nw.reshape(1, d), mod, w_up, w_down, final_w.reshape(1, d))


def _out_kernel(of_ref, ob_ref, g_ref, nw_ref, x_ref, mod_ref, w_ref, o_ref, y0_ref, y1_ref, *, n_heads, dh):
    i = pl.program_id(0)
    d = w_ref.shape[1]
    n_chunks = d // V7X_MXU_COLS
    heads_per_chunk = -(-n_heads // n_chunks)

    @pl.when(i == 0)
    def _():
        y1_ref[...] = jnp.zeros_like(y1_ref)

    def body(y_write, y_read):
        ones = jnp.full((dh, LANES), 1.0 / dh, BF16) if dh & (dh - 1) == 0 else None
        for c in range(n_chunks):
            cols = slice(c * V7X_MXU_COLS, (c + 1) * V7X_MXU_COLS)
            o_ref[:, cols] = x_ref[:, cols] + mod_ref[0, 2:3, cols] * _dot(y_read[...], w_ref[:, cols])
            for h in range(c * heads_per_chunk, min((c + 1) * heads_per_chunk, n_heads)):
                sl = slice(h * dh, (h + 1) * dh)
                o = of_ref[:, sl].astype(F32) + ob_ref[:, sl].astype(F32)
                if ones is None:
                    ms = _dot((o * o).astype(BF16), jnp.ones((dh, LANES), BF16)) * (1.0 / dh)
                else:
                    ms = _dot((o * o).astype(BF16), ones)
                inv = lax.rsqrt(ms + EPS)
                y = o * nw_ref[...] * _silu(g_ref[:, sl].astype(F32))
                for k in range(dh // LANES):
                    lo = h * dh + k * LANES
                    y_write[:, lo:lo + LANES] = (y[:, k * LANES:(k + 1) * LANES] * inv).astype(BF16)

    @pl.when(i % 2 == 0)
    def _():
        body(y0_ref, y1_ref)

    @pl.when(i % 2 == 1)
    def _():
        body(y1_ref, y0_ref)


def _out_proj(o_f, o_b, gate_src, gate_blk, norm_w, x, mod, w_out, layer, tok, *, tm=256):
    m, vd = o_f.shape
    d = x.shape[1]
    dh = norm_w.shape[0]
    tm = _token_tile(tok, tm)
    n_tiles = m // tm
    assert d % V7X_MXU_COLS == 0 and dh % LANES == 0
    cur = lambda i: jnp.minimum(i, n_tiles - 1)
    prev = lambda i: jnp.maximum(i - 1, 0)
    return pl.pallas_call(
        functools.partial(_out_kernel, n_heads=vd // dh, dh=dh),
        grid=(n_tiles + 1,),
        in_specs=[
            pl.BlockSpec((tm, vd), lambda i: (cur(i), 0)),
            pl.BlockSpec((tm, vd), lambda i: (cur(i), 0)),
            pl.BlockSpec((tm, vd), lambda i: (cur(i), gate_blk)),
            pl.BlockSpec((1, dh), lambda i: (0, 0)),
            pl.BlockSpec((tm, d), lambda i: (prev(i), 0)),
            pl.BlockSpec((1, 6, d), lambda i: (_mod_row(prev(i), tm, tok), 0, 0)),
            pl.BlockSpec((None, vd, d), lambda i: (layer, 0, 0), pipeline_mode=pl.Buffered(1)),
        ],
        out_specs=pl.BlockSpec((tm, d), lambda i: (prev(i), 0)),
        out_shape=jax.ShapeDtypeStruct((m, d), F32),
        scratch_shapes=[pltpu.VMEM((tm, vd), BF16), pltpu.VMEM((tm, vd), BF16)],
        compiler_params=_params("arbitrary"),
        name="out_proj",
    )(o_f, o_b, gate_src, norm_w.reshape(1, dh), x, mod, w_out)


def _bwd_block(n, tok):
    nc, per = tok.n_ctx_blocks, tok.blocks_per_dec
    r = jnp.maximum(n - nc, 0)
    return jnp.where(n < nc, n, nc + (r // per) * per + (per - 1 - r % per))


def _dec_batch(n, tok):
    return jnp.maximum(n - tok.n_ctx_blocks, 0) // tok.blocks_per_dec


def _state_block(n, tok):
    return jnp.minimum(n, tok.n_ctx_blocks - 1)


def _seq_flags(n, tok):
    nc, per = tok.n_ctx_blocks, tok.blocks_per_dec
    is_ctx = n < nc
    r = jnp.maximum(n - nc, 0) % per
    return is_ctx, jnp.logical_and(jnp.logical_not(is_ctx), r == 0)


def _dn_prep_kernel(x_ref, shift_ref, w_ref, o_ref, *, n_q_blocks, dk):
    j = pl.program_id(1)
    x = x_ref[...]
    tc = x.shape[1]
    acc = None
    for s in range(CONV_K):
        xs = x.astype(F32) if s == CONV_K // 2 else _dot(shift_ref[0, s], x)
        term = xs * w_ref[s:s + 1, :]
        acc = term if acc is None else acc + term
    y = _silu(acc)

    def normed(scale):
        ones = jnp.ones((dk, dk), BF16)
        for g in range(tc // dk):
            blk = y[:, g * dk:(g + 1) * dk]
            ss = _dot((blk * blk).astype(BF16), ones)
            o_ref[:, g * dk:(g + 1) * dk] = (blk * (lax.rsqrt(ss + EPS) * scale)).astype(o_ref.dtype)

    @pl.when(j < n_q_blocks)
    def _():
        normed(dk ** -0.5)

    @pl.when(jnp.logical_and(j >= n_q_blocks, j < 2 * n_q_blocks))
    def _():
        normed(1.0)

    @pl.when(j >= 2 * n_q_blocks)
    def _():
        o_ref[...] = y.astype(o_ref.dtype)


def _conv_shift_matrices(tt):
    t = jnp.arange(tt)[:, None]
    u = jnp.arange(tt)[None, :]
    mats = []
    for seg in (tt, GRID_W):
        same = (t // seg) == (u // seg)
        mats.append(jnp.stack([jnp.logical_and(u == t + s - CONV_K // 2, same) for s in range(CONV_K)]))
    return jnp.stack(mats).astype(BF16)


def _dn_prep(proj, conv_w, n_conv, kd, dk, tok):
    m = proj.shape[0]
    tt = tok.seq
    tc = _tile(kd, 2048)
    n_ctx = tok.n_ctx_blocks
    return pl.pallas_call(
        functools.partial(_dn_prep_kernel, n_q_blocks=kd // tc, dk=dk),
        grid=(m // tt, n_conv // tc),
        in_specs=[
            pl.BlockSpec((tt, tc), lambda i, j: (i, j)),
            pl.BlockSpec((1, CONV_K, tt, tt), lambda i, j: (jnp.where(i < n_ctx, 0, 1), 0, 0, 0)),
            pl.BlockSpec((CONV_K, tc), lambda i, j: (0, j)),
        ],
        out_specs=pl.BlockSpec((tt, tc), lambda i, j: (i, j)),
        out_shape=jax.ShapeDtypeStruct((m, n_conv), BF16),
        compiler_params=_params("parallel", "parallel"),
        name="dn_prep",
    )(proj, _conv_shift_matrices(tt), conv_w)


def _dn_gate_kernel(x_ref, a_ref, bias_ref, o_ref, gt_ref, *, hv, tt):
    x = x_ref[...]
    lanes = x.shape[1]
    lane = lax.broadcasted_iota(jnp.int32, (1, lanes), 1)
    is_fwd = lane < 3 * hv
    g = -a_ref[...] * _softplus(x + bias_ref[...])
    beta = pltpu.roll(_sigmoid(x), 2 * hv, 1)
    lower = _tri(CHUNK, upper=False)
    upper = _tri(CHUNK, upper=True)
    for c in range(tt // CHUNK):
        rows = slice(c * CHUNK, (c + 1) * CHUNK)
        gch = g[rows]
        gc = jnp.where(is_fwd, _split_dot(lower, gch), _split_dot(upper, gch))
        g_end = jnp.where(is_fwd, gc[CHUNK - 1:CHUNK, :], gc[0:1, :])
        eg = jnp.exp(gc)
        o_ref[0, rows, :] = beta[rows]
        o_ref[1, rows, :] = gc
        o_ref[2, rows, :] = eg
        o_ref[3, rows, :] = jnp.exp(g_end - gc)
        o_ref[4, rows, :] = beta[rows] * eg
    gt_ref[...] = o_ref[1].T


def _dn_gates(raw, a_log, dt_bias, tok):
    m, lanes = raw.shape
    hv = lanes // 4
    tt = tok.seq
    zeros = jnp.zeros((2 * hv,), F32)
    a_full = jnp.concatenate([zeros, jnp.exp(a_log).reshape(-1)]).reshape(1, lanes)
    bias_full = jnp.concatenate([zeros, dt_bias.reshape(-1)]).reshape(1, lanes)
    return pl.pallas_call(
        functools.partial(_dn_gate_kernel, hv=hv, tt=tt),
        grid=(m // tt,),
        in_specs=[
            pl.BlockSpec((tt, lanes), lambda i: (i, 0)),
            pl.BlockSpec((1, lanes), lambda i: (0, 0)),
            pl.BlockSpec((1, lanes), lambda i: (0, 0)),
        ],
        out_specs=[
            pl.BlockSpec((5, tt, lanes), lambda i: (0, i, 0)),
            pl.BlockSpec((lanes, tt), lambda i: (0, i)),
        ],
        out_shape=[jax.ShapeDtypeStruct((5, m, lanes), F32), jax.ShapeDtypeStruct((lanes, m), F32)],
        compiler_params=_params("parallel"),
        name="dn_gates",
    )(raw, a_full, bias_full)


def _dn_scan_kernel(qf_ref, kf_ref, vf_ref, gf_ref, gtf_ref, qb_ref, kb_ref, vb_ref, gb_ref, gtb_ref, s0_ref,
                    of_ref, ob_ref, ns_ref, s_ref, u0_ref, wq_ref, attn_ref, kd_ref, ge_ref, *, tok, hv, dv, hkb):
    hk0 = pl.program_id(0) * hkb
    n = pl.program_id(1)
    tt = tok.seq
    nch = tt // CHUNK
    dk = qf_ref.shape[1] // hkb
    assert dv == 2 * CHUNK and CHUNK == 4 * SUB
    has_prev = n > 0
    is_ctx, dec_start = _seq_flags(jnp.maximum(n - 1, 0), tok)
    wslot = n % 2
    rslot = 1 - wslot
    scratch = (u0_ref, wq_ref, attn_ref, kd_ref, ge_ref)

    @pl.when(n == 0)
    def _():
        s_ref[...] = jnp.zeros_like(s_ref)
        for ref in scratch:
            ref[1] = jnp.zeros(ref.shape[1:], ref.dtype)

    @pl.when(jnp.logical_and(has_prev, is_ctx))
    def _():
        s_ref[...] = jnp.zeros_like(s_ref)

    @pl.when(jnp.logical_and(has_prev, dec_start))
    def _():
        s_ref[...] = s0_ref[0, 0]

    bf = lambda x: x.astype(BF16)
    lanes = gf_ref.shape[2]
    lane_iota = lax.broadcasted_iota(jnp.int32, (1, lanes), 1)
    ri = lax.broadcasted_iota(jnp.int32, (CHUNK, 2 * CHUNK), 0)
    li = lax.broadcasted_iota(jnp.int32, (CHUNK, 2 * CHUNK), 1)
    left = li < CHUNK
    ci = li % CHUNK
    same = [(ri // b) == (ci // b) for b in (SUB, 2 * SUB)]
    zero_b = jnp.zeros((CHUNK, 2 * CHUNK), BF16)

    def block_diag(xb):
        return jnp.concatenate([jnp.where(left, xb, zero_b), jnp.where(left, zero_b, xb)], axis=0)

    def pair_dot(a, b):
        return _dot(bf(a), block_diag(bf(b)))

    dirs = ((qf_ref, kf_ref, vf_ref, gf_ref, gtf_ref, of_ref), (qb_ref, kb_ref, vb_ref, gb_ref, gtb_ref, ob_ref))

    def prepare():
        ch = {}
        for d, kh in [(d, kh) for d in range(2) for kh in range(hkb)]:
            q_ref, k_ref, v_ref, g_ref, gt_ref, _ = dirs[d]
            incl = (ri >= ci) if d == 0 else (ri <= ci)
            strict = (ri > ci) if d == 0 else (ri < ci)
            end_row = CHUNK - 1 if d == 0 else 0
            q = q_ref[:, kh * dk:(kh + 1) * dk]
            k = k_ref[:, kh * dk:(kh + 1) * dk]
            kf32 = k.astype(F32)
            qf32 = q.astype(F32)
            cols, g_rows, vs = [], [], []
            for e in range(2):
                a_lane = 2 * hv + d * hv + 2 * (hk0 + kh) + e
                sel = lane_iota == a_lane
                cols.append([jnp.sum(jnp.where(sel, g_ref[i], 0.0), axis=1, keepdims=True) for i in range(5)])
                g_rows.append(gt_ref[pl.ds(a_lane, 1), :])
                vs.append(v_ref[:, (2 * kh + e) * dv:(2 * kh + e + 1) * dv].astype(F32))
            for c in range(nch):
                rows = slice(c * CHUNK, (c + 1) * CHUNK)
                pair = lambda i: jnp.where(left, cols[0][i][rows], cols[1][i][rows])
                k2 = jnp.concatenate([k[rows], k[rows]], axis=0)
                kk = _dot_nt(k[rows], k2)
                qk = _dot_nt(q[rows], k2)
                g_row = jnp.concatenate([g_rows[0][:, rows], g_rows[1][:, rows]], axis=1)
                diff = pair(1) - g_row
                decay = jnp.where(incl, jnp.exp(jnp.where(incl, diff, 0.0)), 0.0)
                p = jnp.where(strict, -(pair(0) * kk * decay), 0.0)
                attn_ref[wslot, d, kh, c] = bf(qk * decay)
                for e in range(2):
                    beta_c, _, eg_c, ekd_c, beg_c = cols[e]
                    qg = qf32[rows] * eg_c[rows]
                    kd_ref[wslot, d, 2 * kh + e, c] = bf(kf32[rows] * ekd_c[rows])
                    wq_ref[wslot, d, 2 * kh + e, c, CHUNK:, :] = bf(qg)
                    g_end = eg_c[c * CHUNK + end_row:c * CHUNK + end_row + 1, :]
                    ge_ref[wslot, d, 2 * kh + e, c] = jnp.broadcast_to(g_end, ge_ref.shape[4:])
                ch[kh, d, c] = dict(
                    pd=jnp.where(same[0], p, 0.0),
                    p32=jnp.where(jnp.logical_and(same[1], jnp.logical_not(same[0])), p, 0.0),
                    p64=jnp.where(same[1], 0.0, p),
                    r=[jnp.concatenate([cols[e][0][rows] * vs[e][rows], cols[e][4][rows] * kf32[rows]], axis=1)
                       for e in range(2)])
            yield
        for c in ch.values():
            c["q"] = pair_dot(c["pd"], c["pd"])
            c["xm"] = c["pd"]
        yield
        for it in range(2):
            for c in ch.values():
                q, xm = c["q"], c["xm"]
                both = _dot(bf(q), jnp.concatenate([block_diag(bf(xm)), block_diag(bf(q))], axis=1))
                c["xm"] = xm + q + both[:, :2 * CHUNK]
                c["q"] = both[:, 2 * CHUNK:]
            yield
        for c in ch.values():
            q, xm = c.pop("q"), c["xm"]
            c["xm"] = xm + q + pair_dot(q, xm)
        yield
        for level in ("p32", "p64"):
            for c in ch.values():
                c["y"] = c[level] + pair_dot(c[level], c["xm"])
            yield
            for c in ch.values():
                xm, y = c["xm"], c.pop("y")
                c["xm"] = xm + y + pair_dot(xm, y)
            yield
        zero_r = jnp.zeros((CHUNK, 2 * dv), BF16)
        for (kh, d, cidx), c in ch.items():
            r0, r1 = c["r"]
            rhs = jnp.concatenate([jnp.concatenate([bf(r0), zero_r], axis=1),
                                   jnp.concatenate([zero_r, bf(r1)], axis=1)], axis=0)
            big = _dot(bf(c["xm"]), rhs)
            sol = [r0 + big[:, :2 * dv], r1 + big[:, 2 * dv:]]
            u0_ref[wslot, d, kh, cidx] = jnp.concatenate([sol[0][:, :dv], sol[1][:, :dv]], axis=1)
            for e in range(2):
                wq_ref[wslot, d, 2 * kh + e, cidx, :CHUNK, :] = bf(sol[e][:, dv:])
        yield

    def recur():
        state = {(d, e): s_ref[d, e] for d in range(2) for e in range(2 * hkb)}
        zero_u = jnp.zeros((CHUNK, dv), BF16)
        for step in range(nch):
            chunk = lambda d: step if d == 0 else nch - 1 - step
            ws = {(d, e): _dot(wq_ref[rslot, d, e, chunk(d)], bf(s)) for (d, e), s in state.items()}
            yield
            for d, kh in [(d, kh) for d in range(2) for kh in range(hkb)]:
                c = chunk(d)
                u0 = u0_ref[rslot, d, kh, c]
                ev = [2 * kh, 2 * kh + 1]
                ub = [bf(u0[:, e * dv:(e + 1) * dv] - ws[d, ev[e]][:CHUNK]) for e in range(2)]
                u2 = jnp.concatenate([jnp.concatenate([ub[0], zero_u], axis=1),
                                      jnp.concatenate([zero_u, ub[1]], axis=1)], axis=0)
                o = (_dot(attn_ref[rslot, d, kh, c], u2)
                     + jnp.concatenate([ws[d, ev[0]][CHUNK:], ws[d, ev[1]][CHUNK:]], axis=1))
                dirs[d][5][c * CHUNK:(c + 1) * CHUNK, 2 * kh * dv:2 * (kh + 1) * dv] = bf(o)
                for e in range(2):
                    state[d, ev[e]] = (ge_ref[rslot, d, ev[e], c][0:1, :] * state[d, ev[e]]
                                       + _dot_tn(kd_ref[rslot, d, ev[e], c], ub[e]))
            yield
        for (d, e), s in state.items():
            s_ref[d, e] = s
        yield

    halves = [recur(), prepare()]
    while halves:
        for g in list(halves):
            if next(g, "done") == "done":
                halves.remove(g)

    @pl.when(jnp.logical_and(has_prev, is_ctx))
    def _():
        ns_ref[0, 0] = s_ref[...]


def _dn_scan_kernel_aliased(*refs, **kw):
    n_in = 11
    return _dn_scan_kernel(*refs[:n_in], *refs[n_in + 1:], **kw)


def _dn_scan(qkv, gates, gates_t, s0, j, new_states, tok, *, hk_n, hv, dk, dv):
    m = qkv.shape[0]
    tt = tok.seq
    rep = hv // hk_n
    assert rep == 2
    lanes = gates.shape[2]
    kd = hk_n * dk
    nb = tok.n_blocks
    nch = tt // CHUNK
    fwd = lambda n: jnp.minimum(n, nb - 1)
    bwd = lambda n: _bwd_block(jnp.minimum(n, nb - 1), tok)
    prev = lambda n: jnp.maximum(n - 1, 0)

    hkb = max(b for b in (4, 2, 1) if hk_n % b == 0)
    hvb = hkb * rep

    def stream(blk):
        return [
            pl.BlockSpec((tt, hkb * dk), lambda h, n: (blk(n), h)),
            pl.BlockSpec((tt, hkb * dk), lambda h, n: (blk(n), hk_n // hkb + h)),
            pl.BlockSpec((tt, hvb * dv), lambda h, n: (blk(n), (2 * kd) // (hvb * dv) + h)),
            pl.BlockSpec((5, tt, lanes), lambda h, n: (0, blk(n), 0)),
            pl.BlockSpec((lanes, tt), lambda h, n: (0, blk(n))),
        ]

    n_batch = tok.n_ctx_blocks
    n_layers = s0.shape[1]
    aliased = new_states is not None
    extra_in = [pl.BlockSpec(memory_space=pl.ANY)] if aliased else []
    extra_args = (new_states,) if aliased else ()
    return pl.pallas_call(
        functools.partial(_dn_scan_kernel_aliased if aliased else _dn_scan_kernel, tok=tok, hv=hv, dv=dv, hkb=hkb),
        grid=(hk_n // hkb, nb + 1),
        in_specs=stream(fwd) + stream(bwd) + [
            pl.BlockSpec((1, 1, 2, hvb, dk, dv), lambda h, n: (_dec_batch(prev(n), tok), j, 0, h, 0, 0)),
        ] + extra_in,
        out_specs=[
            pl.BlockSpec((tt, hvb * dv), lambda h, n: (prev(n), h)),
            pl.BlockSpec((tt, hvb * dv), lambda h, n: (_bwd_block(prev(n), tok), h)),
            pl.BlockSpec((1, 1, 2, hvb, dk, dv), lambda h, n: (_state_block(prev(n), tok), j, 0, h, 0, 0)),
        ],
        out_shape=[
            jax.ShapeDtypeStruct((m, hv * dv), BF16),
            jax.ShapeDtypeStruct((m, hv * dv), BF16),
            jax.ShapeDtypeStruct((n_batch, n_layers, 2, hv, dk, dv), F32),
        ],
        input_output_aliases={11: 2} if aliased else {},
        scratch_shapes=[
            pltpu.VMEM((2, hvb, dk, dv), F32),
            pltpu.VMEM((2, 2, hkb, nch, CHUNK, rep * dv), F32),
            pltpu.VMEM((2, 2, hvb, nch, 2 * CHUNK, dk), BF16),
            pltpu.VMEM((2, 2, hkb, nch, CHUNK, rep * CHUNK), BF16),
            pltpu.VMEM((2, 2, hvb, nch, CHUNK, dk), BF16),
            pltpu.VMEM((2, 2, hvb, nch, SUBLANES, dv), F32),
        ],
        compiler_params=_params("parallel", "arbitrary"),
        name="dn_scan",
    )(qkv, qkv, qkv, gates, gates_t, qkv, qkv, qkv, gates, gates_t, s0, *extra_args)


def _chunk_cums(la_f, la_b, cbf_ref, cbb_ref, tt):
    lower = _tri(CHUNK, upper=False)
    upper = _tri(CHUNK, upper=True)
    for c in range(tt // CHUNK):
        rows = slice(c * CHUNK, (c + 1) * CHUNK)
        cbf_ref[rows, :] = _split_dot(lower, la_f[rows])
        cbb_ref[rows, :] = _split_dot(upper, la_b[rows])


def _log_sigmoid(x):
    return jnp.minimum(x, 0.0) - jnp.log(1.0 + jnp.exp(-jnp.abs(x)))


def _gla_prep_kernel(a1_ref, w2_ref, b_ref, cbf_ref, cbb_ref, *, tt):
    a1 = a1_ref[...]
    la = [_log_sigmoid(_dot(a1, w2_ref[d]) + b_ref[d]) * (1.0 / GLA_TAU) for d in range(2)]
    _chunk_cums(la[0], la[1], cbf_ref, cbb_ref, tt)


def _gla_prep(a1, w_a2, b_a, tok):
    m, r2 = a1.shape
    kd = w_a2.shape[2]
    tt = tok.seq
    tc = _tile(kd, 1024)
    zeros = jnp.zeros((GLA_RANK, kd), F32)
    w2 = jnp.stack([jnp.concatenate([w_a2[0], zeros]), jnp.concatenate([zeros, w_a2[1]])])
    out = jax.ShapeDtypeStruct((m, kd), F32)
    return pl.pallas_call(
        functools.partial(_gla_prep_kernel, tt=tt),
        grid=(m // tt, kd // tc),
        in_specs=[
            pl.BlockSpec((tt, r2), lambda i, j: (i, 0)),
            pl.BlockSpec((2, r2, tc), lambda i, j: (0, 0, j)),
            pl.BlockSpec((2, 1, tc), lambda i, j: (0, 0, j)),
        ],
        out_specs=[pl.BlockSpec((tt, tc), lambda i, j: (i, j))] * 2,
        out_shape=[out, out],
        compiler_params=_params("parallel", "parallel"),
        name="gla_prep",
    )(a1, w2, b_a.reshape(2, 1, kd))


def _hgrn_prep_kernel(ff_ref, fb_ref, lb_ref, kf_ref, kb_ref, cbf_ref, cbb_ref, *, layer, tt):
    logits = lb_ref[...]
    ex = jnp.exp(logits - jnp.max(logits, axis=0, keepdims=True))
    probs = ex / jnp.sum(ex, axis=0, keepdims=True)
    lb = jnp.sum(probs[1:layer + 1], axis=0, keepdims=True)
    la = []
    for f_ref, k_ref in ((ff_ref, kf_ref), (fb_ref, kb_ref)):
        fl = f_ref[...].astype(F32)
        sg = _sigmoid(fl)
        la.append(jnp.log(lb + (1.0 - lb) * sg))
        k_ref[...] = (1.0 - lb) * (1.0 - sg)
    _chunk_cums(la[0], la[1], cbf_ref, cbb_ref, tt)


def _hgrn_prep(proj, lb_logits, layer, d, tok):
    m = proj.shape[0]
    depth = lb_logits.shape[0]
    tt = tok.seq
    tc = _tile(d, 2048)
    nb = d // tc
    out = jax.ShapeDtypeStruct((m, d), F32)
    return pl.pallas_call(
        functools.partial(_hgrn_prep_kernel, layer=layer, tt=tt),
        grid=(m // tt, nb),
        in_specs=[
            pl.BlockSpec((tt, tc), lambda i, j: (i, 3 * nb + j)),
            pl.BlockSpec((tt, tc), lambda i, j: (i, 4 * nb + j)),
            pl.BlockSpec((depth, tc), lambda i, j: (0, j)),
        ],
        out_specs=[pl.BlockSpec((tt, tc), lambda i, j: (i, j))] * 4,
        out_shape=[out] * 4,
        compiler_params=_params("parallel", "parallel"),
        name="hgrn_prep",
    )(proj, proj, lb_logits)


def _gla_offdiag_pairs(c):
    pairs = []
    size = c
    while size > SUB:
        half = size // 2
        for start in range(0, c, size):
            pairs.append(((start + half, start + size), (start, start + half)))
        size = half
    return pairs


def _gla_operands(q, k, cb, rev):
    c, dk = q.shape
    zero_rows = lambda n: jnp.zeros((n, dk), BF16)

    def padded(x, rows):
        parts = ([zero_rows(rows[0])] if rows[0] else []) + [x] + ([zero_rows(c - rows[1])] if rows[1] < c else [])
        return jnp.concatenate(parts, axis=0) if len(parts) > 1 else x

    q_parts, k_parts = [], []
    for qr, kr in _gla_offdiag_pairs(c):
        if rev:
            qr, kr = kr, qr
        ref_row = qr[0] - 1 if not rev else qr[1]
        ref = cb[ref_row:ref_row + 1, :]
        qs, ks = slice(*qr), slice(*kr)
        q_parts.append(padded((q[qs] * jnp.exp(cb[qs] - ref)).astype(BF16), qr))
        k_parts.append(padded((k[ks] * jnp.exp(ref - cb[ks])).astype(BF16), kr))
    refs = []
    for blk in range(c // SUB):
        lo, hi = blk * SUB, (blk + 1) * SUB
        if not rev:
            ref = cb[lo - 1:lo, :] if blk > 0 else jnp.zeros_like(cb[0:1, :])
        else:
            ref = cb[hi:hi + 1, :] if hi < c else jnp.zeros_like(cb[0:1, :])
        refs.append(jnp.broadcast_to(ref, (SUB, dk)))
    ref_d = jnp.concatenate(refs, axis=0)
    q_diag = (q * jnp.exp(cb - ref_d)).astype(BF16)
    k_diag = (k * jnp.exp(jnp.minimum(ref_d - cb, EXP_CLAMP))).astype(BF16)
    return jnp.concatenate(q_parts, axis=1), jnp.concatenate(k_parts, axis=1), q_diag, k_diag


def _gla_diag_mask(c, rev):
    ri = lax.broadcasted_iota(jnp.int32, (c, c), 0)
    ci = lax.broadcasted_iota(jnp.int32, (c, c), 1)
    same_block = (ri // SUB) == (ci // SUB)
    return jnp.logical_and(same_block, (ri <= ci) if rev else (ri >= ci))


def _gla_scan_kernel(qf_ref, kf_ref, vf_ref, cf_ref, qb_ref, kb_ref, vb_ref, cb_ref, s0_ref,
                     of_ref, ob_ref, ns_ref, s_ref, *, tok, q_silu, scale, hb, dk, dv):
    n = pl.program_id(1)
    tt = tok.seq
    nch = tt // CHUNK
    is_ctx, dec_start = _seq_flags(n, tok)
    streams = [(d, h) for d in range(2) for h in range(hb)]

    @pl.when(is_ctx)
    def _():
        s_ref[...] = jnp.zeros_like(s_ref)

    @pl.when(dec_start)
    def _():
        for d, h in streams:
            s_ref[d, h] = s0_ref[0, 0, d, h].T

    dirs = ((qf_ref, kf_ref, vf_ref, cf_ref, of_ref), (qb_ref, kb_ref, vb_ref, cb_ref, ob_ref))
    ops = {}
    for d, (q_ref, k_ref, v_ref, c_ref, _) in enumerate(dirs):
        rev = d == 1
        for h in range(hb):
            q = q_ref[:, h * dk:(h + 1) * dk].astype(F32)
            if q_silu:
                q = _silu(q)
            q = q * scale
            k = k_ref[:, h * dk:(h + 1) * dk].astype(F32)
            v = v_ref[:, h * dv:(h + 1) * dv].astype(BF16)
            cb = c_ref[:, h * dk:(h + 1) * dk]
            for c in range(nch):
                rows = slice(c * CHUNK, (c + 1) * CHUNK)
                qc, kc, cbc = q[rows], k[rows], cb[rows]
                end = cbc[0:1, :] if rev else cbc[CHUNK - 1:CHUNK, :]
                ops[d, h, c] = dict(
                    intra=_gla_operands(qc, kc, cbc, rev), v=v[rows],
                    q_state=(qc * jnp.exp(cbc)).astype(BF16),
                    k_state=(kc * jnp.exp(end - cbc)).astype(BF16),
                    s_decay=jnp.exp(end))
    a_off = {key: _dot_nt(op["intra"][0], op["intra"][1]) for key, op in ops.items()}
    a_diag = {key: _dot_nt(op["intra"][2], op["intra"][3]) for key, op in ops.items()}
    masks = [_gla_diag_mask(CHUNK, rev) for rev in (False, True)]
    attn = {key: (a_off[key] + jnp.where(masks[key[0]], a_diag[key], 0.0)).astype(BF16) for key in ops}
    o_intra = {key: _dot(attn[key], op["v"]) for key, op in ops.items()}
    s_inc = {key: _dot_tn(op["v"], op["k_state"]) for key, op in ops.items()}

    state = {key: s_ref[key] for key in streams}
    for step in range(nch):
        for (d, h), s in state.items():
            c = step if d == 0 else nch - 1 - step
            dirs[d][4][c * CHUNK:(c + 1) * CHUNK, h * dv:(h + 1) * dv] = (
                o_intra[d, h, c] + _dot_nt(ops[d, h, c]["q_state"], s.astype(BF16))).astype(BF16)
            state[d, h] = ops[d, h, c]["s_decay"] * s + s_inc[d, h, c]
    for key, s in state.items():
        s_ref[key] = s

    @pl.when(is_ctx)
    def _():
        for d, h in streams:
            ns_ref[0, d, h] = s_ref[d, h].T


def _gla_scan(q_src, q_blk, kf_src, kf_blk, kb_src, kb_blk, v_src, v_blk, cb_f, cb_b, s0, j, tok,
              *, heads, dk, dv, q_silu, hb):
    m = q_src.shape[0]
    tt = tok.seq
    assert heads % hb == 0 and q_blk % hb == 0 and kf_blk % hb == 0 and kb_blk % hb == 0 and v_blk % hb == 0
    bwd = lambda n: _bwd_block(n, tok)
    fwd = lambda n: n

    def stream(blk, k_blk):
        return [
            pl.BlockSpec((tt, hb * dk), lambda h, n: (blk(n), q_blk // hb + h)),
            pl.BlockSpec((tt, hb * dk), lambda h, n: (blk(n), k_blk // hb + h)),
            pl.BlockSpec((tt, hb * dv), lambda h, n: (blk(n), v_blk // hb + h)),
            pl.BlockSpec((tt, hb * dk), lambda h, n: (blk(n), h)),
        ]

    n_batch = tok.n_ctx_blocks
    return pl.pallas_call(
        functools.partial(_gla_scan_kernel, tok=tok, q_silu=q_silu, scale=dk ** -0.5, hb=hb, dk=dk, dv=dv),
        grid=(heads // hb, tok.n_blocks),
        in_specs=stream(fwd, kf_blk) + stream(bwd, kb_blk) + [
            pl.BlockSpec((1, 1, 2, hb, dk, dv), lambda h, n: (_dec_batch(n, tok), j, 0, h, 0, 0)),
        ],
        out_specs=[
            pl.BlockSpec((tt, hb * dv), lambda h, n: (n, h)),
            pl.BlockSpec((tt, hb * dv), lambda h, n: (bwd(n), h)),
            pl.BlockSpec((1, 2, hb, dk, dv), lambda h, n: (_state_block(n, tok), 0, h, 0, 0)),
        ],
        out_shape=[
            jax.ShapeDtypeStruct((m, heads * dv), BF16),
            jax.ShapeDtypeStruct((m, heads * dv), BF16),
            jax.ShapeDtypeStruct((n_batch, 2, heads, dk, dv), F32),
        ],
        scratch_shapes=[pltpu.VMEM((2, hb, dv, dk), F32)],
        compiler_params=_params("parallel", "arbitrary"),
        name="gla_scan",
    )(q_src, kf_src, v_src, cb_f, q_src, kb_src, v_src, cb_b, s0)


def kernel(x_prompt, x_sample, c, state_deltanet, state_gla, state_hgrn, c_ctx, ada_w, ada_b, norm1, norm2, final_norm, w_up, w_down, dn_w_in, dn_conv, dn_A_log, dn_dt_bias, dn_norm, dn_w_out, gla_w_in, gla_w_a2, gla_b_a, gla_norm, gla_w_out, hgrn_w_in, hgrn_lb_logits, hgrn_norm, hgrn_w_out):
    batch, seq, d = x_prompt.shape
    dec_batch, dec_seq, _ = x_sample.shape
    depth = ada_w.shape[0]
    tok = Tokens(m_ctx=batch * seq, seq=seq, dec_batch=dec_batch, dec_seq=dec_seq)
    assert seq % CHUNK == 0 and dec_seq % seq == 0 and seq % GRID_W == 0

    dn_hv, dn_dk, dn_dv = state_deltanet.shape[3:]
    dn_vd = dn_hv * dn_dv
    n_conv = dn_conv.shape[2]
    dn_kd = (n_conv - dn_vd) // 2
    dn_hk = dn_kd // dn_dk
    gla_h, gla_dk, gla_dv = state_gla.shape[3:]
    gla_kd, gla_vd = gla_h * gla_dk, gla_h * gla_dv
    hg_h, hg_dk, hg_dv = state_hgrn.shape[3:]

    x = jnp.concatenate([x_prompt.reshape(tok.m_ctx, d), x_sample.reshape(dec_batch * dec_seq, d)], axis=0)
    n_rows = 1 + dec_batch
    pad = (-n_rows) % 8
    cond = jnp.concatenate([c_ctx[None, :], c, jnp.zeros((pad, d), F32)], axis=0)
    mod_all = _mod_table(cond, ada_w, ada_b).reshape(depth, n_rows + pad, 6, d)

    bf = lambda w: w.astype(BF16)
    w_up_b, w_down_b = bf(w_up), bf(w_down)
    dn_w_in_b, dn_w_out_b = bf(dn_w_in), bf(dn_w_out)
    gla_w_in_b, gla_w_out_b = bf(gla_w_in), bf(gla_w_out)
    hgrn_w_in_b, hgrn_w_out_b = bf(hgrn_w_in), bf(hgrn_w_out)

    new_states = ([], [], [])
    dn_states = None
    for i in range(depth):
        kind, j = i % N_MIXERS, i // N_MIXERS
        mod = mod_all[i]
        if kind == 0:
            n_main = n_conv + dn_vd
            proj = _proj(x, norm1[i], mod, dn_w_in_b, j, 0, n_main, tok)
            raw = _proj(x, norm1[i], mod, dn_w_in, j, n_main, dn_w_in.shape[2] - n_main, tok, precise=True)
            qkv = _dn_prep(proj, dn_conv[j], n_conv, dn_kd, dn_dk, tok)
            gates, gates_t = _dn_gates(raw, dn_A_log[j], dn_dt_bias[j], tok)
            o_f, o_b, dn_states = _dn_scan(qkv, gates, gates_t, state_deltanet, j, dn_states, tok,
                                           hk_n=dn_hk, hv=dn_hv, dk=dn_dk, dv=dn_dv)
            x = _out_proj(o_f, o_b, proj, n_conv // dn_vd, dn_norm[j], x, mod, dn_w_out_b, j, tok)
        elif kind == 1:
            n_main = 2 * gla_kd + 2 * gla_vd
            proj = _proj(x, norm1[i], mod, gla_w_in_b, j, 0, n_main, tok)
            w_a1 = gla_w_in[j:j + 1, :, n_main:]
            a1 = _proj(x, norm1[i], mod, w_a1, 0, 0, w_a1.shape[2], tok, precise=True)
            cb_f, cb_b = _gla_prep(a1, gla_w_a2[j], gla_b_a[j], tok)
            o_f, o_b, ns = _gla_scan(proj, 0, proj, gla_h, proj, gla_h, proj, (2 * gla_kd) // gla_dv,
                                     cb_f, cb_b, state_gla, j, tok,
                                     heads=gla_h, dk=gla_dk, dv=gla_dv, q_silu=False, hb=min(gla_h, 2))
            x = _out_proj(o_f, o_b, proj, (2 * gla_kd + gla_vd) // gla_vd, gla_norm[j], x, mod,
                          gla_w_out_b, j, tok)
        else:
            proj = _proj(x, norm1[i], mod, hgrn_w_in_b, j, 0, hgrn_w_in.shape[2], tok)
            k_f, k_b, cb_f, cb_b = _hgrn_prep(proj, hgrn_lb_logits, i, d, tok)
            o_f, o_b, ns = _gla_scan(proj, 0, k_f, 0, k_b, 0, proj, d // hg_dv, cb_f, cb_b, state_hgrn, j, tok,
                                     heads=hg_h, dk=hg_dk, dv=hg_dv, q_silu=True, hb=min(hg_h, 8))
            x = _out_proj(o_f, o_b, proj, 2, hgrn_norm[j], x, mod, hgrn_w_out_b, j, tok)
        if kind != 0:
            new_states[kind].append(ns)
        x = _mlp(x, norm2[i], mod, w_up_b, w_down_b, i, final_norm, tok, final=(i == depth - 1))

    y_ctx, y_dec = x
    y_prompt = y_ctx.reshape(batch, seq, d)
    y_sample = y_dec.reshape(dec_batch, dec_seq, d)
    stacked = [s[0][:, None] if len(s) == 1 else jnp.stack(s, axis=1) for s in new_states[1:]]
    return (y_prompt, y_sample, dn_states) + tuple(stacked)
```

```python
import functools
from typing import NamedTuple

import jax
import jax.numpy as jnp
from jax import lax
from jax.experimental import pallas as pl
from jax.experimental.pallas import tpu as pltpu

F32 = jnp.float32
BF16 = jnp.bfloat16

EPS = 1e-6
GRID_W = 64
CONV_K = 5
GLA_RANK = 16
GLA_TAU = 16.0
N_MIXERS = 3
CHUNK = 64
SUB = 16
EXP_CLAMP = 80.0
LANES = 128
SUBLANES = 8
PROLOGUE_ROWS = 256
V7X_MXU_COLS = 256
V7X_VMEM_LIMIT_BYTES = 56 * 1024 * 1024
HIGHEST = lax.Precision.HIGHEST


class Tokens(NamedTuple):
    m_ctx: int
    seq: int
    dec_batch: int
    dec_seq: int

    @property
    def m(self):
        return self.m_ctx + self.dec_batch * self.dec_seq

    @property
    def n_ctx_blocks(self):
        return self.m_ctx // self.seq

    @property
    def blocks_per_dec(self):
        return self.dec_seq // self.seq

    @property
    def n_blocks(self):
        return self.m // self.seq


def _tile(n, pref):
    t = min(n, pref)
    while n % t:
        t //= 2
    return t


def _token_tile(tok, pref):
    t = pref
    while tok.m_ctx % t or tok.dec_seq % t:
        t //= 2
    return t


def _params(*sem):
    return pltpu.CompilerParams(dimension_semantics=sem, vmem_limit_bytes=V7X_VMEM_LIMIT_BYTES)


def _mod_row(i, tm, tok):
    start = i * tm
    return jnp.where(start < tok.m_ctx, 0, 1 + (start - tok.m_ctx) // tok.dec_seq)


def _sigmoid(x):
    return 0.5 + 0.5 * jnp.tanh(0.5 * x)


def _silu(x):
    half = 0.5 * x
    return half + half * jnp.tanh(half)


def _softplus(x):
    return jnp.maximum(x, 0.0) + jnp.log(1.0 + jnp.exp(-jnp.abs(x)))


def _norm_mod(x, nw, shift, scale):
    ms = jnp.mean(x * x, axis=-1, keepdims=True)
    return x * lax.rsqrt(ms + EPS) * nw * (1.0 + scale) + shift


def _dot(a, b):
    return jnp.dot(a, b, preferred_element_type=F32)


def _dot_nt(a, b):
    return lax.dot_general(a, b, (((1,), (1,)), ((), ())), preferred_element_type=F32)


def _dot_tn(a, b):
    return lax.dot_general(a, b, (((0,), (0,)), ((), ())), preferred_element_type=F32)


def _split_dot(tri, x):
    hi = x.astype(BF16)
    lo = (x - hi.astype(F32)).astype(BF16)
    return _dot(tri, hi) + _dot(tri, lo)


def _split_matmul(a, b):
    ah = a.astype(BF16)
    al = (a - ah.astype(F32)).astype(BF16)
    bh = b.astype(BF16)
    bl = (b - bh.astype(F32)).astype(BF16)
    return _dot(ah, bh) + _dot(al, bh) + _dot(ah, bl)


def _tri(n, upper):
    r = lax.broadcasted_iota(jnp.int32, (n, n), 0)
    c = lax.broadcasted_iota(jnp.int32, (n, n), 1)
    return jnp.where((r <= c) if upper else (r >= c), 1.0, 0.0).astype(BF16)


def _mod_kernel(c_ref, w_ref, b_ref, o_ref):
    c = c_ref[...]
    s = _silu(c)
    o_ref[0] = jnp.dot(s, w_ref[0], precision=HIGHEST, preferred_element_type=F32) + b_ref[0]


def _mod_table(cond, ada_w, ada_b):
    depth, d, n = ada_w.shape
    rows = cond.shape[0]
    tn = _tile(n, 2048)
    return pl.pallas_call(
        _mod_kernel,
        grid=(depth, n // tn),
        in_specs=[
            pl.BlockSpec((rows, d), lambda l, j: (0, 0)),
            pl.BlockSpec((1, d, tn), lambda l, j: (l, 0, j)),
            pl.BlockSpec((1, 1, tn), lambda l, j: (l, 0, j)),
        ],
        out_specs=pl.BlockSpec((1, rows, tn), lambda l, j: (l, 0, j)),
        out_shape=jax.ShapeDtypeStruct((depth, rows, n), F32),
        compiler_params=_params("parallel", "parallel"),
        name="mod_table",
    )(cond, ada_w, ada_b.reshape(depth, 1, n))


def _proj_kernel(x_ref, nw_ref, mod_ref, w_ref, *rest, gated):
    if gated:
        wg_ref, o_ref, og_ref, h_ref = rest
    else:
        o_ref, h_ref = rest

    @pl.when(pl.program_id(1) == 0)
    def _():
        tm = x_ref.shape[0]
        step = min(tm, PROLOGUE_ROWS)
        for r in range(0, tm, step):
            rows = slice(r, r + step)
            h = _norm_mod(x_ref[rows, :], nw_ref[...], mod_ref[0, 0:1, :], mod_ref[0, 1:2, :])
            h_ref[rows, :] = h.astype(BF16)
            if gated:
                og_ref[rows, :] = _split_matmul(h, wg_ref[...])

    o_ref[...] = _dot(h_ref[...], w_ref[...]).astype(o_ref.dtype)


def _proj(x, nw, mod, w, layer, n, tok, *, gate_w=None, gate_layer=0, gate_col0=0, tm=1024, tn=2048):
    m, d = x.shape
    tm = _token_tile(tok, tm)
    tn = _tile(n, tn)
    gated = gate_w is not None
    in_specs = [
        pl.BlockSpec((tm, d), lambda i, j: (i, 0)),
        pl.BlockSpec((1, d), lambda i, j: (0, 0)),
        pl.BlockSpec((1, 6, d), lambda i, j: (_mod_row(i, tm, tok), 0, 0)),
        pl.BlockSpec((None, d, tn), lambda i, j: (layer, 0, j)),
    ]
    out_specs = [pl.BlockSpec((tm, tn), lambda i, j: (i, j))]
    out_shape = [jax.ShapeDtypeStruct((m, n), BF16)]
    args = [x, nw.reshape(1, d), mod, w]
    if gated:
        ng = gate_w.shape[2] - gate_col0
        assert gate_col0 % ng == 0
        in_specs.append(pl.BlockSpec((None, d, ng), lambda i, j: (gate_layer, 0, gate_col0 // ng)))
        out_specs.append(pl.BlockSpec((tm, ng), lambda i, j: (i, 0)))
        out_shape.append(jax.ShapeDtypeStruct((m, ng), F32))
        args.append(gate_w)
    out = pl.pallas_call(
        functools.partial(_proj_kernel, gated=gated),
        grid=(m // tm, n // tn),
        in_specs=in_specs,
        out_specs=out_specs,
        out_shape=out_shape,
        scratch_shapes=[pltpu.VMEM((tm, d), BF16)],
        compiler_params=_params("parallel", "arbitrary"),
        name="proj_gated" if gated else "proj",
    )(*args)
    return out if gated else out[0]


def _mlp_kernel(x_ref, nw_ref, mod_ref, wu_ref, wd_ref, fw_ref, *rest, n_ctx_tiles):
    out_refs, (h_ref, acc_ref) = rest[:-2], rest[-2:]
    final = n_ctx_tiles is not None
    i = pl.program_id(0)
    f = pl.program_id(1)

    @pl.when(f == 0)
    def _():
        h = _norm_mod(x_ref[...], nw_ref[...], mod_ref[0, 3:4, :], mod_ref[0, 4:5, :])
        h_ref[...] = h.astype(BF16)
        acc_ref[...] = jnp.zeros_like(acc_ref)

    a = jnp.maximum(_dot(h_ref[...], wu_ref[...]), 0.0)
    acc_ref[...] += _dot((a * a).astype(BF16), wd_ref[...])

    last = f == pl.num_programs(1) - 1

    def result():
        y = x_ref[...] + mod_ref[0, 5:6, :] * acc_ref[...]
        if final:
            ms = jnp.mean(y * y, axis=-1, keepdims=True)
            y = y * lax.rsqrt(ms + EPS) * fw_ref[...]
        return y

    if not final:
        @pl.when(last)
        def _():
            out_refs[0][...] = result()
    else:
        @pl.when(jnp.logical_and(last, i < n_ctx_tiles))
        def _():
            out_refs[0][...] = result()

        @pl.when(jnp.logical_and(last, i >= n_ctx_tiles))
        def _():
            out_refs[1][...] = result()


def _mlp(x, nw, mod, w_up, w_down, layer, final_w, tok, *, final, tm=512, tf=1024):
    m, d = x.shape
    ff = w_up.shape[2]
    tm = _token_tile(tok, tm)
    tf = _tile(ff, tf)
    n_ctx_tiles = tok.m_ctx // tm
    if final:
        out_specs = [pl.BlockSpec((tm, d), lambda i, f: (jnp.minimum(i, n_ctx_tiles - 1), 0)),
                     pl.BlockSpec((tm, d), lambda i, f: (jnp.maximum(i - n_ctx_tiles, 0), 0))]
        out_shape = [jax.ShapeDtypeStruct((tok.m_ctx, d), F32), jax.ShapeDtypeStruct((m - tok.m_ctx, d), F32)]
    else:
        out_specs = pl.BlockSpec((tm, d), lambda i, f: (i, 0))
        out_shape = jax.ShapeDtypeStruct((m, d), F32)
    return pl.pallas_call(
        functools.partial(_mlp_kernel, n_ctx_tiles=n_ctx_tiles if final else None),
        grid=(m // tm, ff // tf),
        in_specs=[
            pl.BlockSpec((tm, d), lambda i, f: (i, 0)),
            pl.BlockSpec((1, d), lambda i, f: (0, 0)),
            pl.BlockSpec((1, 6, d), lambda i, f: (_mod_row(i, tm, tok), 0, 0)),
            pl.BlockSpec((None, d, tf), lambda i, f: (layer, 0, f)),
            pl.BlockSpec((None, tf, d), lambda i, f: (layer, f, 0)),
            pl.BlockSpec((1, d), lambda i, f: (0, 0)),
        ],
        out_specs=out_specs,
        out_shape=out_shape,
        scratch_shapes=[pltpu.VMEM((tm, d), BF16), pltpu.VMEM((tm, d), F32)],
        compiler_params=_params("arbitrary" if final else "parallel", "arbitrary"),
        name="mlp_final" if final else "mlp",
    )(x, nw.reshape(1, d), mod, w_up, w_down, final_w.reshape(1, d))


def _out_kernel(of_ref, ob_ref, g_ref, nw_ref, x_ref, mod_ref, w_ref, o_ref, y0_ref, y1_ref, *, n_heads, dh):
    i = pl.program_id(0)
    d = w_ref.shape[1]
    n_chunks = d // V7X_MXU_COLS
    heads_per_chunk = -(-n_heads // n_chunks)

    @pl.when(i == 0)
    def _():
        y1_ref[...] = jnp.zeros_like(y1_ref)

    def body(y_write, y_read):
        ones = jnp.full((dh, LANES), 1.0 / dh, BF16) if dh & (dh - 1) == 0 else None
        for c in range(n_chunks):
            cols = slice(c * V7X_MXU_COLS, (c + 1) * V7X_MXU_COLS)
            o_ref[:, cols] = x_ref[:, cols] + mod_ref[0, 2:3, cols] * _dot(y_read[...], w_ref[:, cols])
            for h in range(c * heads_per_chunk, min((c + 1) * heads_per_chunk, n_heads)):
                sl = slice(h * dh, (h + 1) * dh)
                o = of_ref[:, sl].astype(F32) + ob_ref[:, sl].astype(F32)
                if ones is None:
                    ms = _dot((o * o).astype(BF16), jnp.ones((dh, LANES), BF16)) * (1.0 / dh)
                else:
                    ms = _dot((o * o).astype(BF16), ones)
                inv = lax.rsqrt(ms + EPS)
                y = o * nw_ref[...] * _silu(g_ref[:, sl].astype(F32))
                for k in range(dh // LANES):
                    lo = h * dh + k * LANES
                    y_write[:, lo:lo + LANES] = (y[:, k * LANES:(k + 1) * LANES] * inv).astype(BF16)

    @pl.when(i % 2 == 0)
    def _():
        body(y0_ref, y1_ref)

    @pl.when(i % 2 == 1)
    def _():
        body(y1_ref, y0_ref)


def _out_proj(o_f, o_b, gate_src, gate_blk, norm_w, x, mod, w_out, layer, tok, *, tm=256):
    m, vd = o_f.shape
    d = x.shape[1]
    dh = norm_w.shape[0]
    tm = _token_tile(tok, tm)
    n_tiles = m // tm
    assert d % V7X_MXU_COLS == 0 and dh % LANES == 0
    cur = lambda i: jnp.minimum(i, n_tiles - 1)
    prev = lambda i: jnp.maximum(i - 1, 0)
    return pl.pallas_call(
        functools.partial(_out_kernel, n_heads=vd // dh, dh=dh),
        grid=(n_tiles + 1,),
        in_specs=[
            pl.BlockSpec((tm, vd), lambda i: (cur(i), 0)),
            pl.BlockSpec((tm, vd), lambda i: (cur(i), 0)),
            pl.BlockSpec((tm, vd), lambda i: (cur(i), gate_blk)),
            pl.BlockSpec((1, dh), lambda i: (0, 0)),
            pl.BlockSpec((tm, d), lambda i: (prev(i), 0)),
            pl.BlockSpec((1, 6, d), lambda i: (_mod_row(prev(i), tm, tok), 0, 0)),
            pl.BlockSpec((None, vd, d), lambda i: (layer, 0, 0), pipeline_mode=pl.Buffered(1)),
        ],
        out_specs=pl.BlockSpec((tm, d), lambda i: (prev(i), 0)),
        out_shape=jax.ShapeDtypeStruct((m, d), F32),
        scratch_shapes=[pltpu.VMEM((tm, vd), BF16), pltpu.VMEM((tm, vd), BF16)],
        compiler_params=_params("arbitrary"),
        name="out_proj",
    )(o_f, o_b, gate_src, norm_w.reshape(1, dh), x, mod, w_out)


def _bwd_block(n, tok):
    nc, per = tok.n_ctx_blocks, tok.blocks_per_dec
    r = jnp.maximum(n - nc, 0)
    return jnp.where(n < nc, n, nc + (r // per) * per + (per - 1 - r % per))


def _dec_batch(n, tok):
    return jnp.maximum(n - tok.n_ctx_blocks, 0) // tok.blocks_per_dec


def _state_block(n, tok):
    return jnp.minimum(n, tok.n_ctx_blocks - 1)


def _seq_flags(n, tok):
    nc, per = tok.n_ctx_blocks, tok.blocks_per_dec
    is_ctx = n < nc
    r = jnp.maximum(n - nc, 0) % per
    return is_ctx, jnp.logical_and(jnp.logical_not(is_ctx), r == 0)


def _dn_prep_kernel(x_ref, shift_ref, w_ref, o_ref, *, n_q_blocks, dk):
    j = pl.program_id(1)
    x = x_ref[...]
    tc = x.shape[1]
    acc = None
    for s in range(CONV_K):
        xs = x.astype(F32) if s == CONV_K // 2 else _dot(shift_ref[0, s], x)
        term = xs * w_ref[s:s + 1, :]
        acc = term if acc is None else acc + term
    y = _silu(acc)

    def normed(scale):
        ones = jnp.ones((dk, dk), BF16)
        for g in range(tc // dk):
            blk = y[:, g * dk:(g + 1) * dk]
            ss = _dot((blk * blk).astype(BF16), ones)
            o_ref[:, g * dk:(g + 1) * dk] = (blk * (lax.rsqrt(ss + EPS) * scale)).astype(o_ref.dtype)

    @pl.when(j < n_q_blocks)
    def _():
        normed(dk ** -0.5)

    @pl.when(jnp.logical_and(j >= n_q_blocks, j < 2 * n_q_blocks))
    def _():
        normed(1.0)

    @pl.when(j >= 2 * n_q_blocks)
    def _():
        o_ref[...] = y.astype(o_ref.dtype)


def _conv_shift_matrices(tt):
    t = jnp.arange(tt)[:, None]
    u = jnp.arange(tt)[None, :]
    mats = []
    for seg in (tt, GRID_W):
        same = (t // seg) == (u // seg)
        mats.append(jnp.stack([jnp.logical_and(u == t + s - CONV_K // 2, same) for s in range(CONV_K)]))
    return jnp.stack(mats).astype(BF16)


def _dn_prep(proj, conv_w, n_conv, kd, dk, tok):
    m = proj.shape[0]
    tt = tok.seq
    tc = _tile(kd, 2048)
    n_ctx = tok.n_ctx_blocks
    return pl.pallas_call(
        functools.partial(_dn_prep_kernel, n_q_blocks=kd // tc, dk=dk),
        grid=(m // tt, n_conv // tc),
        in_specs=[
            pl.BlockSpec((tt, tc), lambda i, j: (i, j)),
            pl.BlockSpec((1, CONV_K, tt, tt), lambda i, j: (jnp.where(i < n_ctx, 0, 1), 0, 0, 0)),
            pl.BlockSpec((CONV_K, tc), lambda i, j: (0, j)),
        ],
        out_specs=pl.BlockSpec((tt, tc), lambda i, j: (i, j)),
        out_shape=jax.ShapeDtypeStruct((m, n_conv), BF16),
        compiler_params=_params("parallel", "parallel"),
        name="dn_prep",
    )(proj, _conv_shift_matrices(tt), conv_w)


def _dn_gate_kernel(x_ref, a_ref, bias_ref, o_ref, gt_ref, *, hv, tt):
    x = x_ref[...]
    lanes = x.shape[1]
    lane = lax.broadcasted_iota(jnp.int32, (1, lanes), 1)
    is_fwd = lane < 3 * hv
    g = -a_ref[...] * _softplus(x + bias_ref[...])
    beta = pltpu.roll(_sigmoid(x), 2 * hv, 1)
    lower = _tri(CHUNK, upper=False)
    upper = _tri(CHUNK, upper=True)
    for c in range(tt // CHUNK):
        rows = slice(c * CHUNK, (c + 1) * CHUNK)
        gch = g[rows]
        gc = jnp.where(is_fwd, _split_dot(lower, gch), _split_dot(upper, gch))
        g_end = jnp.where(is_fwd, gc[CHUNK - 1:CHUNK, :], gc[0:1, :])
        eg = jnp.exp(gc)
        o_ref[0, rows, :] = beta[rows]
        o_ref[1, rows, :] = gc
        o_ref[2, rows, :] = eg
        o_ref[3, rows, :] = jnp.exp(g_end - gc)
        o_ref[4, rows, :] = beta[rows] * eg
    gt_ref[...] = o_ref[1].T


def _dn_gates(raw, a_log, dt_bias, tok):
    m, lanes = raw.shape
    hv = lanes // 4
    tt = tok.seq
    zeros = jnp.zeros((2 * hv,), F32)
    a_full = jnp.concatenate([zeros, jnp.exp(a_log).reshape(-1)]).reshape(1, lanes)
    bias_full = jnp.concatenate([zeros, dt_bias.reshape(-1)]).reshape(1, lanes)
    return pl.pallas_call(
        functools.partial(_dn_gate_kernel, hv=hv, tt=tt),
        grid=(m // tt,),
        in_specs=[
            pl.BlockSpec((tt, lanes), lambda i: (i, 0)),
            pl.BlockSpec((1, lanes), lambda i: (0, 0)),
            pl.BlockSpec((1, lanes), lambda i: (0, 0)),
        ],
        out_specs=[
            pl.BlockSpec((5, tt, lanes), lambda i: (0, i, 0)),
            pl.BlockSpec((lanes, tt), lambda i: (0, i)),
        ],
        out_shape=[jax.ShapeDtypeStruct((5, m, lanes), F32), jax.ShapeDtypeStruct((lanes, m), F32)],
        compiler_params=_params("parallel"),
        name="dn_gates",
    )(raw, a_full, bias_full)


def _dn_scan_kernel(qf_ref, kf_ref, vf_ref, gf_ref, gtf_ref, qb_ref, kb_ref, vb_ref, gb_ref, gtb_ref, s0_ref,
                    of_ref, ob_ref, ns_ref, s_ref, u0_ref, wq_ref, attn_ref, kd_ref, ge_ref, *, tok, hv, dv, hkb):
    hk0 = pl.program_id(0) * hkb
    n = pl.program_id(1)
    tt = tok.seq
    nch = tt // CHUNK
    dk = qf_ref.shape[1] // hkb
    assert dv == 2 * CHUNK and CHUNK == 4 * SUB
    has_prev = n > 0
    is_ctx, dec_start = _seq_flags(jnp.maximum(n - 1, 0), tok)
    wslot = n % 2
    rslot = 1 - wslot
    scratch = (u0_ref, wq_ref, attn_ref, kd_ref, ge_ref)

    @pl.when(n == 0)
    def _():
        s_ref[...] = jnp.zeros_like(s_ref)
        for ref in scratch:
            ref[1] = jnp.zeros(ref.shape[1:], ref.dtype)

    @pl.when(jnp.logical_and(has_prev, is_ctx))
    def _():
        s_ref[...] = jnp.zeros_like(s_ref)

    @pl.when(jnp.logical_and(has_prev, dec_start))
    def _():
        s_ref[...] = s0_ref[0, 0]

    bf = lambda x: x.astype(BF16)
    lanes = gf_ref.shape[2]
    lane_iota = lax.broadcasted_iota(jnp.int32, (1, lanes), 1)
    ri = lax.broadcasted_iota(jnp.int32, (CHUNK, 2 * CHUNK), 0)
    li = lax.broadcasted_iota(jnp.int32, (CHUNK, 2 * CHUNK), 1)
    left = li < CHUNK
    ci = li % CHUNK
    same = [(ri // b) == (ci // b) for b in (SUB, 2 * SUB)]
    zero_b = jnp.zeros((CHUNK, 2 * CHUNK), BF16)

    def block_diag(xb):
        return jnp.concatenate([jnp.where(left, xb, zero_b), jnp.where(left, zero_b, xb)], axis=0)

    def pair_dot(a, b):
        return _dot(bf(a), block_diag(bf(b)))

    dirs = ((qf_ref, kf_ref, vf_ref, gf_ref, gtf_ref, of_ref), (qb_ref, kb_ref, vb_ref, gb_ref, gtb_ref, ob_ref))

    def prepare():
        ch = {}
        for d, kh in [(d, kh) for d in range(2) for kh in range(hkb)]:
            q_ref, k_ref, v_ref, g_ref, gt_ref, _ = dirs[d]
            incl = (ri >= ci) if d == 0 else (ri <= ci)
            strict = (ri > ci) if d == 0 else (ri < ci)
            end_row = CHUNK - 1 if d == 0 else 0
            q = q_ref[:, kh * dk:(kh + 1) * dk]
            k = k_ref[:, kh * dk:(kh + 1) * dk]
            kf32 = k.astype(F32)
            qf32 = q.astype(F32)
            cols, g_rows, vs = [], [], []
            for e in range(2):
                a_lane = 2 * hv + d * hv + 2 * (hk0 + kh) + e
                sel = lane_iota == a_lane
                cols.append([jnp.sum(jnp.where(sel, g_ref[i], 0.0), axis=1, keepdims=True) for i in range(5)])
                g_rows.append(gt_ref[pl.ds(a_lane, 1), :])
                vs.append(v_ref[:, (2 * kh + e) * dv:(2 * kh + e + 1) * dv].astype(F32))
            for c in range(nch):
                rows = slice(c * CHUNK, (c + 1) * CHUNK)
                pair = lambda i: jnp.where(left, cols[0][i][rows], cols[1][i][rows])
                k2 = jnp.concatenate([k[rows], k[rows]], axis=0)
                kk = _dot_nt(k[rows], k2)
                qk = _dot_nt(q[rows], k2)
                g_row = jnp.concatenate([g_rows[0][:, rows], g_rows[1][:, rows]], axis=1)
                diff = pair(1) - g_row
                decay = jnp.where(incl, jnp.exp(jnp.where(incl, diff, 0.0)), 0.0)
                p = jnp.where(strict, -(pair(0) * kk * decay), 0.0)
                attn_ref[wslot, d, kh, c] = bf(qk * decay)
                for e in range(2):
                    beta_c, _, eg_c, ekd_c, beg_c = cols[e]
                    qg = qf32[rows] * eg_c[rows]
                    kd_ref[wslot, d, 2 * kh + e, c] = bf(kf32[rows] * ekd_c[rows])
                    wq_ref[wslot, d, 2 * kh + e, c, CHUNK:, :] = bf(qg)
                    g_end = eg_c[c * CHUNK + end_row:c * CHUNK + end_row + 1, :]
                    ge_ref[wslot, d, 2 * kh + e, c] = jnp.broadcast_to(g_end, ge_ref.shape[4:])
                ch[kh, d, c] = dict(
                    pd=jnp.where(same[0], p, 0.0),
                    p32=jnp.where(jnp.logical_and(same[1], jnp.logical_not(same[0])), p, 0.0),
                    p64=jnp.where(same[1], 0.0, p),
                    r=[jnp.concatenate([cols[e][0][rows] * vs[e][rows], cols[e][4][rows] * kf32[rows]], axis=1)
                       for e in range(2)])
            yield
        for c in ch.values():
            c["q"] = pair_dot(c["pd"], c["pd"])
            c["xm"] = c["pd"]
        yield
        for it in range(2):
            for c in ch.values():
                q, xm = c["q"], c["xm"]
                both = _dot(bf(q), jnp.concatenate([block_diag(bf(xm)), block_diag(bf(q))], axis=1))
                c["xm"] = xm + q + both[:, :2 * CHUNK]
                c["q"] = both[:, 2 * CHUNK:]
            yield
        for c in ch.values():
            q, xm = c.pop("q"), c["xm"]
            c["xm"] = xm + q + pair_dot(q, xm)
        yield
        for level in ("p32", "p64"):
            for c in ch.values():
                c["y"] = c[level] + pair_dot(c[level], c["xm"])
            yield
            for c in ch.values():
                xm, y = c["xm"], c.pop("y")
                c["xm"] = xm + y + pair_dot(xm, y)
            yield
        zero_r = jnp.zeros((CHUNK, 2 * dv), BF16)
        for (kh, d, cidx), c in ch.items():
            r0, r1 = c["r"]
            rhs = jnp.concatenate([jnp.concatenate([bf(r0), zero_r], axis=1),
                                   jnp.concatenate([zero_r, bf(r1)], axis=1)], axis=0)
            big = _dot(bf(c["xm"]), rhs)
            sol = [r0 + big[:, :2 * dv], r1 + big[:, 2 * dv:]]
            u0_ref[wslot, d, kh, cidx] = jnp.concatenate([sol[0][:, :dv], sol[1][:, :dv]], axis=1)
            for e in range(2):
                wq_ref[wslot, d, 2 * kh + e, cidx, :CHUNK, :] = bf(sol[e][:, dv:])
        yield

    def recur():
        state = {(d, e): s_ref[d, e] for d in range(2) for e in range(2 * hkb)}
        zero_u = jnp.zeros((CHUNK, dv), BF16)
        for step in range(nch):
            chunk = lambda d: step if d == 0 else nch - 1 - step
            ws = {(d, e): _dot(wq_ref[rslot, d, e, chunk(d)], bf(s)) for (d, e), s in state.items()}
            yield
            for d, kh in [(d, kh) for d in range(2) for kh in range(hkb)]:
                c = chunk(d)
                u0 = u0_ref[rslot, d, kh, c]
                ev = [2 * kh, 2 * kh + 1]
                ub = [bf(u0[:, e * dv:(e + 1) * dv] - ws[d, ev[e]][:CHUNK]) for e in range(2)]
                u2 = jnp.concatenate([jnp.concatenate([ub[0], zero_u], axis=1),
                                      jnp.concatenate([zero_u, ub[1]], axis=1)], axis=0)
                o = (_dot(attn_ref[rslot, d, kh, c], u2)
                     + jnp.concatenate([ws[d, ev[0]][CHUNK:], ws[d, ev[1]][CHUNK:]], axis=1))
                dirs[d][5][c * CHUNK:(c + 1) * CHUNK, 2 * kh * dv:2 * (kh + 1) * dv] = bf(o)
                for e in range(2):
                    state[d, ev[e]] = (ge_ref[rslot, d, ev[e], c][0:1, :] * state[d, ev[e]]
                                       + _dot_tn(kd_ref[rslot, d, ev[e], c], ub[e]))
            yield
        for (d, e), s in state.items():
            s_ref[d, e] = s
        yield

    halves = [recur(), prepare()]
    while halves:
        for g in list(halves):
            if next(g, "done") == "done":
                halves.remove(g)

    @pl.when(jnp.logical_and(has_prev, is_ctx))
    def _():
        ns_ref[0, 0] = s_ref[...]


def _dn_scan_kernel_aliased(*refs, **kw):
    n_in = 11
    return _dn_scan_kernel(*refs[:n_in], *refs[n_in + 1:], **kw)


def _dn_scan(qkv, gates, gates_t, s0, j, new_states, tok, *, hk_n, hv, dk, dv):
    m = qkv.shape[0]
    tt = tok.seq
    rep = hv // hk_n
    assert rep == 2
    lanes = gates.shape[2]
    kd = hk_n * dk
    nb = tok.n_blocks
    nch = tt // CHUNK
    fwd = lambda n: jnp.minimum(n, nb - 1)
    bwd = lambda n: _bwd_block(jnp.minimum(n, nb - 1), tok)
    prev = lambda n: jnp.maximum(n - 1, 0)

    hkb = max(b for b in (4, 2, 1) if hk_n % b == 0)
    hvb = hkb * rep

    def stream(blk):
        return [
            pl.BlockSpec((tt, hkb * dk), lambda h, n: (blk(n), h)),
            pl.BlockSpec((tt, hkb * dk), lambda h, n: (blk(n), hk_n // hkb + h)),
            pl.BlockSpec((tt, hvb * dv), lambda h, n: (blk(n), (2 * kd) // (hvb * dv) + h)),
            pl.BlockSpec((5, tt, lanes), lambda h, n: (0, blk(n), 0)),
            pl.BlockSpec((lanes, tt), lambda h, n: (0, blk(n))),
        ]

    n_batch = tok.n_ctx_blocks
    n_layers = s0.shape[1]
    aliased = new_states is not None
    extra_in = [pl.BlockSpec(memory_space=pl.ANY)] if aliased else []
    extra_args = (new_states,) if aliased else ()
    return pl.pallas_call(
        functools.partial(_dn_scan_kernel_aliased if aliased else _dn_scan_kernel, tok=tok, hv=hv, dv=dv, hkb=hkb),
        grid=(hk_n // hkb, nb + 1),
        in_specs=stream(fwd) + stream(bwd) + [
            pl.BlockSpec((1, 1, 2, hvb, dk, dv), lambda h, n: (_dec_batch(prev(n), tok), j, 0, h, 0, 0)),
        ] + extra_in,
        out_specs=[
            pl.BlockSpec((tt, hvb * dv), lambda h, n: (prev(n), h)),
            pl.BlockSpec((tt, hvb * dv), lambda h, n: (_bwd_block(prev(n), tok), h)),
            pl.BlockSpec((1, 1, 2, hvb, dk, dv), lambda h, n: (_state_block(prev(n), tok), j, 0, h, 0, 0)),
        ],
        out_shape=[
            jax.ShapeDtypeStruct((m, hv * dv), BF16),
            jax.ShapeDtypeStruct((m, hv * dv), BF16),
            jax.ShapeDtypeStruct((n_batch, n_layers, 2, hv, dk, dv), F32),
        ],
        input_output_aliases={11: 2} if aliased else {},
        scratch_shapes=[
            pltpu.VMEM((2, hvb, dk, dv), F32),
            pltpu.VMEM((2, 2, hkb, nch, CHUNK, rep * dv), F32),
            pltpu.VMEM((2, 2, hvb, nch, 2 * CHUNK, dk), BF16),
            pltpu.VMEM((2, 2, hkb, nch, CHUNK, rep * CHUNK), BF16),
            pltpu.VMEM((2, 2, hvb, nch, CHUNK, dk), BF16),
            pltpu.VMEM((2, 2, hvb, nch, SUBLANES, dv), F32),
        ],
        compiler_params=_params("parallel", "arbitrary"),
        name="dn_scan",
    )(qkv, qkv, qkv, gates, gates_t, qkv, qkv, qkv, gates, gates_t, s0, *extra_args)


def _chunk_cums(la_f, la_b, cbf_ref, cbb_ref, tt):
    lower = _tri(CHUNK, upper=False)
    upper = _tri(CHUNK, upper=True)
    for c in range(tt // CHUNK):
        rows = slice(c * CHUNK, (c + 1) * CHUNK)
        cbf_ref[rows, :] = _split_dot(lower, la_f[rows])
        cbb_ref[rows, :] = _split_dot(upper, la_b[rows])


def _log_sigmoid(x):
    return jnp.minimum(x, 0.0) - jnp.log(1.0 + jnp.exp(-jnp.abs(x)))


def _gla_prep_kernel(a1_ref, w2_ref, b_ref, cbf_ref, cbb_ref, *, tt):
    a1 = a1_ref[...]
    la = [_log_sigmoid(_dot(a1, w2_ref[d]) + b_ref[d]) * (1.0 / GLA_TAU) for d in range(2)]
    _chunk_cums(la[0], la[1], cbf_ref, cbb_ref, tt)


def _gla_prep(a1, w_a2, b_a, tok):
    m, r2 = a1.shape
    kd = w_a2.shape[2]
    tt = tok.seq
    tc = _tile(kd, 1024)
    zeros = jnp.zeros((GLA_RANK, kd), F32)
    w2 = jnp.stack([jnp.concatenate([w_a2[0], zeros]), jnp.concatenate([zeros, w_a2[1]])])
    out = jax.ShapeDtypeStruct((m, kd), F32)
    return pl.pallas_call(
        functools.partial(_gla_prep_kernel, tt=tt),
        grid=(m // tt, kd // tc),
        in_specs=[
            pl.BlockSpec((tt, r2), lambda i, j: (i, 0)),
            pl.BlockSpec((2, r2, tc), lambda i, j: (0, 0, j)),
            pl.BlockSpec((2, 1, tc), lambda i, j: (0, 0, j)),
        ],
        out_specs=[pl.BlockSpec((tt, tc), lambda i, j: (i, j))] * 2,
        out_shape=[out, out],
        compiler_params=_params("parallel", "parallel"),
        name="gla_prep",
    )(a1, w2, b_a.reshape(2, 1, kd))


def _hgrn_prep_kernel(ff_ref, fb_ref, lb_ref, kf_ref, kb_ref, cbf_ref, cbb_ref, *, layer, tt):
    logits = lb_ref[...]
    ex = jnp.exp(logits - jnp.max(logits, axis=0, keepdims=True))
    probs = ex / jnp.sum(ex, axis=0, keepdims=True)
    lb = jnp.sum(probs[1:layer + 1], axis=0, keepdims=True)
    la = []
    for f_ref, k_ref in ((ff_ref, kf_ref), (fb_ref, kb_ref)):
        fl = f_ref[...].astype(F32)
        sg = _sigmoid(fl)
        la.append(jnp.log(lb + (1.0 - lb) * sg))
        k_ref[...] = (1.0 - lb) * (1.0 - sg)
    _chunk_cums(la[0], la[1], cbf_ref, cbb_ref, tt)


def _hgrn_prep(proj, lb_logits, layer, d, tok):
    m = proj.shape[0]
    depth = lb_logits.shape[0]
    tt = tok.seq
    tc = _tile(d, 2048)
    nb = d // tc
    out = jax.ShapeDtypeStruct((m, d), F32)
    return pl.pallas_call(
        functools.partial(_hgrn_prep_kernel, layer=layer, tt=tt),
        grid=(m // tt, nb),
        in_specs=[
            pl.BlockSpec((tt, tc), lambda i, j: (i, 3 * nb + j)),
            pl.BlockSpec((tt, tc), lambda i, j: (i, 4 * nb + j)),
            pl.BlockSpec((depth, tc), lambda i, j: (0, j)),
        ],
        out_specs=[pl.BlockSpec((tt, tc), lambda i, j: (i, j))] * 4,
        out_shape=[out] * 4,
        compiler_params=_params("parallel", "parallel"),
        name="hgrn_prep",
    )(proj, proj, lb_logits)


def _gla_offdiag_pairs(c):
    pairs = []
    size = c
    while size > SUB:
        half = size // 2
        for start in range(0, c, size):
            pairs.append(((start + half, start + size), (start, start + half)))
        size = half
    return pairs


def _gla_operands(q, k, cb, rev):
    c, dk = q.shape
    zero_rows = lambda n: jnp.zeros((n, dk), BF16)

    def padded(x, rows):
        parts = ([zero_rows(rows[0])] if rows[0] else []) + [x] + ([zero_rows(c - rows[1])] if rows[1] < c else [])
        return jnp.concatenate(parts, axis=0) if len(parts) > 1 else x

    q_parts, k_parts = [], []
    for qr, kr in _gla_offdiag_pairs(c):
        if rev:
            qr, kr = kr, qr
        ref_row = qr[0] - 1 if not rev else qr[1]
        ref = cb[ref_row:ref_row + 1, :]
        qs, ks = slice(*qr), slice(*kr)
        q_parts.append(padded((q[qs] * jnp.exp(cb[qs] - ref)).astype(BF16), qr))
        k_parts.append(padded((k[ks] * jnp.exp(ref - cb[ks])).astype(BF16), kr))
    refs = []
    for blk in range(c // SUB):
        lo, hi = blk * SUB, (blk + 1) * SUB
        if not rev:
            ref = cb[lo - 1:lo, :] if blk > 0 else jnp.zeros_like(cb[0:1, :])
        else:
            ref = cb[hi:hi + 1, :] if hi < c else jnp.zeros_like(cb[0:1, :])
        refs.append(jnp.broadcast_to(ref, (SUB, dk)))
    ref_d = jnp.concatenate(refs, axis=0)
    q_diag = (q * jnp.exp(cb - ref_d)).astype(BF16)
    k_diag = (k * jnp.exp(jnp.minimum(ref_d - cb, EXP_CLAMP))).astype(BF16)
    return jnp.concatenate(q_parts, axis=1), jnp.concatenate(k_parts, axis=1), q_diag, k_diag


def _gla_diag_mask(c, rev):
    ri = lax.broadcasted_iota(jnp.int32, (c, c), 0)
    ci = lax.broadcasted_iota(jnp.int32, (c, c), 1)
    same_block = (ri // SUB) == (ci // SUB)
    return jnp.logical_and(same_block, (ri <= ci) if rev else (ri >= ci))


def _gla_scan_kernel(qf_ref, kf_ref, vf_ref, cf_ref, qb_ref, kb_ref, vb_ref, cb_ref, s0_ref,
                     of_ref, ob_ref, ns_ref, s_ref, *, tok, q_silu, scale, hb, dk, dv):
    n = pl.program_id(1)
    tt = tok.seq
    nch = tt // CHUNK
    is_ctx, dec_start = _seq_flags(n, tok)
    streams = [(d, h) for d in range(2) for h in range(hb)]

    @pl.when(is_ctx)
    def _():
        s_ref[...] = jnp.zeros_like(s_ref)

    @pl.when(dec_start)
    def _():
        for d, h in streams:
            s_ref[d, h] = s0_ref[0, 0, d, h].T

    dirs = ((qf_ref, kf_ref, vf_ref, cf_ref, of_ref), (qb_ref, kb_ref, vb_ref, cb_ref, ob_ref))
    ops = {}
    for d, (q_ref, k_ref, v_ref, c_ref, _) in enumerate(dirs):
        rev = d == 1
        for h in range(hb):
            q = q_ref[:, h * dk:(h + 1) * dk].astype(F32)
            if q_silu:
                q = _silu(q)
            q = q * scale
            k = k_ref[:, h * dk:(h + 1) * dk].astype(F32)
            v = v_ref[:, h * dv:(h + 1) * dv].astype(BF16)
            cb = c_ref[:, h * dk:(h + 1) * dk]
            for c in range(nch):
                rows = slice(c * CHUNK, (c + 1) * CHUNK)
                qc, kc, cbc = q[rows], k[rows], cb[rows]
                end = cbc[0:1, :] if rev else cbc[CHUNK - 1:CHUNK, :]
                ops[d, h, c] = dict(
                    intra=_gla_operands(qc, kc, cbc, rev), v=v[rows],
                    q_state=(qc * jnp.exp(cbc)).astype(BF16),
                    k_state=(kc * jnp.exp(end - cbc)).astype(BF16),
                    s_decay=jnp.exp(end))
    a_off = {key: _dot_nt(op["intra"][0], op["intra"][1]) for key, op in ops.items()}
    a_diag = {key: _dot_nt(op["intra"][2], op["intra"][3]) for key, op in ops.items()}
    masks = [_gla_diag_mask(CHUNK, rev) for rev in (False, True)]
    attn = {key: (a_off[key] + jnp.where(masks[key[0]], a_diag[key], 0.0)).astype(BF16) for key in ops}
    o_intra = {key: _dot(attn[key], op["v"]) for key, op in ops.items()}
    s_inc = {key: _dot_tn(op["v"], op["k_state"]) for key, op in ops.items()}

    state = {key: s_ref[key] for key in streams}
    for step in range(nch):
        for (d, h), s in state.items():
            c = step if d == 0 else nch - 1 - step
            dirs[d][4][c * CHUNK:(c + 1) * CHUNK, h * dv:(h + 1) * dv] = (
                o_intra[d, h, c] + _dot_nt(ops[d, h, c]["q_state"], s.astype(BF16))).astype(BF16)
            state[d, h] = ops[d, h, c]["s_decay"] * s + s_inc[d, h, c]
    for key, s in state.items():
        s_ref[key] = s

    @pl.when(is_ctx)
    def _():
        for d, h in streams:
            ns_ref[0, d, h] = s_ref[d, h].T


def _gla_scan(q_src, q_blk, kf_src, kf_blk, kb_src, kb_blk, v_src, v_blk, cb_f, cb_b, s0, j, tok,
              *, heads, dk, dv, q_silu, hb):
    m = q_src.shape[0]
    tt = tok.seq
    assert heads % hb == 0 and q_blk % hb == 0 and kf_blk % hb == 0 and kb_blk % hb == 0 and v_blk % hb == 0
    bwd = lambda n: _bwd_block(n, tok)
    fwd = lambda n: n

    def stream(blk, k_blk):
        return [
            pl.BlockSpec((tt, hb * dk), lambda h, n: (blk(n), q_blk // hb + h)),
            pl.BlockSpec((tt, hb * dk), lambda h, n: (blk(n), k_blk // hb + h)),
            pl.BlockSpec((tt, hb * dv), lambda h, n: (blk(n), v_blk // hb + h)),
            pl.BlockSpec((tt, hb * dk), lambda h, n: (blk(n), h)),
        ]

    n_batch = tok.n_ctx_blocks
    return pl.pallas_call(
        functools.partial(_gla_scan_kernel, tok=tok, q_silu=q_silu, scale=dk ** -0.5, hb=hb, dk=dk, dv=dv),
        grid=(heads // hb, tok.n_blocks),
        in_specs=stream(fwd, kf_blk) + stream(bwd, kb_blk) + [
            pl.BlockSpec((1, 1, 2, hb, dk, dv), lambda h, n: (_dec_batch(n, tok), j, 0, h, 0, 0)),
        ],
        out_specs=[
            pl.BlockSpec((tt, hb * dv), lambda h, n: (n, h)),
            pl.BlockSpec((tt, hb * dv), lambda h, n: (bwd(n), h)),
            pl.BlockSpec((1, 2, hb, dk, dv), lambda h, n: (_state_block(n, tok), 0, h, 0, 0)),
        ],
        out_shape=[
            jax.ShapeDtypeStruct((m, heads * dv), BF16),
            jax.ShapeDtypeStruct((m, heads * dv), BF16),
            jax.ShapeDtypeStruct((n_batch, 2, heads, dk, dv), F32),
        ],
        scratch_shapes=[pltpu.VMEM((2, hb, dv, dk), F32)],
        compiler_params=_params("parallel", "arbitrary"),
        name="gla_scan",
    )(q_src, kf_src, v_src, cb_f, q_src, kb_src, v_src, cb_b, s0)


def kernel(x_prompt, x_sample, c, state_deltanet, state_gla, state_hgrn, c_ctx, ada_w, ada_b, norm1, norm2, final_norm, w_up, w_down, dn_w_in, dn_conv, dn_A_log, dn_dt_bias, dn_norm, dn_w_out, gla_w_in, gla_w_a2, gla_b_a, gla_norm, gla_w_out, hgrn_w_in, hgrn_lb_logits, hgrn_norm, hgrn_w_out):
    batch, seq, d = x_prompt.shape
    dec_batch, dec_seq, _ = x_sample.shape
    depth = ada_w.shape[0]
    tok = Tokens(m_ctx=batch * seq, seq=seq, dec_batch=dec_batch, dec_seq=dec_seq)
    assert seq % CHUNK == 0 and dec_seq % seq == 0 and seq % GRID_W == 0

    dn_hv, dn_dk, dn_dv = state_deltanet.shape[3:]
    dn_vd = dn_hv * dn_dv
    n_conv = dn_conv.shape[2]
    dn_kd = (n_conv - dn_vd) // 2
    dn_hk = dn_kd // dn_dk
    gla_h, gla_dk, gla_dv = state_gla.shape[3:]
    gla_kd, gla_vd = gla_h * gla_dk, gla_h * gla_dv
    hg_h, hg_dk, hg_dv = state_hgrn.shape[3:]

    x = jnp.concatenate([x_prompt.reshape(tok.m_ctx, d), x_sample.reshape(dec_batch * dec_seq, d)], axis=0)
    n_rows = 1 + dec_batch
    pad = (-n_rows) % 8
    cond = jnp.concatenate([c_ctx[None, :], c, jnp.zeros((pad, d), F32)], axis=0)
    mod_all = _mod_table(cond, ada_w, ada_b).reshape(depth, n_rows + pad, 6, d)

    bf = lambda w: w.astype(BF16)
    w_up_b, w_down_b = bf(w_up), bf(w_down)
    dn_w_in_b, dn_w_out_b = bf(dn_w_in), bf(dn_w_out)
    gla_w_in_b, gla_w_out_b = bf(gla_w_in), bf(gla_w_out)
    hgrn_w_in_b, hgrn_w_out_b = bf(hgrn_w_in), bf(hgrn_w_out)

    new_states = ([], [], [])
    dn_states = None
    for i in range(depth):
        kind, j = i % N_MIXERS, i // N_MIXERS
        mod = mod_all[i]
        if kind == 0:
            n_main = n_conv + dn_vd
            proj, raw = _proj(x, norm1[i], mod, dn_w_in_b, j, n_main, tok,
                              gate_w=dn_w_in, gate_layer=j, gate_col0=n_main)
            qkv = _dn_prep(proj, dn_conv[j], n_conv, dn_kd, dn_dk, tok)
            gates, gates_t = _dn_gates(raw, dn_A_log[j], dn_dt_bias[j], tok)
            o_f, o_b, dn_states = _dn_scan(qkv, gates, gates_t, state_deltanet, j, dn_states, tok,
                                           hk_n=dn_hk, hv=dn_hv, dk=dn_dk, dv=dn_dv)
            x = _out_proj(o_f, o_b, proj, n_conv // dn_vd, dn_norm[j], x, mod, dn_w_out_b, j, tok)
        elif kind == 1:
            n_main = 2 * gla_kd + 2 * gla_vd
            w_a1 = gla_w_in[j:j + 1, :, n_main:]
            proj, a1 = _proj(x, norm1[i], mod, gla_w_in_b, j, n_main, tok, gate_w=w_a1)
            cb_f, cb_b = _gla_prep(a1, gla_w_a2[j], gla_b_a[j], tok)
            o_f, o_b, ns = _gla_scan(proj, 0, proj, gla_h, proj, gla_h, proj, (2 * gla_kd) // gla_dv,
                                     cb_f, cb_b, state_gla, j, tok,
                                     heads=gla_h, dk=gla_dk, dv=gla_dv, q_silu=False, hb=min(gla_h, 2))
            x = _out_proj(o_f, o_b, proj, (2 * gla_kd + gla_vd) // gla_vd, gla_norm[j], x, mod,
                          gla_w_out_b, j, tok)
        else:
            proj = _proj(x, norm1[i], mod, hgrn_w_in_b, j, hgrn_w_in.shape[2], tok)
            k_f, k_b, cb_f, cb_b = _hgrn_prep(proj, hgrn_lb_logits, i, d, tok)
            o_f, o_b, ns = _gla_scan(proj, 0, k_f, 0, k_b, 0, proj, d // hg_dv, cb_f, cb_b, state_hgrn, j, tok,
                                     heads=hg_h, dk=hg_dk, dv=hg_dv, q_silu=True, hb=min(hg_h, 8))
            x = _out_proj(o_f, o_b, proj, 2, hgrn_norm[j], x, mod, hgrn_w_out_b, j, tok)
        if kind != 0:
            new_states[kind].append(ns)
        x = _mlp(x, norm2[i], mod, w_up_b, w_down_b, i, final_norm, tok, final=(i == depth - 1))

    y_ctx, y_dec = x
    y_prompt = y_ctx.reshape(batch, seq, d)
    y_sample = y_dec.reshape(dec_batch, dec_seq, d)
    stacked = [s[0][:, None] if len(s) == 1 else jnp.stack(s, axis=1) for s in new_states[1:]]
    return (y_prompt, y_sample, dn_states) + tuple(stacked)
```

```python
import functools
from typing import NamedTuple

import jax
import jax.numpy as jnp
from jax import lax
from jax.experimental import pallas as pl
from jax.experimental.pallas import tpu as pltpu

F32 = jnp.float32
BF16 = jnp.bfloat16

EPS = 1e-6
GRID_W = 64
CONV_K = 5
GLA_RANK = 16
GLA_TAU = 16.0
N_MIXERS = 3
CHUNK = 64
SUB = 16
DIAG = 4
NEG_BIG = -1e30
LANES = 128
SUBLANES = 8
PROLOGUE_ROWS = 256
V7X_MXU_COLS = 256
V7X_VMEM_LIMIT_BYTES = 56 * 1024 * 1024
HIGHEST = lax.Precision.HIGHEST


class Tokens(NamedTuple):
    m_ctx: int
    seq: int
    dec_batch: int
    dec_seq: int

    @property
    def m(self):
        return self.m_ctx + self.dec_batch * self.dec_seq

    @property
    def n_ctx_blocks(self):
        return self.m_ctx // self.seq

    @property
    def blocks_per_dec(self):
        return self.dec_seq // self.seq

    @property
    def n_blocks(self):
        return self.m // self.seq


def _tile(n, pref):
    t = min(n, pref)
    while n % t:
        t //= 2
    return t


def _token_tile(tok, pref):
    t = pref
    while tok.m_ctx % t or tok.dec_seq % t:
        t //= 2
    return t


def _params(*sem):
    return pltpu.CompilerParams(dimension_semantics=sem, vmem_limit_bytes=V7X_VMEM_LIMIT_BYTES)


def _mod_row(i, tm, tok):
    start = i * tm
    return jnp.where(start < tok.m_ctx, 0, 1 + (start - tok.m_ctx) // tok.dec_seq)


def _sigmoid(x):
    return 0.5 + 0.5 * jnp.tanh(0.5 * x)


def _silu(x):
    half = 0.5 * x
    return half + half * jnp.tanh(half)


def _softplus(x):
    return jnp.maximum(x, 0.0) + jnp.log(1.0 + jnp.exp(-jnp.abs(x)))


def _norm_mod(x, nw, shift, scale):
    ms = jnp.mean(x * x, axis=-1, keepdims=True)
    return x * lax.rsqrt(ms + EPS) * nw * (1.0 + scale) + shift


def _dot(a, b):
    return jnp.dot(a, b, preferred_element_type=F32)


def _dot_nt(a, b):
    return lax.dot_general(a, b, (((1,), (1,)), ((), ())), preferred_element_type=F32)


def _dot_tn(a, b):
    return lax.dot_general(a, b, (((0,), (0,)), ((), ())), preferred_element_type=F32)


def _split_dot(tri, x):
    hi = x.astype(BF16)
    lo = (x - hi.astype(F32)).astype(BF16)
    return _dot(tri, hi) + _dot(tri, lo)


def _split_matmul(a, b):
    ah = a.astype(BF16)
    al = (a - ah.astype(F32)).astype(BF16)
    bh = b.astype(BF16)
    bl = (b - bh.astype(F32)).astype(BF16)
    return _dot(ah, bh) + _dot(al, bh) + _dot(ah, bl)


def _tri(n, upper):
    r = lax.broadcasted_iota(jnp.int32, (n, n), 0)
    c = lax.broadcasted_iota(jnp.int32, (n, n), 1)
    return jnp.where((r <= c) if upper else (r >= c), 1.0, 0.0).astype(BF16)


def _mod_kernel(c_ref, w_ref, b_ref, o_ref):
    c = c_ref[...]
    s = _silu(c)
    o_ref[0] = jnp.dot(s, w_ref[0], precision=HIGHEST, preferred_element_type=F32) + b_ref[0]


def _mod_table(cond, ada_w, ada_b):
    depth, d, n = ada_w.shape
    rows = cond.shape[0]
    tn = _tile(n, 2048)
    return pl.pallas_call(
        _mod_kernel,
        grid=(depth, n // tn),
        in_specs=[
            pl.BlockSpec((rows, d), lambda l, j: (0, 0)),
            pl.BlockSpec((1, d, tn), lambda l, j: (l, 0, j)),
            pl.BlockSpec((1, 1, tn), lambda l, j: (l, 0, j)),
        ],
        out_specs=pl.BlockSpec((1, rows, tn), lambda l, j: (l, 0, j)),
        out_shape=jax.ShapeDtypeStruct((depth, rows, n), F32),
        compiler_params=_params("parallel", "parallel"),
        name="mod_table",
    )(cond, ada_w, ada_b.reshape(depth, 1, n))


def _proj_kernel(x_ref, nw_ref, mod_ref, w_ref, *rest, gated):
    if gated:
        wg_ref, o_ref, og_ref, h_ref = rest
    else:
        o_ref, h_ref = rest

    @pl.when(pl.program_id(1) == 0)
    def _():
        tm = x_ref.shape[0]
        step = min(tm, PROLOGUE_ROWS)
        for r in range(0, tm, step):
            rows = slice(r, r + step)
            h = _norm_mod(x_ref[rows, :], nw_ref[...], mod_ref[0, 0:1, :], mod_ref[0, 1:2, :])
            h_ref[rows, :] = h.astype(BF16)
            if gated:
                og_ref[rows, :] = _split_matmul(h, wg_ref[...])

    o_ref[...] = _dot(h_ref[...], w_ref[...]).astype(o_ref.dtype)


def _proj(x, nw, mod, w, layer, n, tok, *, gate_w=None, gate_layer=0, gate_col0=0, tm=1024, tn=2048):
    m, d = x.shape
    tm = _token_tile(tok, tm)
    tn = _tile(n, tn)
    gated = gate_w is not None
    in_specs = [
        pl.BlockSpec((tm, d), lambda i, j: (i, 0)),
        pl.BlockSpec((1, d), lambda i, j: (0, 0)),
        pl.BlockSpec((1, 6, d), lambda i, j: (_mod_row(i, tm, tok), 0, 0)),
        pl.BlockSpec((None, d, tn), lambda i, j: (layer, 0, j)),
    ]
    out_specs = [pl.BlockSpec((tm, tn), lambda i, j: (i, j))]
    out_shape = [jax.ShapeDtypeStruct((m, n), BF16)]
    args = [x, nw.reshape(1, d), mod, w]
    if gated:
        ng = gate_w.shape[2] - gate_col0
        assert gate_col0 % ng == 0
        in_specs.append(pl.BlockSpec((None, d, ng), lambda i, j: (gate_layer, 0, gate_col0 // ng)))
        out_specs.append(pl.BlockSpec((tm, ng), lambda i, j: (i, 0)))
        out_shape.append(jax.ShapeDtypeStruct((m, ng), F32))
        args.append(gate_w)
    out = pl.pallas_call(
        functools.partial(_proj_kernel, gated=gated),
        grid=(m // tm, n // tn),
        in_specs=in_specs,
        out_specs=out_specs,
        out_shape=out_shape,
        scratch_shapes=[pltpu.VMEM((tm, d), BF16)],
        compiler_params=_params("parallel", "arbitrary"),
        name="proj_gated" if gated else "proj",
    )(*args)
    return out if gated else out[0]


def _mlp_kernel(x_ref, nw_ref, mod_ref, wu_ref, wd_ref, fw_ref, *rest, n_ctx_tiles):
    out_refs, (h_ref, acc_ref) = rest[:-2], rest[-2:]
    final = n_ctx_tiles is not None
    i = pl.program_id(0)
    f = pl.program_id(1)

    @pl.when(f == 0)
    def _():
        h = _norm_mod(x_ref[...], nw_ref[...], mod_ref[0, 3:4, :], mod_ref[0, 4:5, :])
        h_ref[...] = h.astype(BF16)
        acc_ref[...] = jnp.zeros_like(acc_ref)

    a = jnp.maximum(_dot(h_ref[...], wu_ref[...]), 0.0)
    acc_ref[...] += _dot((a * a).astype(BF16), wd_ref[...])

    last = f == pl.num_programs(1) - 1

    def result():
        y = x_ref[...] + mod_ref[0, 5:6, :] * acc_ref[...]
        if final:
            ms = jnp.mean(y * y, axis=-1, keepdims=True)
            y = y * lax.rsqrt(ms + EPS) * fw_ref[...]
        return y

    if not final:
        @pl.when(last)
        def _():
            out_refs[0][...] = result()
    else:
        @pl.when(jnp.logical_and(last, i < n_ctx_tiles))
        def _():
            out_refs[0][...] = result()

        @pl.when(jnp.logical_and(last, i >= n_ctx_tiles))
        def _():
            out_refs[1][...] = result()


def _mlp(x, nw, mod, w_up, w_down, layer, final_w, tok, *, final, tm=512, tf=1024):
    m, d = x.shape
    ff = w_up.shape[2]
    tm = _token_tile(tok, tm)
    tf = _tile(ff, tf)
    n_ctx_tiles = tok.m_ctx // tm
    if final:
        out_specs = [pl.BlockSpec((tm, d), lambda i, f: (jnp.minimum(i, n_ctx_tiles - 1), 0)),
                     pl.BlockSpec((tm, d), lambda i, f: (jnp.maximum(i - n_ctx_tiles, 0), 0))]
        out_shape = [jax.ShapeDtypeStruct((tok.m_ctx, d), F32), jax.ShapeDtypeStruct((m - tok.m_ctx, d), F32)]
    else:
        out_specs = pl.BlockSpec((tm, d), lambda i, f: (i, 0))
        out_shape = jax.ShapeDtypeStruct((m, d), F32)
    return pl.pallas_call(
        functools.partial(_mlp_kernel, n_ctx_tiles=n_ctx_tiles if final else None),
        grid=(m // tm, ff // tf),
        in_specs=[
            pl.BlockSpec((tm, d), lambda i, f: (i, 0)),
            pl.BlockSpec((1, d), lambda i, f: (0, 0)),
            pl.BlockSpec((1, 6, d), lambda i, f: (_mod_row(i, tm, tok), 0, 0)),
            pl.BlockSpec((None, d, tf), lambda i, f: (layer, 0, f)),
            pl.BlockSpec((None, tf, d), lambda i, f: (layer, f, 0)),
            pl.BlockSpec((1, d), lambda i, f: (0, 0)),
        ],
        out_specs=out_specs,
        out_shape=out_shape,
        scratch_shapes=[pltpu.VMEM((tm, d), BF16), pltpu.VMEM((tm, d), F32)],
        compiler_params=_params("arbitrary" if final else "parallel", "arbitrary"),
        name="mlp_final" if final else "mlp",
    )(x, nw.reshape(1, d), mod, w_up, w_down, final_w.reshape(1, d))


def _out_kernel(of_ref, ob_ref, g_ref, nw_ref, x_ref, mod_ref, w_ref, o_ref, y0_ref, y1_ref, *, n_heads, dh):
    i = pl.program_id(0)
    d = w_ref.shape[1]
    n_chunks = d // V7X_MXU_COLS
    heads_per_chunk = -(-n_heads // n_chunks)

    @pl.when(i == 0)
    def _():
        y1_ref[...] = jnp.zeros_like(y1_ref)

    def body(y_write, y_read):
        ones = jnp.full((dh, LANES), 1.0 / dh, BF16) if dh & (dh - 1) == 0 else None
        for c in range(n_chunks):
            cols = slice(c * V7X_MXU_COLS, (c + 1) * V7X_MXU_COLS)
            o_ref[:, cols] = x_ref[:, cols] + mod_ref[0, 2:3, cols] * _dot(y_read[...], w_ref[:, cols])
            for h in range(c * heads_per_chunk, min((c + 1) * heads_per_chunk, n_heads)):
                sl = slice(h * dh, (h + 1) * dh)
                o = of_ref[:, sl].astype(F32) + ob_ref[:, sl].astype(F32)
                if ones is None:
                    ms = _dot((o * o).astype(BF16), jnp.ones((dh, LANES), BF16)) * (1.0 / dh)
                else:
                    ms = _dot((o * o).astype(BF16), ones)
                inv = lax.rsqrt(ms + EPS)
                y = o * nw_ref[...] * _silu(g_ref[:, sl].astype(F32))
                for k in range(dh // LANES):
                    lo = h * dh + k * LANES
                    y_write[:, lo:lo + LANES] = (y[:, k * LANES:(k + 1) * LANES] * inv).astype(BF16)

    @pl.when(i % 2 == 0)
    def _():
        body(y0_ref, y1_ref)

    @pl.when(i % 2 == 1)
    def _():
        body(y1_ref, y0_ref)


def _out_proj(o_f, o_b, gate_src, gate_blk, norm_w, x, mod, w_out, layer, tok, *, tm=256):
    m, vd = o_f.shape
    d = x.shape[1]
    dh = norm_w.shape[0]
    tm = _token_tile(tok, tm)
    n_tiles = m // tm
    assert d % V7X_MXU_COLS == 0 and dh % LANES == 0
    cur = lambda i: jnp.minimum(i, n_tiles - 1)
    prev = lambda i: jnp.maximum(i - 1, 0)
    return pl.pallas_call(
        functools.partial(_out_kernel, n_heads=vd // dh, dh=dh),
        grid=(n_tiles + 1,),
        in_specs=[
            pl.BlockSpec((tm, vd), lambda i: (cur(i), 0)),
            pl.BlockSpec((tm, vd), lambda i: (cur(i), 0)),
            pl.BlockSpec((tm, vd), lambda i: (cur(i), gate_blk)),
            pl.BlockSpec((1, dh), lambda i: (0, 0)),
            pl.BlockSpec((tm, d), lambda i: (prev(i), 0)),
            pl.BlockSpec((1, 6, d), lambda i: (_mod_row(prev(i), tm, tok), 0, 0)),
            pl.BlockSpec((None, vd, d), lambda i: (layer, 0, 0), pipeline_mode=pl.Buffered(1)),
        ],
        out_specs=pl.BlockSpec((tm, d), lambda i: (prev(i), 0)),
        out_shape=jax.ShapeDtypeStruct((m, d), F32),
        scratch_shapes=[pltpu.VMEM((tm, vd), BF16), pltpu.VMEM((tm, vd), BF16)],
        compiler_params=_params("arbitrary"),
        name="out_proj",
    )(o_f, o_b, gate_src, norm_w.reshape(1, dh), x, mod, w_out)


def _bwd_block(n, tok):
    nc, per = tok.n_ctx_blocks, tok.blocks_per_dec
    r = jnp.maximum(n - nc, 0)
    return jnp.where(n < nc, n, nc + (r // per) * per + (per - 1 - r % per))


def _dec_batch(n, tok):
    return jnp.maximum(n - tok.n_ctx_blocks, 0) // tok.blocks_per_dec


def _state_block(n, tok):
    return jnp.minimum(n, tok.n_ctx_blocks - 1)


def _seq_flags(n, tok):
    nc, per = tok.n_ctx_blocks, tok.blocks_per_dec
    is_ctx = n < nc
    r = jnp.maximum(n - nc, 0) % per
    return is_ctx, jnp.logical_and(jnp.logical_not(is_ctx), r == 0)


def _dn_prep_kernel(x_ref, shift_ref, w_ref, o_ref, *, n_q_blocks, dk):
    j = pl.program_id(1)
    x = x_ref[...]
    tc = x.shape[1]
    acc = None
    for s in range(CONV_K):
        xs = x.astype(F32) if s == CONV_K // 2 else _dot(shift_ref[0, s], x)
        term = xs * w_ref[s:s + 1, :]
        acc = term if acc is None else acc + term
    y = _silu(acc)

    def normed(scale):
        ones = jnp.ones((dk, dk), BF16)
        for g in range(tc // dk):
            blk = y[:, g * dk:(g + 1) * dk]
            ss = _dot((blk * blk).astype(BF16), ones)
            o_ref[:, g * dk:(g + 1) * dk] = (blk * (lax.rsqrt(ss + EPS) * scale)).astype(o_ref.dtype)

    @pl.when(j < n_q_blocks)
    def _():
        normed(dk ** -0.5)

    @pl.when(jnp.logical_and(j >= n_q_blocks, j < 2 * n_q_blocks))
    def _():
        normed(1.0)

    @pl.when(j >= 2 * n_q_blocks)
    def _():
        o_ref[...] = y.astype(o_ref.dtype)


def _conv_shift_matrices(tt):
    t = jnp.arange(tt)[:, None]
    u = jnp.arange(tt)[None, :]
    mats = []
    for seg in (tt, GRID_W):
        same = (t // seg) == (u // seg)
        mats.append(jnp.stack([jnp.logical_and(u == t + s - CONV_K // 2, same) for s in range(CONV_K)]))
    return jnp.stack(mats).astype(BF16)


def _dn_prep(proj, conv_w, n_conv, kd, dk, tok):
    m = proj.shape[0]
    tt = tok.seq
    tc = _tile(kd, 2048)
    n_ctx = tok.n_ctx_blocks
    return pl.pallas_call(
        functools.partial(_dn_prep_kernel, n_q_blocks=kd // tc, dk=dk),
        grid=(m // tt, n_conv // tc),
        in_specs=[
            pl.BlockSpec((tt, tc), lambda i, j: (i, j)),
            pl.BlockSpec((1, CONV_K, tt, tt), lambda i, j: (jnp.where(i < n_ctx, 0, 1), 0, 0, 0)),
            pl.BlockSpec((CONV_K, tc), lambda i, j: (0, j)),
        ],
        out_specs=pl.BlockSpec((tt, tc), lambda i, j: (i, j)),
        out_shape=jax.ShapeDtypeStruct((m, n_conv), BF16),
        compiler_params=_params("parallel", "parallel"),
        name="dn_prep",
    )(proj, _conv_shift_matrices(tt), conv_w)


def _dn_gate_kernel(x_ref, a_ref, bias_ref, o_ref, gt_ref, *, hv, tt):
    x = x_ref[...]
    lanes = x.shape[1]
    lane = lax.broadcasted_iota(jnp.int32, (1, lanes), 1)
    is_fwd = lane < 3 * hv
    g = -a_ref[...] * _softplus(x + bias_ref[...])
    beta = pltpu.roll(_sigmoid(x), 2 * hv, 1)
    lower = _tri(CHUNK, upper=False)
    upper = _tri(CHUNK, upper=True)
    for c in range(tt // CHUNK):
        rows = slice(c * CHUNK, (c + 1) * CHUNK)
        gch = g[rows]
        gc = jnp.where(is_fwd, _split_dot(lower, gch), _split_dot(upper, gch))
        g_end = jnp.where(is_fwd, gc[CHUNK - 1:CHUNK, :], gc[0:1, :])
        eg = jnp.exp(gc)
        o_ref[0, rows, :] = beta[rows]
        o_ref[1, rows, :] = gc
        o_ref[2, rows, :] = eg
        o_ref[3, rows, :] = jnp.exp(g_end - gc)
        o_ref[4, rows, :] = beta[rows] * eg
    gt_ref[...] = o_ref[1].T


def _dn_gates(raw, a_log, dt_bias, tok):
    m, lanes = raw.shape
    hv = lanes // 4
    tt = tok.seq
    zeros = jnp.zeros((2 * hv,), F32)
    a_full = jnp.concatenate([zeros, jnp.exp(a_log).reshape(-1)]).reshape(1, lanes)
    bias_full = jnp.concatenate([zeros, dt_bias.reshape(-1)]).reshape(1, lanes)
    return pl.pallas_call(
        functools.partial(_dn_gate_kernel, hv=hv, tt=tt),
        grid=(m // tt,),
        in_specs=[
            pl.BlockSpec((tt, lanes), lambda i: (i, 0)),
            pl.BlockSpec((1, lanes), lambda i: (0, 0)),
            pl.BlockSpec((1, lanes), lambda i: (0, 0)),
        ],
        out_specs=[
            pl.BlockSpec((5, tt, lanes), lambda i: (0, i, 0)),
            pl.BlockSpec((lanes, tt), lambda i: (0, i)),
        ],
        out_shape=[jax.ShapeDtypeStruct((5, m, lanes), F32), jax.ShapeDtypeStruct((lanes, m), F32)],
        compiler_params=_params("parallel"),
        name="dn_gates",
    )(raw, a_full, bias_full)


def _dn_scan_kernel(qf_ref, kf_ref, vf_ref, gf_ref, gtf_ref, qb_ref, kb_ref, vb_ref, gb_ref, gtb_ref, s0_ref,
                    of_ref, ob_ref, ns_ref, s_ref, u0_ref, wq_ref, attn_ref, kd_ref, ge_ref, *, tok, hv, dv, hkb):
    hk0 = pl.program_id(0) * hkb
    n = pl.program_id(1)
    tt = tok.seq
    nch = tt // CHUNK
    dk = qf_ref.shape[1] // hkb
    assert dv == 2 * CHUNK and CHUNK == 4 * SUB
    has_prev = n > 0
    is_ctx, dec_start = _seq_flags(jnp.maximum(n - 1, 0), tok)
    wslot = n % 2
    rslot = 1 - wslot
    scratch = (u0_ref, wq_ref, attn_ref, kd_ref, ge_ref)

    @pl.when(n == 0)
    def _():
        s_ref[...] = jnp.zeros_like(s_ref)
        for ref in scratch:
            ref[1] = jnp.zeros(ref.shape[1:], ref.dtype)

    @pl.when(jnp.logical_and(has_prev, is_ctx))
    def _():
        s_ref[...] = jnp.zeros_like(s_ref)

    @pl.when(jnp.logical_and(has_prev, dec_start))
    def _():
        s_ref[...] = s0_ref[0, 0]

    bf = lambda x: x.astype(BF16)
    lanes = gf_ref.shape[2]
    lane_iota = lax.broadcasted_iota(jnp.int32, (1, lanes), 1)
    ri = lax.broadcasted_iota(jnp.int32, (CHUNK, 2 * CHUNK), 0)
    li = lax.broadcasted_iota(jnp.int32, (CHUNK, 2 * CHUNK), 1)
    left = li < CHUNK
    ci = li % CHUNK
    same = [(ri // b) == (ci // b) for b in (SUB, 2 * SUB)]
    zero_b = jnp.zeros((CHUNK, 2 * CHUNK), BF16)

    def block_diag(xb):
        return jnp.concatenate([jnp.where(left, xb, zero_b), jnp.where(left, zero_b, xb)], axis=0)

    def pair_dot(a, b):
        return _dot(bf(a), block_diag(bf(b)))

    dirs = ((qf_ref, kf_ref, vf_ref, gf_ref, gtf_ref, of_ref), (qb_ref, kb_ref, vb_ref, gb_ref, gtb_ref, ob_ref))

    def prepare():
        ch = {}
        for d, kh in [(d, kh) for d in range(2) for kh in range(hkb)]:
            q_ref, k_ref, v_ref, g_ref, gt_ref, _ = dirs[d]
            incl = (ri >= ci) if d == 0 else (ri <= ci)
            strict = (ri > ci) if d == 0 else (ri < ci)
            end_row = CHUNK - 1 if d == 0 else 0
            q = q_ref[:, kh * dk:(kh + 1) * dk]
            k = k_ref[:, kh * dk:(kh + 1) * dk]
            kf32 = k.astype(F32)
            qf32 = q.astype(F32)
            cols, g_rows, vs = [], [], []
            for e in range(2):
                a_lane = 2 * hv + d * hv + 2 * (hk0 + kh) + e
                sel = lane_iota == a_lane
                cols.append([jnp.sum(jnp.where(sel, g_ref[i], 0.0), axis=1, keepdims=True) for i in range(5)])
                g_rows.append(gt_ref[pl.ds(a_lane, 1), :])
                vs.append(v_ref[:, (2 * kh + e) * dv:(2 * kh + e + 1) * dv].astype(F32))
            for c in range(nch):
                rows = slice(c * CHUNK, (c + 1) * CHUNK)
                pair = lambda i: jnp.where(left, cols[0][i][rows], cols[1][i][rows])
                k2 = jnp.concatenate([k[rows], k[rows]], axis=0)
                kk = _dot_nt(k[rows], k2)
                qk = _dot_nt(q[rows], k2)
                g_row = jnp.concatenate([g_rows[0][:, rows], g_rows[1][:, rows]], axis=1)
                diff = pair(1) - g_row
                decay = jnp.where(incl, jnp.exp(jnp.where(incl, diff, 0.0)), 0.0)
                p = jnp.where(strict, -(pair(0) * kk * decay), 0.0)
                attn_ref[wslot, d, kh, c] = bf(qk * decay)
                for e in range(2):
                    beta_c, _, eg_c, ekd_c, beg_c = cols[e]
                    qg = qf32[rows] * eg_c[rows]
                    kd_ref[wslot, d, 2 * kh + e, c] = bf(kf32[rows] * ekd_c[rows])
                    wq_ref[wslot, d, 2 * kh + e, c, CHUNK:, :] = bf(qg)
                    g_end = eg_c[c * CHUNK + end_row:c * CHUNK + end_row + 1, :]
                    ge_ref[wslot, d, 2 * kh + e, c] = jnp.broadcast_to(g_end, ge_ref.shape[4:])
                ch[kh, d, c] = dict(
                    pd=jnp.where(same[0], p, 0.0),
                    p32=jnp.where(jnp.logical_and(same[1], jnp.logical_not(same[0])), p, 0.0),
                    p64=jnp.where(same[1], 0.0, p),
                    r=[jnp.concatenate([cols[e][0][rows] * vs[e][rows], cols[e][4][rows] * kf32[rows]], axis=1)
                       for e in range(2)])
            yield
        for c in ch.values():
            c["q"] = pair_dot(c["pd"], c["pd"])
            c["xm"] = c["pd"]
        yield
        for it in range(2):
            for c in ch.values():
                q, xm = c["q"], c["xm"]
                both = _dot(bf(q), jnp.concatenate([block_diag(bf(xm)), block_diag(bf(q))], axis=1))
                c["xm"] = xm + q + both[:, :2 * CHUNK]
                c["q"] = both[:, 2 * CHUNK:]
            yield
        for c in ch.values():
            q, xm = c.pop("q"), c["xm"]
            c["xm"] = xm + q + pair_dot(q, xm)
        yield
        for level in ("p32", "p64"):
            for c in ch.values():
                c["y"] = c[level] + pair_dot(c[level], c["xm"])
            yield
            for c in ch.values():
                xm, y = c["xm"], c.pop("y")
                c["xm"] = xm + y + pair_dot(xm, y)
            yield
        zero_r = jnp.zeros((CHUNK, 2 * dv), BF16)
        for (kh, d, cidx), c in ch.items():
            r0, r1 = c["r"]
            rhs = jnp.concatenate([jnp.concatenate([bf(r0), zero_r], axis=1),
                                   jnp.concatenate([zero_r, bf(r1)], axis=1)], axis=0)
            big = _dot(bf(c["xm"]), rhs)
            sol = [r0 + big[:, :2 * dv], r1 + big[:, 2 * dv:]]
            u0_ref[wslot, d, kh, cidx] = jnp.concatenate([sol[0][:, :dv], sol[1][:, :dv]], axis=1)
            for e in range(2):
                wq_ref[wslot, d, 2 * kh + e, cidx, :CHUNK, :] = bf(sol[e][:, dv:])
        yield

    def recur():
        state = {(d, e): s_ref[d, e] for d in range(2) for e in range(2 * hkb)}
        zero_u = jnp.zeros((CHUNK, dv), BF16)
        for step in range(nch):
            chunk = lambda d: step if d == 0 else nch - 1 - step
            ws = {(d, e): _dot(wq_ref[rslot, d, e, chunk(d)], bf(s)) for (d, e), s in state.items()}
            yield
            for d, kh in [(d, kh) for d in range(2) for kh in range(hkb)]:
                c = chunk(d)
                u0 = u0_ref[rslot, d, kh, c]
                ev = [2 * kh, 2 * kh + 1]
                ub = [bf(u0[:, e * dv:(e + 1) * dv] - ws[d, ev[e]][:CHUNK]) for e in range(2)]
                u2 = jnp.concatenate([jnp.concatenate([ub[0], zero_u], axis=1),
                                      jnp.concatenate([zero_u, ub[1]], axis=1)], axis=0)
                o = (_dot(attn_ref[rslot, d, kh, c], u2)
                     + jnp.concatenate([ws[d, ev[0]][CHUNK:], ws[d, ev[1]][CHUNK:]], axis=1))
                dirs[d][5][c * CHUNK:(c + 1) * CHUNK, 2 * kh * dv:2 * (kh + 1) * dv] = bf(o)
                for e in range(2):
                    state[d, ev[e]] = (ge_ref[rslot, d, ev[e], c][0:1, :] * state[d, ev[e]]
                                       + _dot_tn(kd_ref[rslot, d, ev[e], c], ub[e]))
            yield
        for (d, e), s in state.items():
            s_ref[d, e] = s
        yield

    halves = [recur(), prepare()]
    while halves:
        for g in list(halves):
            if next(g, "done") == "done":
                halves.remove(g)

    @pl.when(jnp.logical_and(has_prev, is_ctx))
    def _():
        ns_ref[0, 0] = s_ref[...]


def _dn_scan_kernel_aliased(*refs, **kw):
    n_in = 11
    return _dn_scan_kernel(*refs[:n_in], *refs[n_in + 1:], **kw)


def _dn_scan(qkv, gates, gates_t, s0, j, new_states, tok, *, hk_n, hv, dk, dv):
    m = qkv.shape[0]
    tt = tok.seq
    rep = hv // hk_n
    assert rep == 2
    lanes = gates.shape[2]
    kd = hk_n * dk
    nb = tok.n_blocks
    nch = tt // CHUNK
    fwd = lambda n: jnp.minimum(n, nb - 1)
    bwd = lambda n: _bwd_block(jnp.minimum(n, nb - 1), tok)
    prev = lambda n: jnp.maximum(n - 1, 0)

    hkb = max(b for b in (4, 2, 1) if hk_n % b == 0)
    hvb = hkb * rep

    def stream(blk):
        return [
            pl.BlockSpec((tt, hkb * dk), lambda h, n: (blk(n), h)),
            pl.BlockSpec((tt, hkb * dk), lambda h, n: (blk(n), hk_n // hkb + h)),
            pl.BlockSpec((tt, hvb * dv), lambda h, n: (blk(n), (2 * kd) // (hvb * dv) + h)),
            pl.BlockSpec((5, tt, lanes), lambda h, n: (0, blk(n), 0)),
            pl.BlockSpec((lanes, tt), lambda h, n: (0, blk(n))),
        ]

    n_batch = tok.n_ctx_blocks
    n_layers = s0.shape[1]
    aliased = new_states is not None
    extra_in = [pl.BlockSpec(memory_space=pl.ANY)] if aliased else []
    extra_args = (new_states,) if aliased else ()
    return pl.pallas_call(
        functools.partial(_dn_scan_kernel_aliased if aliased else _dn_scan_kernel, tok=tok, hv=hv, dv=dv, hkb=hkb),
        grid=(hk_n // hkb, nb + 1),
        in_specs=stream(fwd) + stream(bwd) + [
            pl.BlockSpec((1, 1, 2, hvb, dk, dv), lambda h, n: (_dec_batch(prev(n), tok), j, 0, h, 0, 0)),
        ] + extra_in,
        out_specs=[
            pl.BlockSpec((tt, hvb * dv), lambda h, n: (prev(n), h)),
            pl.BlockSpec((tt, hvb * dv), lambda h, n: (_bwd_block(prev(n), tok), h)),
            pl.BlockSpec((1, 1, 2, hvb, dk, dv), lambda h, n: (_state_block(prev(n), tok), j, 0, h, 0, 0)),
        ],
        out_shape=[
            jax.ShapeDtypeStruct((m, hv * dv), BF16),
            jax.ShapeDtypeStruct((m, hv * dv), BF16),
            jax.ShapeDtypeStruct((n_batch, n_layers, 2, hv, dk, dv), F32),
        ],
        input_output_aliases={11: 2} if aliased else {},
        scratch_shapes=[
            pltpu.VMEM((2, hvb, dk, dv), F32),
            pltpu.VMEM((2, 2, hkb, nch, CHUNK, rep * dv), F32),
            pltpu.VMEM((2, 2, hvb, nch, 2 * CHUNK, dk), BF16),
            pltpu.VMEM((2, 2, hkb, nch, CHUNK, rep * CHUNK), BF16),
            pltpu.VMEM((2, 2, hvb, nch, CHUNK, dk), BF16),
            pltpu.VMEM((2, 2, hvb, nch, SUBLANES, dv), F32),
        ],
        compiler_params=_params("parallel", "arbitrary"),
        name="dn_scan",
    )(qkv, qkv, qkv, gates, gates_t, qkv, qkv, qkv, gates, gates_t, s0, *extra_args)


def _chunk_cums(la_f, la_b, cbf_ref, cbb_ref, tt):
    lower = _tri(CHUNK, upper=False)
    upper = _tri(CHUNK, upper=True)
    for c in range(tt // CHUNK):
        rows = slice(c * CHUNK, (c + 1) * CHUNK)
        cbf_ref[rows, :] = _split_dot(lower, la_f[rows])
        cbb_ref[rows, :] = _split_dot(upper, la_b[rows])


def _log_sigmoid(x):
    return jnp.minimum(x, 0.0) - jnp.log(1.0 + jnp.exp(-jnp.abs(x)))


def _gla_prep_kernel(a1_ref, w2_ref, b_ref, cbf_ref, cbb_ref, *, tt):
    a1 = a1_ref[...]
    la = [_log_sigmoid(_dot(a1, w2_ref[d]) + b_ref[d]) * (1.0 / GLA_TAU) for d in range(2)]
    _chunk_cums(la[0], la[1], cbf_ref, cbb_ref, tt)


def _gla_prep(a1, w_a2, b_a, tok):
    m, r2 = a1.shape
    kd = w_a2.shape[2]
    tt = tok.seq
    tc = _tile(kd, 1024)
    zeros = jnp.zeros((GLA_RANK, kd), F32)
    w2 = jnp.stack([jnp.concatenate([w_a2[0], zeros]), jnp.concatenate([zeros, w_a2[1]])])
    out = jax.ShapeDtypeStruct((m, kd), F32)
    return pl.pallas_call(
        functools.partial(_gla_prep_kernel, tt=tt),
        grid=(m // tt, kd // tc),
        in_specs=[
            pl.BlockSpec((tt, r2), lambda i, j: (i, 0)),
            pl.BlockSpec((2, r2, tc), lambda i, j: (0, 0, j)),
            pl.BlockSpec((2, 1, tc), lambda i, j: (0, 0, j)),
        ],
        out_specs=[pl.BlockSpec((tt, tc), lambda i, j: (i, j))] * 2,
        out_shape=[out, out],
        compiler_params=_params("parallel", "parallel"),
        name="gla_prep",
    )(a1, w2, b_a.reshape(2, 1, kd))


def _hgrn_prep_kernel(ff_ref, fb_ref, lb_ref, kf_ref, kb_ref, cbf_ref, cbb_ref, *, layer, tt):
    logits = lb_ref[...]
    ex = jnp.exp(logits - jnp.max(logits, axis=0, keepdims=True))
    probs = ex / jnp.sum(ex, axis=0, keepdims=True)
    lb = jnp.sum(probs[1:layer + 1], axis=0, keepdims=True)
    la = []
    for f_ref, k_ref in ((ff_ref, kf_ref), (fb_ref, kb_ref)):
        fl = f_ref[...].astype(F32)
        sg = 1.0 / (1.0 + jnp.exp(-fl))
        la.append(jnp.log(lb + (1.0 - lb) * sg))
        k_ref[...] = (1.0 - lb) * (1.0 - sg)
    _chunk_cums(la[0], la[1], cbf_ref, cbb_ref, tt)


def _hgrn_prep(proj, lb_logits, layer, d, tok):
    m = proj.shape[0]
    depth = lb_logits.shape[0]
    tt = tok.seq
    tc = _tile(d, 2048)
    nb = d // tc
    out = jax.ShapeDtypeStruct((m, d), F32)
    return pl.pallas_call(
        functools.partial(_hgrn_prep_kernel, layer=layer, tt=tt),
        grid=(m // tt, nb),
        in_specs=[
            pl.BlockSpec((tt, tc), lambda i, j: (i, 3 * nb + j)),
            pl.BlockSpec((tt, tc), lambda i, j: (i, 4 * nb + j)),
            pl.BlockSpec((depth, tc), lambda i, j: (0, j)),
        ],
        out_specs=[pl.BlockSpec((tt, tc), lambda i, j: (i, j))] * 4,
        out_shape=[out] * 4,
        compiler_params=_params("parallel", "parallel"),
        name="hgrn_prep",
    )(proj, proj, lb_logits)


def _gla_offdiag_pairs(c):
    pairs = []
    size = c
    while size > SUB:
        half = size // 2
        for start in range(0, c, size):
            pairs.append(((start + half, start + size), (start, start + half)))
        size = half
    return pairs


def _gla_operands(q, k, cb, rev):
    c, dk = q.shape
    zero_rows = lambda n: jnp.zeros((n, dk), BF16)

    def padded(x, rows):
        parts = ([zero_rows(rows[0])] if rows[0] else []) + [x] + ([zero_rows(c - rows[1])] if rows[1] < c else [])
        return jnp.concatenate(parts, axis=0) if len(parts) > 1 else x

    q_parts, k_parts = [], []
    for qr, kr in _gla_offdiag_pairs(c):
        if rev:
            qr, kr = kr, qr
        ref_row = qr[0] - 1 if not rev else qr[1]
        ref = cb[ref_row:ref_row + 1, :]
        qs, ks = slice(*qr), slice(*kr)
        q_parts.append(padded((q[qs] * jnp.exp(cb[qs] - ref)).astype(BF16), qr))
        k_parts.append(padded((k[ks] * jnp.exp(ref - cb[ks])).astype(BF16), kr))
    row = lax.broadcasted_iota(jnp.int32, (c, 1), 0)
    levels = []
    half = SUB // 2
    while half >= DIAG:
        size = 2 * half
        refs = []
        for start in range(0, c, size):
            r = start + half if rev else start + half - 1
            refs.append(jnp.broadcast_to(cb[r:r + 1, :], (size, dk)))
        ref = jnp.concatenate(refs, axis=0)
        later = (row % size) >= half
        is_q = jnp.logical_not(later) if rev else later
        qt = (q * jnp.exp(jnp.where(is_q, cb - ref, NEG_BIG))).astype(BF16)
        kt = (k * jnp.exp(jnp.where(is_q, NEG_BIG, ref - cb))).astype(BF16)
        levels.append((qt, kt))
        half //= 2
    diag = [q.astype(BF16)]
    for d in range(1, DIAG):
        shift = c - d if rev else d
        valid = (row % DIAG) < DIAG - d if rev else (row % DIAG) >= d
        diag.append((q * jnp.exp(jnp.where(valid, cb - pltpu.roll(cb, shift, 0), NEG_BIG))).astype(BF16))
    return jnp.concatenate(q_parts, axis=1), jnp.concatenate(k_parts, axis=1), levels, diag, k.astype(BF16)


def _gla_level_mask(c, size):
    ri = lax.broadcasted_iota(jnp.int32, (c, c), 0)
    ci = lax.broadcasted_iota(jnp.int32, (c, c), 1)
    return (ri // size) == (ci // size)


def _gla_subdiag_mask(c, d, rev):
    ri = lax.broadcasted_iota(jnp.int32, (c, c), 0)
    ci = lax.broadcasted_iota(jnp.int32, (c, c), 1)
    on = (ci - ri == d) if rev else (ri - ci == d)
    return jnp.logical_and(on, (ri // DIAG) == (ci // DIAG))


def _gla_scan_kernel(qf_ref, kf_ref, vf_ref, cf_ref, qb_ref, kb_ref, vb_ref, cb_ref, s0_ref,
                     of_ref, ob_ref, ns_ref, s_ref, *, tok, q_silu, scale, hb, dk, dv):
    n = pl.program_id(1)
    tt = tok.seq
    nch = tt // CHUNK
    is_ctx, dec_start = _seq_flags(n, tok)
    streams = [(d, h) for d in range(2) for h in range(hb)]

    @pl.when(is_ctx)
    def _():
        s_ref[...] = jnp.zeros_like(s_ref)

    @pl.when(dec_start)
    def _():
        for d, h in streams:
            s_ref[d, h] = s0_ref[0, 0, d, h].T

    dirs = ((qf_ref, kf_ref, vf_ref, cf_ref, of_ref), (qb_ref, kb_ref, vb_ref, cb_ref, ob_ref))
    ops = {}
    for d, (q_ref, k_ref, v_ref, c_ref, _) in enumerate(dirs):
        rev = d == 1
        for h in range(hb):
            q = q_ref[:, h * dk:(h + 1) * dk].astype(F32)
            if q_silu:
                q = _silu(q)
            q = q * scale
            k = k_ref[:, h * dk:(h + 1) * dk].astype(F32)
            v = v_ref[:, h * dv:(h + 1) * dv].astype(BF16)
            cb = c_ref[:, h * dk:(h + 1) * dk]
            for c in range(nch):
                rows = slice(c * CHUNK, (c + 1) * CHUNK)
                qc, kc, cbc = q[rows], k[rows], cb[rows]
                end = cbc[0:1, :] if rev else cbc[CHUNK - 1:CHUNK, :]
                ops[d, h, c] = dict(
                    intra=_gla_operands(qc, kc, cbc, rev), v=v[rows],
                    q_state=(qc * jnp.exp(cbc)).astype(BF16),
                    k_state=(kc * jnp.exp(end - cbc)).astype(BF16),
                    s_decay=jnp.exp(end))
    acc = {key: _dot_nt(op["intra"][0], op["intra"][1]) for key, op in ops.items()}
    n_levels = len(next(iter(ops.values()))["intra"][2])
    for lv in range(n_levels):
        mask = _gla_level_mask(CHUNK, SUB >> lv)
        part = {key: _dot_nt(*op["intra"][2][lv]) for key, op in ops.items()}
        acc = {key: acc[key] + jnp.where(mask, part[key], 0.0) for key in ops}
    for d in range(DIAG):
        masks = [_gla_subdiag_mask(CHUNK, d, rev) for rev in (False, True)]
        part = {key: _dot_nt(op["intra"][3][d], op["intra"][4]) for key, op in ops.items()}
        acc = {key: acc[key] + jnp.where(masks[key[0]], part[key], 0.0) for key in ops}
    attn = {key: a.astype(BF16) for key, a in acc.items()}
    o_intra = {key: _dot(attn[key], op["v"]) for key, op in ops.items()}
    s_inc = {key: _dot_tn(op["v"], op["k_state"]) for key, op in ops.items()}

    state = {key: s_ref[key] for key in streams}
    for step in range(nch):
        for (d, h), s in state.items():
            c = step if d == 0 else nch - 1 - step
            dirs[d][4][c * CHUNK:(c + 1) * CHUNK, h * dv:(h + 1) * dv] = (
                o_intra[d, h, c] + _dot_nt(ops[d, h, c]["q_state"], s.astype(BF16))).astype(BF16)
            state[d, h] = ops[d, h, c]["s_decay"] * s + s_inc[d, h, c]
    for key, s in state.items():
        s_ref[key] = s

    @pl.when(is_ctx)
    def _():
        for d, h in streams:
            ns_ref[0, d, h] = s_ref[d, h].T


def _gla_scan(q_src, q_blk, kf_src, kf_blk, kb_src, kb_blk, v_src, v_blk, cb_f, cb_b, s0, j, tok,
              *, heads, dk, dv, q_silu, hb):
    m = q_src.shape[0]
    tt = tok.seq
    assert heads % hb == 0 and q_blk % hb == 0 and kf_blk % hb == 0 and kb_blk % hb == 0 and v_blk % hb == 0
    bwd = lambda n: _bwd_block(n, tok)
    fwd = lambda n: n

    def stream(blk, k_blk):
        return [
            pl.BlockSpec((tt, hb * dk), lambda h, n: (blk(n), q_blk // hb + h)),
            pl.BlockSpec((tt, hb * dk), lambda h, n: (blk(n), k_blk // hb + h)),
            pl.BlockSpec((tt, hb * dv), lambda h, n: (blk(n), v_blk // hb + h)),
            pl.BlockSpec((tt, hb * dk), lambda h, n: (blk(n), h)),
        ]

    n_batch = tok.n_ctx_blocks
    return pl.pallas_call(
        functools.partial(_gla_scan_kernel, tok=tok, q_silu=q_silu, scale=dk ** -0.5, hb=hb, dk=dk, dv=dv),
        grid=(heads // hb, tok.n_blocks),
        in_specs=stream(fwd, kf_blk) + stream(bwd, kb_blk) + [
            pl.BlockSpec((1, 1, 2, hb, dk, dv), lambda h, n: (_dec_batch(n, tok), j, 0, h, 0, 0)),
        ],
        out_specs=[
            pl.BlockSpec((tt, hb * dv), lambda h, n: (n, h)),
            pl.BlockSpec((tt, hb * dv), lambda h, n: (bwd(n), h)),
            pl.BlockSpec((1, 2, hb, dk, dv), lambda h, n: (_state_block(n, tok), 0, h, 0, 0)),
        ],
        out_shape=[
            jax.ShapeDtypeStruct((m, heads * dv), BF16),
            jax.ShapeDtypeStruct((m, heads * dv), BF16),
            jax.ShapeDtypeStruct((n_batch, 2, heads, dk, dv), F32),
        ],
        scratch_shapes=[pltpu.VMEM((2, hb, dv, dk), F32)],
        compiler_params=_params("parallel", "arbitrary"),
        name="gla_scan",
    )(q_src, kf_src, v_src, cb_f, q_src, kb_src, v_src, cb_b, s0)


def kernel(x_prompt, x_sample, c, state_deltanet, state_gla, state_hgrn, c_ctx, ada_w, ada_b, norm1, norm2, final_norm, w_up, w_down, dn_w_in, dn_conv, dn_A_log, dn_dt_bias, dn_norm, dn_w_out, gla_w_in, gla_w_a2, gla_b_a, gla_norm, gla_w_out, hgrn_w_in, hgrn_lb_logits, hgrn_norm, hgrn_w_out):
    batch, seq, d = x_prompt.shape
    dec_batch, dec_seq, _ = x_sample.shape
    depth = ada_w.shape[0]
    tok = Tokens(m_ctx=batch * seq, seq=seq, dec_batch=dec_batch, dec_seq=dec_seq)
    assert seq % CHUNK == 0 and dec_seq % seq == 0 and seq % GRID_W == 0

    dn_hv, dn_dk, dn_dv = state_deltanet.shape[3:]
    dn_vd = dn_hv * dn_dv
    n_conv = dn_conv.shape[2]
    dn_kd = (n_conv - dn_vd) // 2
    dn_hk = dn_kd // dn_dk
    gla_h, gla_dk, gla_dv = state_gla.shape[3:]
    gla_kd, gla_vd = gla_h * gla_dk, gla_h * gla_dv
    hg_h, hg_dk, hg_dv = state_hgrn.shape[3:]

    x = jnp.concatenate([x_prompt.reshape(tok.m_ctx, d), x_sample.reshape(dec_batch * dec_seq, d)], axis=0)
    n_rows = 1 + dec_batch
    pad = (-n_rows) % 8
    cond = jnp.concatenate([c_ctx[None, :], c, jnp.zeros((pad, d), F32)], axis=0)
    mod_all = _mod_table(cond, ada_w, ada_b).reshape(depth, n_rows + pad, 6, d)

    bf = lambda w: w.astype(BF16)
    w_up_b, w_down_b = bf(w_up), bf(w_down)
    dn_w_in_b, dn_w_out_b = bf(dn_w_in), bf(dn_w_out)
    gla_w_in_b, gla_w_out_b = bf(gla_w_in), bf(gla_w_out)
    hgrn_w_in_b, hgrn_w_out_b = bf(hgrn_w_in), bf(hgrn_w_out)

    new_states = ([], [], [])
    dn_states = None
    for i in range(depth):
        kind, j = i % N_MIXERS, i // N_MIXERS
        mod = mod_all[i]
        if kind == 0:
            n_main = n_conv + dn_vd
            proj, raw = _proj(x, norm1[i], mod, dn_w_in_b, j, n_main, tok,
                              gate_w=dn_w_in, gate_layer=j, gate_col0=n_main)
            qkv = _dn_prep(proj, dn_conv[j], n_conv, dn_kd, dn_dk, tok)
            gates, gates_t = _dn_gates(raw, dn_A_log[j], dn_dt_bias[j], tok)
            o_f, o_b, dn_states = _dn_scan(qkv, gates, gates_t, state_deltanet, j, dn_states, tok,
                                           hk_n=dn_hk, hv=dn_hv, dk=dn_dk, dv=dn_dv)
            x = _out_proj(o_f, o_b, proj, n_conv // dn_vd, dn_norm[j], x, mod, dn_w_out_b, j, tok)
        elif kind == 1:
            n_main = 2 * gla_kd + 2 * gla_vd
            w_a1 = gla_w_in[j:j + 1, :, n_main:]
            proj, a1 = _proj(x, norm1[i], mod, gla_w_in_b, j, n_main, tok, gate_w=w_a1)
            cb_f, cb_b = _gla_prep(a1, gla_w_a2[j], gla_b_a[j], tok)
            o_f, o_b, ns = _gla_scan(proj, 0, proj, gla_h, proj, gla_h, proj, (2 * gla_kd) // gla_dv,
                                     cb_f, cb_b, state_gla, j, tok,
                                     heads=gla_h, dk=gla_dk, dv=gla_dv, q_silu=False, hb=min(gla_h, 2))
            x = _out_proj(o_f, o_b, proj, (2 * gla_kd + gla_vd) // gla_vd, gla_norm[j], x, mod,
                          gla_w_out_b, j, tok)
        else:
            proj = _proj(x, norm1[i], mod, hgrn_w_in_b, j, hgrn_w_in.shape[2], tok)
            k_f, k_b, cb_f, cb_b = _hgrn_prep(proj, hgrn_lb_logits, i, d, tok)
            o_f, o_b, ns = _gla_scan(proj, 0, k_f, 0, k_b, 0, proj, d // hg_dv, cb_f, cb_b, state_hgrn, j, tok,
                                     heads=hg_h, dk=hg_dk, dv=hg_dv, q_silu=True, hb=min(hg_h, 8))
            x = _out_proj(o_f, o_b, proj, 2, hgrn_norm[j], x, mod, hgrn_w_out_b, j, tok)
        if kind != 0:
            new_states[kind].append(ns)
        x = _mlp(x, norm2[i], mod, w_up_b, w_down_b, i, final_norm, tok, final=(i == depth - 1))

    y_ctx, y_dec = x
    y_prompt = y_ctx.reshape(batch, seq, d)
    y_sample = y_dec.reshape(dec_batch, dec_seq, d)
    stacked = [s[0][:, None] if len(s) == 1 else jnp.stack(s, axis=1) for s in new_states[1:]]
    return (y_prompt, y_sample, dn_states) + tuple(stacked)
```

```python
import functools
from typing import NamedTuple

import jax
import jax.numpy as jnp
from jax import lax
from jax.experimental import pallas as pl
from jax.experimental.pallas import tpu as pltpu

F32 = jnp.float32
BF16 = jnp.bfloat16

EPS = 1e-6
GRID_W = 64
CONV_K = 5
GLA_RANK = 16
GLA_TAU = 16.0
N_MIXERS = 3
CHUNK = 64
SUB = 16
DIAG = 4
NEG_BIG = -1e30
LANES = 128
SUBLANES = 8
PROLOGUE_ROWS = 256
V7X_MXU_COLS = 256
V7X_VMEM_LIMIT_BYTES = 56 * 1024 * 1024
HIGHEST = lax.Precision.HIGHEST


class Tokens(NamedTuple):
    m_ctx: int
    seq: int
    dec_batch: int
    dec_seq: int

    @property
    def m(self):
        return self.m_ctx + self.dec_batch * self.dec_seq

    @property
    def n_ctx_blocks(self):
        return self.m_ctx // self.seq

    @property
    def blocks_per_dec(self):
        return self.dec_seq // self.seq

    @property
    def n_blocks(self):
        return self.m // self.seq


def _tile(n, pref):
    t = min(n, pref)
    while n % t:
        t //= 2
    return t


def _token_tile(tok, pref):
    t = pref
    while tok.m_ctx % t or tok.dec_seq % t:
        t //= 2
    return t


def _params(*sem):
    return pltpu.CompilerParams(dimension_semantics=sem, vmem_limit_bytes=V7X_VMEM_LIMIT_BYTES)


def _mod_row(i, tm, tok):
    start = i * tm
    return jnp.where(start < tok.m_ctx, 0, 1 + (start - tok.m_ctx) // tok.dec_seq)


def _sigmoid(x):
    return 0.5 + 0.5 * jnp.tanh(0.5 * x)


def _silu(x):
    half = 0.5 * x
    return half + half * jnp.tanh(half)


def _softplus(x):
    return jnp.maximum(x, 0.0) + jnp.log(1.0 + jnp.exp(-jnp.abs(x)))


def _norm_mod(x, nw, shift, scale):
    ms = jnp.mean(x * x, axis=-1, keepdims=True)
    return x * lax.rsqrt(ms + EPS) * nw * (1.0 + scale) + shift


def _dot(a, b):
    return jnp.dot(a, b, preferred_element_type=F32)


def _dot_nt(a, b):
    return lax.dot_general(a, b, (((1,), (1,)), ((), ())), preferred_element_type=F32)


def _dot_tn(a, b):
    return lax.dot_general(a, b, (((0,), (0,)), ((), ())), preferred_element_type=F32)


def _split_dot(tri, x):
    hi = x.astype(BF16)
    lo = (x - hi.astype(F32)).astype(BF16)
    return _dot(tri, hi) + _dot(tri, lo)


def _split_matmul(a, b):
    ah = a.astype(BF16)
    al = (a - ah.astype(F32)).astype(BF16)
    bh = b.astype(BF16)
    bl = (b - bh.astype(F32)).astype(BF16)
    return _dot(ah, bh) + _dot(al, bh) + _dot(ah, bl)


def _tri(n, upper):
    r = lax.broadcasted_iota(jnp.int32, (n, n), 0)
    c = lax.broadcasted_iota(jnp.int32, (n, n), 1)
    return jnp.where((r <= c) if upper else (r >= c), 1.0, 0.0).astype(BF16)


def _mod_kernel(c_ref, w_ref, b_ref, o_ref):
    c = c_ref[...]
    s = _silu(c)
    o_ref[0] = jnp.dot(s, w_ref[0], precision=HIGHEST, preferred_element_type=F32) + b_ref[0]


def _mod_table(cond, ada_w, ada_b):
    depth, d, n = ada_w.shape
    rows = cond.shape[0]
    tn = _tile(n, 2048)
    return pl.pallas_call(
        _mod_kernel,
        grid=(depth, n // tn),
        in_specs=[
            pl.BlockSpec((rows, d), lambda l, j: (0, 0)),
            pl.BlockSpec((1, d, tn), lambda l, j: (l, 0, j)),
            pl.BlockSpec((1, 1, tn), lambda l, j: (l, 0, j)),
        ],
        out_specs=pl.BlockSpec((1, rows, tn), lambda l, j: (l, 0, j)),
        out_shape=jax.ShapeDtypeStruct((depth, rows, n), F32),
        compiler_params=_params("parallel", "parallel"),
        name="mod_table",
    )(cond, ada_w, ada_b.reshape(depth, 1, n))


def _proj_kernel(x_ref, nw_ref, mod_ref, w_ref, *rest, gated):
    if gated:
        wg_ref, o_ref, og_ref, h_ref = rest
    else:
        o_ref, h_ref = rest

    @pl.when(pl.program_id(1) == 0)
    def _():
        tm = x_ref.shape[0]
        step = min(tm, PROLOGUE_ROWS)
        for r in range(0, tm, step):
            rows = slice(r, r + step)
            h = _norm_mod(x_ref[rows, :], nw_ref[...], mod_ref[0, 0:1, :], mod_ref[0, 1:2, :])
            h_ref[rows, :] = h.astype(BF16)
            if gated:
                og_ref[rows, :] = _split_matmul(h, wg_ref[...])

    o_ref[...] = _dot(h_ref[...], w_ref[...]).astype(o_ref.dtype)


def _proj(x, nw, mod, w, layer, n, tok, *, gate_w=None, gate_layer=0, gate_col0=0, tm=1024, tn=2048):
    m, d = x.shape
    tm = _token_tile(tok, tm)
    tn = _tile(n, tn)
    gated = gate_w is not None
    in_specs = [
        pl.BlockSpec((tm, d), lambda i, j: (i, 0)),
        pl.BlockSpec((1, d), lambda i, j: (0, 0)),
        pl.BlockSpec((1, 6, d), lambda i, j: (_mod_row(i, tm, tok), 0, 0)),
        pl.BlockSpec((None, d, tn), lambda i, j: (layer, 0, j)),
    ]
    out_specs = [pl.BlockSpec((tm, tn), lambda i, j: (i, j))]
    out_shape = [jax.ShapeDtypeStruct((m, n), BF16)]
    args = [x, nw.reshape(1, d), mod, w]
    if gated:
        ng = gate_w.shape[2] - gate_col0
        assert gate_col0 % ng == 0
        in_specs.append(pl.BlockSpec((None, d, ng), lambda i, j: (gate_layer, 0, gate_col0 // ng)))
        out_specs.append(pl.BlockSpec((tm, ng), lambda i, j: (i, 0)))
        out_shape.append(jax.ShapeDtypeStruct((m, ng), F32))
        args.append(gate_w)
    out = pl.pallas_call(
        functools.partial(_proj_kernel, gated=gated),
        grid=(m // tm, n // tn),
        in_specs=in_specs,
        out_specs=out_specs,
        out_shape=out_shape,
        scratch_shapes=[pltpu.VMEM((tm, d), BF16)],
        compiler_params=_params("parallel", "arbitrary"),
        name="proj_gated" if gated else "proj",
    )(*args)
    return out if gated else out[0]


def _mlp_kernel(x_ref, nw_ref, mod_ref, wu_ref, wd_ref, fw_ref, *rest, n_ctx_tiles):
    out_refs, (h_ref, acc_ref) = rest[:-2], rest[-2:]
    final = n_ctx_tiles is not None
    i = pl.program_id(0)
    f = pl.program_id(1)

    @pl.when(f == 0)
    def _():
        h = _norm_mod(x_ref[...], nw_ref[...], mod_ref[0, 3:4, :], mod_ref[0, 4:5, :])
        h_ref[...] = h.astype(BF16)
        acc_ref[...] = jnp.zeros_like(acc_ref)

    a = jnp.maximum(_dot(h_ref[...], wu_ref[...]), 0.0)
    acc_ref[...] += _dot((a * a).astype(BF16), wd_ref[...])

    last = f == pl.num_programs(1) - 1

    def result():
        y = x_ref[...] + mod_ref[0, 5:6, :] * acc_ref[...]
        if final:
            ms = jnp.mean(y * y, axis=-1, keepdims=True)
            y = y * lax.rsqrt(ms + EPS) * fw_ref[...]
        return y

    if not final:
        @pl.when(last)
        def _():
            out_refs[0][...] = result()
    else:
        @pl.when(jnp.logical_and(last, i < n_ctx_tiles))
        def _():
            out_refs[0][...] = result()

        @pl.when(jnp.logical_and(last, i >= n_ctx_tiles))
        def _():
            out_refs[1][...] = result()


def _mlp(x, nw, mod, w_up, w_down, layer, final_w, tok, *, final, tm=512, tf=1024):
    m, d = x.shape
    ff = w_up.shape[2]
    tm = _token_tile(tok, tm)
    tf = _tile(ff, tf)
    n_ctx_tiles = tok.m_ctx // tm
    if final:
        out_specs = [pl.BlockSpec((tm, d), lambda i, f: (jnp.minimum(i, n_ctx_tiles - 1), 0)),
                     pl.BlockSpec((tm, d), lambda i, f: (jnp.maximum(i - n_ctx_tiles, 0), 0))]
        out_shape = [jax.ShapeDtypeStruct((tok.m_ctx, d), F32), jax.ShapeDtypeStruct((m - tok.m_ctx, d), F32)]
    else:
        out_specs = pl.BlockSpec((tm, d), lambda i, f: (i, 0))
        out_shape = jax.ShapeDtypeStruct((m, d), F32)
    return pl.pallas_call(
        functools.partial(_mlp_kernel, n_ctx_tiles=n_ctx_tiles if final else None),
        grid=(m // tm, ff // tf),
        in_specs=[
            pl.BlockSpec((tm, d), lambda i, f: (i, 0)),
            pl.BlockSpec((1, d), lambda i, f: (0, 0)),
            pl.BlockSpec((1, 6, d), lambda i, f: (_mod_row(i, tm, tok), 0, 0)),
            pl.BlockSpec((None, d, tf), lambda i, f: (layer, 0, f)),
            pl.BlockSpec((None, tf, d), lambda i, f: (layer, f, 0)),
            pl.BlockSpec((1, d), lambda i, f: (0, 0)),
        ],
        out_specs=out_specs,
        out_shape=out_shape,
        scratch_shapes=[pltpu.VMEM((tm, d), BF16), pltpu.VMEM((tm, d), F32)],
        compiler_params=_params("arbitrary" if final else "parallel", "arbitrary"),
        name="mlp_final" if final else "mlp",
    )(x, nw.reshape(1, d), mod, w_up, w_down, final_w.reshape(1, d))


def _out_kernel(of_ref, ob_ref, g_ref, nw_ref, x_ref, mod_ref, w_ref, o_ref, y0_ref, y1_ref, *, n_heads, dh):
    i = pl.program_id(0)
    d = w_ref.shape[1]
    n_chunks = d // V7X_MXU_COLS
    heads_per_chunk = -(-n_heads // n_chunks)

    @pl.when(i == 0)
    def _():
        y1_ref[...] = jnp.zeros_like(y1_ref)

    def body(y_write, y_read):
        ones = jnp.full((dh, LANES), 1.0 / dh, BF16) if dh & (dh - 1) == 0 else None
        for c in range(n_chunks):
            cols = slice(c * V7X_MXU_COLS, (c + 1) * V7X_MXU_COLS)
            o_ref[:, cols] = x_ref[:, cols] + mod_ref[0, 2:3, cols] * _dot(y_read[...], w_ref[:, cols])
            for h in range(c * heads_per_chunk, min((c + 1) * heads_per_chunk, n_heads)):
                sl = slice(h * dh, (h + 1) * dh)
                o = of_ref[:, sl].astype(F32) + ob_ref[:, sl].astype(F32)
                if ones is None:
                    ms = _dot((o * o).astype(BF16), jnp.ones((dh, LANES), BF16)) * (1.0 / dh)
                else:
                    ms = _dot((o * o).astype(BF16), ones)
                inv = lax.rsqrt(ms + EPS)
                y = o * nw_ref[...] * _silu(g_ref[:, sl].astype(F32))
                for k in range(dh // LANES):
                    lo = h * dh + k * LANES
                    y_write[:, lo:lo + LANES] = (y[:, k * LANES:(k + 1) * LANES] * inv).astype(BF16)

    @pl.when(i % 2 == 0)
    def _():
        body(y0_ref, y1_ref)

    @pl.when(i % 2 == 1)
    def _():
        body(y1_ref, y0_ref)


def _out_proj(o_f, o_b, gate_src, gate_blk, norm_w, x, mod, w_out, layer, tok, *, tm=256):
    m, vd = o_f.shape
    d = x.shape[1]
    dh = norm_w.shape[0]
    tm = _token_tile(tok, tm)
    n_tiles = m // tm
    assert d % V7X_MXU_COLS == 0 and dh % LANES == 0
    cur = lambda i: jnp.minimum(i, n_tiles - 1)
    prev = lambda i: jnp.maximum(i - 1, 0)
    return pl.pallas_call(
        functools.partial(_out_kernel, n_heads=vd // dh, dh=dh),
        grid=(n_tiles + 1,),
        in_specs=[
            pl.BlockSpec((tm, vd), lambda i: (cur(i), 0)),
            pl.BlockSpec((tm, vd), lambda i: (cur(i), 0)),
            pl.BlockSpec((tm, vd), lambda i: (cur(i), gate_blk)),
            pl.BlockSpec((1, dh), lambda i: (0, 0)),
            pl.BlockSpec((tm, d), lambda i: (prev(i), 0)),
            pl.BlockSpec((1, 6, d), lambda i: (_mod_row(prev(i), tm, tok), 0, 0)),
            pl.BlockSpec((None, vd, d), lambda i: (layer, 0, 0), pipeline_mode=pl.Buffered(1)),
        ],
        out_specs=pl.BlockSpec((tm, d), lambda i: (prev(i), 0)),
        out_shape=jax.ShapeDtypeStruct((m, d), F32),
        scratch_shapes=[pltpu.VMEM((tm, vd), BF16), pltpu.VMEM((tm, vd), BF16)],
        compiler_params=_params("arbitrary"),
        name="out_proj",
    )(o_f, o_b, gate_src, norm_w.reshape(1, dh), x, mod, w_out)


def _bwd_block(n, tok):
    nc, per = tok.n_ctx_blocks, tok.blocks_per_dec
    r = jnp.maximum(n - nc, 0)
    return jnp.where(n < nc, n, nc + (r // per) * per + (per - 1 - r % per))


def _dec_batch(n, tok):
    return jnp.maximum(n - tok.n_ctx_blocks, 0) // tok.blocks_per_dec


def _state_block(n, tok):
    return jnp.minimum(n, tok.n_ctx_blocks - 1)


def _seq_flags(n, tok):
    nc, per = tok.n_ctx_blocks, tok.blocks_per_dec
    is_ctx = n < nc
    r = jnp.maximum(n - nc, 0) % per
    return is_ctx, jnp.logical_and(jnp.logical_not(is_ctx), r == 0)


def _dn_prep_kernel(x_ref, shift_ref, w_ref, o_ref, *, n_q_blocks, dk):
    j = pl.program_id(1)
    x = x_ref[...]
    tc = x.shape[1]
    acc = None
    for s in range(CONV_K):
        xs = x.astype(F32) if s == CONV_K // 2 else _dot(shift_ref[0, s], x)
        term = xs * w_ref[s:s + 1, :]
        acc = term if acc is None else acc + term
    y = _silu(acc)

    def normed(scale):
        ones = jnp.ones((dk, dk), BF16)
        for g in range(tc // dk):
            blk = y[:, g * dk:(g + 1) * dk]
            ss = _dot((blk * blk).astype(BF16), ones)
            o_ref[:, g * dk:(g + 1) * dk] = (blk * (lax.rsqrt(ss + EPS) * scale)).astype(o_ref.dtype)

    @pl.when(j < n_q_blocks)
    def _():
        normed(dk ** -0.5)

    @pl.when(jnp.logical_and(j >= n_q_blocks, j < 2 * n_q_blocks))
    def _():
        normed(1.0)

    @pl.when(j >= 2 * n_q_blocks)
    def _():
        o_ref[...] = y.astype(o_ref.dtype)


def _conv_shift_matrices(tt):
    t = jnp.arange(tt)[:, None]
    u = jnp.arange(tt)[None, :]
    mats = []
    for seg in (tt, GRID_W):
        same = (t // seg) == (u // seg)
        mats.append(jnp.stack([jnp.logical_and(u == t + s - CONV_K // 2, same) for s in range(CONV_K)]))
    return jnp.stack(mats).astype(BF16)


def _dn_prep(proj, conv_w, n_conv, kd, dk, tok):
    m = proj.shape[0]
    tt = tok.seq
    tc = _tile(kd, 2048)
    n_ctx = tok.n_ctx_blocks
    return pl.pallas_call(
        functools.partial(_dn_prep_kernel, n_q_blocks=kd // tc, dk=dk),
        grid=(m // tt, n_conv // tc),
        in_specs=[
            pl.BlockSpec((tt, tc), lambda i, j: (i, j)),
            pl.BlockSpec((1, CONV_K, tt, tt), lambda i, j: (jnp.where(i < n_ctx, 0, 1), 0, 0, 0)),
            pl.BlockSpec((CONV_K, tc), lambda i, j: (0, j)),
        ],
        out_specs=pl.BlockSpec((tt, tc), lambda i, j: (i, j)),
        out_shape=jax.ShapeDtypeStruct((m, n_conv), BF16),
        compiler_params=_params("parallel", "parallel"),
        name="dn_prep",
    )(proj, _conv_shift_matrices(tt), conv_w)


def _dn_gate_kernel(x_ref, a_ref, bias_ref, o_ref, gt_ref, *, hv, tt):
    x = x_ref[...]
    lanes = x.shape[1]
    lane = lax.broadcasted_iota(jnp.int32, (1, lanes), 1)
    is_fwd = lane < 3 * hv
    g = -a_ref[...] * _softplus(x + bias_ref[...])
    beta = pltpu.roll(_sigmoid(x), 2 * hv, 1)
    lower = _tri(CHUNK, upper=False)
    upper = _tri(CHUNK, upper=True)
    for c in range(tt // CHUNK):
        rows = slice(c * CHUNK, (c + 1) * CHUNK)
        gch = g[rows]
        gc = jnp.where(is_fwd, _split_dot(lower, gch), _split_dot(upper, gch))
        g_end = jnp.where(is_fwd, gc[CHUNK - 1:CHUNK, :], gc[0:1, :])
        eg = jnp.exp(gc)
        o_ref[0, rows, :] = beta[rows]
        o_ref[1, rows, :] = gc
        o_ref[2, rows, :] = eg
        o_ref[3, rows, :] = jnp.exp(g_end - gc)
        o_ref[4, rows, :] = beta[rows] * eg
    gt_ref[...] = o_ref[1].T


def _dn_gates(raw, a_log, dt_bias, tok):
    m, lanes = raw.shape
    hv = lanes // 4
    tt = tok.seq
    zeros = jnp.zeros((2 * hv,), F32)
    a_full = jnp.concatenate([zeros, jnp.exp(a_log).reshape(-1)]).reshape(1, lanes)
    bias_full = jnp.concatenate([zeros, dt_bias.reshape(-1)]).reshape(1, lanes)
    return pl.pallas_call(
        functools.partial(_dn_gate_kernel, hv=hv, tt=tt),
        grid=(m // tt,),
        in_specs=[
            pl.BlockSpec((tt, lanes), lambda i: (i, 0)),
            pl.BlockSpec((1, lanes), lambda i: (0, 0)),
            pl.BlockSpec((1, lanes), lambda i: (0, 0)),
        ],
        out_specs=[
            pl.BlockSpec((5, tt, lanes), lambda i: (0, i, 0)),
            pl.BlockSpec((lanes, tt), lambda i: (0, i)),
        ],
        out_shape=[jax.ShapeDtypeStruct((5, m, lanes), F32), jax.ShapeDtypeStruct((lanes, m), F32)],
        compiler_params=_params("parallel"),
        name="dn_gates",
    )(raw, a_full, bias_full)


def _dn_scan_kernel(qf_ref, kf_ref, vf_ref, gf_ref, gtf_ref, qb_ref, kb_ref, vb_ref, gb_ref, gtb_ref, s0_ref,
                    of_ref, ob_ref, ns_ref, s_ref, u0_ref, wq_ref, attn_ref, kd_ref, ge_ref, *, tok, hv, dv, hkb):
    hk0 = pl.program_id(0) * hkb
    n = pl.program_id(1)
    tt = tok.seq
    nch = tt // CHUNK
    dk = qf_ref.shape[1] // hkb
    assert dv == 2 * CHUNK and CHUNK == 4 * SUB
    has_prev = n > 0
    is_ctx, dec_start = _seq_flags(jnp.maximum(n - 1, 0), tok)
    wslot = n % 2
    rslot = 1 - wslot
    scratch = (u0_ref, wq_ref, attn_ref, kd_ref, ge_ref)

    @pl.when(n == 0)
    def _():
        s_ref[...] = jnp.zeros_like(s_ref)
        for ref in scratch:
            ref[1] = jnp.zeros(ref.shape[1:], ref.dtype)

    @pl.when(jnp.logical_and(has_prev, is_ctx))
    def _():
        s_ref[...] = jnp.zeros_like(s_ref)

    @pl.when(jnp.logical_and(has_prev, dec_start))
    def _():
        s_ref[...] = s0_ref[0, 0]

    bf = lambda x: x.astype(BF16)
    lanes = gf_ref.shape[2]
    lane_iota = lax.broadcasted_iota(jnp.int32, (1, lanes), 1)
    ri = lax.broadcasted_iota(jnp.int32, (CHUNK, 2 * CHUNK), 0)
    li = lax.broadcasted_iota(jnp.int32, (CHUNK, 2 * CHUNK), 1)
    left = li < CHUNK
    ci = li % CHUNK
    same = [(ri // b) == (ci // b) for b in (SUB, 2 * SUB)]
    zero_b = jnp.zeros((CHUNK, 2 * CHUNK), BF16)

    def block_diag(xb):
        return jnp.concatenate([jnp.where(left, xb, zero_b), jnp.where(left, zero_b, xb)], axis=0)

    def pair_dot(a, b):
        return _dot(bf(a), block_diag(bf(b)))

    dirs = ((qf_ref, kf_ref, vf_ref, gf_ref, gtf_ref, of_ref), (qb_ref, kb_ref, vb_ref, gb_ref, gtb_ref, ob_ref))

    def prepare():
        ch = {}
        for d, kh in [(d, kh) for d in range(2) for kh in range(hkb)]:
            q_ref, k_ref, v_ref, g_ref, gt_ref, _ = dirs[d]
            incl = (ri >= ci) if d == 0 else (ri <= ci)
            strict = (ri > ci) if d == 0 else (ri < ci)
            end_row = CHUNK - 1 if d == 0 else 0
            q = q_ref[:, kh * dk:(kh + 1) * dk]
            k = k_ref[:, kh * dk:(kh + 1) * dk]
            kf32 = k.astype(F32)
            qf32 = q.astype(F32)
            cols, g_rows, vs = [], [], []
            for e in range(2):
                a_lane = 2 * hv + d * hv + 2 * (hk0 + kh) + e
                sel = lane_iota == a_lane
                cols.append([jnp.sum(jnp.where(sel, g_ref[i], 0.0), axis=1, keepdims=True) for i in range(5)])
                g_rows.append(gt_ref[pl.ds(a_lane, 1), :])
                vs.append(v_ref[:, (2 * kh + e) * dv:(2 * kh + e + 1) * dv].astype(F32))
            for c in range(nch):
                rows = slice(c * CHUNK, (c + 1) * CHUNK)
                pair = lambda i: jnp.where(left, cols[0][i][rows], cols[1][i][rows])
                k2 = jnp.concatenate([k[rows], k[rows]], axis=0)
                kk = _dot_nt(k[rows], k2)
                qk = _dot_nt(q[rows], k2)
                g_row = jnp.concatenate([g_rows[0][:, rows], g_rows[1][:, rows]], axis=1)
                diff = pair(1) - g_row
                decay = jnp.where(incl, jnp.exp(jnp.where(incl, diff, 0.0)), 0.0)
                p = jnp.where(strict, -(pair(0) * kk * decay), 0.0)
                attn_ref[wslot, d, kh, c] = bf(qk * decay)
                for e in range(2):
                    beta_c, _, eg_c, ekd_c, beg_c = cols[e]
                    qg = qf32[rows] * eg_c[rows]
                    kd_ref[wslot, d, 2 * kh + e, c] = bf(kf32[rows] * ekd_c[rows])
                    wq_ref[wslot, d, 2 * kh + e, c, CHUNK:, :] = bf(qg)
                    g_end = eg_c[c * CHUNK + end_row:c * CHUNK + end_row + 1, :]
                    ge_ref[wslot, d, 2 * kh + e, c] = jnp.broadcast_to(g_end, ge_ref.shape[4:])
                ch[kh, d, c] = dict(
                    pd=jnp.where(same[0], p, 0.0),
                    p32=jnp.where(jnp.logical_and(same[1], jnp.logical_not(same[0])), p, 0.0),
                    p64=jnp.where(same[1], 0.0, p),
                    r=[jnp.concatenate([cols[e][0][rows] * vs[e][rows], cols[e][4][rows] * kf32[rows]], axis=1)
                       for e in range(2)])
            yield
        for c in ch.values():
            c["q"] = pair_dot(c["pd"], c["pd"])
            c["xm"] = c["pd"]
        yield
        for it in range(2):
            for c in ch.values():
                q, xm = c["q"], c["xm"]
                both = _dot(bf(q), jnp.concatenate([block_diag(bf(xm)), block_diag(bf(q))], axis=1))
                c["xm"] = xm + q + both[:, :2 * CHUNK]
                c["q"] = both[:, 2 * CHUNK:]
            yield
        for c in ch.values():
            q, xm = c.pop("q"), c["xm"]
            c["xm"] = xm + q + pair_dot(q, xm)
        yield
        for level in ("p32", "p64"):
            for c in ch.values():
                c["y"] = c[level] + pair_dot(c[level], c["xm"])
            yield
            for c in ch.values():
                xm, y = c["xm"], c.pop("y")
                c["xm"] = xm + y + pair_dot(xm, y)
            yield
        zero_r = jnp.zeros((CHUNK, 2 * dv), BF16)
        for (kh, d, cidx), c in ch.items():
            r0, r1 = c["r"]
            rhs = jnp.concatenate([jnp.concatenate([bf(r0), zero_r], axis=1),
                                   jnp.concatenate([zero_r, bf(r1)], axis=1)], axis=0)
            big = _dot(bf(c["xm"]), rhs)
            sol = [r0 + big[:, :2 * dv], r1 + big[:, 2 * dv:]]
            u0_ref[wslot, d, kh, cidx] = jnp.concatenate([sol[0][:, :dv], sol[1][:, :dv]], axis=1)
            for e in range(2):
                wq_ref[wslot, d, 2 * kh + e, cidx, :CHUNK, :] = bf(sol[e][:, dv:])
        yield

    def recur():
        state = {(d, e): s_ref[d, e] for d in range(2) for e in range(2 * hkb)}
        zero_u = jnp.zeros((CHUNK, dv), BF16)
        for step in range(nch):
            chunk = lambda d: step if d == 0 else nch - 1 - step
            ws = {(d, e): _dot(wq_ref[rslot, d, e, chunk(d)], bf(s)) for (d, e), s in state.items()}
            yield
            for d, kh in [(d, kh) for d in range(2) for kh in range(hkb)]:
                c = chunk(d)
                u0 = u0_ref[rslot, d, kh, c]
                ev = [2 * kh, 2 * kh + 1]
                ub = [bf(u0[:, e * dv:(e + 1) * dv] - ws[d, ev[e]][:CHUNK]) for e in range(2)]
                u2 = jnp.concatenate([jnp.concatenate([ub[0], zero_u], axis=1),
                                      jnp.concatenate([zero_u, ub[1]], axis=1)], axis=0)
                o = (_dot(attn_ref[rslot, d, kh, c], u2)
                     + jnp.concatenate([ws[d, ev[0]][CHUNK:], ws[d, ev[1]][CHUNK:]], axis=1))
                dirs[d][5][c * CHUNK:(c + 1) * CHUNK, 2 * kh * dv:2 * (kh + 1) * dv] = bf(o)
                for e in range(2):
                    state[d, ev[e]] = (ge_ref[rslot, d, ev[e], c][0:1, :] * state[d, ev[e]]
                                       + _dot_tn(kd_ref[rslot, d, ev[e], c], ub[e]))
            yield
        for (d, e), s in state.items():
            s_ref[d, e] = s
        yield

    halves = [recur(), prepare()]
    while halves:
        for g in list(halves):
            if next(g, "done") == "done":
                halves.remove(g)

    @pl.when(jnp.logical_and(has_prev, is_ctx))
    def _():
        ns_ref[0, 0] = s_ref[...]


def _dn_scan_kernel_aliased(*refs, **kw):
    n_in = 11
    return _dn_scan_kernel(*refs[:n_in], *refs[n_in + 1:], **kw)


def _dn_scan(qkv, gates, gates_t, s0, j, new_states, tok, *, hk_n, hv, dk, dv):
    m = qkv.shape[0]
    tt = tok.seq
    rep = hv // hk_n
    assert rep == 2
    lanes = gates.shape[2]
    kd = hk_n * dk
    nb = tok.n_blocks
    nch = tt // CHUNK
    fwd = lambda n: jnp.minimum(n, nb - 1)
    bwd = lambda n: _bwd_block(jnp.minimum(n, nb - 1), tok)
    prev = lambda n: jnp.maximum(n - 1, 0)

    hkb = max(b for b in (4, 2, 1) if hk_n % b == 0)
    hvb = hkb * rep

    def stream(blk):
        return [
            pl.BlockSpec((tt, hkb * dk), lambda h, n: (blk(n), h)),
            pl.BlockSpec((tt, hkb * dk), lambda h, n: (blk(n), hk_n // hkb + h)),
            pl.BlockSpec((tt, hvb * dv), lambda h, n: (blk(n), (2 * kd) // (hvb * dv) + h)),
            pl.BlockSpec((5, tt, lanes), lambda h, n: (0, blk(n), 0)),
            pl.BlockSpec((lanes, tt), lambda h, n: (0, blk(n))),
        ]

    n_batch = tok.n_ctx_blocks
    n_layers = s0.shape[1]
    aliased = new_states is not None
    extra_in = [pl.BlockSpec(memory_space=pl.ANY)] if aliased else []
    extra_args = (new_states,) if aliased else ()
    return pl.pallas_call(
        functools.partial(_dn_scan_kernel_aliased if aliased else _dn_scan_kernel, tok=tok, hv=hv, dv=dv, hkb=hkb),
        grid=(hk_n // hkb, nb + 1),
        in_specs=stream(fwd) + stream(bwd) + [
            pl.BlockSpec((1, 1, 2, hvb, dk, dv), lambda h, n: (_dec_batch(prev(n), tok), j, 0, h, 0, 0)),
        ] + extra_in,
        out_specs=[
            pl.BlockSpec((tt, hvb * dv), lambda h, n: (prev(n), h)),
            pl.BlockSpec((tt, hvb * dv), lambda h, n: (_bwd_block(prev(n), tok), h)),
            pl.BlockSpec((1, 1, 2, hvb, dk, dv), lambda h, n: (_state_block(prev(n), tok), j, 0, h, 0, 0)),
        ],
        out_shape=[
            jax.ShapeDtypeStruct((m, hv * dv), BF16),
            jax.ShapeDtypeStruct((m, hv * dv), BF16),
            jax.ShapeDtypeStruct((n_batch, n_layers, 2, hv, dk, dv), F32),
        ],
        input_output_aliases={11: 2} if aliased else {},
        scratch_shapes=[
            pltpu.VMEM((2, hvb, dk, dv), F32),
            pltpu.VMEM((2, 2, hkb, nch, CHUNK, rep * dv), F32),
            pltpu.VMEM((2, 2, hvb, nch, 2 * CHUNK, dk), BF16),
            pltpu.VMEM((2, 2, hkb, nch, CHUNK, rep * CHUNK), BF16),
            pltpu.VMEM((2, 2, hvb, nch, CHUNK, dk), BF16),
            pltpu.VMEM((2, 2, hvb, nch, SUBLANES, dv), F32),
        ],
        compiler_params=_params("parallel", "arbitrary"),
        name="dn_scan",
    )(qkv, qkv, qkv, gates, gates_t, qkv, qkv, qkv, gates, gates_t, s0, *extra_args)


def _chunk_cums(la_f, la_b, cbf_ref, cbb_ref, tt):
    lower = _tri(CHUNK, upper=False)
    upper = _tri(CHUNK, upper=True)
    for c in range(tt // CHUNK):
        rows = slice(c * CHUNK, (c + 1) * CHUNK)
        cbf_ref[rows, :] = _split_dot(lower, la_f[rows])
        cbb_ref[rows, :] = _split_dot(upper, la_b[rows])


def _log_sigmoid(x):
    return jnp.minimum(x, 0.0) - jnp.log(1.0 + jnp.exp(-jnp.abs(x)))


def _gla_prep_kernel(a1_ref, w2_ref, b_ref, cbf_ref, cbb_ref, *, tt):
    a1 = a1_ref[...]
    la = [_log_sigmoid(_dot(a1, w2_ref[d]) + b_ref[d]) * (1.0 / GLA_TAU) for d in range(2)]
    _chunk_cums(la[0], la[1], cbf_ref, cbb_ref, tt)


def _gla_prep(a1, w_a2, b_a, tok):
    m, r2 = a1.shape
    kd = w_a2.shape[2]
    tt = tok.seq
    tc = _tile(kd, 1024)
    zeros = jnp.zeros((GLA_RANK, kd), F32)
    w2 = jnp.stack([jnp.concatenate([w_a2[0], zeros]), jnp.concatenate([zeros, w_a2[1]])])
    out = jax.ShapeDtypeStruct((m, kd), F32)
    return pl.pallas_call(
        functools.partial(_gla_prep_kernel, tt=tt),
        grid=(m // tt, kd // tc),
        in_specs=[
            pl.BlockSpec((tt, r2), lambda i, j: (i, 0)),
            pl.BlockSpec((2, r2, tc), lambda i, j: (0, 0, j)),
            pl.BlockSpec((2, 1, tc), lambda i, j: (0, 0, j)),
        ],
        out_specs=[pl.BlockSpec((tt, tc), lambda i, j: (i, j))] * 2,
        out_shape=[out, out],
        compiler_params=_params("parallel", "parallel"),
        name="gla_prep",
    )(a1, w2, b_a.reshape(2, 1, kd))


def _hgrn_prep_kernel(ff_ref, fb_ref, lb_ref, kf_ref, kb_ref, cbf_ref, cbb_ref, *, layer, tt):
    logits = lb_ref[...]
    ex = jnp.exp(logits - jnp.max(logits, axis=0, keepdims=True))
    probs = ex / jnp.sum(ex, axis=0, keepdims=True)
    lb = jnp.sum(probs[1:layer + 1], axis=0, keepdims=True)
    la = []
    for f_ref, k_ref in ((ff_ref, kf_ref), (fb_ref, kb_ref)):
        fl = f_ref[...].astype(F32)
        sg = 1.0 / (1.0 + jnp.exp(-fl))
        s = lb + (1.0 - lb) * sg
        la.append(jnp.where(s > 0.0, jnp.log(jnp.where(s > 0.0, s, 1.0)), fl))
        k_ref[...] = (1.0 - lb) * (1.0 - sg)
    _chunk_cums(la[0], la[1], cbf_ref, cbb_ref, tt)


def _hgrn_prep(proj, lb_logits, layer, d, tok):
    m = proj.shape[0]
    depth = lb_logits.shape[0]
    tt = tok.seq
    tc = _tile(d, 2048)
    nb = d // tc
    out = jax.ShapeDtypeStruct((m, d), F32)
    return pl.pallas_call(
        functools.partial(_hgrn_prep_kernel, layer=layer, tt=tt),
        grid=(m // tt, nb),
        in_specs=[
            pl.BlockSpec((tt, tc), lambda i, j: (i, 3 * nb + j)),
            pl.BlockSpec((tt, tc), lambda i, j: (i, 4 * nb + j)),
            pl.BlockSpec((depth, tc), lambda i, j: (0, j)),
        ],
        out_specs=[pl.BlockSpec((tt, tc), lambda i, j: (i, j))] * 4,
        out_shape=[out] * 4,
        compiler_params=_params("parallel", "parallel"),
        name="hgrn_prep",
    )(proj, proj, lb_logits)


def _gla_offdiag_pairs(c):
    pairs = []
    size = c
    while size > SUB:
        half = size // 2
        for start in range(0, c, size):
            pairs.append(((start + half, start + size), (start, start + half)))
        size = half
    return pairs


def _gla_operands(q, k, cb, rev):
    c, dk = q.shape
    zero_rows = lambda n: jnp.zeros((n, dk), BF16)

    def padded(x, rows):
        parts = ([zero_rows(rows[0])] if rows[0] else []) + [x] + ([zero_rows(c - rows[1])] if rows[1] < c else [])
        return jnp.concatenate(parts, axis=0) if len(parts) > 1 else x

    q_parts, k_parts = [], []
    for qr, kr in _gla_offdiag_pairs(c):
        if rev:
            qr, kr = kr, qr
        ref_row = qr[0] - 1 if not rev else qr[1]
        ref = cb[ref_row:ref_row + 1, :]
        qs, ks = slice(*qr), slice(*kr)
        q_parts.append(padded((q[qs] * jnp.exp(cb[qs] - ref)).astype(BF16), qr))
        k_parts.append(padded((k[ks] * jnp.exp(ref - cb[ks])).astype(BF16), kr))
    row = lax.broadcasted_iota(jnp.int32, (c, 1), 0)
    levels = []
    half = SUB // 2
    while half >= DIAG:
        size = 2 * half
        refs = []
        for start in range(0, c, size):
            r = start + half if rev else start + half - 1
            refs.append(jnp.broadcast_to(cb[r:r + 1, :], (size, dk)))
        ref = jnp.concatenate(refs, axis=0)
        later = (row % size) >= half
        is_q = jnp.logical_not(later) if rev else later
        qt = (q * jnp.exp(jnp.where(is_q, cb - ref, NEG_BIG))).astype(BF16)
        kt = (k * jnp.exp(jnp.where(is_q, NEG_BIG, ref - cb))).astype(BF16)
        levels.append((qt, kt))
        half //= 2
    diag = [q.astype(BF16)]
    for d in range(1, DIAG):
        shift = c - d if rev else d
        valid = (row % DIAG) < DIAG - d if rev else (row % DIAG) >= d
        diag.append((q * jnp.exp(jnp.where(valid, cb - pltpu.roll(cb, shift, 0), NEG_BIG))).astype(BF16))
    return jnp.concatenate(q_parts, axis=1), jnp.concatenate(k_parts, axis=1), levels, diag, k.astype(BF16)


def _gla_level_mask(c, size):
    ri = lax.broadcasted_iota(jnp.int32, (c, c), 0)
    ci = lax.broadcasted_iota(jnp.int32, (c, c), 1)
    return (ri // size) == (ci // size)


def _gla_subdiag_mask(c, d, rev):
    ri = lax.broadcasted_iota(jnp.int32, (c, c), 0)
    ci = lax.broadcasted_iota(jnp.int32, (c, c), 1)
    on = (ci - ri == d) if rev else (ri - ci == d)
    return jnp.logical_and(on, (ri // DIAG) == (ci // DIAG))


def _gla_scan_kernel(qf_ref, kf_ref, vf_ref, cf_ref, qb_ref, kb_ref, vb_ref, cb_ref, s0_ref,
                     of_ref, ob_ref, ns_ref, s_ref, *, tok, q_silu, scale, hb, dk, dv):
    n = pl.program_id(1)
    tt = tok.seq
    nch = tt // CHUNK
    is_ctx, dec_start = _seq_flags(n, tok)
    streams = [(d, h) for d in range(2) for h in range(hb)]

    @pl.when(is_ctx)
    def _():
        s_ref[...] = jnp.zeros_like(s_ref)

    @pl.when(dec_start)
    def _():
        for d, h in streams:
            s_ref[d, h] = s0_ref[0, 0, d, h].T

    dirs = ((qf_ref, kf_ref, vf_ref, cf_ref, of_ref), (qb_ref, kb_ref, vb_ref, cb_ref, ob_ref))
    ops = {}
    for d, (q_ref, k_ref, v_ref, c_ref, _) in enumerate(dirs):
        rev = d == 1
        for h in range(hb):
            q = q_ref[:, h * dk:(h + 1) * dk].astype(F32)
            if q_silu:
                q = _silu(q)
            q = q * scale
            k = k_ref[:, h * dk:(h + 1) * dk].astype(F32)
            v = v_ref[:, h * dv:(h + 1) * dv].astype(BF16)
            cb = c_ref[:, h * dk:(h + 1) * dk]
            for c in range(nch):
                rows = slice(c * CHUNK, (c + 1) * CHUNK)
                qc, kc, cbc = q[rows], k[rows], cb[rows]
                end = cbc[0:1, :] if rev else cbc[CHUNK - 1:CHUNK, :]
                ops[d, h, c] = dict(
                    intra=_gla_operands(qc, kc, cbc, rev), v=v[rows],
                    q_state=(qc * jnp.exp(cbc)).astype(BF16),
                    k_state=(kc * jnp.exp(end - cbc)).astype(BF16),
                    s_decay=jnp.exp(end))
    acc = {key: _dot_nt(op["intra"][0], op["intra"][1]) for key, op in ops.items()}
    n_levels = len(next(iter(ops.values()))["intra"][2])
    for lv in range(n_levels):
        mask = _gla_level_mask(CHUNK, SUB >> lv)
        part = {key: _dot_nt(*op["intra"][2][lv]) for key, op in ops.items()}
        acc = {key: acc[key] + jnp.where(mask, part[key], 0.0) for key in ops}
    for d in range(DIAG):
        masks = [_gla_subdiag_mask(CHUNK, d, rev) for rev in (False, True)]
        part = {key: _dot_nt(op["intra"][3][d], op["intra"][4]) for key, op in ops.items()}
        acc = {key: acc[key] + jnp.where(masks[key[0]], part[key], 0.0) for key in ops}
    attn = {key: a.astype(BF16) for key, a in acc.items()}
    o_intra = {key: _dot(attn[key], op["v"]) for key, op in ops.items()}
    s_inc = {key: _dot_tn(op["v"], op["k_state"]) for key, op in ops.items()}

    state = {key: s_ref[key] for key in streams}
    for step in range(nch):
        for (d, h), s in state.items():
            c = step if d == 0 else nch - 1 - step
            dirs[d][4][c * CHUNK:(c + 1) * CHUNK, h * dv:(h + 1) * dv] = (
                o_intra[d, h, c] + _dot_nt(ops[d, h, c]["q_state"], s.astype(BF16))).astype(BF16)
            state[d, h] = ops[d, h, c]["s_decay"] * s + s_inc[d, h, c]
    for key, s in state.items():
        s_ref[key] = s

    @pl.when(is_ctx)
    def _():
        for d, h in streams:
            ns_ref[0, d, h] = s_ref[d, h].T


def _gla_scan(q_src, q_blk, kf_src, kf_blk, kb_src, kb_blk, v_src, v_blk, cb_f, cb_b, s0, j, tok,
              *, heads, dk, dv, q_silu, hb):
    m = q_src.shape[0]
    tt = tok.seq
    assert heads % hb == 0 and q_blk % hb == 0 and kf_blk % hb == 0 and kb_blk % hb == 0 and v_blk % hb == 0
    bwd = lambda n: _bwd_block(n, tok)
    fwd = lambda n: n

    def stream(blk, k_blk):
        return [
            pl.BlockSpec((tt, hb * dk), lambda h, n: (blk(n), q_blk // hb + h)),
            pl.BlockSpec((tt, hb * dk), lambda h, n: (blk(n), k_blk // hb + h)),
            pl.BlockSpec((tt, hb * dv), lambda h, n: (blk(n), v_blk // hb + h)),
            pl.BlockSpec((tt, hb * dk), lambda h, n: (blk(n), h)),
        ]

    n_batch = tok.n_ctx_blocks
    return pl.pallas_call(
        functools.partial(_gla_scan_kernel, tok=tok, q_silu=q_silu, scale=dk ** -0.5, hb=hb, dk=dk, dv=dv),
        grid=(heads // hb, tok.n_blocks),
        in_specs=stream(fwd, kf_blk) + stream(bwd, kb_blk) + [
            pl.BlockSpec((1, 1, 2, hb, dk, dv), lambda h, n: (_dec_batch(n, tok), j, 0, h, 0, 0)),
        ],
        out_specs=[
            pl.BlockSpec((tt, hb * dv), lambda h, n: (n, h)),
            pl.BlockSpec((tt, hb * dv), lambda h, n: (bwd(n), h)),
            pl.BlockSpec((1, 2, hb, dk, dv), lambda h, n: (_state_block(n, tok), 0, h, 0, 0)),
        ],
        out_shape=[
            jax.ShapeDtypeStruct((m, heads * dv), BF16),
            jax.ShapeDtypeStruct((m, heads * dv), BF16),
            jax.ShapeDtypeStruct((n_batch, 2, heads, dk, dv), F32),
        ],
        scratch_shapes=[pltpu.VMEM((2, hb, dv, dk), F32)],
        compiler_params=_params("parallel", "arbitrary"),
        name="gla_scan",
    )(q_src, kf_src, v_src, cb_f, q_src, kb_src, v_src, cb_b, s0)


def kernel(x_prompt, x_sample, c, state_deltanet, state_gla, state_hgrn, c_ctx, ada_w, ada_b, norm1, norm2, final_norm, w_up, w_down, dn_w_in, dn_conv, dn_A_log, dn_dt_bias, dn_norm, dn_w_out, gla_w_in, gla_w_a2, gla_b_a, gla_norm, gla_w_out, hgrn_w_in, hgrn_lb_logits, hgrn_norm, hgrn_w_out):
    batch, seq, d = x_prompt.shape
    dec_batch, dec_seq, _ = x_sample.shape
    depth = ada_w.shape[0]
    tok = Tokens(m_ctx=batch * seq, seq=seq, dec_batch=dec_batch, dec_seq=dec_seq)
    assert seq % CHUNK == 0 and dec_seq % seq == 0 and seq % GRID_W == 0

    dn_hv, dn_dk, dn_dv = state_deltanet.shape[3:]
    dn_vd = dn_hv * dn_dv
    n_conv = dn_conv.shape[2]
    dn_kd = (n_conv - dn_vd) // 2
    dn_hk = dn_kd // dn_dk
    gla_h, gla_dk, gla_dv = state_gla.shape[3:]
    gla_kd, gla_vd = gla_h * gla_dk, gla_h * gla_dv
    hg_h, hg_dk, hg_dv = state_hgrn.shape[3:]

    x = jnp.concatenate([x_prompt.reshape(tok.m_ctx, d), x_sample.reshape(dec_batch * dec_seq, d)], axis=0)
    n_rows = 1 + dec_batch
    pad = (-n_rows) % 8
    cond = jnp.concatenate([c_ctx[None, :], c, jnp.zeros((pad, d), F32)], axis=0)
    mod_all = _mod_table(cond, ada_w, ada_b).reshape(depth, n_rows + pad, 6, d)

    bf = lambda w: w.astype(BF16)
    w_up_b, w_down_b = bf(w_up), bf(w_down)
    dn_w_in_b, dn_w_out_b = bf(dn_w_in), bf(dn_w_out)
    gla_w_in_b, gla_w_out_b = bf(gla_w_in), bf(gla_w_out)
    hgrn_w_in_b, hgrn_w_out_b = bf(hgrn_w_in), bf(hgrn_w_out)

    new_states = ([], [], [])
    dn_states = None
    for i in range(depth):
        kind, j = i % N_MIXERS, i // N_MIXERS
        mod = mod_all[i]
        if kind == 0:
            n_main = n_conv + dn_vd
            proj, raw = _proj(x, norm1[i], mod, dn_w_in_b, j, n_main, tok,
                              gate_w=dn_w_in, gate_layer=j, gate_col0=n_main)
            qkv = _dn_prep(proj, dn_conv[j], n_conv, dn_kd, dn_dk, tok)
            gates, gates_t = _dn_gates(raw, dn_A_log[j], dn_dt_bias[j], tok)
            o_f, o_b, dn_states = _dn_scan(qkv, gates, gates_t, state_deltanet, j, dn_states, tok,
                                           hk_n=dn_hk, hv=dn_hv, dk=dn_dk, dv=dn_dv)
            x = _out_proj(o_f, o_b, proj, n_conv // dn_vd, dn_norm[j], x, mod, dn_w_out_b, j, tok)
        elif kind == 1:
            n_main = 2 * gla_kd + 2 * gla_vd
            w_a1 = gla_w_in[j:j + 1, :, n_main:]
            proj, a1 = _proj(x, norm1[i], mod, gla_w_in_b, j, n_main, tok, gate_w=w_a1)
            cb_f, cb_b = _gla_prep(a1, gla_w_a2[j], gla_b_a[j], tok)
            o_f, o_b, ns = _gla_scan(proj, 0, proj, gla_h, proj, gla_h, proj, (2 * gla_kd) // gla_dv,
                                     cb_f, cb_b, state_gla, j, tok,
                                     heads=gla_h, dk=gla_dk, dv=gla_dv, q_silu=False, hb=min(gla_h, 2))
            x = _out_proj(o_f, o_b, proj, (2 * gla_kd + gla_vd) // gla_vd, gla_norm[j], x, mod,
                          gla_w_out_b, j, tok)
        else:
            proj = _proj(x, norm1[i], mod, hgrn_w_in_b, j, hgrn_w_in.shape[2], tok)
            k_f, k_b, cb_f, cb_b = _hgrn_prep(proj, hgrn_lb_logits, i, d, tok)
            o_f, o_b, ns = _gla_scan(proj, 0, k_f, 0, k_b, 0, proj, d // hg_dv, cb_f, cb_b, state_hgrn, j, tok,
                                     heads=hg_h, dk=hg_dk, dv=hg_dv, q_silu=True, hb=min(hg_h, 8))
            x = _out_proj(o_f, o_b, proj, 2, hgrn_norm[j], x, mod, hgrn_w_out_b, j, tok)
        if kind != 0:
            new_states[kind].append(ns)
        x = _mlp(x, norm2[i], mod, w_up_b, w_down_b, i, final_norm, tok, final=(i == depth - 1))

    y_ctx, y_dec = x
    y_prompt = y_ctx.reshape(batch, seq, d)
    y_sample = y_dec.reshape(dec_batch, dec_seq, d)
    stacked = [s[0][:, None] if len(s) == 1 else jnp.stack(s, axis=1) for s in new_states[1:]]
    return (y_prompt, y_sample, dn_states) + tuple(stacked)
```

```python
import functools
from typing import NamedTuple

import jax
import jax.numpy as jnp
from jax import lax
from jax.experimental import pallas as pl
from jax.experimental.pallas import tpu as pltpu

F32 = jnp.float32
BF16 = jnp.bfloat16

EPS = 1e-6
GRID_W = 64
CONV_K = 5
GLA_RANK = 16
GLA_TAU = 16.0
N_MIXERS = 3
CHUNK = 64
SUB = 16
DIAG = 4
NEG_BIG = -1e30
LOG2_E = 1.4426950408889634
LANES = 128
SUBLANES = 8
PROLOGUE_ROWS = 256
V7X_MXU_COLS = 256
V7X_VMEM_LIMIT_BYTES = 56 * 1024 * 1024
HIGHEST = lax.Precision.HIGHEST


class Tokens(NamedTuple):
    m_ctx: int
    seq: int
    dec_batch: int
    dec_seq: int

    @property
    def m(self):
        return self.m_ctx + self.dec_batch * self.dec_seq

    @property
    def n_ctx_blocks(self):
        return self.m_ctx // self.seq

    @property
    def blocks_per_dec(self):
        return self.dec_seq // self.seq

    @property
    def n_blocks(self):
        return self.m // self.seq


def _tile(n, pref):
    t = min(n, pref)
    while n % t:
        t //= 2
    return t


def _token_tile(tok, pref):
    t = pref
    while tok.m_ctx % t or tok.dec_seq % t:
        t //= 2
    return t


def _params(*sem):
    return pltpu.CompilerParams(dimension_semantics=sem, vmem_limit_bytes=V7X_VMEM_LIMIT_BYTES)


def _mod_row(i, tm, tok):
    start = i * tm
    return jnp.where(start < tok.m_ctx, 0, 1 + (start - tok.m_ctx) // tok.dec_seq)


def _sigmoid(x):
    return 0.5 + 0.5 * jnp.tanh(0.5 * x)


def _silu(x):
    half = 0.5 * x
    return half + half * jnp.tanh(half)


def _softplus(x):
    return jnp.maximum(x, 0.0) + jnp.log(1.0 + jnp.exp(-jnp.abs(x)))


def _norm_mod(x, nw, shift, scale):
    ms = jnp.mean(x * x, axis=-1, keepdims=True)
    return x * lax.rsqrt(ms + EPS) * nw * (1.0 + scale) + shift


def _dot(a, b):
    return jnp.dot(a, b, preferred_element_type=F32)


def _dot_nt(a, b):
    return lax.dot_general(a, b, (((1,), (1,)), ((), ())), preferred_element_type=F32)


def _dot_tn(a, b):
    return lax.dot_general(a, b, (((0,), (0,)), ((), ())), preferred_element_type=F32)


def _split_dot(tri, x):
    hi = x.astype(BF16)
    lo = (x - hi.astype(F32)).astype(BF16)
    return _dot(tri, hi) + _dot(tri, lo)


def _split_matmul(a, b):
    ah = a.astype(BF16)
    al = (a - ah.astype(F32)).astype(BF16)
    bh = b.astype(BF16)
    bl = (b - bh.astype(F32)).astype(BF16)
    return _dot(ah, bh) + _dot(al, bh) + _dot(ah, bl)


def _tri(n, upper):
    r = lax.broadcasted_iota(jnp.int32, (n, n), 0)
    c = lax.broadcasted_iota(jnp.int32, (n, n), 1)
    return jnp.where((r <= c) if upper else (r >= c), 1.0, 0.0).astype(BF16)


def _mod_kernel(c_ref, w_ref, b_ref, o_ref):
    c = c_ref[...]
    s = _silu(c)
    o_ref[0] = jnp.dot(s, w_ref[0], precision=HIGHEST, preferred_element_type=F32) + b_ref[0]


def _mod_table(cond, ada_w, ada_b):
    depth, d, n = ada_w.shape
    rows = cond.shape[0]
    tn = _tile(n, 2048)
    return pl.pallas_call(
        _mod_kernel,
        grid=(depth, n // tn),
        in_specs=[
            pl.BlockSpec((rows, d), lambda l, j: (0, 0)),
            pl.BlockSpec((1, d, tn), lambda l, j: (l, 0, j)),
            pl.BlockSpec((1, 1, tn), lambda l, j: (l, 0, j)),
        ],
        out_specs=pl.BlockSpec((1, rows, tn), lambda l, j: (l, 0, j)),
        out_shape=jax.ShapeDtypeStruct((depth, rows, n), F32),
        compiler_params=_params("parallel", "parallel"),
        name="mod_table",
    )(cond, ada_w, ada_b.reshape(depth, 1, n))


def _proj_kernel(x_ref, nw_ref, mod_ref, w_ref, *rest, gated):
    if gated:
        wg_ref, o_ref, og_ref, h_ref = rest
    else:
        o_ref, h_ref = rest

    @pl.when(pl.program_id(1) == 0)
    def _():
        tm = x_ref.shape[0]
        step = min(tm, PROLOGUE_ROWS)
        for r in range(0, tm, step):
            rows = slice(r, r + step)
            h = _norm_mod(x_ref[rows, :], nw_ref[...], mod_ref[0, 0:1, :], mod_ref[0, 1:2, :])
            h_ref[rows, :] = h.astype(BF16)
            if gated:
                og_ref[rows, :] = _split_matmul(h, wg_ref[...])

    o_ref[...] = _dot(h_ref[...], w_ref[...]).astype(o_ref.dtype)


def _proj(x, nw, mod, w, layer, n, tok, *, gate_w=None, gate_layer=0, gate_col0=0, tm=1024, tn=2048):
    m, d = x.shape
    tm = _token_tile(tok, tm)
    tn = _tile(n, tn)
    gated = gate_w is not None
    in_specs = [
        pl.BlockSpec((tm, d), lambda i, j: (i, 0)),
        pl.BlockSpec((1, d), lambda i, j: (0, 0)),
        pl.BlockSpec((1, 6, d), lambda i, j: (_mod_row(i, tm, tok), 0, 0)),
        pl.BlockSpec((None, d, tn), lambda i, j: (layer, 0, j)),
    ]
    out_specs = [pl.BlockSpec((tm, tn), lambda i, j: (i, j))]
    out_shape = [jax.ShapeDtypeStruct((m, n), BF16)]
    args = [x, nw.reshape(1, d), mod, w]
    if gated:
        ng = gate_w.shape[2] - gate_col0
        assert gate_col0 % ng == 0
        in_specs.append(pl.BlockSpec((None, d, ng), lambda i, j: (gate_layer, 0, gate_col0 // ng)))
        out_specs.append(pl.BlockSpec((tm, ng), lambda i, j: (i, 0)))
        out_shape.append(jax.ShapeDtypeStruct((m, ng), F32))
        args.append(gate_w)
    out = pl.pallas_call(
        functools.partial(_proj_kernel, gated=gated),
        grid=(m // tm, n // tn),
        in_specs=in_specs,
        out_specs=out_specs,
        out_shape=out_shape,
        scratch_shapes=[pltpu.VMEM((tm, d), BF16)],
        compiler_params=_params("parallel", "arbitrary"),
        name="proj_gated" if gated else "proj",
    )(*args)
    return out if gated else out[0]


def _mlp_kernel(x_ref, nw_ref, mod_ref, wu_ref, wd_ref, fw_ref, *rest, n_ctx_tiles):
    out_refs, (h_ref, acc_ref) = rest[:-2], rest[-2:]
    final = n_ctx_tiles is not None
    i = pl.program_id(0)
    f = pl.program_id(1)

    @pl.when(f == 0)
    def _():
        h = _norm_mod(x_ref[...], nw_ref[...], mod_ref[0, 3:4, :], mod_ref[0, 4:5, :])
        h_ref[...] = h.astype(BF16)
        acc_ref[...] = jnp.zeros_like(acc_ref)

    a = jnp.maximum(_dot(h_ref[...], wu_ref[...]), 0.0)
    acc_ref[...] += _dot((a * a).astype(BF16), wd_ref[...])

    last = f == pl.num_programs(1) - 1

    def result():
        y = x_ref[...] + mod_ref[0, 5:6, :] * acc_ref[...]
        if final:
            ms = jnp.mean(y * y, axis=-1, keepdims=True)
            y = y * lax.rsqrt(ms + EPS) * fw_ref[...]
        return y

    if not final:
        @pl.when(last)
        def _():
            out_refs[0][...] = result()
    else:
        @pl.when(jnp.logical_and(last, i < n_ctx_tiles))
        def _():
            out_refs[0][...] = result()

        @pl.when(jnp.logical_and(last, i >= n_ctx_tiles))
        def _():
            out_refs[1][...] = result()


def _mlp(x, nw, mod, w_up, w_down, layer, final_w, tok, *, final, tm=512, tf=1024):
    m, d = x.shape
    ff = w_up.shape[2]
    tm = _token_tile(tok, tm)
    tf = _tile(ff, tf)
    n_ctx_tiles = tok.m_ctx // tm
    if final:
        out_specs = [pl.BlockSpec((tm, d), lambda i, f: (jnp.minimum(i, n_ctx_tiles - 1), 0)),
                     pl.BlockSpec((tm, d), lambda i, f: (jnp.maximum(i - n_ctx_tiles, 0), 0))]
        out_shape = [jax.ShapeDtypeStruct((tok.m_ctx, d), F32), jax.ShapeDtypeStruct((m - tok.m_ctx, d), F32)]
    else:
        out_specs = pl.BlockSpec((tm, d), lambda i, f: (i, 0))
        out_shape = jax.ShapeDtypeStruct((m, d), F32)
    return pl.pallas_call(
        functools.partial(_mlp_kernel, n_ctx_tiles=n_ctx_tiles if final else None),
        grid=(m // tm, ff // tf),
        in_specs=[
            pl.BlockSpec((tm, d), lambda i, f: (i, 0)),
            pl.BlockSpec((1, d), lambda i, f: (0, 0)),
            pl.BlockSpec((1, 6, d), lambda i, f: (_mod_row(i, tm, tok), 0, 0)),
            pl.BlockSpec((None, d, tf), lambda i, f: (layer, 0, f)),
            pl.BlockSpec((None, tf, d), lambda i, f: (layer, f, 0)),
            pl.BlockSpec((1, d), lambda i, f: (0, 0)),
        ],
        out_specs=out_specs,
        out_shape=out_shape,
        scratch_shapes=[pltpu.VMEM((tm, d), BF16), pltpu.VMEM((tm, d), F32)],
        compiler_params=_params("arbitrary" if final else "parallel", "arbitrary"),
        name="mlp_final" if final else "mlp",
    )(x, nw.reshape(1, d), mod, w_up, w_down, final_w.reshape(1, d))


def _out_kernel(of_ref, ob_ref, g_ref, nw_ref, x_ref, mod_ref, w_ref, o_ref, y0_ref, y1_ref, *, n_heads, dh):
    i = pl.program_id(0)
    d = w_ref.shape[1]
    n_chunks = d // V7X_MXU_COLS
    heads_per_chunk = -(-n_heads // n_chunks)

    @pl.when(i == 0)
    def _():
        y1_ref[...] = jnp.zeros_like(y1_ref)

    def body(y_write, y_read):
        ones = jnp.full((dh, LANES), 1.0 / dh, BF16) if dh & (dh - 1) == 0 else None
        for c in range(n_chunks):
            cols = slice(c * V7X_MXU_COLS, (c + 1) * V7X_MXU_COLS)
            o_ref[:, cols] = x_ref[:, cols] + mod_ref[0, 2:3, cols] * _dot(y_read[...], w_ref[:, cols])
            for h in range(c * heads_per_chunk, min((c + 1) * heads_per_chunk, n_heads)):
                sl = slice(h * dh, (h + 1) * dh)
                o = of_ref[:, sl].astype(F32) + ob_ref[:, sl].astype(F32)
                if ones is None:
                    ms = _dot((o * o).astype(BF16), jnp.ones((dh, LANES), BF16)) * (1.0 / dh)
                else:
                    ms = _dot((o * o).astype(BF16), ones)
                inv = lax.rsqrt(ms + EPS)
                y = o * nw_ref[...] * _silu(g_ref[:, sl].astype(F32))
                for k in range(dh // LANES):
                    lo = h * dh + k * LANES
                    y_write[:, lo:lo + LANES] = (y[:, k * LANES:(k + 1) * LANES] * inv).astype(BF16)

    @pl.when(i % 2 == 0)
    def _():
        body(y0_ref, y1_ref)

    @pl.when(i % 2 == 1)
    def _():
        body(y1_ref, y0_ref)


def _out_proj(o_f, o_b, gate_src, gate_blk, norm_w, x, mod, w_out, layer, tok, *, tm=256):
    m, vd = o_f.shape
    d = x.shape[1]
    dh = norm_w.shape[0]
    tm = _token_tile(tok, tm)
    n_tiles = m // tm
    assert d % V7X_MXU_COLS == 0 and dh % LANES == 0
    cur = lambda i: jnp.minimum(i, n_tiles - 1)
    prev = lambda i: jnp.maximum(i - 1, 0)
    return pl.pallas_call(
        functools.partial(_out_kernel, n_heads=vd // dh, dh=dh),
        grid=(n_tiles + 1,),
        in_specs=[
            pl.BlockSpec((tm, vd), lambda i: (cur(i), 0)),
            pl.BlockSpec((tm, vd), lambda i: (cur(i), 0)),
            pl.BlockSpec((tm, vd), lambda i: (cur(i), gate_blk)),
            pl.BlockSpec((1, dh), lambda i: (0, 0)),
            pl.BlockSpec((tm, d), lambda i: (prev(i), 0)),
            pl.BlockSpec((1, 6, d), lambda i: (_mod_row(prev(i), tm, tok), 0, 0)),
            pl.BlockSpec((None, vd, d), lambda i: (layer, 0, 0), pipeline_mode=pl.Buffered(1)),
        ],
        out_specs=pl.BlockSpec((tm, d), lambda i: (prev(i), 0)),
        out_shape=jax.ShapeDtypeStruct((m, d), F32),
        scratch_shapes=[pltpu.VMEM((tm, vd), BF16), pltpu.VMEM((tm, vd), BF16)],
        compiler_params=_params("arbitrary"),
        name="out_proj",
    )(o_f, o_b, gate_src, norm_w.reshape(1, dh), x, mod, w_out)


def _bwd_block(n, tok):
    nc, per = tok.n_ctx_blocks, tok.blocks_per_dec
    r = jnp.maximum(n - nc, 0)
    return jnp.where(n < nc, n, nc + (r // per) * per + (per - 1 - r % per))


def _dec_batch(n, tok):
    return jnp.maximum(n - tok.n_ctx_blocks, 0) // tok.blocks_per_dec


def _state_block(n, tok):
    return jnp.minimum(n, tok.n_ctx_blocks - 1)


def _seq_flags(n, tok):
    nc, per = tok.n_ctx_blocks, tok.blocks_per_dec
    is_ctx = n < nc
    r = jnp.maximum(n - nc, 0) % per
    return is_ctx, jnp.logical_and(jnp.logical_not(is_ctx), r == 0)


def _dn_prep_kernel(x_ref, shift_ref, w_ref, o_ref, *, n_q_blocks, dk):
    j = pl.program_id(1)
    x = x_ref[...]
    tc = x.shape[1]
    acc = None
    for s in range(CONV_K):
        xs = x.astype(F32) if s == CONV_K // 2 else _dot(shift_ref[0, s], x)
        term = xs * w_ref[s:s + 1, :]
        acc = term if acc is None else acc + term
    y = _silu(acc)

    def normed(scale):
        ones = jnp.ones((dk, dk), BF16)
        for g in range(tc // dk):
            blk = y[:, g * dk:(g + 1) * dk]
            ss = _dot((blk * blk).astype(BF16), ones)
            o_ref[:, g * dk:(g + 1) * dk] = (blk * (lax.rsqrt(ss + EPS) * scale)).astype(o_ref.dtype)

    @pl.when(j < n_q_blocks)
    def _():
        normed(dk ** -0.5)

    @pl.when(jnp.logical_and(j >= n_q_blocks, j < 2 * n_q_blocks))
    def _():
        normed(1.0)

    @pl.when(j >= 2 * n_q_blocks)
    def _():
        o_ref[...] = y.astype(o_ref.dtype)


def _conv_shift_matrices(tt):
    t = jnp.arange(tt)[:, None]
    u = jnp.arange(tt)[None, :]
    mats = []
    for seg in (tt, GRID_W):
        same = (t // seg) == (u // seg)
        mats.append(jnp.stack([jnp.logical_and(u == t + s - CONV_K // 2, same) for s in range(CONV_K)]))
    return jnp.stack(mats).astype(BF16)


def _dn_prep(proj, conv_w, n_conv, kd, dk, tok):
    m = proj.shape[0]
    tt = tok.seq
    tc = _tile(kd, 2048)
    n_ctx = tok.n_ctx_blocks
    return pl.pallas_call(
        functools.partial(_dn_prep_kernel, n_q_blocks=kd // tc, dk=dk),
        grid=(m // tt, n_conv // tc),
        in_specs=[
            pl.BlockSpec((tt, tc), lambda i, j: (i, j)),
            pl.BlockSpec((1, CONV_K, tt, tt), lambda i, j: (jnp.where(i < n_ctx, 0, 1), 0, 0, 0)),
            pl.BlockSpec((CONV_K, tc), lambda i, j: (0, j)),
        ],
        out_specs=pl.BlockSpec((tt, tc), lambda i, j: (i, j)),
        out_shape=jax.ShapeDtypeStruct((m, n_conv), BF16),
        compiler_params=_params("parallel", "parallel"),
        name="dn_prep",
    )(proj, _conv_shift_matrices(tt), conv_w)


def _dn_gate_kernel(x_ref, a_ref, bias_ref, o_ref, gt_ref, *, hv, tt):
    x = x_ref[...]
    lanes = x.shape[1]
    lane = lax.broadcasted_iota(jnp.int32, (1, lanes), 1)
    is_fwd = lane < 3 * hv
    g = -a_ref[...] * _softplus(x + bias_ref[...])
    beta = pltpu.roll(_sigmoid(x), 2 * hv, 1)
    lower = _tri(CHUNK, upper=False)
    upper = _tri(CHUNK, upper=True)
    for c in range(tt // CHUNK):
        rows = slice(c * CHUNK, (c + 1) * CHUNK)
        gch = g[rows]
        gc = jnp.where(is_fwd, _split_dot(lower, gch), _split_dot(upper, gch))
        g_end = jnp.where(is_fwd, gc[CHUNK - 1:CHUNK, :], gc[0:1, :])
        eg = jnp.exp(gc)
        o_ref[0, rows, :] = beta[rows]
        o_ref[1, rows, :] = gc
        o_ref[2, rows, :] = eg
        o_ref[3, rows, :] = jnp.exp(g_end - gc)
        o_ref[4, rows, :] = beta[rows] * eg
    gt_ref[...] = o_ref[1].T


def _dn_gates(raw, a_log, dt_bias, tok):
    m, lanes = raw.shape
    hv = lanes // 4
    tt = tok.seq
    zeros = jnp.zeros((2 * hv,), F32)
    a_full = jnp.concatenate([zeros, jnp.exp(a_log).reshape(-1)]).reshape(1, lanes)
    bias_full = jnp.concatenate([zeros, dt_bias.reshape(-1)]).reshape(1, lanes)
    return pl.pallas_call(
        functools.partial(_dn_gate_kernel, hv=hv, tt=tt),
        grid=(m // tt,),
        in_specs=[
            pl.BlockSpec((tt, lanes), lambda i: (i, 0)),
            pl.BlockSpec((1, lanes), lambda i: (0, 0)),
            pl.BlockSpec((1, lanes), lambda i: (0, 0)),
        ],
        out_specs=[
            pl.BlockSpec((5, tt, lanes), lambda i: (0, i, 0)),
            pl.BlockSpec((lanes, tt), lambda i: (0, i)),
        ],
        out_shape=[jax.ShapeDtypeStruct((5, m, lanes), F32), jax.ShapeDtypeStruct((lanes, m), F32)],
        compiler_params=_params("parallel"),
        name="dn_gates",
    )(raw, a_full, bias_full)


def _dn_scan_kernel(qf_ref, kf_ref, vf_ref, gf_ref, gtf_ref, qb_ref, kb_ref, vb_ref, gb_ref, gtb_ref, s0_ref,
                    of_ref, ob_ref, ns_ref, s_ref, u0_ref, wq_ref, attn_ref, kd_ref, ge_ref, *, tok, hv, dv, hkb):
    hk0 = pl.program_id(0) * hkb
    n = pl.program_id(1)
    tt = tok.seq
    nch = tt // CHUNK
    dk = qf_ref.shape[1] // hkb
    assert dv == 2 * CHUNK and CHUNK == 4 * SUB
    has_prev = n > 0
    is_ctx, dec_start = _seq_flags(jnp.maximum(n - 1, 0), tok)
    wslot = n % 2
    rslot = 1 - wslot
    scratch = (u0_ref, wq_ref, attn_ref, kd_ref, ge_ref)

    @pl.when(n == 0)
    def _():
        s_ref[...] = jnp.zeros_like(s_ref)
        for ref in scratch:
            ref[1] = jnp.zeros(ref.shape[1:], ref.dtype)

    @pl.when(jnp.logical_and(has_prev, is_ctx))
    def _():
        s_ref[...] = jnp.zeros_like(s_ref)

    @pl.when(jnp.logical_and(has_prev, dec_start))
    def _():
        s_ref[...] = s0_ref[0, 0]

    bf = lambda x: x.astype(BF16)
    lanes = gf_ref.shape[2]
    lane_iota = lax.broadcasted_iota(jnp.int32, (1, lanes), 1)
    ri = lax.broadcasted_iota(jnp.int32, (CHUNK, 2 * CHUNK), 0)
    li = lax.broadcasted_iota(jnp.int32, (CHUNK, 2 * CHUNK), 1)
    left = li < CHUNK
    ci = li % CHUNK
    same = [(ri // b) == (ci // b) for b in (SUB, 2 * SUB)]
    zero_b = jnp.zeros((CHUNK, 2 * CHUNK), BF16)

    def block_diag(xb):
        return jnp.concatenate([jnp.where(left, xb, zero_b), jnp.where(left, zero_b, xb)], axis=0)

    def pair_dot(a, b):
        return _dot(bf(a), block_diag(bf(b)))

    dirs = ((qf_ref, kf_ref, vf_ref, gf_ref, gtf_ref, of_ref), (qb_ref, kb_ref, vb_ref, gb_ref, gtb_ref, ob_ref))

    def prepare():
        ch = {}
        for d, kh in [(d, kh) for d in range(2) for kh in range(hkb)]:
            q_ref, k_ref, v_ref, g_ref, gt_ref, _ = dirs[d]
            incl = (ri >= ci) if d == 0 else (ri <= ci)
            strict = (ri > ci) if d == 0 else (ri < ci)
            end_row = CHUNK - 1 if d == 0 else 0
            q = q_ref[:, kh * dk:(kh + 1) * dk]
            k = k_ref[:, kh * dk:(kh + 1) * dk]
            kf32 = k.astype(F32)
            qf32 = q.astype(F32)
            cols, g_rows, vs = [], [], []
            for e in range(2):
                a_lane = 2 * hv + d * hv + 2 * (hk0 + kh) + e
                sel = lane_iota == a_lane
                cols.append([jnp.sum(jnp.where(sel, g_ref[i], 0.0), axis=1, keepdims=True) for i in range(5)])
                g_rows.append(gt_ref[pl.ds(a_lane, 1), :])
                vs.append(v_ref[:, (2 * kh + e) * dv:(2 * kh + e + 1) * dv].astype(F32))
            for c in range(nch):
                rows = slice(c * CHUNK, (c + 1) * CHUNK)
                pair = lambda i: jnp.where(left, cols[0][i][rows], cols[1][i][rows])
                k2 = jnp.concatenate([k[rows], k[rows]], axis=0)
                kk = _dot_nt(k[rows], k2)
                qk = _dot_nt(q[rows], k2)
                g_row = jnp.concatenate([g_rows[0][:, rows], g_rows[1][:, rows]], axis=1)
                diff = pair(1) - g_row
                decay = jnp.where(incl, jnp.exp(jnp.where(incl, diff, 0.0)), 0.0)
                p = jnp.where(strict, -(pair(0) * kk * decay), 0.0)
                attn_ref[wslot, d, kh, c] = bf(qk * decay)
                for e in range(2):
                    beta_c, _, eg_c, ekd_c, beg_c = cols[e]
                    qg = qf32[rows] * eg_c[rows]
                    kd_ref[wslot, d, 2 * kh + e, c] = bf(kf32[rows] * ekd_c[rows])
                    wq_ref[wslot, d, 2 * kh + e, c, CHUNK:, :] = bf(qg)
                    g_end = eg_c[c * CHUNK + end_row:c * CHUNK + end_row + 1, :]
                    ge_ref[wslot, d, 2 * kh + e, c] = jnp.broadcast_to(g_end, ge_ref.shape[4:])
                ch[kh, d, c] = dict(
                    pd=jnp.where(same[0], p, 0.0),
                    p32=jnp.where(jnp.logical_and(same[1], jnp.logical_not(same[0])), p, 0.0),
                    p64=jnp.where(same[1], 0.0, p),
                    r=[jnp.concatenate([cols[e][0][rows] * vs[e][rows], cols[e][4][rows] * kf32[rows]], axis=1)
                       for e in range(2)])
            yield
        for c in ch.values():
            c["q"] = pair_dot(c["pd"], c["pd"])
            c["xm"] = c["pd"]
        yield
        for it in range(2):
            for c in ch.values():
                q, xm = c["q"], c["xm"]
                both = _dot(bf(q), jnp.concatenate([block_diag(bf(xm)), block_diag(bf(q))], axis=1))
                c["xm"] = xm + q + both[:, :2 * CHUNK]
                c["q"] = both[:, 2 * CHUNK:]
            yield
        for c in ch.values():
            q, xm = c.pop("q"), c["xm"]
            c["xm"] = xm + q + pair_dot(q, xm)
        yield
        for level in ("p32", "p64"):
            for c in ch.values():
                c["y"] = c[level] + pair_dot(c[level], c["xm"])
            yield
            for c in ch.values():
                xm, y = c["xm"], c.pop("y")
                c["xm"] = xm + y + pair_dot(xm, y)
            yield
        zero_r = jnp.zeros((CHUNK, 2 * dv), BF16)
        for (kh, d, cidx), c in ch.items():
            r0, r1 = c["r"]
            rhs = jnp.concatenate([jnp.concatenate([bf(r0), zero_r], axis=1),
                                   jnp.concatenate([zero_r, bf(r1)], axis=1)], axis=0)
            big = _dot(bf(c["xm"]), rhs)
            sol = [r0 + big[:, :2 * dv], r1 + big[:, 2 * dv:]]
            u0_ref[wslot, d, kh, cidx] = jnp.concatenate([sol[0][:, :dv], sol[1][:, :dv]], axis=1)
            for e in range(2):
                wq_ref[wslot, d, 2 * kh + e, cidx, :CHUNK, :] = bf(sol[e][:, dv:])
        yield

    def recur():
        state = {(d, e): s_ref[d, e] for d in range(2) for e in range(2 * hkb)}
        zero_u = jnp.zeros((CHUNK, dv), BF16)
        for step in range(nch):
            chunk = lambda d: step if d == 0 else nch - 1 - step
            ws = {(d, e): _dot(wq_ref[rslot, d, e, chunk(d)], bf(s)) for (d, e), s in state.items()}
            yield
            for d, kh in [(d, kh) for d in range(2) for kh in range(hkb)]:
                c = chunk(d)
                u0 = u0_ref[rslot, d, kh, c]
                ev = [2 * kh, 2 * kh + 1]
                ub = [bf(u0[:, e * dv:(e + 1) * dv] - ws[d, ev[e]][:CHUNK]) for e in range(2)]
                u2 = jnp.concatenate([jnp.concatenate([ub[0], zero_u], axis=1),
                                      jnp.concatenate([zero_u, ub[1]], axis=1)], axis=0)
                o = (_dot(attn_ref[rslot, d, kh, c], u2)
                     + jnp.concatenate([ws[d, ev[0]][CHUNK:], ws[d, ev[1]][CHUNK:]], axis=1))
                dirs[d][5][c * CHUNK:(c + 1) * CHUNK, 2 * kh * dv:2 * (kh + 1) * dv] = bf(o)
                for e in range(2):
                    state[d, ev[e]] = (ge_ref[rslot, d, ev[e], c][0:1, :] * state[d, ev[e]]
                                       + _dot_tn(kd_ref[rslot, d, ev[e], c], ub[e]))
            yield
        for (d, e), s in state.items():
            s_ref[d, e] = s
        yield

    halves = [recur(), prepare()]
    while halves:
        for g in list(halves):
            if next(g, "done") == "done":
                halves.remove(g)

    @pl.when(jnp.logical_and(has_prev, is_ctx))
    def _():
        ns_ref[0, 0] = s_ref[...]


def _dn_scan_kernel_aliased(*refs, **kw):
    n_in = 11
    return _dn_scan_kernel(*refs[:n_in], *refs[n_in + 1:], **kw)


def _dn_scan(qkv, gates, gates_t, s0, j, new_states, tok, *, hk_n, hv, dk, dv):
    m = qkv.shape[0]
    tt = tok.seq
    rep = hv // hk_n
    assert rep == 2
    lanes = gates.shape[2]
    kd = hk_n * dk
    nb = tok.n_blocks
    nch = tt // CHUNK
    fwd = lambda n: jnp.minimum(n, nb - 1)
    bwd = lambda n: _bwd_block(jnp.minimum(n, nb - 1), tok)
    prev = lambda n: jnp.maximum(n - 1, 0)

    hkb = max(b for b in (4, 2, 1) if hk_n % b == 0)
    hvb = hkb * rep

    def stream(blk):
        return [
            pl.BlockSpec((tt, hkb * dk), lambda h, n: (blk(n), h)),
            pl.BlockSpec((tt, hkb * dk), lambda h, n: (blk(n), hk_n // hkb + h)),
            pl.BlockSpec((tt, hvb * dv), lambda h, n: (blk(n), (2 * kd) // (hvb * dv) + h)),
            pl.BlockSpec((5, tt, lanes), lambda h, n: (0, blk(n), 0)),
            pl.BlockSpec((lanes, tt), lambda h, n: (0, blk(n))),
        ]

    n_batch = tok.n_ctx_blocks
    n_layers = s0.shape[1]
    aliased = new_states is not None
    extra_in = [pl.BlockSpec(memory_space=pl.ANY)] if aliased else []
    extra_args = (new_states,) if aliased else ()
    return pl.pallas_call(
        functools.partial(_dn_scan_kernel_aliased if aliased else _dn_scan_kernel, tok=tok, hv=hv, dv=dv, hkb=hkb),
        grid=(hk_n // hkb, nb + 1),
        in_specs=stream(fwd) + stream(bwd) + [
            pl.BlockSpec((1, 1, 2, hvb, dk, dv), lambda h, n: (_dec_batch(prev(n), tok), j, 0, h, 0, 0)),
        ] + extra_in,
        out_specs=[
            pl.BlockSpec((tt, hvb * dv), lambda h, n: (prev(n), h)),
            pl.BlockSpec((tt, hvb * dv), lambda h, n: (_bwd_block(prev(n), tok), h)),
            pl.BlockSpec((1, 1, 2, hvb, dk, dv), lambda h, n: (_state_block(prev(n), tok), j, 0, h, 0, 0)),
        ],
        out_shape=[
            jax.ShapeDtypeStruct((m, hv * dv), BF16),
            jax.ShapeDtypeStruct((m, hv * dv), BF16),
            jax.ShapeDtypeStruct((n_batch, n_layers, 2, hv, dk, dv), F32),
        ],
        input_output_aliases={11: 2} if aliased else {},
        scratch_shapes=[
            pltpu.VMEM((2, hvb, dk, dv), F32),
            pltpu.VMEM((2, 2, hkb, nch, CHUNK, rep * dv), F32),
            pltpu.VMEM((2, 2, hvb, nch, 2 * CHUNK, dk), BF16),
            pltpu.VMEM((2, 2, hkb, nch, CHUNK, rep * CHUNK), BF16),
            pltpu.VMEM((2, 2, hvb, nch, CHUNK, dk), BF16),
            pltpu.VMEM((2, 2, hvb, nch, SUBLANES, dv), F32),
        ],
        compiler_params=_params("parallel", "arbitrary"),
        name="dn_scan",
    )(qkv, qkv, qkv, gates, gates_t, qkv, qkv, qkv, gates, gates_t, s0, *extra_args)


def _chunk_cums(la_f, la_b, cbf_ref, cbb_ref, tt):
    lower = _tri(CHUNK, upper=False)
    upper = _tri(CHUNK, upper=True)
    for c in range(tt // CHUNK):
        rows = slice(c * CHUNK, (c + 1) * CHUNK)
        cbf_ref[rows, :] = _split_dot(lower, la_f[rows])
        cbb_ref[rows, :] = _split_dot(upper, la_b[rows])


def _log_sigmoid(x):
    return jnp.minimum(x, 0.0) - jnp.log(1.0 + jnp.exp(-jnp.abs(x)))


def _gla_prep_kernel(a1_ref, w2_ref, b_ref, cbf_ref, cbb_ref, *, tt):
    a1 = a1_ref[...]
    la = [_log_sigmoid(_dot(a1, w2_ref[d]) + b_ref[d]) * (1.0 / GLA_TAU) for d in range(2)]
    _chunk_cums(la[0], la[1], cbf_ref, cbb_ref, tt)


def _gla_prep(a1, w_a2, b_a, tok):
    m, r2 = a1.shape
    kd = w_a2.shape[2]
    tt = tok.seq
    tc = _tile(kd, 1024)
    zeros = jnp.zeros((GLA_RANK, kd), F32)
    w2 = jnp.stack([jnp.concatenate([w_a2[0], zeros]), jnp.concatenate([zeros, w_a2[1]])])
    out = jax.ShapeDtypeStruct((m, kd), F32)
    return pl.pallas_call(
        functools.partial(_gla_prep_kernel, tt=tt),
        grid=(m // tt, kd // tc),
        in_specs=[
            pl.BlockSpec((tt, r2), lambda i, j: (i, 0)),
            pl.BlockSpec((2, r2, tc), lambda i, j: (0, 0, j)),
            pl.BlockSpec((2, 1, tc), lambda i, j: (0, 0, j)),
        ],
        out_specs=[pl.BlockSpec((tt, tc), lambda i, j: (i, j))] * 2,
        out_shape=[out, out],
        compiler_params=_params("parallel", "parallel"),
        name="gla_prep",
    )(a1, w2, b_a.reshape(2, 1, kd))


def _hgrn_prep_kernel(ff_ref, fb_ref, lb_ref, kf_ref, kb_ref, cbf_ref, cbb_ref, *, layer, tt):
    logits = lb_ref[...]
    ex = jnp.exp(logits - jnp.max(logits, axis=0, keepdims=True))
    probs = ex / jnp.sum(ex, axis=0, keepdims=True)
    lb = jnp.sum(probs[1:layer + 1], axis=0, keepdims=True)
    la = []
    for f_ref, k_ref in ((ff_ref, kf_ref), (fb_ref, kb_ref)):
        fl = f_ref[...].astype(F32)
        sg = 1.0 / (1.0 + jnp.exp(-fl))
        s = lb + (1.0 - lb) * sg
        la.append(jnp.where(s > 0.0, jnp.log(jnp.where(s > 0.0, s, 1.0)), fl))
        k_ref[...] = (1.0 - lb) * (1.0 - sg)
    _chunk_cums(la[0], la[1], cbf_ref, cbb_ref, tt)


def _hgrn_prep(proj, lb_logits, layer, d, tok):
    m = proj.shape[0]
    depth = lb_logits.shape[0]
    tt = tok.seq
    tc = _tile(d, 2048)
    nb = d // tc
    out = jax.ShapeDtypeStruct((m, d), F32)
    return pl.pallas_call(
        functools.partial(_hgrn_prep_kernel, layer=layer, tt=tt),
        grid=(m // tt, nb),
        in_specs=[
            pl.BlockSpec((tt, tc), lambda i, j: (i, 3 * nb + j)),
            pl.BlockSpec((tt, tc), lambda i, j: (i, 4 * nb + j)),
            pl.BlockSpec((depth, tc), lambda i, j: (0, j)),
        ],
        out_specs=[pl.BlockSpec((tt, tc), lambda i, j: (i, j))] * 4,
        out_shape=[out] * 4,
        compiler_params=_params("parallel", "parallel"),
        name="hgrn_prep",
    )(proj, proj, lb_logits)


def _gla_offdiag_pairs(c):
    pairs = []
    size = c
    while size > SUB:
        half = size // 2
        for start in range(0, c, size):
            pairs.append(((start + half, start + size), (start, start + half)))
        size = half
    return pairs


def _gla_operands(q, k, cb, rev):
    c, dk = q.shape
    zero_rows = lambda n: jnp.zeros((n, dk), BF16)

    def padded(x, rows):
        parts = ([zero_rows(rows[0])] if rows[0] else []) + [x] + ([zero_rows(c - rows[1])] if rows[1] < c else [])
        return jnp.concatenate(parts, axis=0) if len(parts) > 1 else x

    q_parts, k_parts = [], []
    for qr, kr in _gla_offdiag_pairs(c):
        if rev:
            qr, kr = kr, qr
        ref_row = qr[0] - 1 if not rev else qr[1]
        ref = cb[ref_row:ref_row + 1, :]
        qs, ks = slice(*qr), slice(*kr)
        q_parts.append(padded((q[qs] * jnp.exp2(cb[qs] - ref)).astype(BF16), qr))
        k_parts.append(padded((k[ks] * jnp.exp2(ref - cb[ks])).astype(BF16), kr))
    row = lax.broadcasted_iota(jnp.int32, (c, 1), 0)
    levels = []
    half = SUB // 2
    while half >= DIAG:
        size = 2 * half
        refs = []
        for start in range(0, c, size):
            r = start + half if rev else start + half - 1
            refs.append(jnp.broadcast_to(cb[r:r + 1, :], (size, dk)))
        ref = jnp.concatenate(refs, axis=0)
        later = (row % size) >= half
        is_q = jnp.logical_not(later) if rev else later
        qt = (q * jnp.exp2(jnp.where(is_q, cb - ref, NEG_BIG))).astype(BF16)
        kt = (k * jnp.exp2(jnp.where(is_q, NEG_BIG, ref - cb))).astype(BF16)
        levels.append((qt, kt))
        half //= 2
    diag = [q.astype(BF16)]
    for d in range(1, DIAG):
        shift = c - d if rev else d
        valid = (row % DIAG) < DIAG - d if rev else (row % DIAG) >= d
        diag.append((q * jnp.exp2(jnp.where(valid, cb - pltpu.roll(cb, shift, 0), NEG_BIG))).astype(BF16))
    return jnp.concatenate(q_parts, axis=1), jnp.concatenate(k_parts, axis=1), levels, diag, k.astype(BF16)


def _gla_level_mask(c, size):
    ri = lax.broadcasted_iota(jnp.int32, (c, c), 0)
    ci = lax.broadcasted_iota(jnp.int32, (c, c), 1)
    return (ri // size) == (ci // size)


def _gla_subdiag_mask(c, d, rev):
    ri = lax.broadcasted_iota(jnp.int32, (c, c), 0)
    ci = lax.broadcasted_iota(jnp.int32, (c, c), 1)
    on = (ci - ri == d) if rev else (ri - ci == d)
    return jnp.logical_and(on, (ri // DIAG) == (ci // DIAG))


def _gla_scan_kernel(qf_ref, kf_ref, vf_ref, cf_ref, qb_ref, kb_ref, vb_ref, cb_ref, s0_ref,
                     of_ref, ob_ref, ns_ref, s_ref, *, tok, q_silu, scale, hb, dk, dv):
    n = pl.program_id(1)
    tt = tok.seq
    nch = tt // CHUNK
    is_ctx, dec_start = _seq_flags(n, tok)
    streams = [(d, h) for d in range(2) for h in range(hb)]

    @pl.when(is_ctx)
    def _():
        s_ref[...] = jnp.zeros_like(s_ref)

    @pl.when(dec_start)
    def _():
        for d, h in streams:
            s_ref[d, h] = s0_ref[0, 0, d, h].T

    dirs = ((qf_ref, kf_ref, vf_ref, cf_ref, of_ref), (qb_ref, kb_ref, vb_ref, cb_ref, ob_ref))
    ops = {}
    for d, (q_ref, k_ref, v_ref, c_ref, _) in enumerate(dirs):
        rev = d == 1
        for h in range(hb):
            q = q_ref[:, h * dk:(h + 1) * dk].astype(F32)
            if q_silu:
                q = _silu(q)
            q = q * scale
            k = k_ref[:, h * dk:(h + 1) * dk].astype(F32)
            v = v_ref[:, h * dv:(h + 1) * dv].astype(BF16)
            cb = c_ref[:, h * dk:(h + 1) * dk] * LOG2_E
            for c in range(nch):
                rows = slice(c * CHUNK, (c + 1) * CHUNK)
                qc, kc, cbc = q[rows], k[rows], cb[rows]
                end = cbc[0:1, :] if rev else cbc[CHUNK - 1:CHUNK, :]
                ops[d, h, c] = dict(
                    intra=_gla_operands(qc, kc, cbc, rev), v=v[rows],
                    q_state=(qc * jnp.exp2(cbc)).astype(BF16),
                    k_state=(kc * jnp.exp2(end - cbc)).astype(BF16),
                    s_decay=jnp.exp2(end))
    acc = {key: _dot_nt(op["intra"][0], op["intra"][1]) for key, op in ops.items()}
    n_levels = len(next(iter(ops.values()))["intra"][2])
    for lv in range(n_levels):
        mask = _gla_level_mask(CHUNK, SUB >> lv)
        part = {key: _dot_nt(*op["intra"][2][lv]) for key, op in ops.items()}
        acc = {key: acc[key] + jnp.where(mask, part[key], 0.0) for key in ops}
    for d in range(DIAG):
        masks = [_gla_subdiag_mask(CHUNK, d, rev) for rev in (False, True)]
        part = {key: _dot_nt(op["intra"][3][d], op["intra"][4]) for key, op in ops.items()}
        acc = {key: acc[key] + jnp.where(masks[key[0]], part[key], 0.0) for key in ops}
    attn = {key: a.astype(BF16) for key, a in acc.items()}
    o_intra = {key: _dot(attn[key], op["v"]) for key, op in ops.items()}
    s_inc = {key: _dot_tn(op["v"], op["k_state"]) for key, op in ops.items()}

    state = {key: s_ref[key] for key in streams}
    for step in range(nch):
        for (d, h), s in state.items():
            c = step if d == 0 else nch - 1 - step
            dirs[d][4][c * CHUNK:(c + 1) * CHUNK, h * dv:(h + 1) * dv] = (
                o_intra[d, h, c] + _dot_nt(ops[d, h, c]["q_state"], s.astype(BF16))).astype(BF16)
            state[d, h] = ops[d, h, c]["s_decay"] * s + s_inc[d, h, c]
    for key, s in state.items():
        s_ref[key] = s

    @pl.when(is_ctx)
    def _():
        for d, h in streams:
            ns_ref[0, d, h] = s_ref[d, h].T


def _gla_scan(q_src, q_blk, kf_src, kf_blk, kb_src, kb_blk, v_src, v_blk, cb_f, cb_b, s0, j, tok,
              *, heads, dk, dv, q_silu, hb):
    m = q_src.shape[0]
    tt = tok.seq
    assert heads % hb == 0 and q_blk % hb == 0 and kf_blk % hb == 0 and kb_blk % hb == 0 and v_blk % hb == 0
    bwd = lambda n: _bwd_block(n, tok)
    fwd = lambda n: n

    def stream(blk, k_blk):
        return [
            pl.BlockSpec((tt, hb * dk), lambda h, n: (blk(n), q_blk // hb + h)),
            pl.BlockSpec((tt, hb * dk), lambda h, n: (blk(n), k_blk // hb + h)),
            pl.BlockSpec((tt, hb * dv), lambda h, n: (blk(n), v_blk // hb + h)),
            pl.BlockSpec((tt, hb * dk), lambda h, n: (blk(n), h)),
        ]

    n_batch = tok.n_ctx_blocks
    return pl.pallas_call(
        functools.partial(_gla_scan_kernel, tok=tok, q_silu=q_silu, scale=dk ** -0.5, hb=hb, dk=dk, dv=dv),
        grid=(heads // hb, tok.n_blocks),
        in_specs=stream(fwd, kf_blk) + stream(bwd, kb_blk) + [
            pl.BlockSpec((1, 1, 2, hb, dk, dv), lambda h, n: (_dec_batch(n, tok), j, 0, h, 0, 0)),
        ],
        out_specs=[
            pl.BlockSpec((tt, hb * dv), lambda h, n: (n, h)),
            pl.BlockSpec((tt, hb * dv), lambda h, n: (bwd(n), h)),
            pl.BlockSpec((1, 2, hb, dk, dv), lambda h, n: (_state_block(n, tok), 0, h, 0, 0)),
        ],
        out_shape=[
            jax.ShapeDtypeStruct((m, heads * dv), BF16),
            jax.ShapeDtypeStruct((m, heads * dv), BF16),
            jax.ShapeDtypeStruct((n_batch, 2, heads, dk, dv), F32),
        ],
        scratch_shapes=[pltpu.VMEM((2, hb, dv, dk), F32)],
        compiler_params=_params("parallel", "arbitrary"),
        name="gla_scan",
    )(q_src, kf_src, v_src, cb_f, q_src, kb_src, v_src, cb_b, s0)


def kernel(x_prompt, x_sample, c, state_deltanet, state_gla, state_hgrn, c_ctx, ada_w, ada_b, norm1, norm2, final_norm, w_up, w_down, dn_w_in, dn_conv, dn_A_log, dn_dt_bias, dn_norm, dn_w_out, gla_w_in, gla_w_a2, gla_b_a, gla_norm, gla_w_out, hgrn_w_in, hgrn_lb_logits, hgrn_norm, hgrn_w_out):
    batch, seq, d = x_prompt.shape
    dec_batch, dec_seq, _ = x_sample.shape
    depth = ada_w.shape[0]
    tok = Tokens(m_ctx=batch * seq, seq=seq, dec_batch=dec_batch, dec_seq=dec_seq)
    assert seq % CHUNK == 0 and dec_seq % seq == 0 and seq % GRID_W == 0

    dn_hv, dn_dk, dn_dv = state_deltanet.shape[3:]
    dn_vd = dn_hv * dn_dv
    n_conv = dn_conv.shape[2]
    dn_kd = (n_conv - dn_vd) // 2
    dn_hk = dn_kd // dn_dk
    gla_h, gla_dk, gla_dv = state_gla.shape[3:]
    gla_kd, gla_vd = gla_h * gla_dk, gla_h * gla_dv
    hg_h, hg_dk, hg_dv = state_hgrn.shape[3:]

    x = jnp.concatenate([x_prompt.reshape(tok.m_ctx, d), x_sample.reshape(dec_batch * dec_seq, d)], axis=0)
    n_rows = 1 + dec_batch
    pad = (-n_rows) % 8
    cond = jnp.concatenate([c_ctx[None, :], c, jnp.zeros((pad, d), F32)], axis=0)
    mod_all = _mod_table(cond, ada_w, ada_b).reshape(depth, n_rows + pad, 6, d)

    bf = lambda w: w.astype(BF16)
    w_up_b, w_down_b = bf(w_up), bf(w_down)
    dn_w_in_b, dn_w_out_b = bf(dn_w_in), bf(dn_w_out)
    gla_w_in_b, gla_w_out_b = bf(gla_w_in), bf(gla_w_out)
    hgrn_w_in_b, hgrn_w_out_b = bf(hgrn_w_in), bf(hgrn_w_out)

    new_states = ([], [], [])
    dn_states = None
    for i in range(depth):
        kind, j = i % N_MIXERS, i // N_MIXERS
        mod = mod_all[i]
        if kind == 0:
            n_main = n_conv + dn_vd
            proj, raw = _proj(x, norm1[i], mod, dn_w_in_b, j, n_main, tok,
                              gate_w=dn_w_in, gate_layer=j, gate_col0=n_main)
            qkv = _dn_prep(proj, dn_conv[j], n_conv, dn_kd, dn_dk, tok)
            gates, gates_t = _dn_gates(raw, dn_A_log[j], dn_dt_bias[j], tok)
            o_f, o_b, dn_states = _dn_scan(qkv, gates, gates_t, state_deltanet, j, dn_states, tok,
                                           hk_n=dn_hk, hv=dn_hv, dk=dn_dk, dv=dn_dv)
            x = _out_proj(o_f, o_b, proj, n_conv // dn_vd, dn_norm[j], x, mod, dn_w_out_b, j, tok)
        elif kind == 1:
            n_main = 2 * gla_kd + 2 * gla_vd
            w_a1 = gla_w_in[j:j + 1, :, n_main:]
            proj, a1 = _proj(x, norm1[i], mod, gla_w_in_b, j, n_main, tok, gate_w=w_a1)
            cb_f, cb_b = _gla_prep(a1, gla_w_a2[j], gla_b_a[j], tok)
            o_f, o_b, ns = _gla_scan(proj, 0, proj, gla_h, proj, gla_h, proj, (2 * gla_kd) // gla_dv,
                                     cb_f, cb_b, state_gla, j, tok,
                                     heads=gla_h, dk=gla_dk, dv=gla_dv, q_silu=False, hb=min(gla_h, 2))
            x = _out_proj(o_f, o_b, proj, (2 * gla_kd + gla_vd) // gla_vd, gla_norm[j], x, mod,
                          gla_w_out_b, j, tok)
        else:
            proj = _proj(x, norm1[i], mod, hgrn_w_in_b, j, hgrn_w_in.shape[2], tok)
            k_f, k_b, cb_f, cb_b = _hgrn_prep(proj, hgrn_lb_logits, i, d, tok)
            o_f, o_b, ns = _gla_scan(proj, 0, k_f, 0, k_b, 0, proj, d // hg_dv, cb_f, cb_b, state_hgrn, j, tok,
                                     heads=hg_h, dk=hg_dk, dv=hg_dv, q_silu=True, hb=min(hg_h, 8))
            x = _out_proj(o_f, o_b, proj, 2, hgrn_norm[j], x, mod, hgrn_w_out_b, j, tok)
        if kind != 0:
            new_states[kind].append(ns)
        x = _mlp(x, norm2[i], mod, w_up_b, w_down_b, i, final_norm, tok, final=(i == depth - 1))

    y_ctx, y_dec = x
    y_prompt = y_ctx.reshape(batch, seq, d)
    y_sample = y_dec.reshape(dec_batch, dec_seq, d)
    stacked = [s[0][:, None] if len(s) == 1 else jnp.stack(s, axis=1) for s in new_states[1:]]
    return (y_prompt, y_sample, dn_states) + tuple(stacked)
```
